```python
import math
import jax
import jax.numpy as jnp
from jax import lax
import numpy as np

D_MODEL = 1024
BATCH = 8
SEQ = 2048
DEPTH = 1
DEC_BATCH = 128
DEC_SEQ = 1
PAST_LEN = 16384
PAGE_SIZE = 128

D_RNN = D_MODEL
RG_BLOCKS = 16
RG_BW = D_RNN // RG_BLOCKS
CONV_W = 4
RG_C = 8.0
GLA_HEADS = 4
GLA_DK = D_MODEL // 2
GLA_DV = D_MODEL
GLA_DKH = GLA_DK // GLA_HEADS
GLA_DVH = GLA_DV // GLA_HEADS
GLA_RANK = 16
GLA_TAU = 16.0
GLA_CHUNK = 64
N_GROUPS = 4
EXP_PER_GROUP = 8
N_EXPERTS = N_GROUPS * EXP_PER_GROUP
D_EXPERT = 512
TOP_K = 2
MOE_BLOCK = 128
ALPHA = (2.0 * DEPTH) ** 0.25
BETA = (8.0 * DEPTH) ** -0.25
LN_EPS = 1e-5
RMS_EPS = 1e-6
N_IN = 2 * D_RNN + 2 * GLA_DK + 2 * GLA_DV + GLA_RANK + 2 * D_MODEL

kernel_name = 'hawk_gla_hiermoe_deepnorm_adaln_step'

F32 = jnp.float32


def _ln(x):
    xf = x.astype(F32)
    mu = jnp.mean(xf, -1, keepdims=True)
    var = jnp.mean(jnp.square(xf - mu), -1, keepdims=True)
    return (xf - mu) * lax.rsqrt(var + LN_EPS)


def layer_norm(x, g, b):
    return (_ln(x) * g.astype(F32) + b.astype(F32)).astype(x.dtype)


def modulate(x, shift, scale):
    return (_ln(x) * (1.0 + scale.astype(F32)) + shift.astype(F32)).astype(x.dtype)


def causal_conv(x, buf, w, b):
    T = x.shape[1]
    xp = jnp.concatenate([buf.astype(x.dtype), x], axis=1)
    y = b
    for i in range(CONV_W):
        y = y + xp[:, i:i + T] * w[i]
    return y, xp[:, -(CONV_W - 1):]


def _lin_combine(left, right):
    a1, b1 = left
    a2, b2 = right
    return a1 * a2, a2 * b1 + b2


def rg_lru(x, h0, wa, ba, wx, bx, lam):
    B, T, _ = x.shape
    xf = x.astype(F32)
    xb = xf.reshape(B, T, RG_BLOCKS, RG_BW)
    r = jax.nn.sigmoid(jnp.einsum('btnc,ncd->btnd', xb, wa.astype(F32)).reshape(B, T, D_RNN) + ba.astype(F32))
    i = jax.nn.sigmoid(jnp.einsum('btnc,ncd->btnd', xb, wx.astype(F32)).reshape(B, T, D_RNN) + bx.astype(F32))
    log_a = -RG_C * r * jax.nn.softplus(-lam.astype(F32))
    a = jnp.exp(log_a)
    u = jnp.sqrt(-jnp.expm1(2.0 * log_a)) * (i * xf)
    u = u.at[:, 0].add(a[:, 0] * h0.astype(F32))
    _, h = lax.associative_scan(_lin_combine, (a, u), axis=1)
    return h, h[:, -1]


def gla_chunked(q, k, v, la, S0):
    B, H, T, dk = q.shape
    dv = v.shape[-1]
    pad = (-T) % GLA_CHUNK
    if pad:
        pw = ((0, 0), (0, 0), (0, pad), (0, 0))
        q, k, v, la = jnp.pad(q, pw), jnp.pad(k, pw), jnp.pad(v, pw), jnp.pad(la, pw)
    n = (T + pad) // GLA_CHUNK
    C = GLA_CHUNK
    qc = q.reshape(B, H, n, C, dk)
    kc = k.reshape(B, H, n, C, dk)
    vc = v.reshape(B, H, n, C, dv)
    bcum = jnp.cumsum(la.reshape(B, H, n, C, dk), axis=3)
    btot = bcum[:, :, :, -1, :]
    q_in = qc * jnp.exp(bcum)
    k_in = kc * jnp.exp(-bcum)
    tri = jnp.tril(jnp.ones((C, C), dtype=bool))
    attn = jnp.where(tri, jnp.einsum('bhnik,bhnjk->bhnij', q_in, k_in), 0.0)
    intra = jnp.einsum('bhnij,bhnjv->bhniv', attn, vc)
    k_out = kc * jnp.exp(btot[:, :, :, None, :] - bcum)
    dS = jnp.einsum('bhnjk,bhnjv->bhnkv', k_out, vc)

    def step(S, inp):
        dec, ds = inp
        return S * dec[..., None] + ds, S

    S_fin, S_starts = lax.scan(step, S0, (jnp.moveaxis(jnp.exp(btot), 2, 0), jnp.moveaxis(dS, 2, 0)))
    S_starts = jnp.moveaxis(S_starts, 0, 2)
    inter = jnp.einsum('bhnik,bhnkv->bhniv', q_in, S_starts)
    o = (intra + inter).reshape(B, H, n * C, dv)[:, :, :T]
    return o, S_fin


def gla_recurrent(q, k, v, la, S0):
    def step(S, inp):
        qt, kt, vt, lat = inp
        S = S * jnp.exp(lat)[..., None] + kt[..., :, None] * vt[..., None, :]
        return S, jnp.einsum('bhk,bhkv->bhv', qt, S)

    mv = lambda t: jnp.moveaxis(t, 2, 0)
    S_fin, o = lax.scan(step, S0, (mv(q), mv(k), mv(v), mv(la)))
    return jnp.moveaxis(o, 0, 2), S_fin


def hier_moe(h, w_grp, b_grp, w_exp, b_exp, we_gate, we_up, we_down):
    N, D = h.shape
    grp_logits = (h @ w_grp + b_grp).astype(F32)
    grp_p = jax.nn.softmax(grp_logits, axis=-1)
    g_sel = jnp.argmax(grp_logits, axis=-1)
    g_w = jnp.take_along_axis(grp_p, g_sel[:, None], axis=-1)[:, 0]
    exp_logits = (h @ w_exp + b_exp).astype(F32).reshape(N, N_GROUPS, EXP_PER_GROUP)
    sel_logits = jnp.take_along_axis(exp_logits, g_sel[:, None, None], axis=1)[:, 0]
    top_v, top_i = lax.top_k(sel_logits, TOP_K)
    e_w = jax.nn.softmax(top_v, axis=-1) * g_w[:, None]
    e_id = g_sel[:, None] * EXP_PER_GROUP + top_i
    A = N * TOP_K
    flat_e = e_id.reshape(A)
    flat_w = e_w.reshape(A)
    flat_tok = jnp.repeat(jnp.arange(N, dtype=jnp.int32), TOP_K)
    order = jnp.argsort(flat_e)
    se, stok, sw = flat_e[order], flat_tok[order], flat_w[order]
    counts = jnp.bincount(flat_e, length=N_EXPERTS)
    starts = jnp.cumsum(counts) - counts
    pcounts = (counts + MOE_BLOCK - 1) // MOE_BLOCK * MOE_BLOCK
    pends = jnp.cumsum(pcounts)
    pstarts = pends - pcounts
    dest = pstarts[se] + jnp.arange(A) - starts[se]
    n_blocks = -(-(A + N_EXPERTS * (MOE_BLOCK - 1)) // MOE_BLOCK)
    P = n_blocks * MOE_BLOCK
    x_buf = jnp.zeros((P, D), h.dtype).at[dest].set(h[stok])
    tok_buf = jnp.zeros((P,), jnp.int32).at[dest].set(stok)
    w_buf = jnp.zeros((P,), F32).at[dest].set(sw)
    blk_e = jnp.minimum(jnp.searchsorted(pends, jnp.arange(n_blocks) * MOE_BLOCK, side='right'), N_EXPERTS - 1)

    def expert_block(args):
        xb, e = args
        return (jax.nn.silu(xb @ we_gate[e]) * (xb @ we_up[e])) @ we_down[e]

    y_buf = lax.map(expert_block, (x_buf.reshape(n_blocks, MOE_BLOCK, D), blk_e)).reshape(P, D)
    return jax.ops.segment_sum(y_buf * w_buf[:, None].astype(y_buf.dtype), tok_buf, num_segments=N)


def trunk_layer(x, c, conv_buf, h0, S0, p, chunked):
    B, T, _ = x.shape
    mod = jax.nn.silu(c) @ p['w_mod'] + p['b_mod']
    sh1, sc1, g1, sh2, sc2, g2 = [m[:, None, :] for m in jnp.split(mod, 6, axis=-1)]
    hmix = modulate(x, sh1, sc1)
    proj = jnp.einsum('btd,dn->btn', hmix, p['w_in']) + p['b_in']
    widths = [D_RNN, D_RNN, GLA_DK, GLA_DK, GLA_DV, GLA_DV, GLA_RANK, D_MODEL]
    offs = [int(o) for o in np.cumsum(widths)]
    rx, ry, q, k, v, gg, alr, gate_a, gate_b = jnp.split(proj, offs, axis=-1)
    xc, conv_new = causal_conv(rx, conv_buf, p['conv_w'], p['conv_b'])
    hr, h_last = rg_lru(xc, h0, p['rg_wa'], p['rg_ba'], p['rg_wx'], p['rg_bx'], p['rg_lambda'])
    y_a = hr.astype(x.dtype) * jax.nn.gelu(ry)
    la = jax.nn.log_sigmoid((alr @ p['gla_wa2'] + p['gla_ba']).astype(F32)) / GLA_TAU

    def heads(t, dh):
        return t.reshape(B, T, GLA_HEADS, dh).transpose(0, 2, 1, 3).astype(F32)

    qh = heads(q, GLA_DKH) * (GLA_DKH ** -0.5)
    kh, vh, lah = heads(k, GLA_DKH), heads(v, GLA_DVH), heads(la, GLA_DKH)
    gla_fn = gla_chunked if chunked else gla_recurrent
    o, S_new = gla_fn(qh, kh, vh, lah, S0.astype(F32))
    o = o * lax.rsqrt(jnp.mean(jnp.square(o), -1, keepdims=True) + RMS_EPS) * p['gla_norm_g'].astype(F32)
    o = o.transpose(0, 2, 1, 3).reshape(B, T, GLA_DV).astype(x.dtype)
    y_b = o * jax.nn.silu(gg)
    merged = jax.nn.sigmoid(gate_a) * (y_a @ p['p_a']) + jax.nn.sigmoid(gate_b) * (y_b @ p['p_b'])
    mix = merged @ p['w_o']
    x = layer_norm(ALPHA * x + g1 * mix, p['ln1_g'], p['ln1_b'])
    hf = modulate(x, sh2, sc2).reshape(B * T, D_MODEL)
    ff = hier_moe(hf, p['w_grp'], p['b_grp'], p['w_exp'], p['b_exp'],
                  p['we_gate'], p['we_up'], p['we_down']).reshape(B, T, D_MODEL)
    x = layer_norm(ALPHA * x + g2 * ff, p['ln2_g'], p['ln2_b'])
    return x, conv_new.astype(x.dtype), h_last.astype(x.dtype), S_new.astype(x.dtype)


def setup_inputs(seed: int = 0) -> dict:
    key = jax.random.key(seed)
    ks = iter(jax.random.split(key, 48))
    L = DEPTH

    def nrm(shape, scale):
        return jax.random.normal(next(ks), shape, F32) * scale

    u = jax.random.uniform(next(ks), (L, D_RNN), F32, minval=0.9, maxval=0.999)
    a0 = u ** (1.0 / RG_C)
    rg_lambda = jnp.log(a0) - jnp.log1p(-a0)
    return {
        'x_prompt': nrm((BATCH, SEQ, D_MODEL), 1.0),
        'x_sample': nrm((DEC_BATCH, DEC_SEQ, D_MODEL), 1.0),
        'state_conv': nrm((L, DEC_BATCH, CONV_W - 1, D_RNN), 1.0),
        'state_rglru': nrm((L, DEC_BATCH, D_RNN), 0.5),
        'state_gla': nrm((L, DEC_BATCH, GLA_HEADS, GLA_DKH, GLA_DVH), 0.5),
        'c_prompt': nrm((BATCH, D_MODEL), 1.0),
        'c_sample': nrm((DEC_BATCH, D_MODEL), 1.0),
        'w_mod': nrm((L, D_MODEL, 6 * D_MODEL), D_MODEL ** -0.5),
        'b_mod': nrm((L, 6 * D_MODEL), 0.02),
        'w_in': nrm((L, D_MODEL, N_IN), D_MODEL ** -0.5),
        'b_in': nrm((L, N_IN), 0.02),
        'conv_w': nrm((L, CONV_W, D_RNN), CONV_W ** -0.5),
        'conv_b': nrm((L, D_RNN), 0.02),
        'rg_wa': nrm((L, RG_BLOCKS, RG_BW, RG_BW), RG_BW ** -0.5),
        'rg_ba': nrm((L, D_RNN), 0.02),
        'rg_wx': nrm((L, RG_BLOCKS, RG_BW, RG_BW), RG_BW ** -0.5),
        'rg_bx': nrm((L, D_RNN), 0.02),
        'rg_lambda': rg_lambda,
        'gla_wa2': nrm((L, GLA_RANK, GLA_DK), GLA_RANK ** -0.5),
        'gla_ba': nrm((L, GLA_DK), 0.02),
        'gla_norm_g': 1.0 + nrm((L, GLA_DVH), 0.02),
        'p_a': nrm((L, D_RNN, D_MODEL), BETA * D_RNN ** -0.5),
        'p_b': nrm((L, GLA_DV, D_MODEL), BETA * GLA_DV ** -0.5),
        'w_o': nrm((L, D_MODEL, D_MODEL), BETA * D_MODEL ** -0.5),
        'ln1_g': 1.0 + nrm((L, D_MODEL), 0.02),
        'ln1_b': nrm((L, D_MODEL), 0.02),
        'w_grp': nrm((L, D_MODEL, N_GROUPS), D_MODEL ** -0.5),
        'b_grp': nrm((L, N_GROUPS), 0.01),
        'w_exp': nrm((L, D_MODEL, N_EXPERTS), D_MODEL ** -0.5),
        'b_exp': nrm((L, N_EXPERTS), 0.01),
        'we_gate': nrm((L, N_EXPERTS, D_MODEL, D_EXPERT), D_MODEL ** -0.5),
        'we_up': nrm((L, N_EXPERTS, D_MODEL, D_EXPERT), D_MODEL ** -0.5),
        'we_down': nrm((L, N_EXPERTS, D_EXPERT, D_MODEL), BETA * D_EXPERT ** -0.5),
        'ln2_g': 1.0 + nrm((L, D_MODEL), 0.02),
        'ln2_b': nrm((L, D_MODEL), 0.02),
    }


def reference(x_prompt, x_sample, state_conv, state_rglru, state_gla, c_prompt, c_sample,
              w_mod, b_mod, w_in, b_in, conv_w, conv_b, rg_wa, rg_ba, rg_wx, rg_bx, rg_lambda,
              gla_wa2, gla_ba, gla_norm_g, p_a, p_b, w_o, ln1_g, ln1_b,
              w_grp, b_grp, w_exp, b_exp, we_gate, we_up, we_down, ln2_g, ln2_b):
    yp, ys = x_prompt, x_sample
    bp = x_prompt.shape[0]
    conv_p, rg_p, gla_p, conv_s, rg_s, gla_s = [], [], [], [], [], []
    for l in range(DEPTH):
        prm = {
            'w_mod': w_mod[l], 'b_mod': b_mod[l], 'w_in': w_in[l], 'b_in': b_in[l],
            'conv_w': conv_w[l], 'conv_b': conv_b[l], 'rg_wa': rg_wa[l], 'rg_ba': rg_ba[l],
            'rg_wx': rg_wx[l], 'rg_bx': rg_bx[l], 'rg_lambda': rg_lambda[l],
            'gla_wa2': gla_wa2[l], 'gla_ba': gla_ba[l], 'gla_norm_g': gla_norm_g[l],
            'p_a': p_a[l], 'p_b': p_b[l], 'w_o': w_o[l], 'ln1_g': ln1_g[l], 'ln1_b': ln1_b[l],
            'w_grp': w_grp[l], 'b_grp': b_grp[l], 'w_exp': w_exp[l], 'b_exp': b_exp[l],
            'we_gate': we_gate[l], 'we_up': we_up[l], 'we_down': we_down[l],
            'ln2_g': ln2_g[l], 'ln2_b': ln2_b[l],
        }
        zc = jnp.zeros((bp, CONV_W - 1, D_RNN), x_prompt.dtype)
        zh = jnp.zeros((bp, D_RNN), x_prompt.dtype)
        zs = jnp.zeros((bp, GLA_HEADS, GLA_DKH, GLA_DVH), x_prompt.dtype)
        yp, cp, hp, sp = trunk_layer(yp, c_prompt, zc, zh, zs, prm, True)
        ys, cs, hs, ss = trunk_layer(ys, c_sample, state_conv[l], state_rglru[l], state_gla[l], prm, False)
        conv_p.append(cp); rg_p.append(hp); gla_p.append(sp)
        conv_s.append(cs); rg_s.append(hs); gla_s.append(ss)
    new_conv_p = jnp.stack(conv_p, 0)
    new_rglru_p = jnp.stack(rg_p, 0)
    new_gla_p = jnp.stack(gla_p, 0)
    new_conv_s = jnp.stack(conv_s, 0)
    new_rglru_s = jnp.stack(rg_s, 0)
    new_gla_s = jnp.stack(gla_s, 0)
    return (yp, ys, new_conv_p, new_rglru_p, new_gla_p, new_conv_s, new_rglru_s, new_gla_s)
```

```python
import functools

import jax
import jax.numpy as jnp
import numpy as np
from jax import lax
from jax.experimental import pallas as pl
from jax.experimental.pallas import tpu as pltpu

F32 = jnp.float32
BF16 = jnp.bfloat16
I32 = jnp.int32

D_MODEL = 1024
D_RNN = D_MODEL
RG_BLOCKS = 16
RG_BW = D_RNN // RG_BLOCKS
CONV_W = 4
RG_C = 8.0
GLA_HEADS = 4
GLA_DK = D_MODEL // 2
GLA_DV = D_MODEL
GLA_DKH = GLA_DK // GLA_HEADS
GLA_DVH = GLA_DV // GLA_HEADS
GLA_RANK = 16
GLA_TAU = 16.0
GLA_CHUNK = 64
N_GROUPS = 4
EXP_PER_GROUP = 8
N_EXPERTS = N_GROUPS * EXP_PER_GROUP
D_EXPERT = 512
DEPTH = 1
ALPHA = (2.0 * DEPTH) ** 0.25
LN_EPS = 1e-5
RMS_EPS = 1e-6

LANES = 128
SUBLANES = 8
MXU_DIM = 256
VMEM_LIMIT_BYTES = 56 * 1024 * 1024

MIX_TT = 256
MOE_TM = 256
ROW_CHUNK = 128
SAMPLE_SB = 8
RG_GROUP = MXU_DIM // RG_BW
N_RG_TILES = RG_BLOCKS // RG_GROUP
N_MAIN = 2 * D_RNN + 2 * GLA_DK + 2 * GLA_DV + 2 * D_MODEL
ROUTE_G0 = N_EXPERTS

O_RX, O_RY, O_Q, O_K, O_V, O_GG, O_GA, O_GB = 0, 1024, 2048, 2560, 3072, 4096, 5120, 6144


def _ln(x):
    mu = jnp.mean(x, -1, keepdims=True)
    xc = x - mu
    var = jnp.mean(xc * xc, -1, keepdims=True)
    return xc * lax.rsqrt(var + LN_EPS)


def _dot(a, b):
    return jnp.dot(a, b, preferred_element_type=F32)


def _expm1(x):
    u = jnp.exp(x)
    small = (u - 1.0) * x / jnp.log(u)
    return jnp.where(u == 1.0, x, jnp.where(jnp.abs(x) < 0.5, small, u - 1.0))


def _const_spec(shape):
    nd = len(shape)
    return pl.BlockSpec(shape, lambda *_: (0,) * nd, pipeline_mode=pl.Buffered(1))


def _mod_kernel(c_ref, w_ref, b_ref, o_ref):
    c = c_ref[...]
    s = jax.nn.silu(c)
    o_ref[...] = _dot(s.astype(BF16), w_ref[...].astype(BF16)) + b_ref[...]


def _mod_call(c_all, w_mod, b_mod):
    n = c_all.shape[0]
    tn = 512
    return pl.pallas_call(
        _mod_kernel,
        grid=(6 * D_MODEL // tn,),
        in_specs=[pl.BlockSpec((n, D_MODEL), lambda i: (0, 0)),
                  pl.BlockSpec((D_MODEL, tn), lambda i: (0, i)),
                  pl.BlockSpec((1, tn), lambda i: (0, i))],
        out_specs=pl.BlockSpec((n, tn), lambda i: (0, i)),
        out_shape=jax.ShapeDtypeStruct((n, 6 * D_MODEL), F32),
        compiler_params=pltpu.CompilerParams(dimension_semantics=("arbitrary",)),
        name="mod",
    )(c_all, w_mod, b_mod)


def _gates(xc, wrg_ref, rba, rbx, lam):
    xcb = xc.astype(BF16)
    parts = [_dot(xcb[:, c * MXU_DIM:(c + 1) * MXU_DIM], wrg_ref[c]) for c in range(N_RG_TILES)]
    r_pre = jnp.concatenate([p[:, :MXU_DIM] for p in parts], axis=1) + rba
    i_pre = jnp.concatenate([p[:, MXU_DIM:] for p in parts], axis=1) + rbx
    r = jax.nn.sigmoid(r_pre)
    ig = jax.nn.sigmoid(i_pre)
    log_a = -RG_C * r * jax.nn.softplus(-lam)
    a = jnp.exp(log_a)
    u = jnp.sqrt(-_expm1(2.0 * log_a)) * (ig * xc)
    return a, u


def _log_decay(hb, walr_ref, balr_ref, wa2_ref, gba_ref):
    alr = _dot(hb, walr_ref[...]) + balr_ref[...]
    return jax.nn.log_sigmoid(_dot(alr.astype(BF16), wa2_ref[...]) + gba_ref[...]) / GLA_TAU


def _post(x, o, gg, ga, gb, y_a, g1, sh2, sc2, gng, pa_ref, pb_ref, wo_ref, l1g, l1b, wr_ref, br):
    rows = x.shape[0]
    heads = []
    for hh in range(GLA_HEADS):
        oh = o[:, hh * GLA_DVH:(hh + 1) * GLA_DVH]
        ms = jnp.mean(oh * oh, -1, keepdims=True)
        heads.append(oh * lax.rsqrt(ms + RMS_EPS) * gng)
    y_b = jnp.concatenate(heads, axis=1) * jax.nn.silu(gg)
    merged = (jax.nn.sigmoid(ga) * _dot(y_a.astype(BF16), pa_ref[...])
              + jax.nn.sigmoid(gb) * _dot(y_b.astype(BF16), pb_ref[...]))
    mix = _dot(merged.astype(BF16), wo_ref[...])
    x1 = _ln(ALPHA * x + g1 * mix) * l1g + l1b
    hf = _ln(x1) * (1.0 + sc2) + sh2
    logits = _dot(hf.astype(BF16), wr_ref[...]) + br
    lane = lax.broadcasted_iota(I32, (rows, LANES), 1).astype(F32)
    neg = jnp.float32(-jnp.inf)
    big = jnp.float32(LANES)
    g_valid = (lane >= ROUTE_G0) & (lane < ROUTE_G0 + N_GROUPS)
    gl = jnp.where(g_valid, logits, neg)
    gmax = jnp.max(gl, -1, keepdims=True)
    g_lane = jnp.min(jnp.where(gl == gmax, lane, big), -1, keepdims=True)
    g_w = 1.0 / jnp.sum(jnp.exp(gl - gmax), -1, keepdims=True)
    e_lo = (g_lane - ROUTE_G0) * EXP_PER_GROUP
    el = jnp.where((lane >= e_lo) & (lane < e_lo + EXP_PER_GROUP), logits, neg)
    t1 = jnp.max(el, -1, keepdims=True)
    i1 = jnp.min(jnp.where(el == t1, lane, big), -1, keepdims=True)
    el2 = jnp.where(lane == i1, neg, el)
    t2 = jnp.max(el2, -1, keepdims=True)
    i2 = jnp.min(jnp.where(el2 == t2, lane, big), -1, keepdims=True)
    e2 = jnp.exp(t2 - t1)
    den = 1.0 + e2
    w1 = (1.0 / den) * g_w
    w2 = (e2 / den) * g_w
    route = jnp.where(lane == 0, i1.astype(F32),
                      jnp.where(lane == 1, i2.astype(F32),
                                jnp.where(lane == 2, w1, jnp.where(lane == 3, w2, 0.0))))
    return x1, hf, route


def _shift_rows(x, s):
    return pltpu.roll(x, s, 0)


def _mixer_kernel(x_ref, mod_ref, win_ref, bin_ref, walr_ref, balr_ref, wa2_ref, gba_ref, gng_ref,
                  cw_ref, cb_ref, wrg_ref, rba_ref, rbx_ref, lam_ref, pa_ref, pb_ref, wo_ref,
                  l1g_ref, l1b_ref, wr_ref, br_ref,
                  x1_ref, hf_ref, route_ref, convn_ref, hlast_ref, sfin_ref,
                  rxbuf, hcar, st_ref, o_scr):
    j = pl.program_id(1)
    tt = x_ref.shape[1]

    @pl.when(j == 0)
    def _init():
        rxbuf[0:SUBLANES, :] = jnp.zeros((SUBLANES, D_RNN), F32)
        hcar[...] = jnp.zeros_like(hcar)
        st_ref[...] = jnp.zeros_like(st_ref)

    x = x_ref[0]
    mod = mod_ref[0]
    sh1, sc1, g1 = mod[0:1], mod[1:2], mod[2:3]
    sh2, sc2 = mod[3:4], mod[4:5]
    hb = (_ln(x) * (1.0 + sc1) + sh1).astype(BF16)

    def proj(lo, hi):
        return _dot(hb, win_ref[:, lo:hi]) + bin_ref[:, lo:hi]

    rx = proj(O_RX, O_RX + D_RNN)
    rxbuf[SUBLANES:SUBLANES + tt, :] = rx
    cw = cw_ref[...]
    xc = cb_ref[...] + rxbuf[SUBLANES - 3:SUBLANES - 3 + tt, :] * cw[0:1]
    xc = xc + rxbuf[SUBLANES - 2:SUBLANES - 2 + tt, :] * cw[1:2]
    xc = xc + rxbuf[SUBLANES - 1:SUBLANES - 1 + tt, :] * cw[2:3]
    xc = xc + rx * cw[3:4]
    rxbuf[0:SUBLANES, :] = rxbuf[tt:tt + SUBLANES, :]

    a, u = _gates(xc, wrg_ref, rba_ref[...], rbx_ref[...], lam_ref[...])
    row = lax.broadcasted_iota(I32, (tt, D_RNN), 0)
    s = 1
    while s < tt:
        keep = row >= s
        a_s = jnp.where(keep, _shift_rows(a, s), 1.0)
        u_s = jnp.where(keep, _shift_rows(u, s), 0.0)
        u = a * u_s + u
        a = a * a_s
        s *= 2
    hseq = a * hcar[0:1, :] + u
    hcar[0:1, :] = hseq[tt - 1:tt, :]
    y_a = hseq * jax.nn.gelu(proj(O_RY, O_RY + D_RNN))

    q = proj(O_Q, O_Q + GLA_DK) * (GLA_DKH ** -0.5)
    k = proj(O_K, O_K + GLA_DK)
    v = proj(O_V, O_V + GLA_DV)
    la = _log_decay(hb, walr_ref, balr_ref, wa2_ref, gba_ref)
    rowk = lax.broadcasted_iota(I32, (tt, GLA_DK), 0) % GLA_CHUNK
    bcum = la
    s = 1
    while s < GLA_CHUNK:
        bcum = bcum + jnp.where(rowk >= s, _shift_rows(bcum, s), 0.0)
        s *= 2
    eb = jnp.exp(bcum)
    q_in = (q * eb).astype(BF16)
    k_in = (k * jnp.exp(-bcum)).astype(BF16)
    tri = (lax.broadcasted_iota(I32, (GLA_CHUNK, GLA_CHUNK), 0)
           >= lax.broadcasted_iota(I32, (GLA_CHUNK, GLA_CHUNK), 1))
    for c in range(tt // GLA_CHUNK):
        r0 = c * GLA_CHUNK
        for hh in range(GLA_HEADS):
            kc = slice(hh * GLA_DKH, (hh + 1) * GLA_DKH)
            vc = slice(hh * GLA_DVH, (hh + 1) * GLA_DVH)
            b = bcum[r0:r0 + GLA_CHUNK, kc]
            btot = b[GLA_CHUNK - 1:GLA_CHUNK, :]
            qi = q_in[r0:r0 + GLA_CHUNK, kc]
            ki = k_in[r0:r0 + GLA_CHUNK, kc]
            k_out = (k[r0:r0 + GLA_CHUNK, kc] * jnp.exp(btot - b)).astype(BF16)
            vh = v[r0:r0 + GLA_CHUNK, vc].astype(BF16)
            attn = lax.dot_general(qi, ki, (((1,), (1,)), ((), ())), preferred_element_type=F32)
            attn = jnp.where(tri, attn, 0.0)
            st = st_ref[hh]
            inter = lax.dot_general(qi, st.astype(BF16), (((1,), (1,)), ((), ())), preferred_element_type=F32)
            o_scr[r0:r0 + GLA_CHUNK, vc] = _dot(attn.astype(BF16), vh) + inter
            d_st = lax.dot_general(vh, k_out, (((0,), (0,)), ((), ())), preferred_element_type=F32)
            st_ref[hh] = st * jnp.exp(btot) + d_st

    x1, hf, route = _post(x, o_scr[...], proj(O_GG, O_GG + GLA_DV), proj(O_GA, O_GA + D_MODEL),
                          proj(O_GB, O_GB + D_MODEL), y_a, g1, sh2, sc2, gng_ref[...],
                          pa_ref, pb_ref, wo_ref, l1g_ref[...], l1b_ref[...], wr_ref, br_ref[...])
    x1_ref[0] = x1
    hf_ref[0] = hf
    route_ref[0] = route

    @pl.when(j == pl.num_programs(1) - 1)
    def _final():
        convn_ref[0] = rxbuf[SUBLANES - (CONV_W - 1):SUBLANES, :]
        hlast_ref[0] = hcar[0:1, :]
        for hh in range(GLA_HEADS):
            sfin_ref[0, hh] = st_ref[hh].T


def _mixer_call(x, mod3, wts):
    b, t, _ = x.shape
    tt = MIX_TT
    assert t % tt == 0 and tt % GLA_CHUNK == 0
    tok = lambda bi, j: (bi, j, 0)
    per_b = lambda bi, j: (bi, 0, 0)
    in_specs = [pl.BlockSpec((1, tt, D_MODEL), tok),
                pl.BlockSpec((1, 6, D_MODEL), per_b)] + [_const_spec(w.shape) for w in wts]
    out_specs = [pl.BlockSpec((1, tt, D_MODEL), tok),
                 pl.BlockSpec((1, tt, D_MODEL), tok),
                 pl.BlockSpec((1, tt, LANES), tok),
                 pl.BlockSpec((1, CONV_W - 1, D_RNN), per_b),
                 pl.BlockSpec((1, 1, D_RNN), per_b),
                 pl.BlockSpec((1, GLA_HEADS, GLA_DKH, GLA_DVH), lambda bi, j: (bi, 0, 0, 0))]
    out_shape = [jax.ShapeDtypeStruct((b, t, D_MODEL), F32),
                 jax.ShapeDtypeStruct((b, t, D_MODEL), F32),
                 jax.ShapeDtypeStruct((b, t, LANES), F32),
                 jax.ShapeDtypeStruct((b, CONV_W - 1, D_RNN), F32),
                 jax.ShapeDtypeStruct((b, 1, D_RNN), F32),
                 jax.ShapeDtypeStruct((b, GLA_HEADS, GLA_DKH, GLA_DVH), F32)]
    scratch = [pltpu.VMEM((tt + SUBLANES, D_RNN), F32),
               pltpu.VMEM((SUBLANES, D_RNN), F32),
               pltpu.VMEM((GLA_HEADS, GLA_DVH, GLA_DKH), F32),
               pltpu.VMEM((tt, GLA_DV), F32)]
    return pl.pallas_call(
        _mixer_kernel,
        grid=(b, t // tt),
        in_specs=in_specs,
        out_specs=out_specs,
        out_shape=out_shape,
        scratch_shapes=scratch,
        compiler_params=pltpu.CompilerParams(dimension_semantics=("arbitrary", "arbitrary"),
                                             vmem_limit_bytes=VMEM_LIMIT_BYTES),
        name="mixer",
    )(x, mod3, *wts)


def _s_pre_kernel(x_ref, mod_ref, sconv_ref, h0_ref, win_ref, bin_ref, walr_ref, balr_ref, wa2_ref, gba_ref,
                  cw_ref, cb_ref, wrg_ref, rba_ref, rbx_ref, lam_ref,
                  convn_ref, hnew_ref, ya_ref, q_ref, k_ref, v_ref, la_ref, gg_ref, ga_ref, gb_ref):
    x = x_ref[...]
    hb = (_ln(x) * (1.0 + mod_ref[1]) + mod_ref[0]).astype(BF16)

    def proj(lo, hi):
        return _dot(hb, win_ref[:, lo:hi]) + bin_ref[:, lo:hi]

    rx = proj(O_RX, O_RX + D_RNN)
    cw = cw_ref[...]
    xc = cb_ref[...] + sconv_ref[0] * cw[0:1]
    xc = xc + sconv_ref[1] * cw[1:2]
    xc = xc + sconv_ref[2] * cw[2:3]
    xc = xc + rx * cw[3:4]
    convn_ref[0] = sconv_ref[1]
    convn_ref[1] = sconv_ref[2]
    convn_ref[2] = rx
    a, u = _gates(xc, wrg_ref, rba_ref[...], rbx_ref[...], lam_ref[...])
    hnew = u + a * h0_ref[...]
    hnew_ref[...] = hnew
    ya_ref[...] = hnew * jax.nn.gelu(proj(O_RY, O_RY + D_RNN))
    q_ref[...] = proj(O_Q, O_Q + GLA_DK) * (GLA_DKH ** -0.5)
    k_ref[...] = proj(O_K, O_K + GLA_DK)
    v_ref[...] = proj(O_V, O_V + GLA_DV)
    la_ref[...] = _log_decay(hb, walr_ref, balr_ref, wa2_ref, gba_ref)
    gg_ref[...] = proj(O_GG, O_GG + GLA_DV)
    ga_ref[...] = proj(O_GA, O_GA + D_MODEL)
    gb_ref[...] = proj(O_GB, O_GB + D_MODEL)


def _s_pre_call(x_s, mod_s, sconv, h0, wts):
    n = x_s.shape[0]
    full = lambda shape: pl.BlockSpec(shape, lambda i: (0,) * len(shape))
    ins = [x_s, mod_s, sconv, h0] + list(wts)
    shapes = [(CONV_W - 1, n, D_RNN), (n, D_RNN), (n, D_RNN), (n, GLA_DK), (n, GLA_DK), (n, GLA_DV),
              (n, GLA_DK), (n, GLA_DV), (n, D_MODEL), (n, D_MODEL)]
    return pl.pallas_call(
        _s_pre_kernel,
        grid=(1,),
        in_specs=[full(a.shape) for a in ins],
        out_specs=[full(s) for s in shapes],
        out_shape=[jax.ShapeDtypeStruct(s, F32) for s in shapes],
        compiler_params=pltpu.CompilerParams(dimension_semantics=("arbitrary",),
                                             vmem_limit_bytes=VMEM_LIMIT_BYTES),
        name="s_pre",
    )(*ins)


def _to_column(row, n):
    eye = lax.broadcasted_iota(I32, (n, n), 0) == lax.broadcasted_iota(I32, (n, n), 1)
    return jnp.sum(jnp.where(eye, jnp.broadcast_to(row, (n, n)), 0.0), axis=1, keepdims=True)


def _s_state_kernel(s_ref, q_ref, k_ref, v_ref, la_ref, snew_ref, o_ref):
    sb = s_ref.shape[0]
    qb = q_ref[...].astype(BF16)
    dec = jnp.exp(la_ref[...])
    kk = k_ref[...]
    vv = v_ref[...]
    for si in range(sb):
        for hh in range(GLA_HEADS):
            kc = slice(hh * GLA_DKH, (hh + 1) * GLA_DKH)
            vc = slice(hh * GLA_DVH, (hh + 1) * GLA_DVH)
            d_col = _to_column(dec[si:si + 1, kc], GLA_DKH)
            k_col = _to_column(kk[si:si + 1, kc], GLA_DKH)
            s_new = s_ref[si, hh] * d_col + k_col * vv[si:si + 1, vc]
            snew_ref[si, hh] = s_new
            o_all = _dot(qb[:, kc], s_new.astype(BF16))
            o_ref[si:si + 1, vc] = o_all[si:si + 1, :]


def _s_state_call(state, q, k, v, la):
    n = state.shape[0]
    sb = SAMPLE_SB
    assert n % sb == 0
    blk = lambda w: pl.BlockSpec((sb, w), lambda i: (i, 0))
    st_spec = pl.BlockSpec((sb, GLA_HEADS, GLA_DKH, GLA_DVH), lambda i: (i, 0, 0, 0))
    return pl.pallas_call(
        _s_state_kernel,
        grid=(n // sb,),
        in_specs=[st_spec, blk(GLA_DK), blk(GLA_DK), blk(GLA_DV), blk(GLA_DK)],
        out_specs=[st_spec, blk(GLA_DV)],
        out_shape=[jax.ShapeDtypeStruct(state.shape, F32), jax.ShapeDtypeStruct((n, GLA_DV), F32)],
        compiler_params=pltpu.CompilerParams(dimension_semantics=("arbitrary",),
                                             vmem_limit_bytes=VMEM_LIMIT_BYTES),
        name="s_state",
    )(state, q, k, v, la)


def _s_post_kernel(x_ref, mod_ref, o_ref, gg_ref, ga_ref, gb_ref, ya_ref, gng_ref, pa_ref, pb_ref, wo_ref,
                   l1g_ref, l1b_ref, wr_ref, br_ref, x1_ref, hf_ref, route_ref):
    x1, hf, route = _post(x_ref[...], o_ref[...], gg_ref[...], ga_ref[...], gb_ref[...], ya_ref[...],
                          mod_ref[2], mod_ref[3], mod_ref[4], gng_ref[...], pa_ref, pb_ref, wo_ref,
                          l1g_ref[...], l1b_ref[...], wr_ref, br_ref[...])
    x1_ref[...] = x1
    hf_ref[...] = hf
    route_ref[...] = route


def _s_post_call(x_s, mod_s, o, gg, ga, gb, ya, wts):
    n = x_s.shape[0]
    full = lambda shape: pl.BlockSpec(shape, lambda i: (0,) * len(shape))
    ins = [x_s, mod_s, o, gg, ga, gb, ya] + list(wts)
    shapes = [(n, D_MODEL), (n, D_MODEL), (n, LANES)]
    return pl.pallas_call(
        _s_post_kernel,
        grid=(1,),
        in_specs=[full(a.shape) for a in ins],
        out_specs=[full(s) for s in shapes],
        out_shape=[jax.ShapeDtypeStruct(s, F32) for s in shapes],
        compiler_params=pltpu.CompilerParams(dimension_semantics=("arbitrary",),
                                             vmem_limit_bytes=VMEM_LIMIT_BYTES),
        name="s_post",
    )(*ins)


def _rank_kernel(eid_ref, rank_ref, cnt_ref, carry):
    n_chunks = eid_ref.shape[0]
    carry[...] = jnp.zeros_like(carry)
    e_iota = lax.broadcasted_iota(I32, (N_EXPERTS, LANES), 0)
    upper = (lax.broadcasted_iota(I32, (LANES, LANES), 0)
             < lax.broadcasted_iota(I32, (LANES, LANES), 1)).astype(BF16)

    def body(c, _):
        ids = eid_ref[c]
        oh0 = (e_iota == ids[0:1, :])
        oh1 = (e_iota == ids[1:2, :])
        both = (oh0 | oh1).astype(F32)
        before = _dot(both.astype(BF16), upper) + carry[:, 0:1]
        r0 = jnp.sum(jnp.where(oh0, before, 0.0), axis=0, keepdims=True)
        r1 = jnp.sum(jnp.where(oh1, before, 0.0), axis=0, keepdims=True)
        rank_ref[c] = jnp.concatenate([r0, r1], axis=0).astype(I32)
        carry[...] = carry[...] + jnp.sum(both, axis=1, keepdims=True)
        return 0

    lax.fori_loop(0, n_chunks, body, 0)
    cnt_ref[...] = carry[...].astype(I32)


def _rank_call(eid3):
    n_chunks = eid3.shape[0]
    full = lambda shape: pl.BlockSpec(shape, lambda i: (0,) * len(shape))
    return pl.pallas_call(
        _rank_kernel,
        grid=(1,),
        in_specs=[full(eid3.shape)],
        out_specs=[full(eid3.shape), full((N_EXPERTS, LANES))],
        out_shape=[jax.ShapeDtypeStruct(eid3.shape, I32), jax.ShapeDtypeStruct((N_EXPERTS, LANES), I32)],
        scratch_shapes=[pltpu.VMEM((N_EXPERTS, LANES), F32)],
        compiler_params=pltpu.CompilerParams(dimension_semantics=("arbitrary",)),
        name="rank",
    )(eid3)


def _row_copy(src_ref, dst_ref, s_row, d_row, sem):
    return pltpu.make_async_copy(src_ref.at[pl.ds(s_row, 1)], dst_ref.at[pl.ds(d_row, 1)], sem)


def _dispatch_kernel(dest_ref, src_ref, buf_in_ref, buf_ref, sem):
    del buf_in_ref
    base = pl.program_id(0) * ROW_CHUNK

    def start(i, _):
        _row_copy(src_ref, buf_ref, base + i // 2, dest_ref[i], sem).start()
        return 0

    lax.fori_loop(0, 2 * ROW_CHUNK, start, 0)

    def wait(i, _):
        _row_copy(src_ref, buf_ref, 0, 0, sem).wait()
        return 0

    lax.fori_loop(0, 2 * ROW_CHUNK, wait, 0)


def _dispatch_call(dest_flat, src, buf0):
    n = src.shape[0]
    assert n % ROW_CHUNK == 0
    return pl.pallas_call(
        _dispatch_kernel,
        grid=(n // ROW_CHUNK,),
        in_specs=[pl.BlockSpec((2 * ROW_CHUNK,), lambda i: (i,), memory_space=pltpu.SMEM),
                  pl.BlockSpec(memory_space=pl.ANY),
                  pl.BlockSpec(memory_space=pl.ANY)],
        out_specs=pl.BlockSpec(memory_space=pl.ANY),
        out_shape=jax.ShapeDtypeStruct(buf0.shape, buf0.dtype),
        scratch_shapes=[pltpu.SemaphoreType.DMA(())],
        input_output_aliases={2: 0},
        compiler_params=pltpu.CompilerParams(dimension_semantics=("arbitrary",), has_side_effects=True),
        name="dispatch",
    )(dest_flat, src, buf0)


def _collect_kernel(dest_ref, ybuf_ref, out_ref, sem):
    base = pl.program_id(0) * ROW_CHUNK

    def start(i, _):
        _row_copy(ybuf_ref, out_ref, dest_ref[i], 2 * base + i, sem).start()
        return 0

    lax.fori_loop(0, 2 * ROW_CHUNK, start, 0)

    def wait(i, _):
        _row_copy(ybuf_ref, out_ref, 0, 0, sem).wait()
        return 0

    lax.fori_loop(0, 2 * ROW_CHUNK, wait, 0)


def _collect_call(dest_flat, ybuf):
    n2 = dest_flat.shape[0]
    assert n2 % (2 * ROW_CHUNK) == 0
    return pl.pallas_call(
        _collect_kernel,
        grid=(n2 // (2 * ROW_CHUNK),),
        in_specs=[pl.BlockSpec((2 * ROW_CHUNK,), lambda i: (i,), memory_space=pltpu.SMEM),
                  pl.BlockSpec(memory_space=pl.ANY)],
        out_specs=pl.BlockSpec(memory_space=pl.ANY),
        out_shape=jax.ShapeDtypeStruct((n2, D_MODEL), F32),
        scratch_shapes=[pltpu.SemaphoreType.DMA(())],
        compiler_params=pltpu.CompilerParams(dimension_semantics=("arbitrary",), has_side_effects=True),
        name="collect",
    )(dest_flat, ybuf)


def _expert_kernel(blk_e_ref, n_used_ref, x_ref, wg_ref, wu_ref, wd_ref, y_ref):
    del blk_e_ref
    i = pl.program_id(0)

    @pl.when(i < n_used_ref[0])
    def _run():
        xb = x_ref[...].astype(BF16)
        g = _dot(xb, wg_ref[0].astype(BF16))
        u = _dot(xb, wu_ref[0].astype(BF16))
        act = (jax.nn.silu(g) * u).astype(BF16)
        y_ref[...] = _dot(act, wd_ref[0].astype(BF16))

    @pl.when(i >= n_used_ref[0])
    def _skip():
        y_ref[...] = jnp.zeros_like(y_ref)


def _expert_call(blk_e, n_used, xbuf, we_gate, we_up, we_down):
    p = xbuf.shape[0]
    assert p % MOE_TM == 0
    grid_spec = pltpu.PrefetchScalarGridSpec(
        num_scalar_prefetch=2,
        grid=(p // MOE_TM,),
        in_specs=[pl.BlockSpec((MOE_TM, D_MODEL), lambda i, be, nu: (i, 0)),
                  pl.BlockSpec((1, D_MODEL, D_EXPERT), lambda i, be, nu: (be[i], 0, 0)),
                  pl.BlockSpec((1, D_MODEL, D_EXPERT), lambda i, be, nu: (be[i], 0, 0)),
                  pl.BlockSpec((1, D_EXPERT, D_MODEL), lambda i, be, nu: (be[i], 0, 0))],
        out_specs=pl.BlockSpec((MOE_TM, D_MODEL), lambda i, be, nu: (i, 0)),
    )
    return pl.pallas_call(
        _expert_kernel,
        grid_spec=grid_spec,
        out_shape=jax.ShapeDtypeStruct((p, D_MODEL), F32),
        compiler_params=pltpu.CompilerParams(dimension_semantics=("arbitrary",),
                                             vmem_limit_bytes=VMEM_LIMIT_BYTES),
        name="experts",
    )(blk_e, n_used, xbuf, we_gate, we_up, we_down)


def _final_kernel(x1_ref, yg_ref, route_ref, g2_ref, l2g_ref, l2b_ref, out_ref):
    x1 = x1_ref[...]
    yg = yg_ref[...]
    route = route_ref[...]
    ff = yg[:, :D_MODEL] * route[:, 2:3] + yg[:, D_MODEL:] * route[:, 3:4]
    out_ref[...] = _ln(ALPHA * x1 + g2_ref[0] * ff) * l2g_ref[...] + l2b_ref[...]


def _final_call(x1, yg2, route, g2, ln2_g, ln2_b, rows_per_g2, tile):
    n = x1.shape[0]
    assert n % tile == 0 and rows_per_g2 % tile == 0
    per = rows_per_g2 // tile
    g2_rows = g2.shape[1]
    return pl.pallas_call(
        _final_kernel,
        grid=(n // tile,),
        in_specs=[pl.BlockSpec((tile, D_MODEL), lambda i: (i, 0)),
                  pl.BlockSpec((tile, 2 * D_MODEL), lambda i: (i, 0)),
                  pl.BlockSpec((tile, LANES), lambda i: (i, 0)),
                  pl.BlockSpec((1, g2_rows, D_MODEL), lambda i: (i // per, 0, 0)),
                  pl.BlockSpec((1, D_MODEL), lambda i: (0, 0)),
                  pl.BlockSpec((1, D_MODEL), lambda i: (0, 0))],
        out_specs=pl.BlockSpec((tile, D_MODEL), lambda i: (i, 0)),
        out_shape=jax.ShapeDtypeStruct((n, D_MODEL), F32),
        compiler_params=pltpu.CompilerParams(dimension_semantics=("arbitrary",)),
        name="final",
    )(x1, yg2, route, g2, ln2_g, ln2_b)


def _block_diag_gate_weights(wa, wx):
    def bd(w):
        w = w.reshape(N_RG_TILES, RG_GROUP, RG_BW, RG_BW)
        eye = jnp.eye(RG_GROUP, dtype=w.dtype)
        return jnp.einsum('tgcd,gh->tgchd', w, eye).reshape(N_RG_TILES, MXU_DIM, MXU_DIM)
    return jnp.concatenate([bd(wa), bd(wx)], axis=2).astype(BF16)


def kernel(x_prompt, x_sample, state_conv, state_rglru, state_gla, c_prompt, c_sample, w_mod, b_mod, w_in, b_in,
           conv_w, conv_b, rg_wa, rg_ba, rg_wx, rg_bx, rg_lambda, gla_wa2, gla_ba, gla_norm_g, p_a, p_b, w_o,
           ln1_g, ln1_b, w_grp, b_grp, w_exp, b_exp, we_gate, we_up, we_down, ln2_g, ln2_b):
    assert w_mod.shape[0] == DEPTH == 1
    bp, tp, _ = x_prompt.shape
    ns = x_sample.shape[0]
    n_p = bp * tp
    n_tok = n_p + ns
    row = lambda a: a.reshape(1, -1)

    lo = 2 * D_RNN + 2 * GLA_DK + 2 * GLA_DV
    w_in0 = w_in[0]
    win_main = jnp.concatenate([w_in0[:, :lo], w_in0[:, lo + GLA_RANK:]], axis=1).astype(BF16)
    bin_main = row(jnp.concatenate([b_in[0, :lo], b_in[0, lo + GLA_RANK:]]))
    walr = jnp.pad(w_in0[:, lo:lo + GLA_RANK], ((0, 0), (0, LANES - GLA_RANK))).astype(BF16)
    balr = row(jnp.pad(b_in[0, lo:lo + GLA_RANK], (0, LANES - GLA_RANK)))
    wa2 = jnp.pad(gla_wa2[0], ((0, LANES - GLA_RANK), (0, 0))).astype(BF16)
    wrg = _block_diag_gate_weights(rg_wa[0], rg_wx[0])
    w_route = jnp.pad(jnp.concatenate([w_exp[0], w_grp[0]], axis=1),
                      ((0, 0), (0, LANES - N_EXPERTS - N_GROUPS))).astype(BF16)
    b_route = row(jnp.pad(jnp.concatenate([b_exp[0], b_grp[0]]), (0, LANES - N_EXPERTS - N_GROUPS)))
    pa, pb, wo = p_a[0].astype(BF16), p_b[0].astype(BF16), w_o[0].astype(BF16)
    pre_w = [win_main, bin_main, walr, balr, wa2, row(gla_ba[0])]
    rec_w = [conv_w[0], row(conv_b[0]), wrg, row(rg_ba[0]), row(rg_bx[0]), row(rg_lambda[0])]
    post_w = [row(gla_norm_g[0]), pa, pb, wo, row(ln1_g[0]), row(ln1_b[0]), w_route, b_route]

    mod = _mod_call(jnp.concatenate([c_prompt, c_sample], axis=0), w_mod[0], row(b_mod[0]))
    mod_p = mod[:bp].reshape(bp, 6, D_MODEL)
    mod_s = mod[bp:].reshape(ns, 6, D_MODEL).transpose(1, 0, 2)

    mix_w = pre_w[:5] + [pre_w[5], post_w[0]] + rec_w + post_w[1:]
    x1_p, hf_p, route_p, conv_p, h_p, s_p = _mixer_call(x_prompt, mod_p, mix_w)

    xs = x_sample.reshape(ns, D_MODEL)
    sconv = state_conv[0].transpose(1, 0, 2)
    conv_s, h_s, ya_s, q_s, k_s, v_s, la_s, gg_s, ga_s, gb_s = _s_pre_call(
        xs, mod_s, sconv, state_rglru[0], pre_w + rec_w)
    s_s, o_s = _s_state_call(state_gla[0], q_s, k_s, v_s, la_s)
    x1_s, hf_s, route_s = _s_post_call(xs, mod_s, o_s, gg_s, ga_s, gb_s, ya_s, post_w)

    hf_all = jnp.concatenate([hf_p.reshape(n_p, D_MODEL), hf_s], axis=0)
    route_pf = route_p.reshape(n_p, LANES)
    eid = jnp.concatenate([route_pf[:, :2], route_s[:, :2]], axis=0).astype(I32)
    assert n_tok % LANES == 0
    n_chunks = n_tok // LANES
    eid3 = eid.T.reshape(2, n_chunks, LANES).transpose(1, 0, 2)
    rank3, cnt = _rank_call(eid3)
    rank = rank3.transpose(1, 0, 2).reshape(2, n_tok).T
    counts = cnt[:, 0]
    pcounts = (counts + MOE_TM - 1) // MOE_TM * MOE_TM
    pends = jnp.cumsum(pcounts)
    pstarts = pends - pcounts
    dest = (pstarts[eid] + rank).reshape(-1).astype(I32)
    n_tiles = -(-(2 * n_tok + N_EXPERTS * (MOE_TM - 1)) // MOE_TM)
    tile_start = jnp.arange(n_tiles, dtype=I32) * MOE_TM
    blk_e = jnp.minimum(jnp.sum(pends[None, :] <= tile_start[:, None], axis=1), N_EXPERTS - 1).astype(I32)
    n_used = (pends[-1] // MOE_TM).astype(I32).reshape(1)

    xbuf = _dispatch_call(dest, hf_all, jnp.zeros((n_tiles * MOE_TM, D_MODEL), F32))
    ybuf = _expert_call(blk_e, n_used, xbuf, we_gate[0], we_up[0], we_down[0])
    yg2 = _collect_call(dest, ybuf).reshape(n_tok, 2 * D_MODEL)

    y_p = _final_call(x1_p.reshape(n_p, D_MODEL), yg2[:n_p], route_pf, mod_p[:, 5:6, :], row(ln2_g[0]),
                      row(ln2_b[0]), tp, MIX_TT)
    y_s = _final_call(x1_s, yg2[n_p:], route_s, mod_s[5][None], row(ln2_g[0]), row(ln2_b[0]), ns, ns)

    return (y_p.reshape(bp, tp, D_MODEL), y_s.reshape(ns, 1, D_MODEL),
            conv_p[None], h_p.reshape(1, bp, D_RNN), s_p[None],
            conv_s.transpose(1, 0, 2)[None], h_s[None], s_s[None])
```

```python
import functools

import jax
import jax.numpy as jnp
import numpy as np
from jax import lax
from jax.experimental import pallas as pl
from jax.experimental.pallas import tpu as pltpu

F32 = jnp.float32
BF16 = jnp.bfloat16
I32 = jnp.int32

D_MODEL = 1024
D_RNN = D_MODEL
RG_BLOCKS = 16
RG_BW = D_RNN // RG_BLOCKS
CONV_W = 4
RG_C = 8.0
GLA_HEADS = 4
GLA_DK = D_MODEL // 2
GLA_DV = D_MODEL
GLA_DKH = GLA_DK // GLA_HEADS
GLA_DVH = GLA_DV // GLA_HEADS
GLA_RANK = 16
GLA_TAU = 16.0
GLA_CHUNK = 64
N_GROUPS = 4
EXP_PER_GROUP = 8
N_EXPERTS = N_GROUPS * EXP_PER_GROUP
D_EXPERT = 512
DEPTH = 1
ALPHA = (2.0 * DEPTH) ** 0.25
LN_EPS = 1e-5
RMS_EPS = 1e-6

LANES = 128
SUBLANES = 8
MXU_DIM = 256
VMEM_LIMIT_BYTES = 56 * 1024 * 1024

MIX_TT = 256
MOE_TM = 256
ROW_CHUNK = 128
SAMPLE_SB = 8
RG_GROUP = MXU_DIM // RG_BW
N_RG_TILES = RG_BLOCKS // RG_GROUP
N_MAIN = 2 * D_RNN + 2 * GLA_DK + 2 * GLA_DV + 2 * D_MODEL
ROUTE_G0 = N_EXPERTS

O_RX, O_RY, O_Q, O_K, O_V, O_GG, O_GA, O_GB = 0, 1024, 2048, 2560, 3072, 4096, 5120, 6144


def _ln(x):
    mu = jnp.mean(x, -1, keepdims=True)
    xc = x - mu
    var = jnp.mean(xc * xc, -1, keepdims=True)
    return xc * lax.rsqrt(var + LN_EPS)


def _dot(a, b):
    return jnp.dot(a, b, preferred_element_type=F32)


def _expm1(x):
    u = jnp.exp(x)
    small = (u - 1.0) * x / jnp.log(u)
    return jnp.where(u == 1.0, x, jnp.where(jnp.abs(x) < 0.5, small, u - 1.0))


def _const_spec(shape):
    nd = len(shape)
    return pl.BlockSpec(shape, lambda *_: (0,) * nd, pipeline_mode=pl.Buffered(1))


assert D_MODEL == SUBLANES * LANES


def _store_token_tiles(ref, rows):
    for s in range(SUBLANES):
        ref[:, s, :] = rows[:, s * LANES:(s + 1) * LANES]


def _load_token_tiles(ref):
    return jnp.concatenate([ref[:, s, :] for s in range(SUBLANES)], axis=1)


def _mod_kernel(c_ref, w_ref, b_ref, o_ref):
    c = c_ref[...]
    s = jax.nn.silu(c)
    o_ref[...] = _dot(s.astype(BF16), w_ref[...].astype(BF16)) + b_ref[...]


def _mod_call(c_all, w_mod, b_mod):
    n = c_all.shape[0]
    tn = 512
    return pl.pallas_call(
        _mod_kernel,
        grid=(6 * D_MODEL // tn,),
        in_specs=[pl.BlockSpec((n, D_MODEL), lambda i: (0, 0)),
                  pl.BlockSpec((D_MODEL, tn), lambda i: (0, i)),
                  pl.BlockSpec((1, tn), lambda i: (0, i))],
        out_specs=pl.BlockSpec((n, tn), lambda i: (0, i)),
        out_shape=jax.ShapeDtypeStruct((n, 6 * D_MODEL), F32),
        compiler_params=pltpu.CompilerParams(dimension_semantics=("arbitrary",)),
        name="mod",
    )(c_all, w_mod, b_mod)


def _gates(xc, wrg_ref, rba, rbx, lam):
    xcb = xc.astype(BF16)
    parts = [_dot(xcb[:, c * MXU_DIM:(c + 1) * MXU_DIM], wrg_ref[c]) for c in range(N_RG_TILES)]
    r_pre = jnp.concatenate([p[:, :MXU_DIM] for p in parts], axis=1) + rba
    i_pre = jnp.concatenate([p[:, MXU_DIM:] for p in parts], axis=1) + rbx
    r = jax.nn.sigmoid(r_pre)
    ig = jax.nn.sigmoid(i_pre)
    log_a = -RG_C * r * jax.nn.softplus(-lam)
    a = jnp.exp(log_a)
    u = jnp.sqrt(-_expm1(2.0 * log_a)) * (ig * xc)
    return a, u


def _log_decay(hb, walr_ref, balr_ref, wa2_ref, gba_ref):
    alr = _dot(hb, walr_ref[...]) + balr_ref[...]
    return jax.nn.log_sigmoid(_dot(alr.astype(BF16), wa2_ref[...]) + gba_ref[...]) / GLA_TAU


def _post(x, o, gg, ga, gb, y_a, g1, sh2, sc2, gng, pa_ref, pb_ref, wo_ref, l1g, l1b, wr_ref, br):
    rows = x.shape[0]
    heads = []
    for hh in range(GLA_HEADS):
        oh = o[:, hh * GLA_DVH:(hh + 1) * GLA_DVH]
        ms = jnp.mean(oh * oh, -1, keepdims=True)
        heads.append(oh * lax.rsqrt(ms + RMS_EPS) * gng)
    y_b = jnp.concatenate(heads, axis=1) * jax.nn.silu(gg)
    merged = (jax.nn.sigmoid(ga) * _dot(y_a.astype(BF16), pa_ref[...])
              + jax.nn.sigmoid(gb) * _dot(y_b.astype(BF16), pb_ref[...]))
    mix = _dot(merged.astype(BF16), wo_ref[...])
    x1 = _ln(ALPHA * x + g1 * mix) * l1g + l1b
    hf = _ln(x1) * (1.0 + sc2) + sh2
    logits = _dot(hf.astype(BF16), wr_ref[...]) + br
    lane = lax.broadcasted_iota(I32, (rows, LANES), 1).astype(F32)
    neg = jnp.float32(-jnp.inf)
    big = jnp.float32(LANES)
    g_valid = (lane >= ROUTE_G0) & (lane < ROUTE_G0 + N_GROUPS)
    gl = jnp.where(g_valid, logits, neg)
    gmax = jnp.max(gl, -1, keepdims=True)
    g_lane = jnp.min(jnp.where(gl == gmax, lane, big), -1, keepdims=True)
    g_w = 1.0 / jnp.sum(jnp.exp(gl - gmax), -1, keepdims=True)
    e_lo = (g_lane - ROUTE_G0) * EXP_PER_GROUP
    el = jnp.where((lane >= e_lo) & (lane < e_lo + EXP_PER_GROUP), logits, neg)
    t1 = jnp.max(el, -1, keepdims=True)
    i1 = jnp.min(jnp.where(el == t1, lane, big), -1, keepdims=True)
    el2 = jnp.where(lane == i1, neg, el)
    t2 = jnp.max(el2, -1, keepdims=True)
    i2 = jnp.min(jnp.where(el2 == t2, lane, big), -1, keepdims=True)
    e2 = jnp.exp(t2 - t1)
    den = 1.0 + e2
    w1 = (1.0 / den) * g_w
    w2 = (e2 / den) * g_w
    route = jnp.where(lane == 0, i1.astype(F32),
                      jnp.where(lane == 1, i2.astype(F32),
                                jnp.where(lane == 2, w1, jnp.where(lane == 3, w2, 0.0))))
    return x1, hf, route


def _shift_rows(x, s):
    return pltpu.roll(x, s, 0)


def _mixer_kernel(x_ref, mod_ref, win_ref, bin_ref, walr_ref, balr_ref, wa2_ref, gba_ref, gng_ref,
                  cw_ref, cb_ref, wrg_ref, rba_ref, rbx_ref, lam_ref, pa_ref, pb_ref, wo_ref,
                  l1g_ref, l1b_ref, wr_ref, br_ref,
                  x1_ref, hf_ref, route_ref, convn_ref, hlast_ref, sfin_ref,
                  rxbuf, hcar, st_ref, o_scr):
    j = pl.program_id(1)
    tt = x_ref.shape[1]

    @pl.when(j == 0)
    def _init():
        rxbuf[0:SUBLANES, :] = jnp.zeros((SUBLANES, D_RNN), F32)
        hcar[...] = jnp.zeros_like(hcar)
        st_ref[...] = jnp.zeros_like(st_ref)

    x = x_ref[0]
    mod = mod_ref[0]
    sh1, sc1, g1 = mod[0:1], mod[1:2], mod[2:3]
    sh2, sc2 = mod[3:4], mod[4:5]
    hb = (_ln(x) * (1.0 + sc1) + sh1).astype(BF16)

    def proj(lo, hi):
        return _dot(hb, win_ref[:, lo:hi]) + bin_ref[:, lo:hi]

    rx = proj(O_RX, O_RX + D_RNN)
    rxbuf[SUBLANES:SUBLANES + tt, :] = rx
    cw = cw_ref[...]
    xc = cb_ref[...] + rxbuf[SUBLANES - 3:SUBLANES - 3 + tt, :] * cw[0:1]
    xc = xc + rxbuf[SUBLANES - 2:SUBLANES - 2 + tt, :] * cw[1:2]
    xc = xc + rxbuf[SUBLANES - 1:SUBLANES - 1 + tt, :] * cw[2:3]
    xc = xc + rx * cw[3:4]
    rxbuf[0:SUBLANES, :] = rxbuf[tt:tt + SUBLANES, :]

    a, u = _gates(xc, wrg_ref, rba_ref[...], rbx_ref[...], lam_ref[...])
    row = lax.broadcasted_iota(I32, (tt, D_RNN), 0)
    s = 1
    while s < tt:
        keep = row >= s
        a_s = jnp.where(keep, _shift_rows(a, s), 1.0)
        u_s = jnp.where(keep, _shift_rows(u, s), 0.0)
        u = a * u_s + u
        a = a * a_s
        s *= 2
    hseq = a * hcar[0:1, :] + u
    hcar[0:1, :] = hseq[tt - 1:tt, :]
    y_a = hseq * jax.nn.gelu(proj(O_RY, O_RY + D_RNN))

    q = proj(O_Q, O_Q + GLA_DK) * (GLA_DKH ** -0.5)
    k = proj(O_K, O_K + GLA_DK)
    v = proj(O_V, O_V + GLA_DV)
    la = _log_decay(hb, walr_ref, balr_ref, wa2_ref, gba_ref)
    rowk = lax.broadcasted_iota(I32, (tt, GLA_DK), 0) % GLA_CHUNK
    bcum = la
    s = 1
    while s < GLA_CHUNK:
        bcum = bcum + jnp.where(rowk >= s, _shift_rows(bcum, s), 0.0)
        s *= 2
    eb = jnp.exp(bcum)
    q_in = (q * eb).astype(BF16)
    k_in = (k * jnp.exp(-bcum)).astype(BF16)
    tri = (lax.broadcasted_iota(I32, (GLA_CHUNK, GLA_CHUNK), 0)
           >= lax.broadcasted_iota(I32, (GLA_CHUNK, GLA_CHUNK), 1))
    for c in range(tt // GLA_CHUNK):
        r0 = c * GLA_CHUNK
        for hh in range(GLA_HEADS):
            kc = slice(hh * GLA_DKH, (hh + 1) * GLA_DKH)
            vc = slice(hh * GLA_DVH, (hh + 1) * GLA_DVH)
            b = bcum[r0:r0 + GLA_CHUNK, kc]
            btot = b[GLA_CHUNK - 1:GLA_CHUNK, :]
            qi = q_in[r0:r0 + GLA_CHUNK, kc]
            ki = k_in[r0:r0 + GLA_CHUNK, kc]
            k_out = (k[r0:r0 + GLA_CHUNK, kc] * jnp.exp(btot - b)).astype(BF16)
            vh = v[r0:r0 + GLA_CHUNK, vc].astype(BF16)
            attn = lax.dot_general(qi, ki, (((1,), (1,)), ((), ())), preferred_element_type=F32)
            attn = jnp.where(tri, attn, 0.0)
            st = st_ref[hh]
            inter = lax.dot_general(qi, st.astype(BF16), (((1,), (1,)), ((), ())), preferred_element_type=F32)
            o_scr[r0:r0 + GLA_CHUNK, vc] = _dot(attn.astype(BF16), vh) + inter
            d_st = lax.dot_general(vh, k_out, (((0,), (0,)), ((), ())), preferred_element_type=F32)
            st_ref[hh] = st * jnp.exp(btot) + d_st

    x1, hf, route = _post(x, o_scr[...], proj(O_GG, O_GG + GLA_DV), proj(O_GA, O_GA + D_MODEL),
                          proj(O_GB, O_GB + D_MODEL), y_a, g1, sh2, sc2, gng_ref[...],
                          pa_ref, pb_ref, wo_ref, l1g_ref[...], l1b_ref[...], wr_ref, br_ref[...])
    x1_ref[0] = x1
    _store_token_tiles(hf_ref.at[0], hf)
    route_ref[0] = route

    @pl.when(j == pl.num_programs(1) - 1)
    def _final():
        convn_ref[0] = rxbuf[SUBLANES - (CONV_W - 1):SUBLANES, :]
        hlast_ref[0] = hcar[0:1, :]
        for hh in range(GLA_HEADS):
            sfin_ref[0, hh] = st_ref[hh].T


def _mixer_call(x, mod3, wts):
    b, t, _ = x.shape
    tt = MIX_TT
    assert t % tt == 0 and tt % GLA_CHUNK == 0
    tok = lambda bi, j: (bi, j, 0)
    per_b = lambda bi, j: (bi, 0, 0)
    in_specs = [pl.BlockSpec((1, tt, D_MODEL), tok),
                pl.BlockSpec((1, 6, D_MODEL), per_b)] + [_const_spec(w.shape) for w in wts]
    out_specs = [pl.BlockSpec((1, tt, D_MODEL), tok),
                 pl.BlockSpec((1, tt, SUBLANES, LANES), lambda bi, j: (bi, j, 0, 0)),
                 pl.BlockSpec((1, tt, LANES), tok),
                 pl.BlockSpec((1, CONV_W - 1, D_RNN), per_b),
                 pl.BlockSpec((1, 1, D_RNN), per_b),
                 pl.BlockSpec((1, GLA_HEADS, GLA_DKH, GLA_DVH), lambda bi, j: (bi, 0, 0, 0))]
    out_shape = [jax.ShapeDtypeStruct((b, t, D_MODEL), F32),
                 jax.ShapeDtypeStruct((b, t, SUBLANES, LANES), F32),
                 jax.ShapeDtypeStruct((b, t, LANES), F32),
                 jax.ShapeDtypeStruct((b, CONV_W - 1, D_RNN), F32),
                 jax.ShapeDtypeStruct((b, 1, D_RNN), F32),
                 jax.ShapeDtypeStruct((b, GLA_HEADS, GLA_DKH, GLA_DVH), F32)]
    scratch = [pltpu.VMEM((tt + SUBLANES, D_RNN), F32),
               pltpu.VMEM((SUBLANES, D_RNN), F32),
               pltpu.VMEM((GLA_HEADS, GLA_DVH, GLA_DKH), F32),
               pltpu.VMEM((tt, GLA_DV), F32)]
    return pl.pallas_call(
        _mixer_kernel,
        grid=(b, t // tt),
        in_specs=in_specs,
        out_specs=out_specs,
        out_shape=out_shape,
        scratch_shapes=scratch,
        compiler_params=pltpu.CompilerParams(dimension_semantics=("arbitrary", "arbitrary"),
                                             vmem_limit_bytes=VMEM_LIMIT_BYTES),
        name="mixer",
    )(x, mod3, *wts)


def _s_pre_kernel(x_ref, mod_ref, sconv_ref, h0_ref, win_ref, bin_ref, walr_ref, balr_ref, wa2_ref, gba_ref,
                  cw_ref, cb_ref, wrg_ref, rba_ref, rbx_ref, lam_ref,
                  convn_ref, hnew_ref, ya_ref, q_ref, k_ref, v_ref, la_ref, gg_ref, ga_ref, gb_ref):
    x = x_ref[...]
    hb = (_ln(x) * (1.0 + mod_ref[1]) + mod_ref[0]).astype(BF16)

    def proj(lo, hi):
        return _dot(hb, win_ref[:, lo:hi]) + bin_ref[:, lo:hi]

    rx = proj(O_RX, O_RX + D_RNN)
    cw = cw_ref[...]
    xc = cb_ref[...] + sconv_ref[0] * cw[0:1]
    xc = xc + sconv_ref[1] * cw[1:2]
    xc = xc + sconv_ref[2] * cw[2:3]
    xc = xc + rx * cw[3:4]
    convn_ref[0] = sconv_ref[1]
    convn_ref[1] = sconv_ref[2]
    convn_ref[2] = rx
    a, u = _gates(xc, wrg_ref, rba_ref[...], rbx_ref[...], lam_ref[...])
    hnew = u + a * h0_ref[...]
    hnew_ref[...] = hnew
    ya_ref[...] = hnew * jax.nn.gelu(proj(O_RY, O_RY + D_RNN))
    q_ref[...] = proj(O_Q, O_Q + GLA_DK) * (GLA_DKH ** -0.5)
    k_ref[...] = proj(O_K, O_K + GLA_DK)
    v_ref[...] = proj(O_V, O_V + GLA_DV)
    la_ref[...] = _log_decay(hb, walr_ref, balr_ref, wa2_ref, gba_ref)
    gg_ref[...] = proj(O_GG, O_GG + GLA_DV)
    ga_ref[...] = proj(O_GA, O_GA + D_MODEL)
    gb_ref[...] = proj(O_GB, O_GB + D_MODEL)


def _s_pre_call(x_s, mod_s, sconv, h0, wts):
    n = x_s.shape[0]
    full = lambda shape: pl.BlockSpec(shape, lambda i: (0,) * len(shape))
    ins = [x_s, mod_s, sconv, h0] + list(wts)
    shapes = [(CONV_W - 1, n, D_RNN), (n, D_RNN), (n, D_RNN), (n, GLA_DK), (n, GLA_DK), (n, GLA_DV),
              (n, GLA_DK), (n, GLA_DV), (n, D_MODEL), (n, D_MODEL)]
    return pl.pallas_call(
        _s_pre_kernel,
        grid=(1,),
        in_specs=[full(a.shape) for a in ins],
        out_specs=[full(s) for s in shapes],
        out_shape=[jax.ShapeDtypeStruct(s, F32) for s in shapes],
        compiler_params=pltpu.CompilerParams(dimension_semantics=("arbitrary",),
                                             vmem_limit_bytes=VMEM_LIMIT_BYTES),
        name="s_pre",
    )(*ins)


def _to_column(row, n):
    eye = lax.broadcasted_iota(I32, (n, n), 0) == lax.broadcasted_iota(I32, (n, n), 1)
    return jnp.sum(jnp.where(eye, jnp.broadcast_to(row, (n, n)), 0.0), axis=1, keepdims=True)


def _s_state_kernel(s_ref, q_ref, k_ref, v_ref, la_ref, snew_ref, o_ref):
    sb = s_ref.shape[0]
    qb = q_ref[...].astype(BF16)
    dec = jnp.exp(la_ref[...])
    kk = k_ref[...]
    vv = v_ref[...]
    for si in range(sb):
        for hh in range(GLA_HEADS):
            kc = slice(hh * GLA_DKH, (hh + 1) * GLA_DKH)
            vc = slice(hh * GLA_DVH, (hh + 1) * GLA_DVH)
            d_col = _to_column(dec[si:si + 1, kc], GLA_DKH)
            k_col = _to_column(kk[si:si + 1, kc], GLA_DKH)
            s_new = s_ref[si, hh] * d_col + k_col * vv[si:si + 1, vc]
            snew_ref[si, hh] = s_new
            o_all = _dot(qb[:, kc], s_new.astype(BF16))
            o_ref[si:si + 1, vc] = o_all[si:si + 1, :]


def _s_state_call(state, q, k, v, la):
    n = state.shape[0]
    sb = SAMPLE_SB
    assert n % sb == 0
    blk = lambda w: pl.BlockSpec((sb, w), lambda i: (i, 0))
    st_spec = pl.BlockSpec((sb, GLA_HEADS, GLA_DKH, GLA_DVH), lambda i: (i, 0, 0, 0))
    return pl.pallas_call(
        _s_state_kernel,
        grid=(n // sb,),
        in_specs=[st_spec, blk(GLA_DK), blk(GLA_DK), blk(GLA_DV), blk(GLA_DK)],
        out_specs=[st_spec, blk(GLA_DV)],
        out_shape=[jax.ShapeDtypeStruct(state.shape, F32), jax.ShapeDtypeStruct((n, GLA_DV), F32)],
        compiler_params=pltpu.CompilerParams(dimension_semantics=("arbitrary",),
                                             vmem_limit_bytes=VMEM_LIMIT_BYTES),
        name="s_state",
    )(state, q, k, v, la)


def _s_post_kernel(x_ref, mod_ref, o_ref, gg_ref, ga_ref, gb_ref, ya_ref, gng_ref, pa_ref, pb_ref, wo_ref,
                   l1g_ref, l1b_ref, wr_ref, br_ref, x1_ref, hf_ref, route_ref):
    x1, hf, route = _post(x_ref[...], o_ref[...], gg_ref[...], ga_ref[...], gb_ref[...], ya_ref[...],
                          mod_ref[2], mod_ref[3], mod_ref[4], gng_ref[...], pa_ref, pb_ref, wo_ref,
                          l1g_ref[...], l1b_ref[...], wr_ref, br_ref[...])
    x1_ref[...] = x1
    _store_token_tiles(hf_ref, hf)
    route_ref[...] = route


def _s_post_call(x_s, mod_s, o, gg, ga, gb, ya, wts):
    n = x_s.shape[0]
    full = lambda shape: pl.BlockSpec(shape, lambda i: (0,) * len(shape))
    ins = [x_s, mod_s, o, gg, ga, gb, ya] + list(wts)
    shapes = [(n, D_MODEL), (n, SUBLANES, LANES), (n, LANES)]
    return pl.pallas_call(
        _s_post_kernel,
        grid=(1,),
        in_specs=[full(a.shape) for a in ins],
        out_specs=[full(s) for s in shapes],
        out_shape=[jax.ShapeDtypeStruct(s, F32) for s in shapes],
        compiler_params=pltpu.CompilerParams(dimension_semantics=("arbitrary",),
                                             vmem_limit_bytes=VMEM_LIMIT_BYTES),
        name="s_post",
    )(*ins)


def _rank_kernel(eid_ref, rank_ref, cnt_ref, carry):
    n_chunks = eid_ref.shape[0]
    carry[...] = jnp.zeros_like(carry)
    e_iota = lax.broadcasted_iota(I32, (N_EXPERTS, LANES), 0)
    upper = (lax.broadcasted_iota(I32, (LANES, LANES), 0)
             < lax.broadcasted_iota(I32, (LANES, LANES), 1)).astype(BF16)

    def body(c, _):
        ids = eid_ref[c]
        oh0 = (e_iota == ids[0:1, :])
        oh1 = (e_iota == ids[1:2, :])
        both = (oh0 | oh1).astype(F32)
        before = _dot(both.astype(BF16), upper) + carry[:, 0:1]
        r0 = jnp.sum(jnp.where(oh0, before, 0.0), axis=0, keepdims=True)
        r1 = jnp.sum(jnp.where(oh1, before, 0.0), axis=0, keepdims=True)
        rank_ref[c] = jnp.concatenate([r0, r1], axis=0).astype(I32)
        carry[...] = carry[...] + jnp.sum(both, axis=1, keepdims=True)
        return 0

    lax.fori_loop(0, n_chunks, body, 0)
    cnt_ref[...] = carry[...].astype(I32)


def _rank_call(eid3):
    n_chunks = eid3.shape[0]
    full = lambda shape: pl.BlockSpec(shape, lambda i: (0,) * len(shape))
    return pl.pallas_call(
        _rank_kernel,
        grid=(1,),
        in_specs=[full(eid3.shape)],
        out_specs=[full(eid3.shape), full((N_EXPERTS, LANES))],
        out_shape=[jax.ShapeDtypeStruct(eid3.shape, I32), jax.ShapeDtypeStruct((N_EXPERTS, LANES), I32)],
        scratch_shapes=[pltpu.VMEM((N_EXPERTS, LANES), F32)],
        compiler_params=pltpu.CompilerParams(dimension_semantics=("arbitrary",)),
        name="rank",
    )(eid3)


def _dispatch_kernel(pends_ref, dest_ref, src_p_ref, src_s_ref, buf_ref, sem, zeros_ref, zsem, *, n_p_chunks):
    i = pl.program_id(0)
    n_tiles = buf_ref.shape[0] // MOE_TM

    @pl.when(i == 0)
    def _clear():
        zeros_ref[...] = jnp.zeros_like(zeros_ref)

        def tile_copy(row0):
            return pltpu.make_async_copy(zeros_ref, buf_ref.at[pl.ds(row0, MOE_TM)], zsem)

        def nonempty(e):
            return pends_ref[e] > jnp.where(e == 0, 0, pends_ref[jnp.maximum(e - 1, 0)])

        def start_tail(e, _):
            @pl.when(nonempty(e))
            def _():
                tile_copy(pends_ref[e] - MOE_TM).start()
            return 0

        def wait_tail(e, _):
            @pl.when(nonempty(e))
            def _():
                tile_copy(0).wait()
            return 0

        def start_unused(j, _):
            tile_copy(j * MOE_TM).start()
            return 0

        def wait_unused(j, _):
            tile_copy(0).wait()
            return 0

        first_unused = pends_ref[N_EXPERTS - 1] // MOE_TM
        lax.fori_loop(0, N_EXPERTS, start_tail, 0)
        lax.fori_loop(first_unused, n_tiles, start_unused, 0)
        lax.fori_loop(0, N_EXPERTS, wait_tail, 0)
        lax.fori_loop(first_unused, n_tiles, wait_unused, 0)

    def scatter_rows(src_ref):
        n = src_ref.shape[0]

        def start_rows(t, _):
            for k in range(2):
                pltpu.make_async_copy(src_ref.at[pl.ds(t, 1)], buf_ref.at[pl.ds(dest_ref[2 * t + k], 1)],
                                      sem).start()
            return 0

        lax.fori_loop(0, n, start_rows, 0, unroll=4)
        for _ in range(2):
            pltpu.make_async_copy(src_ref, buf_ref.at[pl.ds(0, n)], sem).wait()

    @pl.when(i < n_p_chunks)
    def _prompt():
        scatter_rows(src_p_ref)

    @pl.when(i >= n_p_chunks)
    def _sample():
        scatter_rows(src_s_ref)


def _dispatch_call(pends, dest_flat, src_p, src_s, n_rows):
    assert src_p.shape[0] % ROW_CHUNK == 0 and src_s.shape[0] % ROW_CHUNK == 0 and n_rows % MOE_TM == 0
    n_p_chunks = src_p.shape[0] // ROW_CHUNK
    n_s_chunks = src_s.shape[0] // ROW_CHUNK
    tile = (ROW_CHUNK, SUBLANES, LANES)
    in_specs = [pl.BlockSpec((2 * ROW_CHUNK,), lambda i, pe: (i,), memory_space=pltpu.SMEM),
                pl.BlockSpec(tile, lambda i, pe: (jnp.minimum(i, n_p_chunks - 1), 0, 0)),
                pl.BlockSpec(tile, lambda i, pe: (jnp.maximum(i - n_p_chunks, 0), 0, 0))]
    scratch = [pltpu.SemaphoreType.DMA(()), pltpu.VMEM((MOE_TM, SUBLANES, LANES), F32),
               pltpu.SemaphoreType.DMA(())]
    return pl.pallas_call(
        functools.partial(_dispatch_kernel, n_p_chunks=n_p_chunks),
        grid_spec=pltpu.PrefetchScalarGridSpec(
            num_scalar_prefetch=1, grid=(n_p_chunks + n_s_chunks,), in_specs=in_specs,
            out_specs=pl.BlockSpec(memory_space=pl.ANY), scratch_shapes=scratch),
        out_shape=jax.ShapeDtypeStruct((n_rows, SUBLANES, LANES), F32),
        compiler_params=pltpu.CompilerParams(dimension_semantics=("arbitrary",), has_side_effects=True),
        name="dispatch",
    )(pends, dest_flat, src_p, src_s)


def _expert_kernel(blk_e_ref, n_used_ref, x_ref, wg_ref, wu_ref, wd_ref, y_ref):
    del blk_e_ref
    i = pl.program_id(0)

    @pl.when(i < n_used_ref[0])
    def _run():
        xb = _load_token_tiles(x_ref).astype(BF16)
        g = _dot(xb, wg_ref[0].astype(BF16))
        u = _dot(xb, wu_ref[0].astype(BF16))
        act = (jax.nn.silu(g) * u).astype(BF16)
        _store_token_tiles(y_ref, _dot(act, wd_ref[0].astype(BF16)))

    @pl.when(i >= n_used_ref[0])
    def _skip():
        y_ref[...] = jnp.zeros_like(y_ref)


def _expert_call(blk_e, n_used, xbuf, we_gate, we_up, we_down):
    p = xbuf.shape[0]
    assert p % MOE_TM == 0
    tile_spec = pl.BlockSpec((MOE_TM, SUBLANES, LANES), lambda i, be, nu: (i, 0, 0))
    grid_spec = pltpu.PrefetchScalarGridSpec(
        num_scalar_prefetch=2,
        grid=(p // MOE_TM,),
        in_specs=[tile_spec,
                  pl.BlockSpec((1, D_MODEL, D_EXPERT), lambda i, be, nu: (be[i], 0, 0)),
                  pl.BlockSpec((1, D_MODEL, D_EXPERT), lambda i, be, nu: (be[i], 0, 0)),
                  pl.BlockSpec((1, D_EXPERT, D_MODEL), lambda i, be, nu: (be[i], 0, 0))],
        out_specs=tile_spec,
    )
    return pl.pallas_call(
        _expert_kernel,
        grid_spec=grid_spec,
        out_shape=jax.ShapeDtypeStruct((p, SUBLANES, LANES), F32),
        compiler_params=pltpu.CompilerParams(dimension_semantics=("arbitrary",),
                                             vmem_limit_bytes=VMEM_LIMIT_BYTES),
        name="experts",
    )(blk_e, n_used, xbuf, we_gate, we_up, we_down)


def _final_kernel(dest_ref, x1_ref, ybuf_ref, route_ref, g2_ref, l2g_ref, l2b_ref, out_ref, yg0, yg1, sem):
    n = x1_ref.shape[0]

    def start_rows(t, _):
        pltpu.make_async_copy(ybuf_ref.at[pl.ds(dest_ref[2 * t], 1)], yg0.at[pl.ds(t, 1)], sem).start()
        pltpu.make_async_copy(ybuf_ref.at[pl.ds(dest_ref[2 * t + 1], 1)], yg1.at[pl.ds(t, 1)], sem).start()
        return 0

    lax.fori_loop(0, n, start_rows, 0, unroll=4)
    pltpu.make_async_copy(ybuf_ref.at[pl.ds(0, n)], yg0, sem).wait()
    pltpu.make_async_copy(ybuf_ref.at[pl.ds(0, n)], yg1, sem).wait()
    route = route_ref[...]
    ff = _load_token_tiles(yg0) * route[:, 2:3] + _load_token_tiles(yg1) * route[:, 3:4]
    out_ref[...] = _ln(ALPHA * x1_ref[...] + g2_ref[0] * ff) * l2g_ref[...] + l2b_ref[...]


def _final_call(dest_flat, x1, ybuf, route, g2, ln2_g, ln2_b, rows_per_g2, tile):
    n = x1.shape[0]
    assert n % tile == 0 and rows_per_g2 % tile == 0
    per = rows_per_g2 // tile
    g2_rows = g2.shape[1]
    return pl.pallas_call(
        _final_kernel,
        grid=(n // tile,),
        in_specs=[pl.BlockSpec((2 * tile,), lambda i: (i,), memory_space=pltpu.SMEM),
                  pl.BlockSpec((tile, D_MODEL), lambda i: (i, 0)),
                  pl.BlockSpec(memory_space=pl.ANY),
                  pl.BlockSpec((tile, LANES), lambda i: (i, 0)),
                  pl.BlockSpec((1, g2_rows, D_MODEL), lambda i: (i // per, 0, 0)),
                  pl.BlockSpec((1, D_MODEL), lambda i: (0, 0)),
                  pl.BlockSpec((1, D_MODEL), lambda i: (0, 0))],
        out_specs=pl.BlockSpec((tile, D_MODEL), lambda i: (i, 0)),
        out_shape=jax.ShapeDtypeStruct((n, D_MODEL), F32),
        scratch_shapes=[pltpu.VMEM((tile, SUBLANES, LANES), F32), pltpu.VMEM((tile, SUBLANES, LANES), F32),
                        pltpu.SemaphoreType.DMA(())],
        compiler_params=pltpu.CompilerParams(dimension_semantics=("arbitrary",)),
        name="final",
    )(dest_flat, x1, ybuf, route, g2, ln2_g, ln2_b)


def _block_diag_gate_weights(wa, wx):
    def bd(w):
        w = w.reshape(N_RG_TILES, RG_GROUP, RG_BW, RG_BW)
        eye = jnp.eye(RG_GROUP, dtype=w.dtype)
        return jnp.einsum('tgcd,gh->tgchd', w, eye).reshape(N_RG_TILES, MXU_DIM, MXU_DIM)
    return jnp.concatenate([bd(wa), bd(wx)], axis=2).astype(BF16)


def kernel(x_prompt, x_sample, state_conv, state_rglru, state_gla, c_prompt, c_sample, w_mod, b_mod, w_in, b_in,
           conv_w, conv_b, rg_wa, rg_ba, rg_wx, rg_bx, rg_lambda, gla_wa2, gla_ba, gla_norm_g, p_a, p_b, w_o,
           ln1_g, ln1_b, w_grp, b_grp, w_exp, b_exp, we_gate, we_up, we_down, ln2_g, ln2_b):
    assert w_mod.shape[0] == DEPTH == 1
    bp, tp, _ = x_prompt.shape
    ns = x_sample.shape[0]
    n_p = bp * tp
    n_tok = n_p + ns
    row = lambda a: a.reshape(1, -1)

    lo = 2 * D_RNN + 2 * GLA_DK + 2 * GLA_DV
    w_in0 = w_in[0]
    win_main = jnp.concatenate([w_in0[:, :lo], w_in0[:, lo + GLA_RANK:]], axis=1).astype(BF16)
    bin_main = row(jnp.concatenate([b_in[0, :lo], b_in[0, lo + GLA_RANK:]]))
    walr = jnp.pad(w_in0[:, lo:lo + GLA_RANK], ((0, 0), (0, LANES - GLA_RANK))).astype(BF16)
    balr = row(jnp.pad(b_in[0, lo:lo + GLA_RANK], (0, LANES - GLA_RANK)))
    wa2 = jnp.pad(gla_wa2[0], ((0, LANES - GLA_RANK), (0, 0))).astype(BF16)
    wrg = _block_diag_gate_weights(rg_wa[0], rg_wx[0])
    w_route = jnp.pad(jnp.concatenate([w_exp[0], w_grp[0]], axis=1),
                      ((0, 0), (0, LANES - N_EXPERTS - N_GROUPS))).astype(BF16)
    b_route = row(jnp.pad(jnp.concatenate([b_exp[0], b_grp[0]]), (0, LANES - N_EXPERTS - N_GROUPS)))
    pa, pb, wo = p_a[0].astype(BF16), p_b[0].astype(BF16), w_o[0].astype(BF16)
    pre_w = [win_main, bin_main, walr, balr, wa2, row(gla_ba[0])]
    rec_w = [conv_w[0], row(conv_b[0]), wrg, row(rg_ba[0]), row(rg_bx[0]), row(rg_lambda[0])]
    post_w = [row(gla_norm_g[0]), pa, pb, wo, row(ln1_g[0]), row(ln1_b[0]), w_route, b_route]

    mod = _mod_call(jnp.concatenate([c_prompt, c_sample], axis=0), w_mod[0], row(b_mod[0]))
    mod_p = mod[:bp].reshape(bp, 6, D_MODEL)
    mod_s = mod[bp:].reshape(ns, 6, D_MODEL).transpose(1, 0, 2)

    mix_w = pre_w[:5] + [pre_w[5], post_w[0]] + rec_w + post_w[1:]
    x1_p, hf_p, route_p, conv_p, h_p, s_p = _mixer_call(x_prompt, mod_p, mix_w)

    xs = x_sample.reshape(ns, D_MODEL)
    sconv = state_conv[0].transpose(1, 0, 2)
    conv_s, h_s, ya_s, q_s, k_s, v_s, la_s, gg_s, ga_s, gb_s = _s_pre_call(
        xs, mod_s, sconv, state_rglru[0], pre_w + rec_w)
    s_s, o_s = _s_state_call(state_gla[0], q_s, k_s, v_s, la_s)
    x1_s, hf_s, route_s = _s_post_call(xs, mod_s, o_s, gg_s, ga_s, gb_s, ya_s, post_w)

    route_pf = route_p.reshape(n_p, LANES)
    eid = jnp.concatenate([route_pf[:, :2], route_s[:, :2]], axis=0).astype(I32)
    assert n_tok % LANES == 0
    n_chunks = n_tok // LANES
    eid3 = eid.T.reshape(2, n_chunks, LANES).transpose(1, 0, 2)
    rank3, cnt = _rank_call(eid3)
    rank = rank3.transpose(1, 0, 2).reshape(2, n_tok).T
    counts = cnt[:, 0]
    pcounts = (counts + MOE_TM - 1) // MOE_TM * MOE_TM
    pends = jnp.cumsum(pcounts)
    pstarts = pends - pcounts
    dest = (pstarts[eid] + rank).reshape(-1).astype(I32)
    n_tiles = -(-(2 * n_tok + N_EXPERTS * (MOE_TM - 1)) // MOE_TM)
    tile_start = jnp.arange(n_tiles, dtype=I32) * MOE_TM
    blk_e = jnp.minimum(jnp.sum(pends[None, :] <= tile_start[:, None], axis=1), N_EXPERTS - 1).astype(I32)
    n_used = (pends[-1] // MOE_TM).astype(I32).reshape(1)

    pends32 = pends.astype(I32)
    xbuf = _dispatch_call(pends32, dest, hf_p.reshape(n_p, SUBLANES, LANES), hf_s, n_tiles * MOE_TM)
    ybuf = _expert_call(blk_e, n_used, xbuf, we_gate[0], we_up[0], we_down[0])

    y_p = _final_call(dest[:2 * n_p], x1_p.reshape(n_p, D_MODEL), ybuf, route_pf, mod_p[:, 5:6, :],
                      row(ln2_g[0]), row(ln2_b[0]), tp, ROW_CHUNK)
    y_s = _final_call(dest[2 * n_p:], x1_s, ybuf, route_s, mod_s[5][None], row(ln2_g[0]), row(ln2_b[0]), ns, ns)

    return (y_p.reshape(bp, tp, D_MODEL), y_s.reshape(ns, 1, D_MODEL),
            conv_p[None], h_p.reshape(1, bp, D_RNN), s_p[None],
            conv_s.transpose(1, 0, 2)[None], h_s[None], s_s[None])
```

```python
import functools

import jax
import jax.numpy as jnp
import numpy as np
from jax import lax
from jax.experimental import pallas as pl
from jax.experimental.pallas import tpu as pltpu

F32 = jnp.float32
BF16 = jnp.bfloat16
I32 = jnp.int32

D_MODEL = 1024
D_RNN = D_MODEL
RG_BLOCKS = 16
RG_BW = D_RNN // RG_BLOCKS
CONV_W = 4
RG_C = 8.0
GLA_HEADS = 4
GLA_DK = D_MODEL // 2
GLA_DV = D_MODEL
GLA_DKH = GLA_DK // GLA_HEADS
GLA_DVH = GLA_DV // GLA_HEADS
GLA_RANK = 16
GLA_TAU = 16.0
GLA_CHUNK = 64
N_GROUPS = 4
EXP_PER_GROUP = 8
N_EXPERTS = N_GROUPS * EXP_PER_GROUP
D_EXPERT = 512
DEPTH = 1
ALPHA = (2.0 * DEPTH) ** 0.25
LN_EPS = 1e-5
RMS_EPS = 1e-6

LANES = 128
SUBLANES = 8
MXU_DIM = 256
VMEM_LIMIT_BYTES = 56 * 1024 * 1024

MIX_TT = 256
MOE_TM = 256
ROW_CHUNK = 128
SAMPLE_SB = 8
RG_GROUP = MXU_DIM // RG_BW
N_RG_TILES = RG_BLOCKS // RG_GROUP
N_MAIN = 2 * D_RNN + 2 * GLA_DK + 2 * GLA_DV + 2 * D_MODEL
ROUTE_G0 = N_EXPERTS

O_RX, O_RY, O_Q, O_K, O_V, O_GG, O_GA, O_GB = 0, 1024, 2048, 2560, 3072, 4096, 5120, 6144


def _ln(x):
    mu = jnp.mean(x, -1, keepdims=True)
    xc = x - mu
    var = jnp.mean(xc * xc, -1, keepdims=True)
    return xc * lax.rsqrt(var + LN_EPS)


def _dot(a, b):
    return jnp.dot(a, b, preferred_element_type=F32)


def _expm1(x):
    u = jnp.exp(x)
    small = (u - 1.0) * x / jnp.log(u)
    return jnp.where(u == 1.0, x, jnp.where(jnp.abs(x) < 0.5, small, u - 1.0))


def _const_spec(shape):
    nd = len(shape)
    return pl.BlockSpec(shape, lambda *_: (0,) * nd, pipeline_mode=pl.Buffered(1))


assert D_MODEL == SUBLANES * LANES


def _store_token_tiles(ref, rows):
    for s in range(SUBLANES):
        ref[:, s, :] = rows[:, s * LANES:(s + 1) * LANES]


def _load_token_tiles(ref):
    return jnp.concatenate([ref[:, s, :] for s in range(SUBLANES)], axis=1)


def _mod_kernel(c_ref, w_ref, b_ref, o_ref):
    c = c_ref[...]
    s = jax.nn.silu(c)
    o_ref[...] = _dot(s.astype(BF16), w_ref[...].astype(BF16)) + b_ref[...]


def _mod_call(c_all, w_mod, b_mod):
    n = c_all.shape[0]
    tn = 512
    return pl.pallas_call(
        _mod_kernel,
        grid=(6 * D_MODEL // tn,),
        in_specs=[pl.BlockSpec((n, D_MODEL), lambda i: (0, 0)),
                  pl.BlockSpec((D_MODEL, tn), lambda i: (0, i)),
                  pl.BlockSpec((1, tn), lambda i: (0, i))],
        out_specs=pl.BlockSpec((n, tn), lambda i: (0, i)),
        out_shape=jax.ShapeDtypeStruct((n, 6 * D_MODEL), F32),
        compiler_params=pltpu.CompilerParams(dimension_semantics=("arbitrary",)),
        name="mod",
    )(c_all, w_mod, b_mod)


def _gates(xc, wrg_ref, rba, rbx, lam):
    xcb = xc.astype(BF16)
    parts = [_dot(xcb[:, c * MXU_DIM:(c + 1) * MXU_DIM], wrg_ref[c]) for c in range(N_RG_TILES)]
    r_pre = jnp.concatenate([p[:, :MXU_DIM] for p in parts], axis=1) + rba
    i_pre = jnp.concatenate([p[:, MXU_DIM:] for p in parts], axis=1) + rbx
    r = jax.nn.sigmoid(r_pre)
    ig = jax.nn.sigmoid(i_pre)
    log_a = -RG_C * r * jax.nn.softplus(-lam)
    a = jnp.exp(log_a)
    u = jnp.sqrt(-_expm1(2.0 * log_a)) * (ig * xc)
    return a, u


def _log_decay(hb, walr_ref, balr_ref, wa2_ref, gba_ref):
    alr = _dot(hb, walr_ref[...]) + balr_ref[...]
    return jax.nn.log_sigmoid(_dot(alr.astype(BF16), wa2_ref[...]) + gba_ref[...]) / GLA_TAU


def _post(x, o, gg, ga, gb, y_a, g1, sh2, sc2, gng, pa_ref, pb_ref, wo_ref, l1g, l1b, wr_ref, br):
    rows = x.shape[0]
    heads = []
    for hh in range(GLA_HEADS):
        oh = o[:, hh * GLA_DVH:(hh + 1) * GLA_DVH]
        ms = jnp.mean(oh * oh, -1, keepdims=True)
        heads.append(oh * lax.rsqrt(ms + RMS_EPS) * gng)
    y_b = jnp.concatenate(heads, axis=1) * jax.nn.silu(gg)
    merged = (jax.nn.sigmoid(ga) * _dot(y_a.astype(BF16), pa_ref[...])
              + jax.nn.sigmoid(gb) * _dot(y_b.astype(BF16), pb_ref[...]))
    mix = _dot(merged.astype(BF16), wo_ref[...])
    x1 = _ln(ALPHA * x + g1 * mix) * l1g + l1b
    hf = _ln(x1) * (1.0 + sc2) + sh2
    logits = _dot(hf.astype(BF16), wr_ref[...]) + br
    lane = lax.broadcasted_iota(I32, (rows, LANES), 1).astype(F32)
    neg = jnp.float32(-jnp.inf)
    big = jnp.float32(LANES)
    g_valid = (lane >= ROUTE_G0) & (lane < ROUTE_G0 + N_GROUPS)
    gl = jnp.where(g_valid, logits, neg)
    gmax = jnp.max(gl, -1, keepdims=True)
    g_lane = jnp.min(jnp.where(gl == gmax, lane, big), -1, keepdims=True)
    g_w = 1.0 / jnp.sum(jnp.exp(gl - gmax), -1, keepdims=True)
    e_lo = (g_lane - ROUTE_G0) * EXP_PER_GROUP
    el = jnp.where((lane >= e_lo) & (lane < e_lo + EXP_PER_GROUP), logits, neg)
    t1 = jnp.max(el, -1, keepdims=True)
    i1 = jnp.min(jnp.where(el == t1, lane, big), -1, keepdims=True)
    el2 = jnp.where(lane == i1, neg, el)
    t2 = jnp.max(el2, -1, keepdims=True)
    i2 = jnp.min(jnp.where(el2 == t2, lane, big), -1, keepdims=True)
    e2 = jnp.exp(t2 - t1)
    den = 1.0 + e2
    w1 = (1.0 / den) * g_w
    w2 = (e2 / den) * g_w
    route = jnp.where(lane == 0, i1.astype(F32),
                      jnp.where(lane == 1, i2.astype(F32),
                                jnp.where(lane == 2, w1, jnp.where(lane == 3, w2, 0.0))))
    return x1, hf, route


def _shift_rows(x, s):
    return pltpu.roll(x, s, 0)


def _mixer_kernel(x_ref, mod_ref, win_ref, bin_ref, walr_ref, balr_ref, wa2_ref, gba_ref, gng_ref,
                  cw_ref, cb_ref, wrg_ref, rba_ref, rbx_ref, lam_ref, pa_ref, pb_ref, wo_ref,
                  l1g_ref, l1b_ref, wr_ref, br_ref,
                  x1_ref, hf_ref, route_ref, convn_ref, hlast_ref, sfin_ref,
                  rxbuf, hcar, st_ref, o_scr):
    j = pl.program_id(1)
    tt = x_ref.shape[1]

    @pl.when(j == 0)
    def _init():
        rxbuf[0:SUBLANES, :] = jnp.zeros((SUBLANES, D_RNN), F32)
        hcar[...] = jnp.zeros_like(hcar)
        st_ref[...] = jnp.zeros_like(st_ref)

    x = x_ref[0]
    mod = mod_ref[0]
    sh1, sc1, g1 = mod[0:1], mod[1:2], mod[2:3]
    sh2, sc2 = mod[3:4], mod[4:5]
    hb = (_ln(x) * (1.0 + sc1) + sh1).astype(BF16)

    def proj(lo, hi):
        return _dot(hb, win_ref[:, lo:hi]) + bin_ref[:, lo:hi]

    rx = proj(O_RX, O_RX + D_RNN)
    rxbuf[SUBLANES:SUBLANES + tt, :] = rx
    cw = cw_ref[...]
    xc = cb_ref[...] + rxbuf[SUBLANES - 3:SUBLANES - 3 + tt, :] * cw[0:1]
    xc = xc + rxbuf[SUBLANES - 2:SUBLANES - 2 + tt, :] * cw[1:2]
    xc = xc + rxbuf[SUBLANES - 1:SUBLANES - 1 + tt, :] * cw[2:3]
    xc = xc + rx * cw[3:4]
    rxbuf[0:SUBLANES, :] = rxbuf[tt:tt + SUBLANES, :]

    a, u = _gates(xc, wrg_ref, rba_ref[...], rbx_ref[...], lam_ref[...])
    row = lax.broadcasted_iota(I32, (tt, D_RNN), 0)
    s = 1
    while s < tt:
        keep = row >= s
        a_s = jnp.where(keep, _shift_rows(a, s), 1.0)
        u_s = jnp.where(keep, _shift_rows(u, s), 0.0)
        u = a * u_s + u
        a = a * a_s
        s *= 2
    hseq = a * hcar[0:1, :] + u
    hcar[0:1, :] = hseq[tt - 1:tt, :]
    y_a = hseq * jax.nn.gelu(proj(O_RY, O_RY + D_RNN))

    q = proj(O_Q, O_Q + GLA_DK) * (GLA_DKH ** -0.5)
    k = proj(O_K, O_K + GLA_DK)
    v = proj(O_V, O_V + GLA_DV)
    la = _log_decay(hb, walr_ref, balr_ref, wa2_ref, gba_ref)
    rowk = lax.broadcasted_iota(I32, (tt, GLA_DK), 0) % GLA_CHUNK
    bcum = la
    s = 1
    while s < GLA_CHUNK:
        bcum = bcum + jnp.where(rowk >= s, _shift_rows(bcum, s), 0.0)
        s *= 2
    eb = jnp.exp(bcum)
    q_in = (q * eb).astype(BF16)
    k_in = (k * jnp.exp(-bcum)).astype(BF16)
    tri = (lax.broadcasted_iota(I32, (GLA_CHUNK, GLA_CHUNK), 0)
           >= lax.broadcasted_iota(I32, (GLA_CHUNK, GLA_CHUNK), 1))
    for c in range(tt // GLA_CHUNK):
        r0 = c * GLA_CHUNK
        for hh in range(GLA_HEADS):
            kc = slice(hh * GLA_DKH, (hh + 1) * GLA_DKH)
            vc = slice(hh * GLA_DVH, (hh + 1) * GLA_DVH)
            b = bcum[r0:r0 + GLA_CHUNK, kc]
            btot = b[GLA_CHUNK - 1:GLA_CHUNK, :]
            qi = q_in[r0:r0 + GLA_CHUNK, kc]
            ki = k_in[r0:r0 + GLA_CHUNK, kc]
            k_out = (k[r0:r0 + GLA_CHUNK, kc] * jnp.exp(btot - b)).astype(BF16)
            vh = v[r0:r0 + GLA_CHUNK, vc].astype(BF16)
            attn = lax.dot_general(qi, ki, (((1,), (1,)), ((), ())), preferred_element_type=F32)
            attn = jnp.where(tri, attn, 0.0)
            st = st_ref[hh]
            inter = lax.dot_general(qi, st.astype(BF16), (((1,), (1,)), ((), ())), preferred_element_type=F32)
            o_scr[r0:r0 + GLA_CHUNK, vc] = _dot(attn.astype(BF16), vh) + inter
            d_st = lax.dot_general(vh, k_out, (((0,), (0,)), ((), ())), preferred_element_type=F32)
            st_ref[hh] = st * jnp.exp(btot) + d_st

    x1, hf, route = _post(x, o_scr[...], proj(O_GG, O_GG + GLA_DV), proj(O_GA, O_GA + D_MODEL),
                          proj(O_GB, O_GB + D_MODEL), y_a, g1, sh2, sc2, gng_ref[...],
                          pa_ref, pb_ref, wo_ref, l1g_ref[...], l1b_ref[...], wr_ref, br_ref[...])
    x1_ref[0] = x1
    _store_token_tiles(hf_ref.at[0], hf)
    route_ref[0] = route

    @pl.when(j == pl.num_programs(1) - 1)
    def _final():
        convn_ref[0] = rxbuf[SUBLANES - (CONV_W - 1):SUBLANES, :]
        hlast_ref[0] = hcar[0:1, :]
        for hh in range(GLA_HEADS):
            sfin_ref[0, hh] = st_ref[hh].T


def _mixer_call(x, mod3, wts):
    b, t, _ = x.shape
    tt = MIX_TT
    assert t % tt == 0 and tt % GLA_CHUNK == 0
    tok = lambda bi, j: (bi, j, 0)
    per_b = lambda bi, j: (bi, 0, 0)
    in_specs = [pl.BlockSpec((1, tt, D_MODEL), tok),
                pl.BlockSpec((1, 6, D_MODEL), per_b)] + [_const_spec(w.shape) for w in wts]
    out_specs = [pl.BlockSpec((1, tt, D_MODEL), tok),
                 pl.BlockSpec((1, tt, SUBLANES, LANES), lambda bi, j: (bi, j, 0, 0)),
                 pl.BlockSpec((1, tt, LANES), tok),
                 pl.BlockSpec((1, CONV_W - 1, D_RNN), per_b),
                 pl.BlockSpec((1, 1, D_RNN), per_b),
                 pl.BlockSpec((1, GLA_HEADS, GLA_DKH, GLA_DVH), lambda bi, j: (bi, 0, 0, 0))]
    out_shape = [jax.ShapeDtypeStruct((b, t, D_MODEL), F32),
                 jax.ShapeDtypeStruct((b, t, SUBLANES, LANES), F32),
                 jax.ShapeDtypeStruct((b, t, LANES), F32),
                 jax.ShapeDtypeStruct((b, CONV_W - 1, D_RNN), F32),
                 jax.ShapeDtypeStruct((b, 1, D_RNN), F32),
                 jax.ShapeDtypeStruct((b, GLA_HEADS, GLA_DKH, GLA_DVH), F32)]
    scratch = [pltpu.VMEM((tt + SUBLANES, D_RNN), F32),
               pltpu.VMEM((SUBLANES, D_RNN), F32),
               pltpu.VMEM((GLA_HEADS, GLA_DVH, GLA_DKH), F32),
               pltpu.VMEM((tt, GLA_DV), F32)]
    return pl.pallas_call(
        _mixer_kernel,
        grid=(b, t // tt),
        in_specs=in_specs,
        out_specs=out_specs,
        out_shape=out_shape,
        scratch_shapes=scratch,
        compiler_params=pltpu.CompilerParams(dimension_semantics=("arbitrary", "arbitrary"),
                                             vmem_limit_bytes=VMEM_LIMIT_BYTES),
        name="mixer",
    )(x, mod3, *wts)


def _s_pre_kernel(x_ref, mod_ref, sconv_ref, h0_ref, win_ref, bin_ref, walr_ref, balr_ref, wa2_ref, gba_ref,
                  cw_ref, cb_ref, wrg_ref, rba_ref, rbx_ref, lam_ref,
                  convn_ref, hnew_ref, ya_ref, q_ref, k_ref, v_ref, la_ref, gg_ref, ga_ref, gb_ref):
    x = x_ref[...]
    hb = (_ln(x) * (1.0 + mod_ref[1]) + mod_ref[0]).astype(BF16)

    def proj(lo, hi):
        return _dot(hb, win_ref[:, lo:hi]) + bin_ref[:, lo:hi]

    rx = proj(O_RX, O_RX + D_RNN)
    cw = cw_ref[...]
    xc = cb_ref[...] + sconv_ref[0] * cw[0:1]
    xc = xc + sconv_ref[1] * cw[1:2]
    xc = xc + sconv_ref[2] * cw[2:3]
    xc = xc + rx * cw[3:4]
    convn_ref[0] = sconv_ref[1]
    convn_ref[1] = sconv_ref[2]
    convn_ref[2] = rx
    a, u = _gates(xc, wrg_ref, rba_ref[...], rbx_ref[...], lam_ref[...])
    hnew = u + a * h0_ref[...]
    hnew_ref[...] = hnew
    ya_ref[...] = hnew * jax.nn.gelu(proj(O_RY, O_RY + D_RNN))
    q_ref[...] = proj(O_Q, O_Q + GLA_DK) * (GLA_DKH ** -0.5)
    k_ref[...] = proj(O_K, O_K + GLA_DK)
    v_ref[...] = proj(O_V, O_V + GLA_DV)
    la_ref[...] = _log_decay(hb, walr_ref, balr_ref, wa2_ref, gba_ref)
    gg_ref[...] = proj(O_GG, O_GG + GLA_DV)
    ga_ref[...] = proj(O_GA, O_GA + D_MODEL)
    gb_ref[...] = proj(O_GB, O_GB + D_MODEL)


def _s_pre_call(x_s, mod_s, sconv, h0, wts):
    n = x_s.shape[0]
    full = lambda shape: pl.BlockSpec(shape, lambda i: (0,) * len(shape))
    ins = [x_s, mod_s, sconv, h0] + list(wts)
    shapes = [(CONV_W - 1, n, D_RNN), (n, D_RNN), (n, D_RNN), (n, GLA_DK), (n, GLA_DK), (n, GLA_DV),
              (n, GLA_DK), (n, GLA_DV), (n, D_MODEL), (n, D_MODEL)]
    return pl.pallas_call(
        _s_pre_kernel,
        grid=(1,),
        in_specs=[full(a.shape) for a in ins],
        out_specs=[full(s) for s in shapes],
        out_shape=[jax.ShapeDtypeStruct(s, F32) for s in shapes],
        compiler_params=pltpu.CompilerParams(dimension_semantics=("arbitrary",),
                                             vmem_limit_bytes=VMEM_LIMIT_BYTES),
        name="s_pre",
    )(*ins)


def _to_column(row, n):
    eye = lax.broadcasted_iota(I32, (n, n), 0) == lax.broadcasted_iota(I32, (n, n), 1)
    return jnp.sum(jnp.where(eye, jnp.broadcast_to(row, (n, n)), 0.0), axis=1, keepdims=True)


def _s_state_kernel(s_ref, q_ref, k_ref, v_ref, la_ref, snew_ref, o_ref):
    sb = s_ref.shape[0]
    qb = q_ref[...].astype(BF16)
    dec = jnp.exp(la_ref[...])
    kk = k_ref[...]
    vv = v_ref[...]
    for si in range(sb):
        for hh in range(GLA_HEADS):
            kc = slice(hh * GLA_DKH, (hh + 1) * GLA_DKH)
            vc = slice(hh * GLA_DVH, (hh + 1) * GLA_DVH)
            d_col = _to_column(dec[si:si + 1, kc], GLA_DKH)
            k_col = _to_column(kk[si:si + 1, kc], GLA_DKH)
            s_new = s_ref[si, hh] * d_col + k_col * vv[si:si + 1, vc]
            snew_ref[si, hh] = s_new
            o_all = _dot(qb[:, kc], s_new.astype(BF16))
            o_ref[si:si + 1, vc] = o_all[si:si + 1, :]


def _s_state_call(state, q, k, v, la):
    n = state.shape[0]
    sb = SAMPLE_SB
    assert n % sb == 0
    blk = lambda w: pl.BlockSpec((sb, w), lambda i: (i, 0))
    st_spec = pl.BlockSpec((sb, GLA_HEADS, GLA_DKH, GLA_DVH), lambda i: (i, 0, 0, 0))
    return pl.pallas_call(
        _s_state_kernel,
        grid=(n // sb,),
        in_specs=[st_spec, blk(GLA_DK), blk(GLA_DK), blk(GLA_DV), blk(GLA_DK)],
        out_specs=[st_spec, blk(GLA_DV)],
        out_shape=[jax.ShapeDtypeStruct(state.shape, F32), jax.ShapeDtypeStruct((n, GLA_DV), F32)],
        compiler_params=pltpu.CompilerParams(dimension_semantics=("arbitrary",),
                                             vmem_limit_bytes=VMEM_LIMIT_BYTES),
        name="s_state",
    )(state, q, k, v, la)


def _s_post_kernel(x_ref, mod_ref, o_ref, gg_ref, ga_ref, gb_ref, ya_ref, gng_ref, pa_ref, pb_ref, wo_ref,
                   l1g_ref, l1b_ref, wr_ref, br_ref, x1_ref, hf_ref, route_ref):
    x1, hf, route = _post(x_ref[...], o_ref[...], gg_ref[...], ga_ref[...], gb_ref[...], ya_ref[...],
                          mod_ref[2], mod_ref[3], mod_ref[4], gng_ref[...], pa_ref, pb_ref, wo_ref,
                          l1g_ref[...], l1b_ref[...], wr_ref, br_ref[...])
    x1_ref[...] = x1
    _store_token_tiles(hf_ref, hf)
    route_ref[...] = route


def _s_post_call(x_s, mod_s, o, gg, ga, gb, ya, wts):
    n = x_s.shape[0]
    full = lambda shape: pl.BlockSpec(shape, lambda i: (0,) * len(shape))
    ins = [x_s, mod_s, o, gg, ga, gb, ya] + list(wts)
    shapes = [(n, D_MODEL), (n, SUBLANES, LANES), (n, LANES)]
    return pl.pallas_call(
        _s_post_kernel,
        grid=(1,),
        in_specs=[full(a.shape) for a in ins],
        out_specs=[full(s) for s in shapes],
        out_shape=[jax.ShapeDtypeStruct(s, F32) for s in shapes],
        compiler_params=pltpu.CompilerParams(dimension_semantics=("arbitrary",),
                                             vmem_limit_bytes=VMEM_LIMIT_BYTES),
        name="s_post",
    )(*ins)


def _rank_kernel(eid_ref, rank_ref, cnt_ref, carry):
    n_chunks = eid_ref.shape[0]
    carry[...] = jnp.zeros_like(carry)
    e_iota = lax.broadcasted_iota(I32, (N_EXPERTS, LANES), 0)
    upper = (lax.broadcasted_iota(I32, (LANES, LANES), 0)
             < lax.broadcasted_iota(I32, (LANES, LANES), 1)).astype(BF16)

    def body(c, _):
        ids = eid_ref[c]
        oh0 = (e_iota == ids[0:1, :])
        oh1 = (e_iota == ids[1:2, :])
        both = (oh0 | oh1).astype(F32)
        before = _dot(both.astype(BF16), upper) + carry[:, 0:1]
        r0 = jnp.sum(jnp.where(oh0, before, 0.0), axis=0, keepdims=True)
        r1 = jnp.sum(jnp.where(oh1, before, 0.0), axis=0, keepdims=True)
        rank_ref[c] = jnp.concatenate([r0, r1], axis=0).astype(I32)
        carry[...] = carry[...] + jnp.sum(both, axis=1, keepdims=True)
        return 0

    lax.fori_loop(0, n_chunks, body, 0)
    cnt_ref[...] = carry[...].astype(I32)


def _rank_call(eid3):
    n_chunks = eid3.shape[0]
    full = lambda shape: pl.BlockSpec(shape, lambda i: (0,) * len(shape))
    return pl.pallas_call(
        _rank_kernel,
        grid=(1,),
        in_specs=[full(eid3.shape)],
        out_specs=[full(eid3.shape), full((N_EXPERTS, LANES))],
        out_shape=[jax.ShapeDtypeStruct(eid3.shape, I32), jax.ShapeDtypeStruct((N_EXPERTS, LANES), I32)],
        scratch_shapes=[pltpu.VMEM((N_EXPERTS, LANES), F32)],
        compiler_params=pltpu.CompilerParams(dimension_semantics=("arbitrary",)),
        name="rank",
    )(eid3)


def _dispatch_kernel(pends_ref, dest_ref, src_p_ref, src_s_ref, buf_ref, sem, zeros_ref, zsem, *, n_p_chunks):
    i = pl.program_id(0)
    n_tiles = buf_ref.shape[0] // MOE_TM

    @pl.when(i == 0)
    def _clear():
        zeros_ref[...] = jnp.zeros_like(zeros_ref)

        def tile_copy(row0):
            return pltpu.make_async_copy(zeros_ref, buf_ref.at[pl.ds(row0, MOE_TM)], zsem)

        def nonempty(e):
            return pends_ref[e] > jnp.where(e == 0, 0, pends_ref[jnp.maximum(e - 1, 0)])

        def start_tail(e, _):
            @pl.when(nonempty(e))
            def _():
                tile_copy(pends_ref[e] - MOE_TM).start()
            return 0

        def wait_tail(e, _):
            @pl.when(nonempty(e))
            def _():
                tile_copy(0).wait()
            return 0

        def start_unused(j, _):
            tile_copy(j * MOE_TM).start()
            return 0

        def wait_unused(j, _):
            tile_copy(0).wait()
            return 0

        first_unused = pends_ref[N_EXPERTS - 1] // MOE_TM
        lax.fori_loop(0, N_EXPERTS, start_tail, 0)
        lax.fori_loop(first_unused, n_tiles, start_unused, 0)
        lax.fori_loop(0, N_EXPERTS, wait_tail, 0)
        lax.fori_loop(first_unused, n_tiles, wait_unused, 0)

    def scatter_rows(src_ref):
        n = src_ref.shape[0]

        def start_rows(t, _):
            for k in range(2):
                pltpu.make_async_copy(src_ref.at[pl.ds(t, 1)], buf_ref.at[pl.ds(dest_ref[2 * t + k], 1)],
                                      sem).start()
            return 0

        lax.fori_loop(0, n, start_rows, 0, unroll=4)
        for _ in range(2):
            pltpu.make_async_copy(src_ref, buf_ref.at[pl.ds(0, n)], sem).wait()

    @pl.when(i < n_p_chunks)
    def _prompt():
        scatter_rows(src_p_ref)

    @pl.when(i >= n_p_chunks)
    def _sample():
        scatter_rows(src_s_ref)


def _dispatch_call(pends, dest_flat, src_p, src_s, n_rows):
    assert src_p.shape[0] % ROW_CHUNK == 0 and src_s.shape[0] % ROW_CHUNK == 0 and n_rows % MOE_TM == 0
    n_p_chunks = src_p.shape[0] // ROW_CHUNK
    n_s_chunks = src_s.shape[0] // ROW_CHUNK
    tile = (ROW_CHUNK, SUBLANES, LANES)
    in_specs = [pl.BlockSpec((2 * ROW_CHUNK,), lambda i, pe: (i,), memory_space=pltpu.SMEM),
                pl.BlockSpec(tile, lambda i, pe: (jnp.minimum(i, n_p_chunks - 1), 0, 0)),
                pl.BlockSpec(tile, lambda i, pe: (jnp.maximum(i - n_p_chunks, 0), 0, 0))]
    scratch = [pltpu.SemaphoreType.DMA(()), pltpu.VMEM((MOE_TM, SUBLANES, LANES), F32),
               pltpu.SemaphoreType.DMA(())]
    return pl.pallas_call(
        functools.partial(_dispatch_kernel, n_p_chunks=n_p_chunks),
        grid_spec=pltpu.PrefetchScalarGridSpec(
            num_scalar_prefetch=1, grid=(n_p_chunks + n_s_chunks,), in_specs=in_specs,
            out_specs=pl.BlockSpec(memory_space=pl.ANY), scratch_shapes=scratch),
        out_shape=jax.ShapeDtypeStruct((n_rows, SUBLANES, LANES), F32),
        compiler_params=pltpu.CompilerParams(dimension_semantics=("arbitrary",), has_side_effects=True),
        name="dispatch",
    )(pends, dest_flat, src_p, src_s)


def _expert_kernel(blk_e_ref, n_used_ref, x_ref, wg_ref, wu_ref, wd_ref, y_ref, wgb, wub, wdb):
    i = pl.program_id(0)
    new_expert = (i == 0) | (blk_e_ref[i] != blk_e_ref[jnp.maximum(i - 1, 0)])

    @pl.when(new_expert)
    def _cast_weights():
        wgb[...] = wg_ref[0].astype(BF16)
        wub[...] = wu_ref[0].astype(BF16)
        wdb[...] = wd_ref[0].astype(BF16)

    @pl.when(i < n_used_ref[0])
    def _run():
        xb = _load_token_tiles(x_ref).astype(BF16)
        g = _dot(xb, wgb[...])
        u = _dot(xb, wub[...])
        act = (jax.nn.silu(g) * u).astype(BF16)
        _store_token_tiles(y_ref, _dot(act, wdb[...]))

    @pl.when(i >= n_used_ref[0])
    def _skip():
        y_ref[...] = jnp.zeros_like(y_ref)


def _expert_call(blk_e, n_used, xbuf, we_gate, we_up, we_down):
    p = xbuf.shape[0]
    assert p % MOE_TM == 0
    tile_spec = pl.BlockSpec((MOE_TM, SUBLANES, LANES), lambda i, be, nu: (i, 0, 0))
    grid_spec = pltpu.PrefetchScalarGridSpec(
        num_scalar_prefetch=2,
        grid=(p // MOE_TM,),
        in_specs=[tile_spec,
                  pl.BlockSpec((1, D_MODEL, D_EXPERT), lambda i, be, nu: (be[i], 0, 0)),
                  pl.BlockSpec((1, D_MODEL, D_EXPERT), lambda i, be, nu: (be[i], 0, 0)),
                  pl.BlockSpec((1, D_EXPERT, D_MODEL), lambda i, be, nu: (be[i], 0, 0))],
        out_specs=tile_spec,
        scratch_shapes=[pltpu.VMEM((D_MODEL, D_EXPERT), BF16), pltpu.VMEM((D_MODEL, D_EXPERT), BF16),
                        pltpu.VMEM((D_EXPERT, D_MODEL), BF16)],
    )
    return pl.pallas_call(
        _expert_kernel,
        grid_spec=grid_spec,
        out_shape=jax.ShapeDtypeStruct((p, SUBLANES, LANES), F32),
        compiler_params=pltpu.CompilerParams(dimension_semantics=("arbitrary",),
                                             vmem_limit_bytes=VMEM_LIMIT_BYTES),
        name="experts",
    )(blk_e, n_used, xbuf, we_gate, we_up, we_down)


def _final_kernel(dest_ref, dest_next_ref, x1_ref, ybuf_ref, route_ref, g2_ref, l2g_ref, l2b_ref, out_ref,
                  yg, sems):
    n = x1_ref.shape[0]
    i = pl.program_id(0)
    slot = i % 2

    def gather(idx_ref, s):
        def start_rows(t, _):
            for k in range(2):
                pltpu.make_async_copy(ybuf_ref.at[pl.ds(idx_ref[2 * t + k], 1)], yg.at[s, k, pl.ds(t, 1)],
                                      sems.at[s]).start()
            return 0
        lax.fori_loop(0, n, start_rows, 0, unroll=4)

    @pl.when(i == 0)
    def _first():
        gather(dest_ref, 0)

    @pl.when(i + 1 < pl.num_programs(0))
    def _prefetch():
        gather(dest_next_ref, 1 - slot)

    for k in range(2):
        pltpu.make_async_copy(ybuf_ref.at[pl.ds(0, n)], yg.at[slot, k], sems.at[slot]).wait()
    route = route_ref[...]
    ff = (_load_token_tiles(yg.at[slot, 0]) * route[:, 2:3]
          + _load_token_tiles(yg.at[slot, 1]) * route[:, 3:4])
    out_ref[...] = _ln(ALPHA * x1_ref[...] + g2_ref[0] * ff) * l2g_ref[...] + l2b_ref[...]


def _final_call(dest_flat, x1, ybuf, route, g2, ln2_g, ln2_b, rows_per_g2, tile):
    n = x1.shape[0]
    assert n % tile == 0 and rows_per_g2 % tile == 0
    per = rows_per_g2 // tile
    g2_rows = g2.shape[1]
    steps = n // tile
    return pl.pallas_call(
        _final_kernel,
        grid=(steps,),
        in_specs=[pl.BlockSpec((2 * tile,), lambda i: (i,), memory_space=pltpu.SMEM),
                  pl.BlockSpec((2 * tile,), lambda i: (jnp.minimum(i + 1, steps - 1),), memory_space=pltpu.SMEM),
                  pl.BlockSpec((tile, D_MODEL), lambda i: (i, 0)),
                  pl.BlockSpec(memory_space=pl.ANY),
                  pl.BlockSpec((tile, LANES), lambda i: (i, 0)),
                  pl.BlockSpec((1, g2_rows, D_MODEL), lambda i: (i // per, 0, 0)),
                  pl.BlockSpec((1, D_MODEL), lambda i: (0, 0)),
                  pl.BlockSpec((1, D_MODEL), lambda i: (0, 0))],
        out_specs=pl.BlockSpec((tile, D_MODEL), lambda i: (i, 0)),
        out_shape=jax.ShapeDtypeStruct((n, D_MODEL), F32),
        scratch_shapes=[pltpu.VMEM((2, 2, tile, SUBLANES, LANES), F32), pltpu.SemaphoreType.DMA((2,))],
        compiler_params=pltpu.CompilerParams(dimension_semantics=("arbitrary",)),
        name="final",
    )(dest_flat, dest_flat, x1, ybuf, route, g2, ln2_g, ln2_b)


def _block_diag_gate_weights(wa, wx):
    def bd(w):
        w = w.reshape(N_RG_TILES, RG_GROUP, RG_BW, RG_BW)
        eye = jnp.eye(RG_GROUP, dtype=w.dtype)
        return jnp.einsum('tgcd,gh->tgchd', w, eye).reshape(N_RG_TILES, MXU_DIM, MXU_DIM)
    return jnp.concatenate([bd(wa), bd(wx)], axis=2).astype(BF16)


def kernel(x_prompt, x_sample, state_conv, state_rglru, state_gla, c_prompt, c_sample, w_mod, b_mod, w_in, b_in,
           conv_w, conv_b, rg_wa, rg_ba, rg_wx, rg_bx, rg_lambda, gla_wa2, gla_ba, gla_norm_g, p_a, p_b, w_o,
           ln1_g, ln1_b, w_grp, b_grp, w_exp, b_exp, we_gate, we_up, we_down, ln2_g, ln2_b):
    assert w_mod.shape[0] == DEPTH == 1
    bp, tp, _ = x_prompt.shape
    ns = x_sample.shape[0]
    n_p = bp * tp
    n_tok = n_p + ns
    row = lambda a: a.reshape(1, -1)

    lo = 2 * D_RNN + 2 * GLA_DK + 2 * GLA_DV
    w_in0 = w_in[0]
    win_main = jnp.concatenate([w_in0[:, :lo], w_in0[:, lo + GLA_RANK:]], axis=1).astype(BF16)
    bin_main = row(jnp.concatenate([b_in[0, :lo], b_in[0, lo + GLA_RANK:]]))
    walr = jnp.pad(w_in0[:, lo:lo + GLA_RANK], ((0, 0), (0, LANES - GLA_RANK))).astype(BF16)
    balr = row(jnp.pad(b_in[0, lo:lo + GLA_RANK], (0, LANES - GLA_RANK)))
    wa2 = jnp.pad(gla_wa2[0], ((0, LANES - GLA_RANK), (0, 0))).astype(BF16)
    wrg = _block_diag_gate_weights(rg_wa[0], rg_wx[0])
    w_route = jnp.pad(jnp.concatenate([w_exp[0], w_grp[0]], axis=1),
                      ((0, 0), (0, LANES - N_EXPERTS - N_GROUPS))).astype(BF16)
    b_route = row(jnp.pad(jnp.concatenate([b_exp[0], b_grp[0]]), (0, LANES - N_EXPERTS - N_GROUPS)))
    pa, pb, wo = p_a[0].astype(BF16), p_b[0].astype(BF16), w_o[0].astype(BF16)
    pre_w = [win_main, bin_main, walr, balr, wa2, row(gla_ba[0])]
    rec_w = [conv_w[0], row(conv_b[0]), wrg, row(rg_ba[0]), row(rg_bx[0]), row(rg_lambda[0])]
    post_w = [row(gla_norm_g[0]), pa, pb, wo, row(ln1_g[0]), row(ln1_b[0]), w_route, b_route]

    mod = _mod_call(jnp.concatenate([c_prompt, c_sample], axis=0), w_mod[0], row(b_mod[0]))
    mod_p = mod[:bp].reshape(bp, 6, D_MODEL)
    mod_s = mod[bp:].reshape(ns, 6, D_MODEL).transpose(1, 0, 2)

    mix_w = pre_w[:5] + [pre_w[5], post_w[0]] + rec_w + post_w[1:]
    x1_p, hf_p, route_p, conv_p, h_p, s_p = _mixer_call(x_prompt, mod_p, mix_w)

    xs = x_sample.reshape(ns, D_MODEL)
    sconv = state_conv[0].transpose(1, 0, 2)
    conv_s, h_s, ya_s, q_s, k_s, v_s, la_s, gg_s, ga_s, gb_s = _s_pre_call(
        xs, mod_s, sconv, state_rglru[0], pre_w + rec_w)
    s_s, o_s = _s_state_call(state_gla[0], q_s, k_s, v_s, la_s)
    x1_s, hf_s, route_s = _s_post_call(xs, mod_s, o_s, gg_s, ga_s, gb_s, ya_s, post_w)

    route_pf = route_p.reshape(n_p, LANES)
    eid = jnp.concatenate([route_pf[:, :2], route_s[:, :2]], axis=0).astype(I32)
    assert n_tok % LANES == 0
    n_chunks = n_tok // LANES
    eid3 = eid.T.reshape(2, n_chunks, LANES).transpose(1, 0, 2)
    rank3, cnt = _rank_call(eid3)
    rank = rank3.transpose(1, 0, 2).reshape(2, n_tok).T
    counts = cnt[:, 0]
    pcounts = (counts + MOE_TM - 1) // MOE_TM * MOE_TM
    pends = jnp.cumsum(pcounts)
    pstarts = pends - pcounts
    dest = (pstarts[eid] + rank).reshape(-1).astype(I32)
    n_tiles = -(-(2 * n_tok + N_EXPERTS * (MOE_TM - 1)) // MOE_TM)
    tile_start = jnp.arange(n_tiles, dtype=I32) * MOE_TM
    blk_e = jnp.minimum(jnp.sum(pends[None, :] <= tile_start[:, None], axis=1), N_EXPERTS - 1).astype(I32)
    n_used = (pends[-1] // MOE_TM).astype(I32).reshape(1)

    pends32 = pends.astype(I32)
    xbuf = _dispatch_call(pends32, dest, hf_p.reshape(n_p, SUBLANES, LANES), hf_s, n_tiles * MOE_TM)
    ybuf = _expert_call(blk_e, n_used, xbuf, we_gate[0], we_up[0], we_down[0])

    y_p = _final_call(dest[:2 * n_p], x1_p.reshape(n_p, D_MODEL), ybuf, route_pf, mod_p[:, 5:6, :],
                      row(ln2_g[0]), row(ln2_b[0]), tp, ROW_CHUNK)
    y_s = _final_call(dest[2 * n_p:], x1_s, ybuf, route_s, mod_s[5][None], row(ln2_g[0]), row(ln2_b[0]), ns, ns)

    return (y_p.reshape(bp, tp, D_MODEL), y_s.reshape(ns, 1, D_MODEL),
            conv_p[None], h_p.reshape(1, bp, D_RNN), s_p[None],
            conv_s.transpose(1, 0, 2)[None], h_s[None], s_s[None])
```

```python
import functools

import jax
import jax.numpy as jnp
import numpy as np
from jax import lax
from jax.experimental import pallas as pl
from jax.experimental.pallas import tpu as pltpu

F32 = jnp.float32
BF16 = jnp.bfloat16
I32 = jnp.int32

D_MODEL = 1024
D_RNN = D_MODEL
RG_BLOCKS = 16
RG_BW = D_RNN // RG_BLOCKS
CONV_W = 4
RG_C = 8.0
GLA_HEADS = 4
GLA_DK = D_MODEL // 2
GLA_DV = D_MODEL
GLA_DKH = GLA_DK // GLA_HEADS
GLA_DVH = GLA_DV // GLA_HEADS
GLA_RANK = 16
GLA_TAU = 16.0
GLA_CHUNK = 64
N_GROUPS = 4
EXP_PER_GROUP = 8
N_EXPERTS = N_GROUPS * EXP_PER_GROUP
D_EXPERT = 512
DEPTH = 1
ALPHA = (2.0 * DEPTH) ** 0.25
LN_EPS = 1e-5
RMS_EPS = 1e-6

LANES = 128
SUBLANES = 8
MXU_DIM = 256
VMEM_LIMIT_BYTES = 56 * 1024 * 1024

MIX_TT = 256
MOE_TM = 256
ROW_CHUNK = 128
SAMPLE_SB = 8
RG_GROUP = MXU_DIM // RG_BW
N_RG_TILES = RG_BLOCKS // RG_GROUP
N_MAIN = 2 * D_RNN + 2 * GLA_DK + 2 * GLA_DV + 2 * D_MODEL
ROUTE_G0 = N_EXPERTS

O_RX, O_RY, O_Q, O_K, O_V, O_GG, O_GA, O_GB = 0, 1024, 2048, 2560, 3072, 4096, 5120, 6144


def _ln(x):
    mu = jnp.mean(x, -1, keepdims=True)
    xc = x - mu
    var = jnp.mean(xc * xc, -1, keepdims=True)
    return xc * lax.rsqrt(var + LN_EPS)


def _dot(a, b):
    return jnp.dot(a, b, preferred_element_type=F32)


def _expm1(x):
    u = jnp.exp(x)
    small = (u - 1.0) * x / jnp.log(u)
    return jnp.where(u == 1.0, x, jnp.where(jnp.abs(x) < 0.5, small, u - 1.0))


def _const_spec(shape):
    nd = len(shape)
    return pl.BlockSpec(shape, lambda *_: (0,) * nd, pipeline_mode=pl.Buffered(1))


assert D_MODEL == SUBLANES * LANES


def _tok_rows(t, count=1):
    return pl.ds(pl.multiple_of(t * SUBLANES, SUBLANES), count * SUBLANES)


def _store_token_tiles(ref, rows):
    n = rows.shape[0]
    for s in range(SUBLANES):
        ref[pl.ds(s, n, stride=SUBLANES), :] = rows[:, s * LANES:(s + 1) * LANES]


def _load_token_tiles(ref):
    n = ref.shape[0] // SUBLANES
    return jnp.concatenate([ref[pl.ds(s, n, stride=SUBLANES), :] for s in range(SUBLANES)], axis=1)


def _mod_kernel(c_ref, w_ref, b_ref, o_ref):
    c = c_ref[...]
    s = jax.nn.silu(c)
    o_ref[...] = _dot(s.astype(BF16), w_ref[...].astype(BF16)) + b_ref[...]


def _mod_call(c_all, w_mod, b_mod):
    n = c_all.shape[0]
    tn = 512
    return pl.pallas_call(
        _mod_kernel,
        grid=(6 * D_MODEL // tn,),
        in_specs=[pl.BlockSpec((n, D_MODEL), lambda i: (0, 0)),
                  pl.BlockSpec((D_MODEL, tn), lambda i: (0, i)),
                  pl.BlockSpec((1, tn), lambda i: (0, i))],
        out_specs=pl.BlockSpec((n, tn), lambda i: (0, i)),
        out_shape=jax.ShapeDtypeStruct((n, 6 * D_MODEL), F32),
        compiler_params=pltpu.CompilerParams(dimension_semantics=("arbitrary",)),
        name="mod",
    )(c_all, w_mod, b_mod)


def _gates(xc, wrg_ref, rba, rbx, lam):
    xcb = xc.astype(BF16)
    parts = [_dot(xcb[:, c * MXU_DIM:(c + 1) * MXU_DIM], wrg_ref[c]) for c in range(N_RG_TILES)]
    r_pre = jnp.concatenate([p[:, :MXU_DIM] for p in parts], axis=1) + rba
    i_pre = jnp.concatenate([p[:, MXU_DIM:] for p in parts], axis=1) + rbx
    r = jax.nn.sigmoid(r_pre)
    ig = jax.nn.sigmoid(i_pre)
    log_a = -RG_C * r * jax.nn.softplus(-lam)
    a = jnp.exp(log_a)
    u = jnp.sqrt(-_expm1(2.0 * log_a)) * (ig * xc)
    return a, u


def _log_decay(hb, walr_ref, balr_ref, wa2_ref, gba_ref):
    alr = _dot(hb, walr_ref[...]) + balr_ref[...]
    return jax.nn.log_sigmoid(_dot(alr.astype(BF16), wa2_ref[...]) + gba_ref[...]) / GLA_TAU


def _post(x, o, gg, ga, gb, y_a, g1, sh2, sc2, gng, pa_ref, pb_ref, wo_ref, l1g, l1b, wr_ref, br):
    rows = x.shape[0]
    heads = []
    for hh in range(GLA_HEADS):
        oh = o[:, hh * GLA_DVH:(hh + 1) * GLA_DVH]
        ms = jnp.mean(oh * oh, -1, keepdims=True)
        heads.append(oh * lax.rsqrt(ms + RMS_EPS) * gng)
    y_b = jnp.concatenate(heads, axis=1) * jax.nn.silu(gg)
    merged = (jax.nn.sigmoid(ga) * _dot(y_a.astype(BF16), pa_ref[...])
              + jax.nn.sigmoid(gb) * _dot(y_b.astype(BF16), pb_ref[...]))
    mix = _dot(merged.astype(BF16), wo_ref[...])
    x1 = _ln(ALPHA * x + g1 * mix) * l1g + l1b
    hf = _ln(x1) * (1.0 + sc2) + sh2
    logits = _dot(hf.astype(BF16), wr_ref[...]) + br
    lane = lax.broadcasted_iota(I32, (rows, LANES), 1).astype(F32)
    neg = jnp.float32(-jnp.inf)
    big = jnp.float32(LANES)
    g_valid = (lane >= ROUTE_G0) & (lane < ROUTE_G0 + N_GROUPS)
    gl = jnp.where(g_valid, logits, neg)
    gmax = jnp.max(gl, -1, keepdims=True)
    g_lane = jnp.min(jnp.where(gl == gmax, lane, big), -1, keepdims=True)
    g_w = 1.0 / jnp.sum(jnp.exp(gl - gmax), -1, keepdims=True)
    e_lo = (g_lane - ROUTE_G0) * EXP_PER_GROUP
    el = jnp.where((lane >= e_lo) & (lane < e_lo + EXP_PER_GROUP), logits, neg)
    t1 = jnp.max(el, -1, keepdims=True)
    i1 = jnp.min(jnp.where(el == t1, lane, big), -1, keepdims=True)
    el2 = jnp.where(lane == i1, neg, el)
    t2 = jnp.max(el2, -1, keepdims=True)
    i2 = jnp.min(jnp.where(el2 == t2, lane, big), -1, keepdims=True)
    e2 = jnp.exp(t2 - t1)
    den = 1.0 + e2
    w1 = (1.0 / den) * g_w
    w2 = (e2 / den) * g_w
    route = jnp.where(lane == 0, i1.astype(F32),
                      jnp.where(lane == 1, i2.astype(F32),
                                jnp.where(lane == 2, w1, jnp.where(lane == 3, w2, 0.0))))
    return x1, hf, route


def _shift_rows(x, s):
    return pltpu.roll(x, s, 0)


def _mixer_kernel(x_ref, mod_ref, win_ref, bin_ref, walr_ref, balr_ref, wa2_ref, gba_ref, gng_ref,
                  cw_ref, cb_ref, wrg_ref, rba_ref, rbx_ref, lam_ref, pa_ref, pb_ref, wo_ref,
                  l1g_ref, l1b_ref, wr_ref, br_ref,
                  x1_ref, hf_ref, route_ref, convn_ref, hlast_ref, sfin_ref,
                  rxbuf, hcar, st_ref, o_scr):
    j = pl.program_id(1)
    tt = x_ref.shape[1]

    @pl.when(j == 0)
    def _init():
        rxbuf[0:SUBLANES, :] = jnp.zeros((SUBLANES, D_RNN), F32)
        hcar[...] = jnp.zeros_like(hcar)
        st_ref[...] = jnp.zeros_like(st_ref)

    x = x_ref[0]
    mod = mod_ref[0]
    sh1, sc1, g1 = mod[0:1], mod[1:2], mod[2:3]
    sh2, sc2 = mod[3:4], mod[4:5]
    hb = (_ln(x) * (1.0 + sc1) + sh1).astype(BF16)

    def proj(lo, hi):
        return _dot(hb, win_ref[:, lo:hi]) + bin_ref[:, lo:hi]

    rx = proj(O_RX, O_RX + D_RNN)
    rxbuf[SUBLANES:SUBLANES + tt, :] = rx
    cw = cw_ref[...]
    xc = cb_ref[...] + rxbuf[SUBLANES - 3:SUBLANES - 3 + tt, :] * cw[0:1]
    xc = xc + rxbuf[SUBLANES - 2:SUBLANES - 2 + tt, :] * cw[1:2]
    xc = xc + rxbuf[SUBLANES - 1:SUBLANES - 1 + tt, :] * cw[2:3]
    xc = xc + rx * cw[3:4]
    rxbuf[0:SUBLANES, :] = rxbuf[tt:tt + SUBLANES, :]

    a, u = _gates(xc, wrg_ref, rba_ref[...], rbx_ref[...], lam_ref[...])
    row = lax.broadcasted_iota(I32, (tt, D_RNN), 0)
    s = 1
    while s < tt:
        keep = row >= s
        a_s = jnp.where(keep, _shift_rows(a, s), 1.0)
        u_s = jnp.where(keep, _shift_rows(u, s), 0.0)
        u = a * u_s + u
        a = a * a_s
        s *= 2
    hseq = a * hcar[0:1, :] + u
    hcar[0:1, :] = hseq[tt - 1:tt, :]
    y_a = hseq * jax.nn.gelu(proj(O_RY, O_RY + D_RNN))

    q = proj(O_Q, O_Q + GLA_DK) * (GLA_DKH ** -0.5)
    k = proj(O_K, O_K + GLA_DK)
    v = proj(O_V, O_V + GLA_DV)
    la = _log_decay(hb, walr_ref, balr_ref, wa2_ref, gba_ref)
    rowk = lax.broadcasted_iota(I32, (tt, GLA_DK), 0) % GLA_CHUNK
    bcum = la
    s = 1
    while s < GLA_CHUNK:
        bcum = bcum + jnp.where(rowk >= s, _shift_rows(bcum, s), 0.0)
        s *= 2
    eb = jnp.exp(bcum)
    q_in = (q * eb).astype(BF16)
    k_in = (k * jnp.exp(-bcum)).astype(BF16)
    tri = (lax.broadcasted_iota(I32, (GLA_CHUNK, GLA_CHUNK), 0)
           >= lax.broadcasted_iota(I32, (GLA_CHUNK, GLA_CHUNK), 1))
    for c in range(tt // GLA_CHUNK):
        r0 = c * GLA_CHUNK
        for hh in range(GLA_HEADS):
            kc = slice(hh * GLA_DKH, (hh + 1) * GLA_DKH)
            vc = slice(hh * GLA_DVH, (hh + 1) * GLA_DVH)
            b = bcum[r0:r0 + GLA_CHUNK, kc]
            btot = b[GLA_CHUNK - 1:GLA_CHUNK, :]
            qi = q_in[r0:r0 + GLA_CHUNK, kc]
            ki = k_in[r0:r0 + GLA_CHUNK, kc]
            k_out = (k[r0:r0 + GLA_CHUNK, kc] * jnp.exp(btot - b)).astype(BF16)
            vh = v[r0:r0 + GLA_CHUNK, vc].astype(BF16)
            attn = lax.dot_general(qi, ki, (((1,), (1,)), ((), ())), preferred_element_type=F32)
            attn = jnp.where(tri, attn, 0.0)
            st = st_ref[hh]
            inter = lax.dot_general(qi, st.astype(BF16), (((1,), (1,)), ((), ())), preferred_element_type=F32)
            o_scr[r0:r0 + GLA_CHUNK, vc] = _dot(attn.astype(BF16), vh) + inter
            d_st = lax.dot_general(vh, k_out, (((0,), (0,)), ((), ())), preferred_element_type=F32)
            st_ref[hh] = st * jnp.exp(btot) + d_st

    x1, hf, route = _post(x, o_scr[...], proj(O_GG, O_GG + GLA_DV), proj(O_GA, O_GA + D_MODEL),
                          proj(O_GB, O_GB + D_MODEL), y_a, g1, sh2, sc2, gng_ref[...],
                          pa_ref, pb_ref, wo_ref, l1g_ref[...], l1b_ref[...], wr_ref, br_ref[...])
    x1_ref[0] = x1
    _store_token_tiles(hf_ref.at[0], hf)
    route_ref[0] = route

    @pl.when(j == pl.num_programs(1) - 1)
    def _final():
        convn_ref[0] = rxbuf[SUBLANES - (CONV_W - 1):SUBLANES, :]
        hlast_ref[0] = hcar[0:1, :]
        for hh in range(GLA_HEADS):
            sfin_ref[0, hh] = st_ref[hh].T


def _mixer_call(x, mod3, wts):
    b, t, _ = x.shape
    tt = MIX_TT
    assert t % tt == 0 and tt % GLA_CHUNK == 0
    tok = lambda bi, j: (bi, j, 0)
    per_b = lambda bi, j: (bi, 0, 0)
    in_specs = [pl.BlockSpec((1, tt, D_MODEL), tok),
                pl.BlockSpec((1, 6, D_MODEL), per_b)] + [_const_spec(w.shape) for w in wts]
    out_specs = [pl.BlockSpec((1, tt, D_MODEL), tok),
                 pl.BlockSpec((1, tt * SUBLANES, LANES), tok),
                 pl.BlockSpec((1, tt, LANES), tok),
                 pl.BlockSpec((1, CONV_W - 1, D_RNN), per_b),
                 pl.BlockSpec((1, 1, D_RNN), per_b),
                 pl.BlockSpec((1, GLA_HEADS, GLA_DKH, GLA_DVH), lambda bi, j: (bi, 0, 0, 0))]
    out_shape = [jax.ShapeDtypeStruct((b, t, D_MODEL), F32),
                 jax.ShapeDtypeStruct((b, t * SUBLANES, LANES), F32),
                 jax.ShapeDtypeStruct((b, t, LANES), F32),
                 jax.ShapeDtypeStruct((b, CONV_W - 1, D_RNN), F32),
                 jax.ShapeDtypeStruct((b, 1, D_RNN), F32),
                 jax.ShapeDtypeStruct((b, GLA_HEADS, GLA_DKH, GLA_DVH), F32)]
    scratch = [pltpu.VMEM((tt + SUBLANES, D_RNN), F32),
               pltpu.VMEM((SUBLANES, D_RNN), F32),
               pltpu.VMEM((GLA_HEADS, GLA_DVH, GLA_DKH), F32),
               pltpu.VMEM((tt, GLA_DV), F32)]
    return pl.pallas_call(
        _mixer_kernel,
        grid=(b, t // tt),
        in_specs=in_specs,
        out_specs=out_specs,
        out_shape=out_shape,
        scratch_shapes=scratch,
        compiler_params=pltpu.CompilerParams(dimension_semantics=("arbitrary", "arbitrary"),
                                             vmem_limit_bytes=VMEM_LIMIT_BYTES),
        name="mixer",
    )(x, mod3, *wts)


def _s_pre_kernel(x_ref, mod_ref, sconv_ref, h0_ref, win_ref, bin_ref, walr_ref, balr_ref, wa2_ref, gba_ref,
                  cw_ref, cb_ref, wrg_ref, rba_ref, rbx_ref, lam_ref,
                  convn_ref, hnew_ref, ya_ref, q_ref, k_ref, v_ref, la_ref, gg_ref, ga_ref, gb_ref):
    x = x_ref[...]
    hb = (_ln(x) * (1.0 + mod_ref[1]) + mod_ref[0]).astype(BF16)

    def proj(lo, hi):
        return _dot(hb, win_ref[:, lo:hi]) + bin_ref[:, lo:hi]

    rx = proj(O_RX, O_RX + D_RNN)
    cw = cw_ref[...]
    xc = cb_ref[...] + sconv_ref[0] * cw[0:1]
    xc = xc + sconv_ref[1] * cw[1:2]
    xc = xc + sconv_ref[2] * cw[2:3]
    xc = xc + rx * cw[3:4]
    convn_ref[0] = sconv_ref[1]
    convn_ref[1] = sconv_ref[2]
    convn_ref[2] = rx
    a, u = _gates(xc, wrg_ref, rba_ref[...], rbx_ref[...], lam_ref[...])
    hnew = u + a * h0_ref[...]
    hnew_ref[...] = hnew
    ya_ref[...] = hnew * jax.nn.gelu(proj(O_RY, O_RY + D_RNN))
    q_ref[...] = proj(O_Q, O_Q + GLA_DK) * (GLA_DKH ** -0.5)
    k_ref[...] = proj(O_K, O_K + GLA_DK)
    v_ref[...] = proj(O_V, O_V + GLA_DV)
    la_ref[...] = _log_decay(hb, walr_ref, balr_ref, wa2_ref, gba_ref)
    gg_ref[...] = proj(O_GG, O_GG + GLA_DV)
    ga_ref[...] = proj(O_GA, O_GA + D_MODEL)
    gb_ref[...] = proj(O_GB, O_GB + D_MODEL)


def _s_pre_call(x_s, mod_s, sconv, h0, wts):
    n = x_s.shape[0]
    full = lambda shape: pl.BlockSpec(shape, lambda i: (0,) * len(shape))
    ins = [x_s, mod_s, sconv, h0] + list(wts)
    shapes = [(CONV_W - 1, n, D_RNN), (n, D_RNN), (n, D_RNN), (n, GLA_DK), (n, GLA_DK), (n, GLA_DV),
              (n, GLA_DK), (n, GLA_DV), (n, D_MODEL), (n, D_MODEL)]
    return pl.pallas_call(
        _s_pre_kernel,
        grid=(1,),
        in_specs=[full(a.shape) for a in ins],
        out_specs=[full(s) for s in shapes],
        out_shape=[jax.ShapeDtypeStruct(s, F32) for s in shapes],
        compiler_params=pltpu.CompilerParams(dimension_semantics=("arbitrary",),
                                             vmem_limit_bytes=VMEM_LIMIT_BYTES),
        name="s_pre",
    )(*ins)


def _to_column(row, n):
    eye = lax.broadcasted_iota(I32, (n, n), 0) == lax.broadcasted_iota(I32, (n, n), 1)
    return jnp.sum(jnp.where(eye, jnp.broadcast_to(row, (n, n)), 0.0), axis=1, keepdims=True)


def _s_state_kernel(s_ref, q_ref, k_ref, v_ref, la_ref, snew_ref, o_ref):
    sb = s_ref.shape[0]
    qb = q_ref[...].astype(BF16)
    dec = jnp.exp(la_ref[...])
    kk = k_ref[...]
    vv = v_ref[...]
    for si in range(sb):
        for hh in range(GLA_HEADS):
            kc = slice(hh * GLA_DKH, (hh + 1) * GLA_DKH)
            vc = slice(hh * GLA_DVH, (hh + 1) * GLA_DVH)
            d_col = _to_column(dec[si:si + 1, kc], GLA_DKH)
            k_col = _to_column(kk[si:si + 1, kc], GLA_DKH)
            s_new = s_ref[si, hh] * d_col + k_col * vv[si:si + 1, vc]
            snew_ref[si, hh] = s_new
            o_all = _dot(qb[:, kc], s_new.astype(BF16))
            o_ref[si:si + 1, vc] = o_all[si:si + 1, :]


def _s_state_call(state, q, k, v, la):
    n = state.shape[0]
    sb = SAMPLE_SB
    assert n % sb == 0
    blk = lambda w: pl.BlockSpec((sb, w), lambda i: (i, 0))
    st_spec = pl.BlockSpec((sb, GLA_HEADS, GLA_DKH, GLA_DVH), lambda i: (i, 0, 0, 0))
    return pl.pallas_call(
        _s_state_kernel,
        grid=(n // sb,),
        in_specs=[st_spec, blk(GLA_DK), blk(GLA_DK), blk(GLA_DV), blk(GLA_DK)],
        out_specs=[st_spec, blk(GLA_DV)],
        out_shape=[jax.ShapeDtypeStruct(state.shape, F32), jax.ShapeDtypeStruct((n, GLA_DV), F32)],
        compiler_params=pltpu.CompilerParams(dimension_semantics=("arbitrary",),
                                             vmem_limit_bytes=VMEM_LIMIT_BYTES),
        name="s_state",
    )(state, q, k, v, la)


def _s_post_kernel(x_ref, mod_ref, o_ref, gg_ref, ga_ref, gb_ref, ya_ref, gng_ref, pa_ref, pb_ref, wo_ref,
                   l1g_ref, l1b_ref, wr_ref, br_ref, x1_ref, hf_ref, route_ref):
    x1, hf, route = _post(x_ref[...], o_ref[...], gg_ref[...], ga_ref[...], gb_ref[...], ya_ref[...],
                          mod_ref[2], mod_ref[3], mod_ref[4], gng_ref[...], pa_ref, pb_ref, wo_ref,
                          l1g_ref[...], l1b_ref[...], wr_ref, br_ref[...])
    x1_ref[...] = x1
    _store_token_tiles(hf_ref, hf)
    route_ref[...] = route


def _s_post_call(x_s, mod_s, o, gg, ga, gb, ya, wts):
    n = x_s.shape[0]
    full = lambda shape: pl.BlockSpec(shape, lambda i: (0,) * len(shape))
    ins = [x_s, mod_s, o, gg, ga, gb, ya] + list(wts)
    shapes = [(n, D_MODEL), (n * SUBLANES, LANES), (n, LANES)]
    return pl.pallas_call(
        _s_post_kernel,
        grid=(1,),
        in_specs=[full(a.shape) for a in ins],
        out_specs=[full(s) for s in shapes],
        out_shape=[jax.ShapeDtypeStruct(s, F32) for s in shapes],
        compiler_params=pltpu.CompilerParams(dimension_semantics=("arbitrary",),
                                             vmem_limit_bytes=VMEM_LIMIT_BYTES),
        name="s_post",
    )(*ins)


def _rank_kernel(eid_ref, rank_ref, cnt_ref, carry):
    n_chunks = eid_ref.shape[0]
    carry[...] = jnp.zeros_like(carry)
    e_iota = lax.broadcasted_iota(I32, (N_EXPERTS, LANES), 0)
    upper = (lax.broadcasted_iota(I32, (LANES, LANES), 0)
             < lax.broadcasted_iota(I32, (LANES, LANES), 1)).astype(BF16)

    def body(c, _):
        ids = eid_ref[c]
        oh0 = (e_iota == ids[0:1, :])
        oh1 = (e_iota == ids[1:2, :])
        both = (oh0 | oh1).astype(F32)
        before = _dot(both.astype(BF16), upper) + carry[:, 0:1]
        r0 = jnp.sum(jnp.where(oh0, before, 0.0), axis=0, keepdims=True)
        r1 = jnp.sum(jnp.where(oh1, before, 0.0), axis=0, keepdims=True)
        rank_ref[c] = jnp.concatenate([r0, r1], axis=0).astype(I32)
        carry[...] = carry[...] + jnp.sum(both, axis=1, keepdims=True)
        return 0

    lax.fori_loop(0, n_chunks, body, 0)
    cnt_ref[...] = carry[...].astype(I32)


def _rank_call(eid3):
    n_chunks = eid3.shape[0]
    full = lambda shape: pl.BlockSpec(shape, lambda i: (0,) * len(shape))
    return pl.pallas_call(
        _rank_kernel,
        grid=(1,),
        in_specs=[full(eid3.shape)],
        out_specs=[full(eid3.shape), full((N_EXPERTS, LANES))],
        out_shape=[jax.ShapeDtypeStruct(eid3.shape, I32), jax.ShapeDtypeStruct((N_EXPERTS, LANES), I32)],
        scratch_shapes=[pltpu.VMEM((N_EXPERTS, LANES), F32)],
        compiler_params=pltpu.CompilerParams(dimension_semantics=("arbitrary",)),
        name="rank",
    )(eid3)


def _dispatch_kernel(pends_ref, dest_ref, src_p_ref, src_s_ref, buf_ref, sem, zeros_ref, zsem, *, n_p_chunks):
    i = pl.program_id(0)
    n_tiles = buf_ref.shape[0] // (MOE_TM * SUBLANES)

    @pl.when(i == 0)
    def _clear():
        zeros_ref[...] = jnp.zeros_like(zeros_ref)

        def tile_copy(row0):
            return pltpu.make_async_copy(zeros_ref, buf_ref.at[_tok_rows(row0, MOE_TM)], zsem)

        def nonempty(e):
            return pends_ref[e] > jnp.where(e == 0, 0, pends_ref[jnp.maximum(e - 1, 0)])

        def start_tail(e, _):
            @pl.when(nonempty(e))
            def _():
                tile_copy(pends_ref[e] - MOE_TM).start()
            return 0

        def wait_tail(e, _):
            @pl.when(nonempty(e))
            def _():
                tile_copy(0).wait()
            return 0

        def start_unused(j, _):
            tile_copy(j * MOE_TM).start()
            return 0

        def wait_unused(j, _):
            tile_copy(0).wait()
            return 0

        first_unused = pends_ref[N_EXPERTS - 1] // MOE_TM
        lax.fori_loop(0, N_EXPERTS, start_tail, 0)
        lax.fori_loop(first_unused, n_tiles, start_unused, 0)
        lax.fori_loop(0, N_EXPERTS, wait_tail, 0)
        lax.fori_loop(first_unused, n_tiles, wait_unused, 0)

    def scatter_rows(src_ref):
        n = src_ref.shape[0] // SUBLANES

        def start_rows(t, _):
            for k in range(2):
                pltpu.make_async_copy(src_ref.at[_tok_rows(t)], buf_ref.at[_tok_rows(dest_ref[2 * t + k])],
                                      sem).start()
            return 0

        lax.fori_loop(0, n, start_rows, 0, unroll=4)
        for _ in range(2):
            pltpu.make_async_copy(src_ref, buf_ref.at[_tok_rows(0, n)], sem).wait()

    @pl.when(i < n_p_chunks)
    def _prompt():
        scatter_rows(src_p_ref)

    @pl.when(i >= n_p_chunks)
    def _sample():
        scatter_rows(src_s_ref)


def _dispatch_call(pends, dest_flat, src_p, src_s, n_rows):
    chunk_rows = ROW_CHUNK * SUBLANES
    assert src_p.shape[0] % chunk_rows == 0 and src_s.shape[0] % chunk_rows == 0 and n_rows % MOE_TM == 0
    n_p_chunks = src_p.shape[0] // chunk_rows
    n_s_chunks = src_s.shape[0] // chunk_rows
    tile = (chunk_rows, LANES)
    in_specs = [pl.BlockSpec((2 * ROW_CHUNK,), lambda i, pe: (i,), memory_space=pltpu.SMEM),
                pl.BlockSpec(tile, lambda i, pe: (jnp.minimum(i, n_p_chunks - 1), 0)),
                pl.BlockSpec(tile, lambda i, pe: (jnp.maximum(i - n_p_chunks, 0), 0))]
    scratch = [pltpu.SemaphoreType.DMA(()), pltpu.VMEM((MOE_TM * SUBLANES, LANES), F32),
               pltpu.SemaphoreType.DMA(())]
    return pl.pallas_call(
        functools.partial(_dispatch_kernel, n_p_chunks=n_p_chunks),
        grid_spec=pltpu.PrefetchScalarGridSpec(
            num_scalar_prefetch=1, grid=(n_p_chunks + n_s_chunks,), in_specs=in_specs,
            out_specs=pl.BlockSpec(memory_space=pl.ANY), scratch_shapes=scratch),
        out_shape=jax.ShapeDtypeStruct((n_rows * SUBLANES, LANES), F32),
        compiler_params=pltpu.CompilerParams(dimension_semantics=("arbitrary",), has_side_effects=True),
        name="dispatch",
    )(pends, dest_flat, src_p, src_s)


def _expert_kernel(blk_e_ref, n_used_ref, x_ref, wg_ref, wu_ref, wd_ref, y_ref, wgb, wub, wdb):
    i = pl.program_id(0)
    new_expert = (i == 0) | (blk_e_ref[i] != blk_e_ref[jnp.maximum(i - 1, 0)])

    @pl.when(new_expert)
    def _cast_weights():
        wgb[...] = wg_ref[0].astype(BF16)
        wub[...] = wu_ref[0].astype(BF16)
        wdb[...] = wd_ref[0].astype(BF16)

    @pl.when(i < n_used_ref[0])
    def _run():
        xb = _load_token_tiles(x_ref).astype(BF16)
        g = _dot(xb, wgb[...])
        u = _dot(xb, wub[...])
        act = (jax.nn.silu(g) * u).astype(BF16)
        _store_token_tiles(y_ref, _dot(act, wdb[...]))

    @pl.when(i >= n_used_ref[0])
    def _skip():
        y_ref[...] = jnp.zeros_like(y_ref)


def _expert_call(blk_e, n_used, xbuf, we_gate, we_up, we_down):
    p = xbuf.shape[0] // SUBLANES
    assert p % MOE_TM == 0
    tile_spec = pl.BlockSpec((MOE_TM * SUBLANES, LANES), lambda i, be, nu: (i, 0))
    grid_spec = pltpu.PrefetchScalarGridSpec(
        num_scalar_prefetch=2,
        grid=(p // MOE_TM,),
        in_specs=[tile_spec,
                  pl.BlockSpec((1, D_MODEL, D_EXPERT), lambda i, be, nu: (be[i], 0, 0)),
                  pl.BlockSpec((1, D_MODEL, D_EXPERT), lambda i, be, nu: (be[i], 0, 0)),
                  pl.BlockSpec((1, D_EXPERT, D_MODEL), lambda i, be, nu: (be[i], 0, 0))],
        out_specs=tile_spec,
        scratch_shapes=[pltpu.VMEM((D_MODEL, D_EXPERT), BF16), pltpu.VMEM((D_MODEL, D_EXPERT), BF16),
                        pltpu.VMEM((D_EXPERT, D_MODEL), BF16)],
    )
    return pl.pallas_call(
        _expert_kernel,
        grid_spec=grid_spec,
        out_shape=jax.ShapeDtypeStruct((p * SUBLANES, LANES), F32),
        compiler_params=pltpu.CompilerParams(dimension_semantics=("arbitrary",),
                                             vmem_limit_bytes=VMEM_LIMIT_BYTES),
        name="experts",
    )(blk_e, n_used, xbuf, we_gate, we_up, we_down)


def _final_kernel(dest_ref, dest_next_ref, x1_ref, ybuf_ref, route_ref, g2_ref, l2g_ref, l2b_ref, out_ref,
                  yg, sems):
    n = x1_ref.shape[0]
    i = pl.program_id(0)
    slot = i % 2

    def gather(idx_ref, s):
        def start_rows(t, _):
            for k in range(2):
                pltpu.make_async_copy(ybuf_ref.at[_tok_rows(idx_ref[2 * t + k])], yg.at[s, k, _tok_rows(t)],
                                      sems.at[s]).start()
            return 0
        lax.fori_loop(0, n, start_rows, 0, unroll=4)

    @pl.when(i == 0)
    def _first():
        gather(dest_ref, 0)

    @pl.when(i + 1 < pl.num_programs(0))
    def _prefetch():
        gather(dest_next_ref, 1 - slot)

    for k in range(2):
        pltpu.make_async_copy(ybuf_ref.at[_tok_rows(0, n)], yg.at[slot, k], sems.at[slot]).wait()
    route = route_ref[...]
    ff = (_load_token_tiles(yg.at[slot, 0]) * route[:, 2:3]
          + _load_token_tiles(yg.at[slot, 1]) * route[:, 3:4])
    out_ref[...] = _ln(ALPHA * x1_ref[...] + g2_ref[0] * ff) * l2g_ref[...] + l2b_ref[...]


def _final_call(dest_flat, x1, ybuf, route, g2, ln2_g, ln2_b, rows_per_g2, tile):
    n = x1.shape[0]
    assert n % tile == 0 and rows_per_g2 % tile == 0
    per = rows_per_g2 // tile
    g2_rows = g2.shape[1]
    steps = n // tile
    return pl.pallas_call(
        _final_kernel,
        grid=(steps,),
        in_specs=[pl.BlockSpec((2 * tile,), lambda i: (i,), memory_space=pltpu.SMEM),
                  pl.BlockSpec((2 * tile,), lambda i: (jnp.minimum(i + 1, steps - 1),), memory_space=pltpu.SMEM),
                  pl.BlockSpec((tile, D_MODEL), lambda i: (i, 0)),
                  pl.BlockSpec(memory_space=pl.ANY),
                  pl.BlockSpec((tile, LANES), lambda i: (i, 0)),
                  pl.BlockSpec((1, g2_rows, D_MODEL), lambda i: (i // per, 0, 0)),
                  pl.BlockSpec((1, D_MODEL), lambda i: (0, 0)),
                  pl.BlockSpec((1, D_MODEL), lambda i: (0, 0))],
        out_specs=pl.BlockSpec((tile, D_MODEL), lambda i: (i, 0)),
        out_shape=jax.ShapeDtypeStruct((n, D_MODEL), F32),
        scratch_shapes=[pltpu.VMEM((2, 2, tile * SUBLANES, LANES), F32), pltpu.SemaphoreType.DMA((2,))],
        compiler_params=pltpu.CompilerParams(dimension_semantics=("arbitrary",)),
        name="final",
    )(dest_flat, dest_flat, x1, ybuf, route, g2, ln2_g, ln2_b)


def _block_diag_gate_weights(wa, wx):
    def bd(w):
        w = w.reshape(N_RG_TILES, RG_GROUP, RG_BW, RG_BW)
        eye = jnp.eye(RG_GROUP, dtype=w.dtype)
        return jnp.einsum('tgcd,gh->tgchd', w, eye).reshape(N_RG_TILES, MXU_DIM, MXU_DIM)
    return jnp.concatenate([bd(wa), bd(wx)], axis=2).astype(BF16)


def kernel(x_prompt, x_sample, state_conv, state_rglru, state_gla, c_prompt, c_sample, w_mod, b_mod, w_in, b_in,
           conv_w, conv_b, rg_wa, rg_ba, rg_wx, rg_bx, rg_lambda, gla_wa2, gla_ba, gla_norm_g, p_a, p_b, w_o,
           ln1_g, ln1_b, w_grp, b_grp, w_exp, b_exp, we_gate, we_up, we_down, ln2_g, ln2_b):
    assert w_mod.shape[0] == DEPTH == 1
    bp, tp, _ = x_prompt.shape
    ns = x_sample.shape[0]
    n_p = bp * tp
    n_tok = n_p + ns
    row = lambda a: a.reshape(1, -1)

    lo = 2 * D_RNN + 2 * GLA_DK + 2 * GLA_DV
    w_in0 = w_in[0]
    win_main = jnp.concatenate([w_in0[:, :lo], w_in0[:, lo + GLA_RANK:]], axis=1).astype(BF16)
    bin_main = row(jnp.concatenate([b_in[0, :lo], b_in[0, lo + GLA_RANK:]]))
    walr = jnp.pad(w_in0[:, lo:lo + GLA_RANK], ((0, 0), (0, LANES - GLA_RANK))).astype(BF16)
    balr = row(jnp.pad(b_in[0, lo:lo + GLA_RANK], (0, LANES - GLA_RANK)))
    wa2 = jnp.pad(gla_wa2[0], ((0, LANES - GLA_RANK), (0, 0))).astype(BF16)
    wrg = _block_diag_gate_weights(rg_wa[0], rg_wx[0])
    w_route = jnp.pad(jnp.concatenate([w_exp[0], w_grp[0]], axis=1),
                      ((0, 0), (0, LANES - N_EXPERTS - N_GROUPS))).astype(BF16)
    b_route = row(jnp.pad(jnp.concatenate([b_exp[0], b_grp[0]]), (0, LANES - N_EXPERTS - N_GROUPS)))
    pa, pb, wo = p_a[0].astype(BF16), p_b[0].astype(BF16), w_o[0].astype(BF16)
    pre_w = [win_main, bin_main, walr, balr, wa2, row(gla_ba[0])]
    rec_w = [conv_w[0], row(conv_b[0]), wrg, row(rg_ba[0]), row(rg_bx[0]), row(rg_lambda[0])]
    post_w = [row(gla_norm_g[0]), pa, pb, wo, row(ln1_g[0]), row(ln1_b[0]), w_route, b_route]

    mod = _mod_call(jnp.concatenate([c_prompt, c_sample], axis=0), w_mod[0], row(b_mod[0]))
    mod_p = mod[:bp].reshape(bp, 6, D_MODEL)
    mod_s = mod[bp:].reshape(ns, 6, D_MODEL).transpose(1, 0, 2)

    mix_w = pre_w[:5] + [pre_w[5], post_w[0]] + rec_w + post_w[1:]
    x1_p, hf_p, route_p, conv_p, h_p, s_p = _mixer_call(x_prompt, mod_p, mix_w)

    xs = x_sample.reshape(ns, D_MODEL)
    sconv = state_conv[0].transpose(1, 0, 2)
    conv_s, h_s, ya_s, q_s, k_s, v_s, la_s, gg_s, ga_s, gb_s = _s_pre_call(
        xs, mod_s, sconv, state_rglru[0], pre_w + rec_w)
    s_s, o_s = _s_state_call(state_gla[0], q_s, k_s, v_s, la_s)
    x1_s, hf_s, route_s = _s_post_call(xs, mod_s, o_s, gg_s, ga_s, gb_s, ya_s, post_w)

    route_pf = route_p.reshape(n_p, LANES)
    eid = jnp.concatenate([route_pf[:, :2], route_s[:, :2]], axis=0).astype(I32)
    assert n_tok % LANES == 0
    n_chunks = n_tok // LANES
    eid3 = eid.T.reshape(2, n_chunks, LANES).transpose(1, 0, 2)
    rank3, cnt = _rank_call(eid3)
    rank = rank3.transpose(1, 0, 2).reshape(2, n_tok).T
    counts = cnt[:, 0]
    pcounts = (counts + MOE_TM - 1) // MOE_TM * MOE_TM
    pends = jnp.cumsum(pcounts)
    pstarts = pends - pcounts
    dest = (pstarts[eid] + rank).reshape(-1).astype(I32)
    n_tiles = -(-(2 * n_tok + N_EXPERTS * (MOE_TM - 1)) // MOE_TM)
    tile_start = jnp.arange(n_tiles, dtype=I32) * MOE_TM
    blk_e = jnp.minimum(jnp.sum(pends[None, :] <= tile_start[:, None], axis=1), N_EXPERTS - 1).astype(I32)
    n_used = (pends[-1] // MOE_TM).astype(I32).reshape(1)

    pends32 = pends.astype(I32)
    xbuf = _dispatch_call(pends32, dest, hf_p.reshape(n_p * SUBLANES, LANES), hf_s, n_tiles * MOE_TM)
    ybuf = _expert_call(blk_e, n_used, xbuf, we_gate[0], we_up[0], we_down[0])

    y_p = _final_call(dest[:2 * n_p], x1_p.reshape(n_p, D_MODEL), ybuf, route_pf, mod_p[:, 5:6, :],
                      row(ln2_g[0]), row(ln2_b[0]), tp, ROW_CHUNK)
    y_s = _final_call(dest[2 * n_p:], x1_s, ybuf, route_s, mod_s[5][None], row(ln2_g[0]), row(ln2_b[0]), ns, ns)

    return (y_p.reshape(bp, tp, D_MODEL), y_s.reshape(ns, 1, D_MODEL),
            conv_p[None], h_p.reshape(1, bp, D_RNN), s_p[None],
            conv_s.transpose(1, 0, 2)[None], h_s[None], s_s[None])
```

```python
import functools

import jax
import jax.numpy as jnp
import numpy as np
from jax import lax
from jax.experimental import pallas as pl
from jax.experimental.pallas import tpu as pltpu

F32 = jnp.float32
BF16 = jnp.bfloat16
I32 = jnp.int32

D_MODEL = 1024
D_RNN = D_MODEL
RG_BLOCKS = 16
RG_BW = D_RNN // RG_BLOCKS
CONV_W = 4
RG_C = 8.0
GLA_HEADS = 4
GLA_DK = D_MODEL // 2
GLA_DV = D_MODEL
GLA_DKH = GLA_DK // GLA_HEADS
GLA_DVH = GLA_DV // GLA_HEADS
GLA_RANK = 16
GLA_TAU = 16.0
GLA_CHUNK = 64
N_GROUPS = 4
EXP_PER_GROUP = 8
N_EXPERTS = N_GROUPS * EXP_PER_GROUP
D_EXPERT = 512
DEPTH = 1
ALPHA = (2.0 * DEPTH) ** 0.25
LN_EPS = 1e-5
RMS_EPS = 1e-6

LANES = 128
SUBLANES = 8
MXU_DIM = 256
VMEM_LIMIT_BYTES = 56 * 1024 * 1024

MIX_TT = 256
STAGE1_COLS = 512
PIPE_DEPTH = 2
MOE_TM = 256
ROW_CHUNK = 128
SAMPLE_SB = 8
RG_GROUP = MXU_DIM // RG_BW
N_RG_TILES = RG_BLOCKS // RG_GROUP
N_MAIN = 2 * D_RNN + 2 * GLA_DK + 2 * GLA_DV + 2 * D_MODEL
ROUTE_G0 = N_EXPERTS

O_RX, O_RY, O_Q, O_K, O_V, O_GG, O_GA, O_GB = 0, 1024, 2048, 2560, 3072, 4096, 5120, 6144


def _ln(x):
    mu = jnp.mean(x, -1, keepdims=True)
    xc = x - mu
    var = jnp.mean(xc * xc, -1, keepdims=True)
    return xc * lax.rsqrt(var + LN_EPS)


def _dot(a, b):
    return jnp.dot(a, b, preferred_element_type=F32)


def _pack_rows(w):
    *lead, k, n = w.shape
    pairs = jnp.swapaxes(w.reshape(*lead, k // 2, 2, n), -1, -2)
    return lax.bitcast_convert_type(pairs, jnp.uint32)


def _w(packed):
    return pltpu.bitcast(packed, BF16)


def _expm1(x):
    u = jnp.exp(x)
    small = (u - 1.0) * x / jnp.log(u)
    return jnp.where(u == 1.0, x, jnp.where(jnp.abs(x) < 0.5, small, u - 1.0))


def _const_spec(shape):
    nd = len(shape)
    return pl.BlockSpec(shape, lambda *_: (0,) * nd, pipeline_mode=pl.Buffered(1))


assert D_MODEL == SUBLANES * LANES


def _tok_rows(t, count=1):
    return pl.ds(pl.multiple_of(t * SUBLANES, SUBLANES), count * SUBLANES)


def _store_token_tiles(ref, rows):
    n = rows.shape[0]
    for s in range(SUBLANES):
        ref[pl.ds(s, n, stride=SUBLANES), :] = rows[:, s * LANES:(s + 1) * LANES]


def _load_token_tiles(ref):
    n = ref.shape[0] // SUBLANES
    return jnp.concatenate([ref[pl.ds(s, n, stride=SUBLANES), :] for s in range(SUBLANES)], axis=1)


def _mod_kernel(c_ref, w_ref, b_ref, o_ref):
    c = c_ref[...]
    s = jax.nn.silu(c)
    o_ref[...] = _dot(s.astype(BF16), w_ref[...].astype(BF16)) + b_ref[...]


def _mod_call(c_all, w_mod, b_mod):
    n = c_all.shape[0]
    tn = 512
    return pl.pallas_call(
        _mod_kernel,
        grid=(6 * D_MODEL // tn,),
        in_specs=[pl.BlockSpec((n, D_MODEL), lambda i: (0, 0)),
                  pl.BlockSpec((D_MODEL, tn), lambda i: (0, i)),
                  pl.BlockSpec((1, tn), lambda i: (0, i))],
        out_specs=pl.BlockSpec((n, tn), lambda i: (0, i)),
        out_shape=jax.ShapeDtypeStruct((n, 6 * D_MODEL), F32),
        compiler_params=pltpu.CompilerParams(dimension_semantics=("arbitrary",)),
        name="mod",
    )(c_all, w_mod, b_mod)


def _no_op():
    pass


def _gate_logits(xc, wrg_ref, rba, rbx):
    xcb = xc.astype(BF16)
    parts = [_dot(xcb[:, c * MXU_DIM:(c + 1) * MXU_DIM], _w(wrg_ref[c])) for c in range(N_RG_TILES)]
    r_pre = jnp.concatenate([p[:, :MXU_DIM] for p in parts], axis=1)
    i_pre = jnp.concatenate([p[:, MXU_DIM:] for p in parts], axis=1)
    return r_pre, i_pre, rba, rbx


def _gate_logits_to(out_ref, xc, wrg_ref):
    xcb = xc.astype(BF16)
    for c in range(N_RG_TILES):
        p = _dot(xcb[:, c * MXU_DIM:(c + 1) * MXU_DIM], _w(wrg_ref[c]))
        out_ref[:, c * MXU_DIM:(c + 1) * MXU_DIM] = p[:, :MXU_DIM]
        out_ref[:, D_RNN + c * MXU_DIM:D_RNN + (c + 1) * MXU_DIM] = p[:, MXU_DIM:]


def _gates(xc, r_pre, i_pre, rba, rbx, lam, between=_no_op):
    r = jax.nn.sigmoid(r_pre + rba)
    between()
    ig = jax.nn.sigmoid(i_pre + rbx)
    log_a = -RG_C * r * jax.nn.softplus(-lam)
    a = jnp.exp(log_a)
    between()
    u = jnp.sqrt(-_expm1(2.0 * log_a)) * (ig * xc)
    between()
    return a, u


def _low_rank(hb, walr_ref, balr_ref):
    return _dot(hb, _w(walr_ref[...])) + balr_ref[...]


def _log_decay_from(alr, wa2_ref, gba_ref):
    return jax.nn.log_sigmoid(_dot(alr.astype(BF16), _w(wa2_ref[...])) + gba_ref[...]) / GLA_TAU


def _log_decay(hb, walr_ref, balr_ref, wa2_ref, gba_ref):
    return _log_decay_from(_low_rank(hb, walr_ref, balr_ref), wa2_ref, gba_ref)


def _post(x, o, gg, ga, gb, y_a, g1, sh2, sc2, gng, pa_ref, pb_ref, wo_ref, l1g, l1b, wr_ref, br,
          between=_no_op):
    rows = x.shape[0]
    heads = []
    for hh in range(GLA_HEADS):
        oh = o[:, hh * GLA_DVH:(hh + 1) * GLA_DVH]
        ms = jnp.mean(oh * oh, -1, keepdims=True)
        heads.append(oh * lax.rsqrt(ms + RMS_EPS) * gng)
    y_b = jnp.concatenate(heads, axis=1) * jax.nn.silu(gg)
    merged = (jax.nn.sigmoid(ga) * _dot(y_a.astype(BF16), _w(pa_ref[...]))
              + jax.nn.sigmoid(gb) * _dot(y_b.astype(BF16), _w(pb_ref[...])))
    mix = _dot(merged.astype(BF16), _w(wo_ref[...]))
    between()
    x1 = _ln(ALPHA * x + g1 * mix) * l1g + l1b
    between()
    hf = _ln(x1) * (1.0 + sc2) + sh2
    logits = _dot(hf.astype(BF16), _w(wr_ref[...])) + br
    lane = lax.broadcasted_iota(I32, (rows, LANES), 1).astype(F32)
    neg = jnp.float32(-jnp.inf)
    big = jnp.float32(LANES)
    g_valid = (lane >= ROUTE_G0) & (lane < ROUTE_G0 + N_GROUPS)
    gl = jnp.where(g_valid, logits, neg)
    gmax = jnp.max(gl, -1, keepdims=True)
    g_lane = jnp.min(jnp.where(gl == gmax, lane, big), -1, keepdims=True)
    g_w = 1.0 / jnp.sum(jnp.exp(gl - gmax), -1, keepdims=True)
    e_lo = (g_lane - ROUTE_G0) * EXP_PER_GROUP
    el = jnp.where((lane >= e_lo) & (lane < e_lo + EXP_PER_GROUP), logits, neg)
    t1 = jnp.max(el, -1, keepdims=True)
    i1 = jnp.min(jnp.where(el == t1, lane, big), -1, keepdims=True)
    el2 = jnp.where(lane == i1, neg, el)
    t2 = jnp.max(el2, -1, keepdims=True)
    i2 = jnp.min(jnp.where(el2 == t2, lane, big), -1, keepdims=True)
    e2 = jnp.exp(t2 - t1)
    den = 1.0 + e2
    w1 = (1.0 / den) * g_w
    w2 = (e2 / den) * g_w
    route = jnp.where(lane == 0, i1.astype(F32),
                      jnp.where(lane == 1, i2.astype(F32),
                                jnp.where(lane == 2, w1, jnp.where(lane == 3, w2, 0.0))))
    return x1, hf, route


def _shift_rows(x, s):
    return pltpu.roll(x, s, 0)


def _mixer_kernel(xn_ref, modn_ref, x_ref, mod_ref, win_ref, bin_ref, walr_ref, balr_ref, wa2_ref, gba_ref,
                  gng_ref, cw_ref, cb_ref, wrg_ref, rba_ref, rbx_ref, lam_ref, pa_ref, pb_ref, wo_ref,
                  l1g_ref, l1b_ref, wr_ref, br_ref,
                  x1_ref, hf_ref, route_ref, convn_ref, hlast_ref, sfin_ref,
                  hbn_scr, hbc_scr, pn_scr, pc_scr, alrn_scr, alrc_scr, xcn_scr, xcc_scr, gaten_scr, gatec_scr,
                  rxbuf, hcar, st_ref, o_scr,
                  *, tiles_per_seq):
    i = pl.program_id(0)
    t = jnp.maximum(i - PIPE_DEPTH, 0)
    j = t % tiles_per_seq
    j1 = jnp.maximum(i - 1, 0) % tiles_per_seq
    tt = x_ref.shape[1]

    @pl.when(i == 0)
    def _no_tile_yet():
        hbc_scr[...] = jnp.zeros_like(hbc_scr)
        pc_scr[...] = jnp.zeros_like(pc_scr)
        alrc_scr[...] = jnp.zeros_like(alrc_scr)
        xcc_scr[...] = jnp.zeros_like(xcc_scr)
        gatec_scr[...] = jnp.zeros_like(gatec_scr)

    @pl.when(j1 == 0)
    def _init_conv():
        rxbuf[0:SUBLANES, :] = jnp.zeros((SUBLANES, D_RNN), F32)

    @pl.when(j == 0)
    def _init():
        hcar[...] = jnp.zeros_like(hcar)
        st_ref[...] = jnp.zeros_like(st_ref)

    def stage0():
        modn = modn_ref[0]
        hbn_scr[...] = (_ln(xn_ref[0]) * (1.0 + modn[1:2]) + modn[0:1]).astype(BF16)

    hbn = hbc_scr[...]

    def _proj_block(lo):
        def run():
            hi = lo + STAGE1_COLS
            pn_scr[:, lo:hi] = _dot(hbn, _w(win_ref[:, lo:hi]))
        return run

    def _alr_block():
        alrn_scr[...] = _dot(hbn, _w(walr_ref[...]))

    pending = [_proj_block(lo) for lo in range(0, N_MAIN, STAGE1_COLS)] + [_alr_block]

    def pump(count=1):
        for _ in range(min(count, len(pending))):
            pending.pop(0)()

    x = x_ref[0]
    mod = mod_ref[0]
    g1, sh2, sc2 = mod[2:3], mod[3:4], mod[4:5]

    def proj(lo, hi):
        return pc_scr[:, lo:hi] + bin_ref[:, lo:hi]

    pump(O_RY // STAGE1_COLS)

    rx = pn_scr[:, O_RX:O_RX + D_RNN] + bin_ref[:, O_RX:O_RX + D_RNN]
    rxbuf[SUBLANES:SUBLANES + tt, :] = rx
    cw = cw_ref[...]
    xcn = cb_ref[...] + rxbuf[SUBLANES - 3:SUBLANES - 3 + tt, :] * cw[0:1]
    xcn = xcn + rxbuf[SUBLANES - 2:SUBLANES - 2 + tt, :] * cw[1:2]
    xcn = xcn + rxbuf[SUBLANES - 1:SUBLANES - 1 + tt, :] * cw[2:3]
    xcn = xcn + rx * cw[3:4]
    rxbuf[0:SUBLANES, :] = rxbuf[tt:tt + SUBLANES, :]
    xcn_scr[...] = xcn
    xc = xcc_scr[...]

    q = proj(O_Q, O_Q + GLA_DK) * (GLA_DKH ** -0.5)
    k = proj(O_K, O_K + GLA_DK)
    v = proj(O_V, O_V + GLA_DV)
    la = _log_decay_from(alrc_scr[...] + balr_ref[...], wa2_ref, gba_ref)
    pump()
    rowk = lax.broadcasted_iota(I32, (tt, GLA_DK), 0) % GLA_CHUNK
    bcum = la
    s = 1
    while s < GLA_CHUNK:
        bcum = bcum + jnp.where(rowk >= s, _shift_rows(bcum, s), 0.0)
        s *= 2
    eb = jnp.exp(bcum)
    q_in = (q * eb).astype(BF16)
    k_in = (k * jnp.exp(-bcum)).astype(BF16)
    pump()
    stage0()
    gelu_ry = jax.nn.gelu(proj(O_RY, O_RY + D_RNN))
    pump()

    a, u = _gates(xc, gatec_scr[:, :D_RNN], gatec_scr[:, D_RNN:], rba_ref[...], rbx_ref[...], lam_ref[...],
                  between=pump)
    row = lax.broadcasted_iota(I32, (tt, D_RNN), 0)
    s = 1
    while s < tt:
        keep = row >= s
        a_s = jnp.where(keep, _shift_rows(a, s), 1.0)
        u_s = jnp.where(keep, _shift_rows(u, s), 0.0)
        u = a * u_s + u
        a = a * a_s
        if s < SUBLANES:
            pump()
        s *= 2
    hseq = a * hcar[0:1, :] + u
    hcar[0:1, :] = hseq[tt - 1:tt, :]
    y_a = hseq * gelu_ry

    tri = (lax.broadcasted_iota(I32, (GLA_CHUNK, GLA_CHUNK), 0)
           >= lax.broadcasted_iota(I32, (GLA_CHUNK, GLA_CHUNK), 1))
    n_chunks = tt // GLA_CHUNK
    nt_dims = (((1,), (1,)), ((), ()))
    attn, d_st, dec, vh_all = {}, {}, {}, {}
    for c in range(n_chunks):
        r0 = c * GLA_CHUNK
        for hh in range(GLA_HEADS):
            kc = slice(hh * GLA_DKH, (hh + 1) * GLA_DKH)
            b = bcum[r0:r0 + GLA_CHUNK, kc]
            btot = b[GLA_CHUNK - 1:GLA_CHUNK, :]
            k_out = (k[r0:r0 + GLA_CHUNK, kc] * jnp.exp(btot - b)).astype(BF16)
            vh = v[r0:r0 + GLA_CHUNK, hh * GLA_DVH:(hh + 1) * GLA_DVH].astype(BF16)
            scores = lax.dot_general(q_in[r0:r0 + GLA_CHUNK, kc], k_in[r0:r0 + GLA_CHUNK, kc], nt_dims,
                                     preferred_element_type=F32)
            attn[c, hh] = jnp.where(tri, scores, 0.0).astype(BF16)
            d_st[c, hh] = lax.dot_general(vh, k_out, (((0,), (0,)), ((), ())), preferred_element_type=F32)
            dec[c, hh] = jnp.exp(btot)
            vh_all[c, hh] = vh
        pump()
    starts = {}
    for hh in range(GLA_HEADS):
        st = st_ref[hh]
        for c in range(n_chunks):
            starts[c, hh] = st.astype(BF16)
            st = st * dec[c, hh] + d_st[c, hh]
        st_ref[hh] = st
    for c in range(n_chunks):
        r0 = c * GLA_CHUNK
        for hh in range(GLA_HEADS):
            qi = q_in[r0:r0 + GLA_CHUNK, hh * GLA_DKH:(hh + 1) * GLA_DKH]
            inter = lax.dot_general(qi, starts[c, hh], nt_dims, preferred_element_type=F32)
            o_scr[r0:r0 + GLA_CHUNK, hh * GLA_DVH:(hh + 1) * GLA_DVH] = _dot(attn[c, hh], vh_all[c, hh]) + inter

    x1, hf, route = _post(x, o_scr[...], proj(O_GG, O_GG + GLA_DV), proj(O_GA, O_GA + D_MODEL),
                          proj(O_GB, O_GB + D_MODEL), y_a, g1, sh2, sc2, gng_ref[...],
                          pa_ref, pb_ref, wo_ref, l1g_ref[...], l1b_ref[...], wr_ref, br_ref[...], between=pump)
    x1_ref[0] = x1
    _store_token_tiles(hf_ref.at[0], hf)
    route_ref[0] = route

    pump(len(pending))
    _gate_logits_to(gaten_scr, xcn, wrg_ref)
    pc_scr[:, O_RY:] = pn_scr[:, O_RY:]
    alrc_scr[...] = alrn_scr[...]
    hbc_scr[...] = hbn_scr[...]
    xcc_scr[...] = xcn_scr[...]
    gatec_scr[...] = gaten_scr[...]

    @pl.when((j1 == tiles_per_seq - 1) & (i >= 1))
    def _final_conv():
        convn_ref[0] = rxbuf[SUBLANES - (CONV_W - 1):SUBLANES, :]

    @pl.when((j == tiles_per_seq - 1) & (i >= PIPE_DEPTH))
    def _final():
        hlast_ref[0] = hcar[0:1, :]
        for hh in range(GLA_HEADS):
            sfin_ref[0, hh] = st_ref[hh].T


def _mixer_call(x, mod3, wts):
    b, t, _ = x.shape
    tt = MIX_TT
    assert t % tt == 0 and tt % GLA_CHUNK == 0
    nt = t // tt
    n_tiles = b * nt
    assert nt > 1
    cur = lambda i: jnp.maximum(i - PIPE_DEPTH, 0)
    nxt = lambda i: jnp.minimum(i, n_tiles - 1)
    tok = lambda i: (cur(i) // nt, cur(i) % nt, 0)
    tok_n = lambda i: (nxt(i) // nt, nxt(i) % nt, 0)
    per_b = lambda i: (cur(i) // nt, 0, 0)
    in_specs = [pl.BlockSpec((1, tt, D_MODEL), tok_n),
                pl.BlockSpec((1, 6, D_MODEL), lambda i: (nxt(i) // nt, 0, 0)),
                pl.BlockSpec((1, tt, D_MODEL), tok),
                pl.BlockSpec((1, 6, D_MODEL), per_b)] + [_const_spec(w.shape) for w in wts]
    out_specs = [pl.BlockSpec((1, tt, D_MODEL), tok),
                 pl.BlockSpec((1, tt * SUBLANES, LANES), tok),
                 pl.BlockSpec((1, tt, LANES), tok),
                 pl.BlockSpec((1, CONV_W - 1, D_RNN), per_b),
                 pl.BlockSpec((1, 1, D_RNN), per_b),
                 pl.BlockSpec((1, GLA_HEADS, GLA_DKH, GLA_DVH), lambda i: (cur(i) // nt, 0, 0, 0))]
    out_shape = [jax.ShapeDtypeStruct((b, t, D_MODEL), F32),
                 jax.ShapeDtypeStruct((b, t * SUBLANES, LANES), F32),
                 jax.ShapeDtypeStruct((b, t, LANES), F32),
                 jax.ShapeDtypeStruct((b, CONV_W - 1, D_RNN), F32),
                 jax.ShapeDtypeStruct((b, 1, D_RNN), F32),
                 jax.ShapeDtypeStruct((b, GLA_HEADS, GLA_DKH, GLA_DVH), F32)]
    scratch = [pltpu.VMEM((tt, D_MODEL), BF16), pltpu.VMEM((tt, D_MODEL), BF16),
               pltpu.VMEM((tt, N_MAIN), F32), pltpu.VMEM((tt, N_MAIN), F32),
               pltpu.VMEM((tt, LANES), F32), pltpu.VMEM((tt, LANES), F32),
               pltpu.VMEM((tt, D_RNN), F32), pltpu.VMEM((tt, D_RNN), F32),
               pltpu.VMEM((tt, 2 * D_RNN), F32), pltpu.VMEM((tt, 2 * D_RNN), F32),
               pltpu.VMEM((tt + SUBLANES, D_RNN), F32),
               pltpu.VMEM((SUBLANES, D_RNN), F32),
               pltpu.VMEM((GLA_HEADS, GLA_DVH, GLA_DKH), F32),
               pltpu.VMEM((tt, GLA_DV), F32)]
    return pl.pallas_call(
        functools.partial(_mixer_kernel, tiles_per_seq=nt),
        grid=(n_tiles + PIPE_DEPTH,),
        in_specs=in_specs,
        out_specs=out_specs,
        out_shape=out_shape,
        scratch_shapes=scratch,
        compiler_params=pltpu.CompilerParams(dimension_semantics=("arbitrary",),
                                             vmem_limit_bytes=VMEM_LIMIT_BYTES),
        name="mixer",
    )(x, mod3, x, mod3, *wts)


def _s_pre_kernel(x_ref, mod_ref, sconv_ref, h0_ref, win_ref, bin_ref, walr_ref, balr_ref, wa2_ref, gba_ref,
                  cw_ref, cb_ref, wrg_ref, rba_ref, rbx_ref, lam_ref,
                  convn_ref, hnew_ref, ya_ref, q_ref, k_ref, v_ref, la_ref, gg_ref, ga_ref, gb_ref):
    x = x_ref[...]
    hb = (_ln(x) * (1.0 + mod_ref[1]) + mod_ref[0]).astype(BF16)

    def proj(lo, hi):
        return _dot(hb, _w(win_ref[:, lo:hi])) + bin_ref[:, lo:hi]

    rx = proj(O_RX, O_RX + D_RNN)
    cw = cw_ref[...]
    xc = cb_ref[...] + sconv_ref[0] * cw[0:1]
    xc = xc + sconv_ref[1] * cw[1:2]
    xc = xc + sconv_ref[2] * cw[2:3]
    xc = xc + rx * cw[3:4]
    convn_ref[0] = sconv_ref[1]
    convn_ref[1] = sconv_ref[2]
    convn_ref[2] = rx
    a, u = _gates(xc, *_gate_logits(xc, wrg_ref, rba_ref[...], rbx_ref[...]), lam_ref[...])
    hnew = u + a * h0_ref[...]
    hnew_ref[...] = hnew
    ya_ref[...] = hnew * jax.nn.gelu(proj(O_RY, O_RY + D_RNN))
    q_ref[...] = proj(O_Q, O_Q + GLA_DK) * (GLA_DKH ** -0.5)
    k_ref[...] = proj(O_K, O_K + GLA_DK)
    v_ref[...] = proj(O_V, O_V + GLA_DV)
    la_ref[...] = _log_decay(hb, walr_ref, balr_ref, wa2_ref, gba_ref)
    gg_ref[...] = proj(O_GG, O_GG + GLA_DV)
    ga_ref[...] = proj(O_GA, O_GA + D_MODEL)
    gb_ref[...] = proj(O_GB, O_GB + D_MODEL)


def _s_pre_call(x_s, mod_s, sconv, h0, wts):
    n = x_s.shape[0]
    full = lambda shape: pl.BlockSpec(shape, lambda i: (0,) * len(shape))
    ins = [x_s, mod_s, sconv, h0] + list(wts)
    shapes = [(CONV_W - 1, n, D_RNN), (n, D_RNN), (n, D_RNN), (n, GLA_DK), (n, GLA_DK), (n, GLA_DV),
              (n, GLA_DK), (n, GLA_DV), (n, D_MODEL), (n, D_MODEL)]
    return pl.pallas_call(
        _s_pre_kernel,
        grid=(1,),
        in_specs=[full(a.shape) for a in ins],
        out_specs=[full(s) for s in shapes],
        out_shape=[jax.ShapeDtypeStruct(s, F32) for s in shapes],
        compiler_params=pltpu.CompilerParams(dimension_semantics=("arbitrary",),
                                             vmem_limit_bytes=VMEM_LIMIT_BYTES),
        name="s_pre",
    )(*ins)


def _to_column(row, n):
    eye = lax.broadcasted_iota(I32, (n, n), 0) == lax.broadcasted_iota(I32, (n, n), 1)
    return jnp.sum(jnp.where(eye, jnp.broadcast_to(row, (n, n)), 0.0), axis=1, keepdims=True)


def _s_state_kernel(s_ref, q_ref, k_ref, v_ref, la_ref, snew_ref, o_ref):
    sb = s_ref.shape[0]
    qb = q_ref[...].astype(BF16)
    dec = jnp.exp(la_ref[...])
    kk = k_ref[...]
    vv = v_ref[...]
    for si in range(sb):
        for hh in range(GLA_HEADS):
            kc = slice(hh * GLA_DKH, (hh + 1) * GLA_DKH)
            vc = slice(hh * GLA_DVH, (hh + 1) * GLA_DVH)
            d_col = _to_column(dec[si:si + 1, kc], GLA_DKH)
            k_col = _to_column(kk[si:si + 1, kc], GLA_DKH)
            s_new = s_ref[si, hh] * d_col + k_col * vv[si:si + 1, vc]
            snew_ref[si, hh] = s_new
            o_all = _dot(qb[:, kc], s_new.astype(BF16))
            o_ref[si:si + 1, vc] = o_all[si:si + 1, :]


def _s_state_call(state, q, k, v, la):
    n = state.shape[0]
    sb = SAMPLE_SB
    assert n % sb == 0
    blk = lambda w: pl.BlockSpec((sb, w), lambda i: (i, 0))
    st_spec = pl.BlockSpec((sb, GLA_HEADS, GLA_DKH, GLA_DVH), lambda i: (i, 0, 0, 0))
    return pl.pallas_call(
        _s_state_kernel,
        grid=(n // sb,),
        in_specs=[st_spec, blk(GLA_DK), blk(GLA_DK), blk(GLA_DV), blk(GLA_DK)],
        out_specs=[st_spec, blk(GLA_DV)],
        out_shape=[jax.ShapeDtypeStruct(state.shape, F32), jax.ShapeDtypeStruct((n, GLA_DV), F32)],
        compiler_params=pltpu.CompilerParams(dimension_semantics=("arbitrary",),
                                             vmem_limit_bytes=VMEM_LIMIT_BYTES),
        name="s_state",
    )(state, q, k, v, la)


def _s_post_kernel(x_ref, mod_ref, o_ref, gg_ref, ga_ref, gb_ref, ya_ref, gng_ref, pa_ref, pb_ref, wo_ref,
                   l1g_ref, l1b_ref, wr_ref, br_ref, x1_ref, hf_ref, route_ref):
    x1, hf, route = _post(x_ref[...], o_ref[...], gg_ref[...], ga_ref[...], gb_ref[...], ya_ref[...],
                          mod_ref[2], mod_ref[3], mod_ref[4], gng_ref[...], pa_ref, pb_ref, wo_ref,
                          l1g_ref[...], l1b_ref[...], wr_ref, br_ref[...])
    x1_ref[...] = x1
    _store_token_tiles(hf_ref, hf)
    route_ref[...] = route


def _s_post_call(x_s, mod_s, o, gg, ga, gb, ya, wts):
    n = x_s.shape[0]
    full = lambda shape: pl.BlockSpec(shape, lambda i: (0,) * len(shape))
    ins = [x_s, mod_s, o, gg, ga, gb, ya] + list(wts)
    shapes = [(n, D_MODEL), (n * SUBLANES, LANES), (n, LANES)]
    return pl.pallas_call(
        _s_post_kernel,
        grid=(1,),
        in_specs=[full(a.shape) for a in ins],
        out_specs=[full(s) for s in shapes],
        out_shape=[jax.ShapeDtypeStruct(s, F32) for s in shapes],
        compiler_params=pltpu.CompilerParams(dimension_semantics=("arbitrary",),
                                             vmem_limit_bytes=VMEM_LIMIT_BYTES),
        name="s_post",
    )(*ins)


def _rank_kernel(eid_ref, rank_ref, cnt_ref, carry):
    n_chunks = eid_ref.shape[0]
    carry[...] = jnp.zeros_like(carry)
    e_iota = lax.broadcasted_iota(I32, (N_EXPERTS, LANES), 0)
    upper = (lax.broadcasted_iota(I32, (LANES, LANES), 0)
             < lax.broadcasted_iota(I32, (LANES, LANES), 1)).astype(BF16)

    def body(c, _):
        ids = eid_ref[c]
        oh0 = (e_iota == ids[0:1, :])
        oh1 = (e_iota == ids[1:2, :])
        both = (oh0 | oh1).astype(F32)
        before = _dot(both.astype(BF16), upper) + carry[:, 0:1]
        r0 = jnp.sum(jnp.where(oh0, before, 0.0), axis=0, keepdims=True)
        r1 = jnp.sum(jnp.where(oh1, before, 0.0), axis=0, keepdims=True)
        rank_ref[c] = jnp.concatenate([r0, r1], axis=0).astype(I32)
        carry[...] = carry[...] + jnp.sum(both, axis=1, keepdims=True)
        return 0

    lax.fori_loop(0, n_chunks, body, 0)
    cnt_ref[...] = carry[...].astype(I32)


def _rank_call(eid3):
    n_chunks = eid3.shape[0]
    full = lambda shape: pl.BlockSpec(shape, lambda i: (0,) * len(shape))
    return pl.pallas_call(
        _rank_kernel,
        grid=(1,),
        in_specs=[full(eid3.shape)],
        out_specs=[full(eid3.shape), full((N_EXPERTS, LANES))],
        out_shape=[jax.ShapeDtypeStruct(eid3.shape, I32), jax.ShapeDtypeStruct((N_EXPERTS, LANES), I32)],
        scratch_shapes=[pltpu.VMEM((N_EXPERTS, LANES), F32)],
        compiler_params=pltpu.CompilerParams(dimension_semantics=("arbitrary",)),
        name="rank",
    )(eid3)


def _dispatch_kernel(pends_ref, dest_ref, src_p_ref, src_s_ref, buf_ref, sem, zeros_ref, zsem, *, n_p_chunks):
    i = pl.program_id(0)
    n_tiles = buf_ref.shape[0] // (MOE_TM * SUBLANES)

    @pl.when(i == 0)
    def _clear():
        zeros_ref[...] = jnp.zeros_like(zeros_ref)

        def tile_copy(row0):
            return pltpu.make_async_copy(zeros_ref, buf_ref.at[_tok_rows(row0, MOE_TM)], zsem)

        def nonempty(e):
            return pends_ref[e] > jnp.where(e == 0, 0, pends_ref[jnp.maximum(e - 1, 0)])

        def start_tail(e, _):
            @pl.when(nonempty(e))
            def _():
                tile_copy(pends_ref[e] - MOE_TM).start()
            return 0

        def wait_tail(e, _):
            @pl.when(nonempty(e))
            def _():
                tile_copy(0).wait()
            return 0

        def start_unused(j, _):
            tile_copy(j * MOE_TM).start()
            return 0

        def wait_unused(j, _):
            tile_copy(0).wait()
            return 0

        first_unused = pends_ref[N_EXPERTS - 1] // MOE_TM
        lax.fori_loop(0, N_EXPERTS, start_tail, 0)
        lax.fori_loop(first_unused, n_tiles, start_unused, 0)
        lax.fori_loop(0, N_EXPERTS, wait_tail, 0)
        lax.fori_loop(first_unused, n_tiles, wait_unused, 0)

    def scatter_rows(src_ref):
        n = src_ref.shape[0] // SUBLANES

        def start_rows(t, _):
            for k in range(2):
                pltpu.make_async_copy(src_ref.at[_tok_rows(t)], buf_ref.at[_tok_rows(dest_ref[2 * t + k])],
                                      sem).start(priority=k)
            return 0

        lax.fori_loop(0, n, start_rows, 0, unroll=4)
        for _ in range(2):
            pltpu.make_async_copy(src_ref, buf_ref.at[_tok_rows(0, n)], sem).wait()

    @pl.when(i < n_p_chunks)
    def _prompt():
        scatter_rows(src_p_ref)

    @pl.when(i >= n_p_chunks)
    def _sample():
        scatter_rows(src_s_ref)


def _dispatch_call(pends, dest_flat, src_p, src_s, n_rows):
    chunk_rows = ROW_CHUNK * SUBLANES
    assert src_p.shape[0] % chunk_rows == 0 and src_s.shape[0] % chunk_rows == 0 and n_rows % MOE_TM == 0
    n_p_chunks = src_p.shape[0] // chunk_rows
    n_s_chunks = src_s.shape[0] // chunk_rows
    tile = (chunk_rows, LANES)
    in_specs = [pl.BlockSpec((2 * ROW_CHUNK,), lambda i, pe: (i,), memory_space=pltpu.SMEM),
                pl.BlockSpec(tile, lambda i, pe: (jnp.minimum(i, n_p_chunks - 1), 0)),
                pl.BlockSpec(tile, lambda i, pe: (jnp.maximum(i - n_p_chunks, 0), 0))]
    scratch = [pltpu.SemaphoreType.DMA(()), pltpu.VMEM((MOE_TM * SUBLANES, LANES), F32),
               pltpu.SemaphoreType.DMA(())]
    return pl.pallas_call(
        functools.partial(_dispatch_kernel, n_p_chunks=n_p_chunks),
        grid_spec=pltpu.PrefetchScalarGridSpec(
            num_scalar_prefetch=1, grid=(n_p_chunks + n_s_chunks,), in_specs=in_specs,
            out_specs=pl.BlockSpec(memory_space=pl.ANY), scratch_shapes=scratch),
        out_shape=jax.ShapeDtypeStruct((n_rows * SUBLANES, LANES), F32),
        compiler_params=pltpu.CompilerParams(dimension_semantics=("arbitrary",), has_side_effects=True),
        name="dispatch",
    )(pends, dest_flat, src_p, src_s)


def _expert_kernel(blk_e_ref, n_used_ref, x_ref, wg_ref, wu_ref, wd_ref, y_ref, wgb, wub, wdb):
    i = pl.program_id(0)
    new_expert = (i == 0) | (blk_e_ref[i] != blk_e_ref[jnp.maximum(i - 1, 0)])

    @pl.when(new_expert)
    def _cast_weights():
        wgb[...] = wg_ref[0].astype(BF16)
        wub[...] = wu_ref[0].astype(BF16)
        wdb[...] = wd_ref[0].astype(BF16)

    @pl.when(i < n_used_ref[0])
    def _run():
        xb = _load_token_tiles(x_ref).astype(BF16)
        g = _dot(xb, wgb[...])
        u = _dot(xb, wub[...])
        act = (jax.nn.silu(g) * u).astype(BF16)
        _store_token_tiles(y_ref, _dot(act, wdb[...]))

    @pl.when(i >= n_used_ref[0])
    def _skip():
        y_ref[...] = jnp.zeros_like(y_ref)


def _expert_call(blk_e, n_used, xbuf, we_gate, we_up, we_down):
    p = xbuf.shape[0] // SUBLANES
    assert p % MOE_TM == 0
    tile_spec = pl.BlockSpec((MOE_TM * SUBLANES, LANES), lambda i, be, nu: (i, 0))
    grid_spec = pltpu.PrefetchScalarGridSpec(
        num_scalar_prefetch=2,
        grid=(p // MOE_TM,),
        in_specs=[tile_spec,
                  pl.BlockSpec((1, D_MODEL, D_EXPERT), lambda i, be, nu: (be[i], 0, 0)),
                  pl.BlockSpec((1, D_MODEL, D_EXPERT), lambda i, be, nu: (be[i], 0, 0)),
                  pl.BlockSpec((1, D_EXPERT, D_MODEL), lambda i, be, nu: (be[i], 0, 0))],
        out_specs=tile_spec,
        scratch_shapes=[pltpu.VMEM((D_MODEL, D_EXPERT), BF16), pltpu.VMEM((D_MODEL, D_EXPERT), BF16),
                        pltpu.VMEM((D_EXPERT, D_MODEL), BF16)],
    )
    return pl.pallas_call(
        _expert_kernel,
        grid_spec=grid_spec,
        out_shape=jax.ShapeDtypeStruct((p * SUBLANES, LANES), F32),
        compiler_params=pltpu.CompilerParams(dimension_semantics=("arbitrary",),
                                             vmem_limit_bytes=VMEM_LIMIT_BYTES),
        name="experts",
    )(blk_e, n_used, xbuf, we_gate, we_up, we_down)


def _final_kernel(dest_ref, dest_next_ref, x1_ref, ybuf_ref, route_ref, g2_ref, l2g_ref, l2b_ref, out_ref,
                  yg, sems):
    n = x1_ref.shape[0]
    i = pl.program_id(0)
    slot = i % 2

    def gather(idx_ref, s):
        def start_rows(t, _):
            for k in range(2):
                pltpu.make_async_copy(ybuf_ref.at[_tok_rows(idx_ref[2 * t + k])], yg.at[s, k, _tok_rows(t)],
                                      sems.at[s]).start(priority=k)
            return 0
        lax.fori_loop(0, n, start_rows, 0, unroll=4)

    @pl.when(i == 0)
    def _first():
        gather(dest_ref, 0)

    @pl.when(i + 1 < pl.num_programs(0))
    def _prefetch():
        gather(dest_next_ref, 1 - slot)

    for k in range(2):
        pltpu.make_async_copy(ybuf_ref.at[_tok_rows(0, n)], yg.at[slot, k], sems.at[slot]).wait()
    route = route_ref[...]
    ff = (_load_token_tiles(yg.at[slot, 0]) * route[:, 2:3]
          + _load_token_tiles(yg.at[slot, 1]) * route[:, 3:4])
    out_ref[...] = _ln(ALPHA * x1_ref[...] + g2_ref[0] * ff) * l2g_ref[...] + l2b_ref[...]


def _final_call(dest_flat, x1, ybuf, route, g2, ln2_g, ln2_b, rows_per_g2, tile):
    n = x1.shape[0]
    assert n % tile == 0 and rows_per_g2 % tile == 0
    per = rows_per_g2 // tile
    g2_rows = g2.shape[1]
    steps = n // tile
    return pl.pallas_call(
        _final_kernel,
        grid=(steps,),
        in_specs=[pl.BlockSpec((2 * tile,), lambda i: (i,), memory_space=pltpu.SMEM),
                  pl.BlockSpec((2 * tile,), lambda i: (jnp.minimum(i + 1, steps - 1),), memory_space=pltpu.SMEM),
                  pl.BlockSpec((tile, D_MODEL), lambda i: (i, 0)),
                  pl.BlockSpec(memory_space=pl.ANY),
                  pl.BlockSpec((tile, LANES), lambda i: (i, 0)),
                  pl.BlockSpec((1, g2_rows, D_MODEL), lambda i: (i // per, 0, 0)),
                  pl.BlockSpec((1, D_MODEL), lambda i: (0, 0)),
                  pl.BlockSpec((1, D_MODEL), lambda i: (0, 0))],
        out_specs=pl.BlockSpec((tile, D_MODEL), lambda i: (i, 0)),
        out_shape=jax.ShapeDtypeStruct((n, D_MODEL), F32),
        scratch_shapes=[pltpu.VMEM((2, 2, tile * SUBLANES, LANES), F32), pltpu.SemaphoreType.DMA((2,))],
        compiler_params=pltpu.CompilerParams(dimension_semantics=("arbitrary",)),
        name="final",
    )(dest_flat, dest_flat, x1, ybuf, route, g2, ln2_g, ln2_b)


def _block_diag_gate_weights(wa, wx):
    def bd(w):
        w = w.reshape(N_RG_TILES, RG_GROUP, RG_BW, RG_BW)
        eye = jnp.eye(RG_GROUP, dtype=w.dtype)
        return jnp.einsum('tgcd,gh->tgchd', w, eye).reshape(N_RG_TILES, MXU_DIM, MXU_DIM)
    return jnp.concatenate([bd(wa), bd(wx)], axis=2).astype(BF16)


def kernel(x_prompt, x_sample, state_conv, state_rglru, state_gla, c_prompt, c_sample, w_mod, b_mod, w_in, b_in,
           conv_w, conv_b, rg_wa, rg_ba, rg_wx, rg_bx, rg_lambda, gla_wa2, gla_ba, gla_norm_g, p_a, p_b, w_o,
           ln1_g, ln1_b, w_grp, b_grp, w_exp, b_exp, we_gate, we_up, we_down, ln2_g, ln2_b):
    assert w_mod.shape[0] == DEPTH == 1
    bp, tp, _ = x_prompt.shape
    ns = x_sample.shape[0]
    n_p = bp * tp
    n_tok = n_p + ns
    row = lambda a: a.reshape(1, -1)

    lo = 2 * D_RNN + 2 * GLA_DK + 2 * GLA_DV
    w_in0 = w_in[0]
    win_main = jnp.concatenate([w_in0[:, :lo], w_in0[:, lo + GLA_RANK:]], axis=1).astype(BF16)
    bin_main = row(jnp.concatenate([b_in[0, :lo], b_in[0, lo + GLA_RANK:]]))
    walr = jnp.pad(w_in0[:, lo:lo + GLA_RANK], ((0, 0), (0, LANES - GLA_RANK))).astype(BF16)
    balr = row(jnp.pad(b_in[0, lo:lo + GLA_RANK], (0, LANES - GLA_RANK)))
    wa2 = jnp.pad(gla_wa2[0], ((0, LANES - GLA_RANK), (0, 0))).astype(BF16)
    wrg = _block_diag_gate_weights(rg_wa[0], rg_wx[0])
    w_route = jnp.pad(jnp.concatenate([w_exp[0], w_grp[0]], axis=1),
                      ((0, 0), (0, LANES - N_EXPERTS - N_GROUPS))).astype(BF16)
    b_route = row(jnp.pad(jnp.concatenate([b_exp[0], b_grp[0]]), (0, LANES - N_EXPERTS - N_GROUPS)))
    pa, pb, wo = p_a[0].astype(BF16), p_b[0].astype(BF16), w_o[0].astype(BF16)
    win_main, walr, wa2, wrg, w_route, pa, pb, wo = map(_pack_rows, (win_main, walr, wa2, wrg, w_route, pa, pb, wo))
    pre_w = [win_main, bin_main, walr, balr, wa2, row(gla_ba[0])]
    rec_w = [conv_w[0], row(conv_b[0]), wrg, row(rg_ba[0]), row(rg_bx[0]), row(rg_lambda[0])]
    post_w = [row(gla_norm_g[0]), pa, pb, wo, row(ln1_g[0]), row(ln1_b[0]), w_route, b_route]

    mod = _mod_call(jnp.concatenate([c_prompt, c_sample], axis=0), w_mod[0], row(b_mod[0]))
    mod_p = mod[:bp].reshape(bp, 6, D_MODEL)
    mod_s = mod[bp:].reshape(ns, 6, D_MODEL).transpose(1, 0, 2)

    mix_w = pre_w[:5] + [pre_w[5], post_w[0]] + rec_w + post_w[1:]
    x1_p, hf_p, route_p, conv_p, h_p, s_p = _mixer_call(x_prompt, mod_p, mix_w)

    xs = x_sample.reshape(ns, D_MODEL)
    sconv = state_conv[0].transpose(1, 0, 2)
    conv_s, h_s, ya_s, q_s, k_s, v_s, la_s, gg_s, ga_s, gb_s = _s_pre_call(
        xs, mod_s, sconv, state_rglru[0], pre_w + rec_w)
    s_s, o_s = _s_state_call(state_gla[0], q_s, k_s, v_s, la_s)
    x1_s, hf_s, route_s = _s_post_call(xs, mod_s, o_s, gg_s, ga_s, gb_s, ya_s, post_w)

    route_pf = route_p.reshape(n_p, LANES)
    eid = jnp.concatenate([route_pf[:, :2], route_s[:, :2]], axis=0).astype(I32)
    assert n_tok % LANES == 0
    n_chunks = n_tok // LANES
    eid3 = eid.T.reshape(2, n_chunks, LANES).transpose(1, 0, 2)
    rank3, cnt = _rank_call(eid3)
    rank = rank3.transpose(1, 0, 2).reshape(2, n_tok).T
    counts = cnt[:, 0]
    pcounts = (counts + MOE_TM - 1) // MOE_TM * MOE_TM
    pends = jnp.cumsum(pcounts)
    pstarts = pends - pcounts
    dest = (pstarts[eid] + rank).reshape(-1).astype(I32)
    n_tiles = -(-(2 * n_tok + N_EXPERTS * (MOE_TM - 1)) // MOE_TM)
    tile_start = jnp.arange(n_tiles, dtype=I32) * MOE_TM
    blk_e = jnp.minimum(jnp.sum(pends[None, :] <= tile_start[:, None], axis=1), N_EXPERTS - 1).astype(I32)
    n_used = (pends[-1] // MOE_TM).astype(I32).reshape(1)

    pends32 = pends.astype(I32)
    xbuf = _dispatch_call(pends32, dest, hf_p.reshape(n_p * SUBLANES, LANES), hf_s, n_tiles * MOE_TM)
    ybuf = _expert_call(blk_e, n_used, xbuf, we_gate[0], we_up[0], we_down[0])

    y_p = _final_call(dest[:2 * n_p], x1_p.reshape(n_p, D_MODEL), ybuf, route_pf, mod_p[:, 5:6, :],
                      row(ln2_g[0]), row(ln2_b[0]), tp, ROW_CHUNK)
    y_s = _final_call(dest[2 * n_p:], x1_s, ybuf, route_s, mod_s[5][None], row(ln2_g[0]), row(ln2_b[0]), ns, ns)

    return (y_p.reshape(bp, tp, D_MODEL), y_s.reshape(ns, 1, D_MODEL),
            conv_p[None], h_p.reshape(1, bp, D_RNN), s_p[None],
            conv_s.transpose(1, 0, 2)[None], h_s[None], s_s[None])
```

```python
import functools

import jax
import jax.numpy as jnp
import numpy as np
from jax import lax
from jax.experimental import pallas as pl
from jax.experimental.pallas import tpu as pltpu

F32 = jnp.float32
BF16 = jnp.bfloat16
I32 = jnp.int32

D_MODEL = 1024
D_RNN = D_MODEL
RG_BLOCKS = 16
RG_BW = D_RNN // RG_BLOCKS
CONV_W = 4
RG_C = 8.0
GLA_HEADS = 4
GLA_DK = D_MODEL // 2
GLA_DV = D_MODEL
GLA_DKH = GLA_DK // GLA_HEADS
GLA_DVH = GLA_DV // GLA_HEADS
GLA_RANK = 16
GLA_TAU = 16.0
GLA_CHUNK = 64
N_GROUPS = 4
EXP_PER_GROUP = 8
N_EXPERTS = N_GROUPS * EXP_PER_GROUP
D_EXPERT = 512
DEPTH = 1
ALPHA = (2.0 * DEPTH) ** 0.25
LN_EPS = 1e-5
RMS_EPS = 1e-6

LANES = 128
SUBLANES = 8
MXU_DIM = 256
VMEM_LIMIT_BYTES = 56 * 1024 * 1024

MIX_TT = 256
STAGE1_COLS = 512
PIPE_DEPTH = 2
MOE_TM = 256
ROW_CHUNK = 128
SAMPLE_SB = 8
RG_GROUP = MXU_DIM // RG_BW
N_RG_TILES = RG_BLOCKS // RG_GROUP
N_MAIN = 2 * D_RNN + 2 * GLA_DK + 2 * GLA_DV + 2 * D_MODEL
ROUTE_G0 = N_EXPERTS

O_RX, O_RY, O_Q, O_K, O_V, O_GG, O_GA, O_GB = 0, 1024, 2048, 2560, 3072, 4096, 5120, 6144


def _ln(x):
    mu = jnp.mean(x, -1, keepdims=True)
    xc = x - mu
    var = jnp.mean(xc * xc, -1, keepdims=True)
    return xc * lax.rsqrt(var + LN_EPS)


def _dot(a, b):
    return jnp.dot(a, b, preferred_element_type=F32)


def _pack_kernel(w_ref, o_ref):
    o_ref[...] = pltpu.bitcast(w_ref[...].astype(BF16), jnp.uint32)


def _pack_rows(w, n=None):
    k = w.shape[0]
    n = w.shape[1] if n is None else n
    cn = min(n, 4 * LANES)
    assert k % (2 * SUBLANES) == 0 and n % cn == 0
    return pl.pallas_call(
        _pack_kernel,
        grid=(n // cn,),
        in_specs=[pl.BlockSpec((k, cn), lambda i: (0, i))],
        out_specs=pl.BlockSpec((k // 2, cn), lambda i: (0, i)),
        out_shape=jax.ShapeDtypeStruct((k // 2, n), jnp.uint32),
        compiler_params=pltpu.CompilerParams(dimension_semantics=("arbitrary",)),
        name="pack",
    )(w)


def _win(win_ref, winb_ref, lo, hi):
    if hi <= O_GA:
        return win_ref[:, lo:hi]
    assert lo >= O_GA
    return winb_ref[:, lo - O_GA:hi - O_GA]


def _w(packed):
    return pltpu.bitcast(packed, BF16)


def _expm1(x):
    u = jnp.exp(x)
    small = (u - 1.0) * x / jnp.log(u)
    return jnp.where(u == 1.0, x, jnp.where(jnp.abs(x) < 0.5, small, u - 1.0))


def _const_spec(shape):
    nd = len(shape)
    return pl.BlockSpec(shape, lambda *_: (0,) * nd, pipeline_mode=pl.Buffered(1))


assert D_MODEL == SUBLANES * LANES


def _tok_rows(t, count=1):
    return pl.ds(pl.multiple_of(t * SUBLANES, SUBLANES), count * SUBLANES)


def _store_token_tiles(ref, rows):
    n = rows.shape[0]
    for s in range(SUBLANES):
        ref[pl.ds(s, n, stride=SUBLANES), :] = rows[:, s * LANES:(s + 1) * LANES]


def _load_token_tiles(ref):
    n = ref.shape[0] // SUBLANES
    return jnp.concatenate([ref[pl.ds(s, n, stride=SUBLANES), :] for s in range(SUBLANES)], axis=1)


def _mod_kernel(c_ref, w_ref, b_ref, o_ref):
    c = c_ref[...]
    s = jax.nn.silu(c)
    o_ref[...] = _dot(s.astype(BF16), w_ref[...].astype(BF16)) + b_ref[...]


def _mod_call(c_all, w_mod, b_mod):
    n = c_all.shape[0]
    tn = 512
    return pl.pallas_call(
        _mod_kernel,
        grid=(6 * D_MODEL // tn,),
        in_specs=[pl.BlockSpec((n, D_MODEL), lambda i: (0, 0)),
                  pl.BlockSpec((D_MODEL, tn), lambda i: (0, i)),
                  pl.BlockSpec((1, tn), lambda i: (0, i))],
        out_specs=pl.BlockSpec((n, tn), lambda i: (0, i)),
        out_shape=jax.ShapeDtypeStruct((n, 6 * D_MODEL), F32),
        compiler_params=pltpu.CompilerParams(dimension_semantics=("arbitrary",)),
        name="mod",
    )(c_all, w_mod, b_mod)


def _no_op():
    pass


def _gate_logits(xc, wrg_ref, rba, rbx):
    xcb = xc.astype(BF16)
    parts = [_dot(xcb[:, c * MXU_DIM:(c + 1) * MXU_DIM], _w(wrg_ref[c])) for c in range(N_RG_TILES)]
    r_pre = jnp.concatenate([p[:, :MXU_DIM] for p in parts], axis=1)
    i_pre = jnp.concatenate([p[:, MXU_DIM:] for p in parts], axis=1)
    return r_pre, i_pre, rba, rbx


def _gate_logits_to(out_ref, xc, wrg_ref):
    xcb = xc.astype(BF16)
    for c in range(N_RG_TILES):
        p = _dot(xcb[:, c * MXU_DIM:(c + 1) * MXU_DIM], _w(wrg_ref[c]))
        out_ref[:, c * MXU_DIM:(c + 1) * MXU_DIM] = p[:, :MXU_DIM]
        out_ref[:, D_RNN + c * MXU_DIM:D_RNN + (c + 1) * MXU_DIM] = p[:, MXU_DIM:]


def _gates(xc, r_pre, i_pre, rba, rbx, lam, between=_no_op):
    r = jax.nn.sigmoid(r_pre + rba)
    between()
    ig = jax.nn.sigmoid(i_pre + rbx)
    log_a = -RG_C * r * jax.nn.softplus(-lam)
    a = jnp.exp(log_a)
    between()
    u = jnp.sqrt(-_expm1(2.0 * log_a)) * (ig * xc)
    between()
    return a, u


def _low_rank(hb, walr_ref, balr_ref):
    return _dot(hb, _w(walr_ref[...])) + balr_ref[...]


def _log_decay_from(alr, wa2_ref, gba_ref):
    return jax.nn.log_sigmoid(_dot(alr.astype(BF16), _w(wa2_ref[...])) + gba_ref[...]) / GLA_TAU


def _log_decay(hb, walr_ref, balr_ref, wa2_ref, gba_ref):
    return _log_decay_from(_low_rank(hb, walr_ref, balr_ref), wa2_ref, gba_ref)


def _post(x, o, gg, ga, gb, y_a, g1, sh2, sc2, gng, pa_ref, pb_ref, wo_ref, l1g, l1b, wr_ref, br,
          between=_no_op):
    rows = x.shape[0]
    heads = []
    for hh in range(GLA_HEADS):
        oh = o[:, hh * GLA_DVH:(hh + 1) * GLA_DVH]
        ms = jnp.mean(oh * oh, -1, keepdims=True)
        heads.append(oh * lax.rsqrt(ms + RMS_EPS) * gng)
    y_b = jnp.concatenate(heads, axis=1) * jax.nn.silu(gg)
    merged = (jax.nn.sigmoid(ga) * _dot(y_a.astype(BF16), _w(pa_ref[...]))
              + jax.nn.sigmoid(gb) * _dot(y_b.astype(BF16), _w(pb_ref[...])))
    mix = _dot(merged.astype(BF16), _w(wo_ref[...]))
    between()
    x1 = _ln(ALPHA * x + g1 * mix) * l1g + l1b
    between()
    hf = _ln(x1) * (1.0 + sc2) + sh2
    logits = _dot(hf.astype(BF16), _w(wr_ref[...])) + br
    lane = lax.broadcasted_iota(I32, (rows, LANES), 1).astype(F32)
    neg = jnp.float32(-jnp.inf)
    big = jnp.float32(LANES)
    g_valid = (lane >= ROUTE_G0) & (lane < ROUTE_G0 + N_GROUPS)
    gl = jnp.where(g_valid, logits, neg)
    gmax = jnp.max(gl, -1, keepdims=True)
    g_lane = jnp.min(jnp.where(gl == gmax, lane, big), -1, keepdims=True)
    g_w = 1.0 / jnp.sum(jnp.exp(gl - gmax), -1, keepdims=True)
    e_lo = (g_lane - ROUTE_G0) * EXP_PER_GROUP
    el = jnp.where((lane >= e_lo) & (lane < e_lo + EXP_PER_GROUP), logits, neg)
    t1 = jnp.max(el, -1, keepdims=True)
    i1 = jnp.min(jnp.where(el == t1, lane, big), -1, keepdims=True)
    el2 = jnp.where(lane == i1, neg, el)
    t2 = jnp.max(el2, -1, keepdims=True)
    i2 = jnp.min(jnp.where(el2 == t2, lane, big), -1, keepdims=True)
    e2 = jnp.exp(t2 - t1)
    den = 1.0 + e2
    w1 = (1.0 / den) * g_w
    w2 = (e2 / den) * g_w
    route = jnp.where(lane == 0, i1.astype(F32),
                      jnp.where(lane == 1, i2.astype(F32),
                                jnp.where(lane == 2, w1, jnp.where(lane == 3, w2, 0.0))))
    return x1, hf, route


def _shift_rows(x, s):
    return pltpu.roll(x, s, 0)


def _mixer_kernel(xn_ref, modn_ref, x_ref, mod_ref, win_ref, winb_ref, bin_ref, walr_ref, balr_ref, wa2_ref, gba_ref,
                  gng_ref, cw_ref, cb_ref, wrg_ref, rba_ref, rbx_ref, lam_ref, pa_ref, pb_ref, wo_ref,
                  l1g_ref, l1b_ref, wr_ref, br_ref,
                  x1_ref, hf_ref, route_ref, convn_ref, hlast_ref, sfin_ref,
                  hbn_scr, hbc_scr, pn_scr, pc_scr, alrn_scr, alrc_scr, xcn_scr, xcc_scr, gaten_scr, gatec_scr,
                  rxbuf, hcar, st_ref, o_scr,
                  *, tiles_per_seq):
    i = pl.program_id(0)
    t = jnp.maximum(i - PIPE_DEPTH, 0)
    j = t % tiles_per_seq
    j1 = jnp.maximum(i - 1, 0) % tiles_per_seq
    tt = x_ref.shape[1]

    @pl.when(i == 0)
    def _no_tile_yet():
        hbc_scr[...] = jnp.zeros_like(hbc_scr)
        pc_scr[...] = jnp.zeros_like(pc_scr)
        alrc_scr[...] = jnp.zeros_like(alrc_scr)
        xcc_scr[...] = jnp.zeros_like(xcc_scr)
        gatec_scr[...] = jnp.zeros_like(gatec_scr)

    @pl.when(j1 == 0)
    def _init_conv():
        rxbuf[0:SUBLANES, :] = jnp.zeros((SUBLANES, D_RNN), F32)

    @pl.when(j == 0)
    def _init():
        hcar[...] = jnp.zeros_like(hcar)
        st_ref[...] = jnp.zeros_like(st_ref)

    def stage0():
        modn = modn_ref[0]
        hbn_scr[...] = (_ln(xn_ref[0]) * (1.0 + modn[1:2]) + modn[0:1]).astype(BF16)

    hbn = hbc_scr[...]

    def _proj_block(lo):
        def run():
            hi = lo + STAGE1_COLS
            pn_scr[:, lo:hi] = _dot(hbn, _w(_win(win_ref, winb_ref, lo, hi)))
        return run

    def _alr_block():
        alrn_scr[...] = _dot(hbn, _w(walr_ref[...]))

    pending = [_proj_block(lo) for lo in range(0, N_MAIN, STAGE1_COLS)] + [_alr_block]

    def pump(count=1):
        for _ in range(min(count, len(pending))):
            pending.pop(0)()

    x = x_ref[0]
    mod = mod_ref[0]
    g1, sh2, sc2 = mod[2:3], mod[3:4], mod[4:5]

    def proj(lo, hi):
        return pc_scr[:, lo:hi] + bin_ref[:, lo:hi]

    pump(O_RY // STAGE1_COLS)

    rx = pn_scr[:, O_RX:O_RX + D_RNN] + bin_ref[:, O_RX:O_RX + D_RNN]
    rxbuf[SUBLANES:SUBLANES + tt, :] = rx
    cw = cw_ref[...]
    xcn = cb_ref[...] + rxbuf[SUBLANES - 3:SUBLANES - 3 + tt, :] * cw[0:1]
    xcn = xcn + rxbuf[SUBLANES - 2:SUBLANES - 2 + tt, :] * cw[1:2]
    xcn = xcn + rxbuf[SUBLANES - 1:SUBLANES - 1 + tt, :] * cw[2:3]
    xcn = xcn + rx * cw[3:4]
    rxbuf[0:SUBLANES, :] = rxbuf[tt:tt + SUBLANES, :]
    xcn_scr[...] = xcn
    xc = xcc_scr[...]

    q = proj(O_Q, O_Q + GLA_DK) * (GLA_DKH ** -0.5)
    k = proj(O_K, O_K + GLA_DK)
    v = proj(O_V, O_V + GLA_DV)
    la = _log_decay_from(alrc_scr[...] + balr_ref[...], wa2_ref, gba_ref)
    pump()
    rowk = lax.broadcasted_iota(I32, (tt, GLA_DK), 0) % GLA_CHUNK
    bcum = la
    s = 1
    while s < GLA_CHUNK:
        bcum = bcum + jnp.where(rowk >= s, _shift_rows(bcum, s), 0.0)
        s *= 2
    eb = jnp.exp(bcum)
    q_in = (q * eb).astype(BF16)
    k_in = (k * jnp.exp(-bcum)).astype(BF16)
    pump()
    stage0()
    gelu_ry = jax.nn.gelu(proj(O_RY, O_RY + D_RNN))
    pump()

    a, u = _gates(xc, gatec_scr[:, :D_RNN], gatec_scr[:, D_RNN:], rba_ref[...], rbx_ref[...], lam_ref[...],
                  between=pump)
    row = lax.broadcasted_iota(I32, (tt, D_RNN), 0)
    s = 1
    while s < tt:
        keep = row >= s
        a_s = jnp.where(keep, _shift_rows(a, s), 1.0)
        u_s = jnp.where(keep, _shift_rows(u, s), 0.0)
        u = a * u_s + u
        a = a * a_s
        if s < SUBLANES:
            pump()
        s *= 2
    hseq = a * hcar[0:1, :] + u
    hcar[0:1, :] = hseq[tt - 1:tt, :]
    y_a = hseq * gelu_ry

    tri = (lax.broadcasted_iota(I32, (GLA_CHUNK, GLA_CHUNK), 0)
           >= lax.broadcasted_iota(I32, (GLA_CHUNK, GLA_CHUNK), 1))
    n_chunks = tt // GLA_CHUNK
    nt_dims = (((1,), (1,)), ((), ()))
    attn, d_st, dec, vh_all = {}, {}, {}, {}
    for c in range(n_chunks):
        r0 = c * GLA_CHUNK
        for hh in range(GLA_HEADS):
            kc = slice(hh * GLA_DKH, (hh + 1) * GLA_DKH)
            b = bcum[r0:r0 + GLA_CHUNK, kc]
            btot = b[GLA_CHUNK - 1:GLA_CHUNK, :]
            k_out = (k[r0:r0 + GLA_CHUNK, kc] * jnp.exp(btot - b)).astype(BF16)
            vh = v[r0:r0 + GLA_CHUNK, hh * GLA_DVH:(hh + 1) * GLA_DVH].astype(BF16)
            scores = lax.dot_general(q_in[r0:r0 + GLA_CHUNK, kc], k_in[r0:r0 + GLA_CHUNK, kc], nt_dims,
                                     preferred_element_type=F32)
            attn[c, hh] = jnp.where(tri, scores, 0.0).astype(BF16)
            d_st[c, hh] = lax.dot_general(vh, k_out, (((0,), (0,)), ((), ())), preferred_element_type=F32)
            dec[c, hh] = jnp.exp(btot)
            vh_all[c, hh] = vh
        pump()
    starts = {}
    for hh in range(GLA_HEADS):
        st = st_ref[hh]
        for c in range(n_chunks):
            starts[c, hh] = st.astype(BF16)
            st = st * dec[c, hh] + d_st[c, hh]
        st_ref[hh] = st
    for c in range(n_chunks):
        r0 = c * GLA_CHUNK
        for hh in range(GLA_HEADS):
            qi = q_in[r0:r0 + GLA_CHUNK, hh * GLA_DKH:(hh + 1) * GLA_DKH]
            inter = lax.dot_general(qi, starts[c, hh], nt_dims, preferred_element_type=F32)
            o_scr[r0:r0 + GLA_CHUNK, hh * GLA_DVH:(hh + 1) * GLA_DVH] = _dot(attn[c, hh], vh_all[c, hh]) + inter

    x1, hf, route = _post(x, o_scr[...], proj(O_GG, O_GG + GLA_DV), proj(O_GA, O_GA + D_MODEL),
                          proj(O_GB, O_GB + D_MODEL), y_a, g1, sh2, sc2, gng_ref[...],
                          pa_ref, pb_ref, wo_ref, l1g_ref[...], l1b_ref[...], wr_ref, br_ref[...], between=pump)
    x1_ref[0] = x1
    _store_token_tiles(hf_ref.at[0], hf)
    route_ref[0] = route

    pump(len(pending))
    _gate_logits_to(gaten_scr, xcn, wrg_ref)
    pc_scr[:, O_RY:] = pn_scr[:, O_RY:]
    alrc_scr[...] = alrn_scr[...]
    hbc_scr[...] = hbn_scr[...]
    xcc_scr[...] = xcn_scr[...]
    gatec_scr[...] = gaten_scr[...]

    @pl.when((j1 == tiles_per_seq - 1) & (i >= 1))
    def _final_conv():
        convn_ref[0] = rxbuf[SUBLANES - (CONV_W - 1):SUBLANES, :]

    @pl.when((j == tiles_per_seq - 1) & (i >= PIPE_DEPTH))
    def _final():
        hlast_ref[0] = hcar[0:1, :]
        for hh in range(GLA_HEADS):
            sfin_ref[0, hh] = st_ref[hh].T


def _mixer_call(x, mod3, wts):
    b, t, _ = x.shape
    tt = MIX_TT
    assert t % tt == 0 and tt % GLA_CHUNK == 0
    nt = t // tt
    n_tiles = b * nt
    assert nt > 1
    cur = lambda i: jnp.maximum(i - PIPE_DEPTH, 0)
    nxt = lambda i: jnp.minimum(i, n_tiles - 1)
    tok = lambda i: (cur(i) // nt, cur(i) % nt, 0)
    tok_n = lambda i: (nxt(i) // nt, nxt(i) % nt, 0)
    per_b = lambda i: (cur(i) // nt, 0, 0)
    in_specs = [pl.BlockSpec((1, tt, D_MODEL), tok_n),
                pl.BlockSpec((1, 6, D_MODEL), lambda i: (nxt(i) // nt, 0, 0)),
                pl.BlockSpec((1, tt, D_MODEL), tok),
                pl.BlockSpec((1, 6, D_MODEL), per_b)] + [_const_spec(w.shape) for w in wts]
    out_specs = [pl.BlockSpec((1, tt, D_MODEL), tok),
                 pl.BlockSpec((1, tt * SUBLANES, LANES), tok),
                 pl.BlockSpec((1, tt, LANES), tok),
                 pl.BlockSpec((1, CONV_W - 1, D_RNN), per_b),
                 pl.BlockSpec((1, 1, D_RNN), per_b),
                 pl.BlockSpec((1, GLA_HEADS, GLA_DKH, GLA_DVH), lambda i: (cur(i) // nt, 0, 0, 0))]
    out_shape = [jax.ShapeDtypeStruct((b, t, D_MODEL), F32),
                 jax.ShapeDtypeStruct((b, t * SUBLANES, LANES), F32),
                 jax.ShapeDtypeStruct((b, t, LANES), F32),
                 jax.ShapeDtypeStruct((b, CONV_W - 1, D_RNN), F32),
                 jax.ShapeDtypeStruct((b, 1, D_RNN), F32),
                 jax.ShapeDtypeStruct((b, GLA_HEADS, GLA_DKH, GLA_DVH), F32)]
    scratch = [pltpu.VMEM((tt, D_MODEL), BF16), pltpu.VMEM((tt, D_MODEL), BF16),
               pltpu.VMEM((tt, N_MAIN), F32), pltpu.VMEM((tt, N_MAIN), F32),
               pltpu.VMEM((tt, LANES), F32), pltpu.VMEM((tt, LANES), F32),
               pltpu.VMEM((tt, D_RNN), F32), pltpu.VMEM((tt, D_RNN), F32),
               pltpu.VMEM((tt, 2 * D_RNN), F32), pltpu.VMEM((tt, 2 * D_RNN), F32),
               pltpu.VMEM((tt + SUBLANES, D_RNN), F32),
               pltpu.VMEM((SUBLANES, D_RNN), F32),
               pltpu.VMEM((GLA_HEADS, GLA_DVH, GLA_DKH), F32),
               pltpu.VMEM((tt, GLA_DV), F32)]
    return pl.pallas_call(
        functools.partial(_mixer_kernel, tiles_per_seq=nt),
        grid=(n_tiles + PIPE_DEPTH,),
        in_specs=in_specs,
        out_specs=out_specs,
        out_shape=out_shape,
        scratch_shapes=scratch,
        compiler_params=pltpu.CompilerParams(dimension_semantics=("arbitrary",),
                                             vmem_limit_bytes=VMEM_LIMIT_BYTES),
        name="mixer",
    )(x, mod3, x, mod3, *wts)


def _s_pre_kernel(x_ref, mod_ref, sconv_ref, h0_ref, win_ref, winb_ref, bin_ref, walr_ref, balr_ref, wa2_ref, gba_ref,
                  cw_ref, cb_ref, wrg_ref, rba_ref, rbx_ref, lam_ref,
                  convn_ref, hnew_ref, ya_ref, q_ref, k_ref, v_ref, la_ref, gg_ref, ga_ref, gb_ref):
    x = x_ref[...]
    hb = (_ln(x) * (1.0 + mod_ref[1]) + mod_ref[0]).astype(BF16)

    def proj(lo, hi):
        return _dot(hb, _w(_win(win_ref, winb_ref, lo, hi))) + bin_ref[:, lo:hi]

    rx = proj(O_RX, O_RX + D_RNN)
    cw = cw_ref[...]
    xc = cb_ref[...] + sconv_ref[0] * cw[0:1]
    xc = xc + sconv_ref[1] * cw[1:2]
    xc = xc + sconv_ref[2] * cw[2:3]
    xc = xc + rx * cw[3:4]
    convn_ref[0] = sconv_ref[1]
    convn_ref[1] = sconv_ref[2]
    convn_ref[2] = rx
    a, u = _gates(xc, *_gate_logits(xc, wrg_ref, rba_ref[...], rbx_ref[...]), lam_ref[...])
    hnew = u + a * h0_ref[...]
    hnew_ref[...] = hnew
    ya_ref[...] = hnew * jax.nn.gelu(proj(O_RY, O_RY + D_RNN))
    q_ref[...] = proj(O_Q, O_Q + GLA_DK) * (GLA_DKH ** -0.5)
    k_ref[...] = proj(O_K, O_K + GLA_DK)
    v_ref[...] = proj(O_V, O_V + GLA_DV)
    la_ref[...] = _log_decay(hb, walr_ref, balr_ref, wa2_ref, gba_ref)
    gg_ref[...] = proj(O_GG, O_GG + GLA_DV)
    ga_ref[...] = proj(O_GA, O_GA + D_MODEL)
    gb_ref[...] = proj(O_GB, O_GB + D_MODEL)


def _s_pre_call(x_s, mod_s, sconv, h0, wts):
    n = x_s.shape[0]
    full = lambda shape: pl.BlockSpec(shape, lambda i: (0,) * len(shape))
    ins = [x_s, mod_s, sconv, h0] + list(wts)
    shapes = [(CONV_W - 1, n, D_RNN), (n, D_RNN), (n, D_RNN), (n, GLA_DK), (n, GLA_DK), (n, GLA_DV),
              (n, GLA_DK), (n, GLA_DV), (n, D_MODEL), (n, D_MODEL)]
    return pl.pallas_call(
        _s_pre_kernel,
        grid=(1,),
        in_specs=[full(a.shape) for a in ins],
        out_specs=[full(s) for s in shapes],
        out_shape=[jax.ShapeDtypeStruct(s, F32) for s in shapes],
        compiler_params=pltpu.CompilerParams(dimension_semantics=("arbitrary",),
                                             vmem_limit_bytes=VMEM_LIMIT_BYTES),
        name="s_pre",
    )(*ins)


def _to_column(row, n):
    eye = lax.broadcasted_iota(I32, (n, n), 0) == lax.broadcasted_iota(I32, (n, n), 1)
    return jnp.sum(jnp.where(eye, jnp.broadcast_to(row, (n, n)), 0.0), axis=1, keepdims=True)


def _s_state_kernel(s_ref, q_ref, k_ref, v_ref, la_ref, snew_ref, o_ref):
    sb = s_ref.shape[0]
    qb = q_ref[...].astype(BF16)
    dec = jnp.exp(la_ref[...])
    kk = k_ref[...]
    vv = v_ref[...]
    for si in range(sb):
        for hh in range(GLA_HEADS):
            kc = slice(hh * GLA_DKH, (hh + 1) * GLA_DKH)
            vc = slice(hh * GLA_DVH, (hh + 1) * GLA_DVH)
            d_col = _to_column(dec[si:si + 1, kc], GLA_DKH)
            k_col = _to_column(kk[si:si + 1, kc], GLA_DKH)
            s_new = s_ref[si, hh] * d_col + k_col * vv[si:si + 1, vc]
            snew_ref[si, hh] = s_new
            o_all = _dot(qb[:, kc], s_new.astype(BF16))
            o_ref[si:si + 1, vc] = o_all[si:si + 1, :]


def _s_state_call(state, q, k, v, la):
    n = state.shape[0]
    sb = SAMPLE_SB
    assert n % sb == 0
    blk = lambda w: pl.BlockSpec((sb, w), lambda i: (i, 0))
    st_spec = pl.BlockSpec((sb, GLA_HEADS, GLA_DKH, GLA_DVH), lambda i: (i, 0, 0, 0))
    return pl.pallas_call(
        _s_state_kernel,
        grid=(n // sb,),
        in_specs=[st_spec, blk(GLA_DK), blk(GLA_DK), blk(GLA_DV), blk(GLA_DK)],
        out_specs=[st_spec, blk(GLA_DV)],
        out_shape=[jax.ShapeDtypeStruct(state.shape, F32), jax.ShapeDtypeStruct((n, GLA_DV), F32)],
        compiler_params=pltpu.CompilerParams(dimension_semantics=("arbitrary",),
                                             vmem_limit_bytes=VMEM_LIMIT_BYTES),
        name="s_state",
    )(state, q, k, v, la)


def _s_post_kernel(x_ref, mod_ref, o_ref, gg_ref, ga_ref, gb_ref, ya_ref, gng_ref, pa_ref, pb_ref, wo_ref,
                   l1g_ref, l1b_ref, wr_ref, br_ref, x1_ref, hf_ref, route_ref):
    x1, hf, route = _post(x_ref[...], o_ref[...], gg_ref[...], ga_ref[...], gb_ref[...], ya_ref[...],
                          mod_ref[2], mod_ref[3], mod_ref[4], gng_ref[...], pa_ref, pb_ref, wo_ref,
                          l1g_ref[...], l1b_ref[...], wr_ref, br_ref[...])
    x1_ref[...] = x1
    _store_token_tiles(hf_ref, hf)
    route_ref[...] = route


def _s_post_call(x_s, mod_s, o, gg, ga, gb, ya, wts):
    n = x_s.shape[0]
    full = lambda shape: pl.BlockSpec(shape, lambda i: (0,) * len(shape))
    ins = [x_s, mod_s, o, gg, ga, gb, ya] + list(wts)
    shapes = [(n, D_MODEL), (n * SUBLANES, LANES), (n, LANES)]
    return pl.pallas_call(
        _s_post_kernel,
        grid=(1,),
        in_specs=[full(a.shape) for a in ins],
        out_specs=[full(s) for s in shapes],
        out_shape=[jax.ShapeDtypeStruct(s, F32) for s in shapes],
        compiler_params=pltpu.CompilerParams(dimension_semantics=("arbitrary",),
                                             vmem_limit_bytes=VMEM_LIMIT_BYTES),
        name="s_post",
    )(*ins)


def _rank_kernel(eid_ref, rank_ref, cnt_ref, carry):
    n_chunks = eid_ref.shape[0]
    carry[...] = jnp.zeros_like(carry)
    e_iota = lax.broadcasted_iota(I32, (N_EXPERTS, LANES), 0)
    upper = (lax.broadcasted_iota(I32, (LANES, LANES), 0)
             < lax.broadcasted_iota(I32, (LANES, LANES), 1)).astype(BF16)

    def body(c, _):
        ids = eid_ref[c]
        oh0 = (e_iota == ids[0:1, :])
        oh1 = (e_iota == ids[1:2, :])
        both = (oh0 | oh1).astype(F32)
        before = _dot(both.astype(BF16), upper) + carry[:, 0:1]
        r0 = jnp.sum(jnp.where(oh0, before, 0.0), axis=0, keepdims=True)
        r1 = jnp.sum(jnp.where(oh1, before, 0.0), axis=0, keepdims=True)
        rank_ref[c] = jnp.concatenate([r0, r1], axis=0).astype(I32)
        carry[...] = carry[...] + jnp.sum(both, axis=1, keepdims=True)
        return 0

    lax.fori_loop(0, n_chunks, body, 0)
    cnt_ref[...] = carry[...].astype(I32)


def _rank_call(eid3):
    n_chunks = eid3.shape[0]
    full = lambda shape: pl.BlockSpec(shape, lambda i: (0,) * len(shape))
    return pl.pallas_call(
        _rank_kernel,
        grid=(1,),
        in_specs=[full(eid3.shape)],
        out_specs=[full(eid3.shape), full((N_EXPERTS, LANES))],
        out_shape=[jax.ShapeDtypeStruct(eid3.shape, I32), jax.ShapeDtypeStruct((N_EXPERTS, LANES), I32)],
        scratch_shapes=[pltpu.VMEM((N_EXPERTS, LANES), F32)],
        compiler_params=pltpu.CompilerParams(dimension_semantics=("arbitrary",)),
        name="rank",
    )(eid3)


def _dispatch_kernel(pends_ref, dest_ref, src_p_ref, src_s_ref, buf_ref, sem, zeros_ref, zsem, *, n_p_chunks):
    i = pl.program_id(0)
    n_tiles = buf_ref.shape[0] // (MOE_TM * SUBLANES)

    @pl.when(i == 0)
    def _clear():
        zeros_ref[...] = jnp.zeros_like(zeros_ref)

        def tile_copy(row0):
            return pltpu.make_async_copy(zeros_ref, buf_ref.at[_tok_rows(row0, MOE_TM)], zsem)

        def nonempty(e):
            return pends_ref[e] > jnp.where(e == 0, 0, pends_ref[jnp.maximum(e - 1, 0)])

        def start_tail(e, _):
            @pl.when(nonempty(e))
            def _():
                tile_copy(pends_ref[e] - MOE_TM).start()
            return 0

        def wait_tail(e, _):
            @pl.when(nonempty(e))
            def _():
                tile_copy(0).wait()
            return 0

        def start_unused(j, _):
            tile_copy(j * MOE_TM).start()
            return 0

        def wait_unused(j, _):
            tile_copy(0).wait()
            return 0

        first_unused = pends_ref[N_EXPERTS - 1] // MOE_TM
        lax.fori_loop(0, N_EXPERTS, start_tail, 0)
        lax.fori_loop(first_unused, n_tiles, start_unused, 0)
        lax.fori_loop(0, N_EXPERTS, wait_tail, 0)
        lax.fori_loop(first_unused, n_tiles, wait_unused, 0)

    def scatter_rows(src_ref):
        n = src_ref.shape[0] // SUBLANES

        def start_rows(t, _):
            for k in range(2):
                pltpu.make_async_copy(src_ref.at[_tok_rows(t)], buf_ref.at[_tok_rows(dest_ref[2 * t + k])],
                                      sem).start(priority=k)
            return 0

        lax.fori_loop(0, n, start_rows, 0, unroll=4)
        for _ in range(2):
            pltpu.make_async_copy(src_ref, buf_ref.at[_tok_rows(0, n)], sem).wait()

    @pl.when(i < n_p_chunks)
    def _prompt():
        scatter_rows(src_p_ref)

    @pl.when(i >= n_p_chunks)
    def _sample():
        scatter_rows(src_s_ref)


def _dispatch_call(pends, dest_flat, src_p, src_s, n_rows):
    chunk_rows = ROW_CHUNK * SUBLANES
    assert src_p.shape[0] % chunk_rows == 0 and src_s.shape[0] % chunk_rows == 0 and n_rows % MOE_TM == 0
    n_p_chunks = src_p.shape[0] // chunk_rows
    n_s_chunks = src_s.shape[0] // chunk_rows
    tile = (chunk_rows, LANES)
    in_specs = [pl.BlockSpec((2 * ROW_CHUNK,), lambda i, pe: (i,), memory_space=pltpu.SMEM),
                pl.BlockSpec(tile, lambda i, pe: (jnp.minimum(i, n_p_chunks - 1), 0)),
                pl.BlockSpec(tile, lambda i, pe: (jnp.maximum(i - n_p_chunks, 0), 0))]
    scratch = [pltpu.SemaphoreType.DMA(()), pltpu.VMEM((MOE_TM * SUBLANES, LANES), F32),
               pltpu.SemaphoreType.DMA(())]
    return pl.pallas_call(
        functools.partial(_dispatch_kernel, n_p_chunks=n_p_chunks),
        grid_spec=pltpu.PrefetchScalarGridSpec(
            num_scalar_prefetch=1, grid=(n_p_chunks + n_s_chunks,), in_specs=in_specs,
            out_specs=pl.BlockSpec(memory_space=pl.ANY), scratch_shapes=scratch),
        out_shape=jax.ShapeDtypeStruct((n_rows * SUBLANES, LANES), F32),
        compiler_params=pltpu.CompilerParams(dimension_semantics=("arbitrary",), has_side_effects=True),
        name="dispatch",
    )(pends, dest_flat, src_p, src_s)


def _expert_kernel(blk_e_ref, n_used_ref, x_ref, wg_ref, wu_ref, wd_ref, y_ref, wgb, wub, wdb):
    i = pl.program_id(0)
    new_expert = (i == 0) | (blk_e_ref[i] != blk_e_ref[jnp.maximum(i - 1, 0)])

    @pl.when(new_expert)
    def _cast_weights():
        wgb[...] = wg_ref[0].astype(BF16)
        wub[...] = wu_ref[0].astype(BF16)
        wdb[...] = wd_ref[0].astype(BF16)

    @pl.when(i < n_used_ref[0])
    def _run():
        xb = _load_token_tiles(x_ref).astype(BF16)
        g = _dot(xb, wgb[...])
        u = _dot(xb, wub[...])
        act = (jax.nn.silu(g) * u).astype(BF16)
        _store_token_tiles(y_ref, _dot(act, wdb[...]))

    @pl.when(i >= n_used_ref[0])
    def _skip():
        y_ref[...] = jnp.zeros_like(y_ref)


def _expert_call(blk_e, n_used, xbuf, we_gate, we_up, we_down):
    p = xbuf.shape[0] // SUBLANES
    assert p % MOE_TM == 0
    tile_spec = pl.BlockSpec((MOE_TM * SUBLANES, LANES), lambda i, be, nu: (i, 0))
    grid_spec = pltpu.PrefetchScalarGridSpec(
        num_scalar_prefetch=2,
        grid=(p // MOE_TM,),
        in_specs=[tile_spec,
                  pl.BlockSpec((1, D_MODEL, D_EXPERT), lambda i, be, nu: (be[i], 0, 0)),
                  pl.BlockSpec((1, D_MODEL, D_EXPERT), lambda i, be, nu: (be[i], 0, 0)),
                  pl.BlockSpec((1, D_EXPERT, D_MODEL), lambda i, be, nu: (be[i], 0, 0))],
        out_specs=tile_spec,
        scratch_shapes=[pltpu.VMEM((D_MODEL, D_EXPERT), BF16), pltpu.VMEM((D_MODEL, D_EXPERT), BF16),
                        pltpu.VMEM((D_EXPERT, D_MODEL), BF16)],
    )
    return pl.pallas_call(
        _expert_kernel,
        grid_spec=grid_spec,
        out_shape=jax.ShapeDtypeStruct((p * SUBLANES, LANES), F32),
        compiler_params=pltpu.CompilerParams(dimension_semantics=("arbitrary",),
                                             vmem_limit_bytes=VMEM_LIMIT_BYTES),
        name="experts",
    )(blk_e, n_used, xbuf, we_gate, we_up, we_down)


def _final_kernel(dest_ref, dest_next_ref, x1_ref, ybuf_ref, route_ref, g2_ref, l2g_ref, l2b_ref, out_ref,
                  yg, sems):
    n = x1_ref.shape[0]
    i = pl.program_id(0)
    slot = i % 2

    def gather(idx_ref, s):
        def start_rows(t, _):
            for k in range(2):
                pltpu.make_async_copy(ybuf_ref.at[_tok_rows(idx_ref[2 * t + k])], yg.at[s, k, _tok_rows(t)],
                                      sems.at[s]).start(priority=k)
            return 0
        lax.fori_loop(0, n, start_rows, 0, unroll=4)

    @pl.when(i == 0)
    def _first():
        gather(dest_ref, 0)

    @pl.when(i + 1 < pl.num_programs(0))
    def _prefetch():
        gather(dest_next_ref, 1 - slot)

    for k in range(2):
        pltpu.make_async_copy(ybuf_ref.at[_tok_rows(0, n)], yg.at[slot, k], sems.at[slot]).wait()
    route = route_ref[...]
    ff = (_load_token_tiles(yg.at[slot, 0]) * route[:, 2:3]
          + _load_token_tiles(yg.at[slot, 1]) * route[:, 3:4])
    out_ref[...] = _ln(ALPHA * x1_ref[...] + g2_ref[0] * ff) * l2g_ref[...] + l2b_ref[...]


def _final_call(dest_flat, x1, ybuf, route, g2, ln2_g, ln2_b, rows_per_g2, tile):
    n = x1.shape[0]
    assert n % tile == 0 and rows_per_g2 % tile == 0
    per = rows_per_g2 // tile
    g2_rows = g2.shape[1]
    steps = n // tile
    return pl.pallas_call(
        _final_kernel,
        grid=(steps,),
        in_specs=[pl.BlockSpec((2 * tile,), lambda i: (i,), memory_space=pltpu.SMEM),
                  pl.BlockSpec((2 * tile,), lambda i: (jnp.minimum(i + 1, steps - 1),), memory_space=pltpu.SMEM),
                  pl.BlockSpec((tile, D_MODEL), lambda i: (i, 0)),
                  pl.BlockSpec(memory_space=pl.ANY),
                  pl.BlockSpec((tile, LANES), lambda i: (i, 0)),
                  pl.BlockSpec((1, g2_rows, D_MODEL), lambda i: (i // per, 0, 0)),
                  pl.BlockSpec((1, D_MODEL), lambda i: (0, 0)),
                  pl.BlockSpec((1, D_MODEL), lambda i: (0, 0))],
        out_specs=pl.BlockSpec((tile, D_MODEL), lambda i: (i, 0)),
        out_shape=jax.ShapeDtypeStruct((n, D_MODEL), F32),
        scratch_shapes=[pltpu.VMEM((2, 2, tile * SUBLANES, LANES), F32), pltpu.SemaphoreType.DMA((2,))],
        compiler_params=pltpu.CompilerParams(dimension_semantics=("arbitrary",)),
        name="final",
    )(dest_flat, dest_flat, x1, ybuf, route, g2, ln2_g, ln2_b)


def _block_diag_gate_weights(wa, wx):
    def bd(w):
        w = w.reshape(N_RG_TILES, RG_GROUP, RG_BW, RG_BW)
        eye = jnp.eye(RG_GROUP, dtype=w.dtype)
        return jnp.einsum('tgcd,gh->tgchd', w, eye).reshape(N_RG_TILES, MXU_DIM, MXU_DIM)
    return jnp.concatenate([bd(wa), bd(wx)], axis=2)


def kernel(x_prompt, x_sample, state_conv, state_rglru, state_gla, c_prompt, c_sample, w_mod, b_mod, w_in, b_in,
           conv_w, conv_b, rg_wa, rg_ba, rg_wx, rg_bx, rg_lambda, gla_wa2, gla_ba, gla_norm_g, p_a, p_b, w_o,
           ln1_g, ln1_b, w_grp, b_grp, w_exp, b_exp, we_gate, we_up, we_down, ln2_g, ln2_b):
    assert w_mod.shape[0] == DEPTH == 1
    bp, tp, _ = x_prompt.shape
    ns = x_sample.shape[0]
    n_p = bp * tp
    n_tok = n_p + ns
    row = lambda a: a.reshape(1, -1)

    lo = 2 * D_RNN + 2 * GLA_DK + 2 * GLA_DV
    w_in0 = w_in[0]
    assert lo == O_GA
    win_a = _pack_rows(w_in0, lo)
    win_b = _pack_rows(w_in0[:, lo + GLA_RANK:])
    bin_main = row(jnp.concatenate([b_in[0, :lo], b_in[0, lo + GLA_RANK:]]))
    walr = _pack_rows(jnp.pad(w_in0[:, lo:lo + GLA_RANK], ((0, 0), (0, LANES - GLA_RANK))))
    balr = row(jnp.pad(b_in[0, lo:lo + GLA_RANK], (0, LANES - GLA_RANK)))
    wa2 = _pack_rows(jnp.pad(gla_wa2[0], ((0, LANES - GLA_RANK), (0, 0))))
    wrg = _pack_rows(_block_diag_gate_weights(rg_wa[0], rg_wx[0]).reshape(RG_BLOCKS * RG_BW, 2 * MXU_DIM))
    wrg = wrg.reshape(N_RG_TILES, MXU_DIM // 2, 2 * MXU_DIM)
    w_route = _pack_rows(jnp.pad(jnp.concatenate([w_exp[0], w_grp[0]], axis=1),
                                 ((0, 0), (0, LANES - N_EXPERTS - N_GROUPS))))
    b_route = row(jnp.pad(jnp.concatenate([b_exp[0], b_grp[0]]), (0, LANES - N_EXPERTS - N_GROUPS)))
    pa, pb, wo = _pack_rows(p_a[0]), _pack_rows(p_b[0]), _pack_rows(w_o[0])
    pre_w = [win_a, win_b, bin_main, walr, balr, wa2, row(gla_ba[0])]
    rec_w = [conv_w[0], row(conv_b[0]), wrg, row(rg_ba[0]), row(rg_bx[0]), row(rg_lambda[0])]
    post_w = [row(gla_norm_g[0]), pa, pb, wo, row(ln1_g[0]), row(ln1_b[0]), w_route, b_route]

    mod = _mod_call(jnp.concatenate([c_prompt, c_sample], axis=0), w_mod[0], row(b_mod[0]))
    mod_p = mod[:bp].reshape(bp, 6, D_MODEL)
    mod_s = mod[bp:].reshape(ns, 6, D_MODEL).transpose(1, 0, 2)

    mix_w = pre_w + [post_w[0]] + rec_w + post_w[1:]
    x1_p, hf_p, route_p, conv_p, h_p, s_p = _mixer_call(x_prompt, mod_p, mix_w)

    xs = x_sample.reshape(ns, D_MODEL)
    sconv = state_conv[0].transpose(1, 0, 2)
    conv_s, h_s, ya_s, q_s, k_s, v_s, la_s, gg_s, ga_s, gb_s = _s_pre_call(
        xs, mod_s, sconv, state_rglru[0], pre_w + rec_w)
    s_s, o_s = _s_state_call(state_gla[0], q_s, k_s, v_s, la_s)
    x1_s, hf_s, route_s = _s_post_call(xs, mod_s, o_s, gg_s, ga_s, gb_s, ya_s, post_w)

    route_pf = route_p.reshape(n_p, LANES)
    eid = jnp.concatenate([route_pf[:, :2], route_s[:, :2]], axis=0).astype(I32)
    assert n_tok % LANES == 0
    n_chunks = n_tok // LANES
    eid3 = eid.T.reshape(2, n_chunks, LANES).transpose(1, 0, 2)
    rank3, cnt = _rank_call(eid3)
    rank = rank3.transpose(1, 0, 2).reshape(2, n_tok).T
    counts = cnt[:, 0]
    pcounts = (counts + MOE_TM - 1) // MOE_TM * MOE_TM
    pends = jnp.cumsum(pcounts)
    pstarts = pends - pcounts
    dest = (pstarts[eid] + rank).reshape(-1).astype(I32)
    n_tiles = -(-(2 * n_tok + N_EXPERTS * (MOE_TM - 1)) // MOE_TM)
    tile_start = jnp.arange(n_tiles, dtype=I32) * MOE_TM
    blk_e = jnp.minimum(jnp.sum(pends[None, :] <= tile_start[:, None], axis=1), N_EXPERTS - 1).astype(I32)
    n_used = (pends[-1] // MOE_TM).astype(I32).reshape(1)

    pends32 = pends.astype(I32)
    xbuf = _dispatch_call(pends32, dest, hf_p.reshape(n_p * SUBLANES, LANES), hf_s, n_tiles * MOE_TM)
    ybuf = _expert_call(blk_e, n_used, xbuf, we_gate[0], we_up[0], we_down[0])

    y_p = _final_call(dest[:2 * n_p], x1_p.reshape(n_p, D_MODEL), ybuf, route_pf, mod_p[:, 5:6, :],
                      row(ln2_g[0]), row(ln2_b[0]), tp, ROW_CHUNK)
    y_s = _final_call(dest[2 * n_p:], x1_s, ybuf, route_s, mod_s[5][None], row(ln2_g[0]), row(ln2_b[0]), ns, ns)

    return (y_p.reshape(bp, tp, D_MODEL), y_s.reshape(ns, 1, D_MODEL),
            conv_p[None], h_p.reshape(1, bp, D_RNN), s_p[None],
            conv_s.transpose(1, 0, 2)[None], h_s[None], s_s[None])
```

```python
import functools

import jax
import jax.numpy as jnp
import numpy as np
from jax import lax
from jax.experimental import pallas as pl
from jax.experimental.pallas import tpu as pltpu

F32 = jnp.float32
BF16 = jnp.bfloat16
I32 = jnp.int32

D_MODEL = 1024
D_RNN = D_MODEL
RG_BLOCKS = 16
RG_BW = D_RNN // RG_BLOCKS
CONV_W = 4
RG_C = 8.0
GLA_HEADS = 4
GLA_DK = D_MODEL // 2
GLA_DV = D_MODEL
GLA_DKH = GLA_DK // GLA_HEADS
GLA_DVH = GLA_DV // GLA_HEADS
GLA_RANK = 16
GLA_TAU = 16.0
GLA_CHUNK = 64
N_GROUPS = 4
EXP_PER_GROUP = 8
N_EXPERTS = N_GROUPS * EXP_PER_GROUP
D_EXPERT = 512
DEPTH = 1
ALPHA = (2.0 * DEPTH) ** 0.25
LN_EPS = 1e-5
RMS_EPS = 1e-6

LANES = 128
SUBLANES = 8
MXU_DIM = 256
VMEM_LIMIT_BYTES = 56 * 1024 * 1024

MIX_TT = 256
STAGE1_COLS = 512
PIPE_DEPTH = 2
MOE_TM = 256
ROW_CHUNK = 128
SAMPLE_SB = 8
RG_GROUP = MXU_DIM // RG_BW
N_RG_TILES = RG_BLOCKS // RG_GROUP
N_MAIN = 2 * D_RNN + 2 * GLA_DK + 2 * GLA_DV + 2 * D_MODEL
ROUTE_G0 = N_EXPERTS

O_RX, O_RY, O_Q, O_K, O_V, O_GG, O_GA, O_GB = 0, 1024, 2048, 2560, 3072, 4096, 5120, 6144


def _ln(x):
    mu = jnp.mean(x, -1, keepdims=True)
    xc = x - mu
    var = jnp.mean(xc * xc, -1, keepdims=True)
    return xc * lax.rsqrt(var + LN_EPS)


def _dot(a, b):
    return jnp.dot(a, b, preferred_element_type=F32)


def _pack_kernel(w_ref, o_ref):
    o_ref[...] = pltpu.bitcast(w_ref[...].astype(BF16), jnp.uint32)


def _pack_rows(w, n=None):
    k = w.shape[0]
    n = w.shape[1] if n is None else n
    cn = min(n, 4 * LANES)
    assert k % (2 * SUBLANES) == 0 and n % cn == 0
    return pl.pallas_call(
        _pack_kernel,
        grid=(n // cn,),
        in_specs=[pl.BlockSpec((k, cn), lambda i: (0, i))],
        out_specs=pl.BlockSpec((k // 2, cn), lambda i: (0, i)),
        out_shape=jax.ShapeDtypeStruct((k // 2, n), jnp.uint32),
        compiler_params=pltpu.CompilerParams(dimension_semantics=("arbitrary",)),
        name="pack",
    )(w)


def _win(win_ref, winb_ref, lo, hi):
    if hi <= O_GA:
        return win_ref[:, lo:hi]
    assert lo >= O_GA
    return winb_ref[:, lo - O_GA:hi - O_GA]


def _w(packed):
    return pltpu.bitcast(packed, BF16)


def _expm1(x):
    u = jnp.exp(x)
    small = (u - 1.0) * x / jnp.log(u)
    return jnp.where(u == 1.0, x, jnp.where(jnp.abs(x) < 0.5, small, u - 1.0))


def _const_spec(shape):
    nd = len(shape)
    return pl.BlockSpec(shape, lambda *_: (0,) * nd, pipeline_mode=pl.Buffered(1))


assert D_MODEL == SUBLANES * LANES


def _tok_rows(t, count=1):
    return pl.ds(pl.multiple_of(t * SUBLANES, SUBLANES), count * SUBLANES)


def _store_token_tiles(ref, rows):
    n = rows.shape[0]
    for s in range(SUBLANES):
        ref[pl.ds(s, n, stride=SUBLANES), :] = rows[:, s * LANES:(s + 1) * LANES]


def _load_token_tiles(ref):
    n = ref.shape[0] // SUBLANES
    return jnp.concatenate([ref[pl.ds(s, n, stride=SUBLANES), :] for s in range(SUBLANES)], axis=1)


def _mod_kernel(c_ref, w_ref, b_ref, o_ref):
    c = c_ref[...]
    s = jax.nn.silu(c)
    o_ref[...] = _dot(s.astype(BF16), w_ref[...].astype(BF16)) + b_ref[...]


def _mod_call(c_all, w_mod, b_mod):
    n = c_all.shape[0]
    tn = 512
    return pl.pallas_call(
        _mod_kernel,
        grid=(6 * D_MODEL // tn,),
        in_specs=[pl.BlockSpec((n, D_MODEL), lambda i: (0, 0)),
                  pl.BlockSpec((D_MODEL, tn), lambda i: (0, i)),
                  pl.BlockSpec((1, tn), lambda i: (0, i))],
        out_specs=pl.BlockSpec((n, tn), lambda i: (0, i)),
        out_shape=jax.ShapeDtypeStruct((n, 6 * D_MODEL), F32),
        compiler_params=pltpu.CompilerParams(dimension_semantics=("arbitrary",)),
        name="mod",
    )(c_all, w_mod, b_mod)


def _no_op():
    pass


def _gate_logits(xc, wrg_ref, rba, rbx):
    xcb = xc.astype(BF16)
    parts = [_dot(xcb[:, c * MXU_DIM:(c + 1) * MXU_DIM], _w(wrg_ref[c])) for c in range(N_RG_TILES)]
    r_pre = jnp.concatenate([p[:, :MXU_DIM] for p in parts], axis=1)
    i_pre = jnp.concatenate([p[:, MXU_DIM:] for p in parts], axis=1)
    return r_pre, i_pre, rba, rbx


def _gate_logits_to(out_ref, xc, wrg_ref):
    xcb = xc.astype(BF16)
    for c in range(N_RG_TILES):
        p = _dot(xcb[:, c * MXU_DIM:(c + 1) * MXU_DIM], _w(wrg_ref[c]))
        out_ref[:, c * MXU_DIM:(c + 1) * MXU_DIM] = p[:, :MXU_DIM]
        out_ref[:, D_RNN + c * MXU_DIM:D_RNN + (c + 1) * MXU_DIM] = p[:, MXU_DIM:]


def _gates(xc, r_pre, i_pre, rba, rbx, lam, between=_no_op):
    r = jax.nn.sigmoid(r_pre + rba)
    between()
    ig = jax.nn.sigmoid(i_pre + rbx)
    log_a = -RG_C * r * jax.nn.softplus(-lam)
    a = jnp.exp(log_a)
    between()
    u = jnp.sqrt(-_expm1(2.0 * log_a)) * (ig * xc)
    between()
    return a, u


def _low_rank(hb, walr_ref, balr_ref):
    return _dot(hb, _w(walr_ref[...])) + balr_ref[...]


def _log_decay_from(alr, wa2_ref, gba_ref):
    return jax.nn.log_sigmoid(_dot(alr.astype(BF16), _w(wa2_ref[...])) + gba_ref[...]) / GLA_TAU


def _log_decay(hb, walr_ref, balr_ref, wa2_ref, gba_ref):
    return _log_decay_from(_low_rank(hb, walr_ref, balr_ref), wa2_ref, gba_ref)


def _post(x, o, gg, ga, gb, y_a, g1, sh2, sc2, gng, pa_ref, pb_ref, wo_ref, l1g, l1b, wr_ref, br,
          between=_no_op):
    rows = x.shape[0]
    heads = []
    for hh in range(GLA_HEADS):
        oh = o[:, hh * GLA_DVH:(hh + 1) * GLA_DVH]
        ms = jnp.mean(oh * oh, -1, keepdims=True)
        heads.append(oh * lax.rsqrt(ms + RMS_EPS) * gng)
    y_b = jnp.concatenate(heads, axis=1) * jax.nn.silu(gg)
    merged = (jax.nn.sigmoid(ga) * _dot(y_a.astype(BF16), _w(pa_ref[...]))
              + jax.nn.sigmoid(gb) * _dot(y_b.astype(BF16), _w(pb_ref[...])))
    mix = _dot(merged.astype(BF16), _w(wo_ref[...]))
    between()
    x1 = _ln(ALPHA * x + g1 * mix) * l1g + l1b
    between()
    hf = _ln(x1) * (1.0 + sc2) + sh2
    logits = _dot(hf.astype(BF16), _w(wr_ref[...])) + br
    lane = lax.broadcasted_iota(I32, (rows, LANES), 1).astype(F32)
    neg = jnp.float32(-jnp.inf)
    big = jnp.float32(LANES)
    g_valid = (lane >= ROUTE_G0) & (lane < ROUTE_G0 + N_GROUPS)
    gl = jnp.where(g_valid, logits, neg)
    gmax = jnp.max(gl, -1, keepdims=True)
    g_lane = jnp.min(jnp.where(gl == gmax, lane, big), -1, keepdims=True)
    g_w = 1.0 / jnp.sum(jnp.exp(gl - gmax), -1, keepdims=True)
    e_lo = (g_lane - ROUTE_G0) * EXP_PER_GROUP
    el = jnp.where((lane >= e_lo) & (lane < e_lo + EXP_PER_GROUP), logits, neg)
    t1 = jnp.max(el, -1, keepdims=True)
    i1 = jnp.min(jnp.where(el == t1, lane, big), -1, keepdims=True)
    el2 = jnp.where(lane == i1, neg, el)
    t2 = jnp.max(el2, -1, keepdims=True)
    i2 = jnp.min(jnp.where(el2 == t2, lane, big), -1, keepdims=True)
    e2 = jnp.exp(t2 - t1)
    den = 1.0 + e2
    w1 = (1.0 / den) * g_w
    w2 = (e2 / den) * g_w
    route = jnp.where(lane == 0, i1.astype(F32),
                      jnp.where(lane == 1, i2.astype(F32),
                                jnp.where(lane == 2, w1, jnp.where(lane == 3, w2, 0.0))))
    return x1, hf, route


def _shift_rows(x, s):
    return pltpu.roll(x, s, 0)


def _mixer_kernel(xn_ref, modn_ref, x_ref, mod_ref, win_ref, winb_ref, bin_ref, walr_ref, balr_ref, wa2_ref, gba_ref,
                  gng_ref, cw_ref, cb_ref, wrg_ref, rba_ref, rbx_ref, lam_ref, pa_ref, pb_ref, wo_ref,
                  l1g_ref, l1b_ref, wr_ref, br_ref,
                  x1_ref, hf_ref, route_ref, routet_ref, convn_ref, hlast_ref, sfin_ref,
                  hbn_scr, hbc_scr, pn_scr, pc_scr, alrn_scr, alrc_scr, xcn_scr, xcc_scr, gaten_scr, gatec_scr,
                  rxbuf, hcar, st_ref, o_scr,
                  *, tiles_per_seq):
    i = pl.program_id(0)
    t = jnp.maximum(i - PIPE_DEPTH, 0)
    j = t % tiles_per_seq
    j1 = jnp.maximum(i - 1, 0) % tiles_per_seq
    tt = x_ref.shape[1]

    @pl.when(i == 0)
    def _no_tile_yet():
        hbc_scr[...] = jnp.zeros_like(hbc_scr)
        pc_scr[...] = jnp.zeros_like(pc_scr)
        alrc_scr[...] = jnp.zeros_like(alrc_scr)
        xcc_scr[...] = jnp.zeros_like(xcc_scr)
        gatec_scr[...] = jnp.zeros_like(gatec_scr)

    @pl.when(j1 == 0)
    def _init_conv():
        rxbuf[0:SUBLANES, :] = jnp.zeros((SUBLANES, D_RNN), F32)

    @pl.when(j == 0)
    def _init():
        hcar[...] = jnp.zeros_like(hcar)
        st_ref[...] = jnp.zeros_like(st_ref)

    def stage0():
        modn = modn_ref[0]
        hbn_scr[...] = (_ln(xn_ref[0]) * (1.0 + modn[1:2]) + modn[0:1]).astype(BF16)

    hbn = hbc_scr[...]

    def _proj_block(lo):
        def run():
            hi = lo + STAGE1_COLS
            pn_scr[:, lo:hi] = _dot(hbn, _w(_win(win_ref, winb_ref, lo, hi)))
        return run

    def _alr_block():
        alrn_scr[...] = _dot(hbn, _w(walr_ref[...]))

    pending = [_proj_block(lo) for lo in range(0, N_MAIN, STAGE1_COLS)] + [_alr_block]

    def pump(count=1):
        for _ in range(min(count, len(pending))):
            pending.pop(0)()

    x = x_ref[0]
    mod = mod_ref[0]
    g1, sh2, sc2 = mod[2:3], mod[3:4], mod[4:5]

    def proj(lo, hi):
        return pc_scr[:, lo:hi] + bin_ref[:, lo:hi]

    pump(O_RY // STAGE1_COLS)

    rx = pn_scr[:, O_RX:O_RX + D_RNN] + bin_ref[:, O_RX:O_RX + D_RNN]
    rxbuf[SUBLANES:SUBLANES + tt, :] = rx
    cw = cw_ref[...]
    xcn = cb_ref[...] + rxbuf[SUBLANES - 3:SUBLANES - 3 + tt, :] * cw[0:1]
    xcn = xcn + rxbuf[SUBLANES - 2:SUBLANES - 2 + tt, :] * cw[1:2]
    xcn = xcn + rxbuf[SUBLANES - 1:SUBLANES - 1 + tt, :] * cw[2:3]
    xcn = xcn + rx * cw[3:4]
    rxbuf[0:SUBLANES, :] = rxbuf[tt:tt + SUBLANES, :]
    xcn_scr[...] = xcn
    xc = xcc_scr[...]

    q = proj(O_Q, O_Q + GLA_DK) * (GLA_DKH ** -0.5)
    k = proj(O_K, O_K + GLA_DK)
    v = proj(O_V, O_V + GLA_DV)
    la = _log_decay_from(alrc_scr[...] + balr_ref[...], wa2_ref, gba_ref)
    pump()
    rowk = lax.broadcasted_iota(I32, (tt, GLA_DK), 0) % GLA_CHUNK
    bcum = la
    s = 1
    while s < GLA_CHUNK:
        bcum = bcum + jnp.where(rowk >= s, _shift_rows(bcum, s), 0.0)
        s *= 2
    eb = jnp.exp(bcum)
    q_in = (q * eb).astype(BF16)
    k_in = (k * jnp.exp(-bcum)).astype(BF16)
    pump()
    stage0()
    gelu_ry = jax.nn.gelu(proj(O_RY, O_RY + D_RNN))
    pump()

    a, u = _gates(xc, gatec_scr[:, :D_RNN], gatec_scr[:, D_RNN:], rba_ref[...], rbx_ref[...], lam_ref[...],
                  between=pump)
    row = lax.broadcasted_iota(I32, (tt, D_RNN), 0)
    s = 1
    while s < tt:
        keep = row >= s
        a_s = jnp.where(keep, _shift_rows(a, s), 1.0)
        u_s = jnp.where(keep, _shift_rows(u, s), 0.0)
        u = a * u_s + u
        a = a * a_s
        if s < SUBLANES:
            pump()
        s *= 2
    hseq = a * hcar[0:1, :] + u
    hcar[0:1, :] = hseq[tt - 1:tt, :]
    y_a = hseq * gelu_ry

    tri = (lax.broadcasted_iota(I32, (GLA_CHUNK, GLA_CHUNK), 0)
           >= lax.broadcasted_iota(I32, (GLA_CHUNK, GLA_CHUNK), 1))
    n_chunks = tt // GLA_CHUNK
    nt_dims = (((1,), (1,)), ((), ()))
    attn, d_st, dec, vh_all = {}, {}, {}, {}
    for c in range(n_chunks):
        r0 = c * GLA_CHUNK
        for hh in range(GLA_HEADS):
            kc = slice(hh * GLA_DKH, (hh + 1) * GLA_DKH)
            b = bcum[r0:r0 + GLA_CHUNK, kc]
            btot = b[GLA_CHUNK - 1:GLA_CHUNK, :]
            k_out = (k[r0:r0 + GLA_CHUNK, kc] * jnp.exp(btot - b)).astype(BF16)
            vh = v[r0:r0 + GLA_CHUNK, hh * GLA_DVH:(hh + 1) * GLA_DVH].astype(BF16)
            scores = lax.dot_general(q_in[r0:r0 + GLA_CHUNK, kc], k_in[r0:r0 + GLA_CHUNK, kc], nt_dims,
                                     preferred_element_type=F32)
            attn[c, hh] = jnp.where(tri, scores, 0.0).astype(BF16)
            d_st[c, hh] = lax.dot_general(vh, k_out, (((0,), (0,)), ((), ())), preferred_element_type=F32)
            dec[c, hh] = jnp.exp(btot)
            vh_all[c, hh] = vh
        pump()
    starts = {}
    for hh in range(GLA_HEADS):
        st = st_ref[hh]
        for c in range(n_chunks):
            starts[c, hh] = st.astype(BF16)
            st = st * dec[c, hh] + d_st[c, hh]
        st_ref[hh] = st
    for c in range(n_chunks):
        r0 = c * GLA_CHUNK
        for hh in range(GLA_HEADS):
            qi = q_in[r0:r0 + GLA_CHUNK, hh * GLA_DKH:(hh + 1) * GLA_DKH]
            inter = lax.dot_general(qi, starts[c, hh], nt_dims, preferred_element_type=F32)
            o_scr[r0:r0 + GLA_CHUNK, hh * GLA_DVH:(hh + 1) * GLA_DVH] = _dot(attn[c, hh], vh_all[c, hh]) + inter

    x1, hf, route = _post(x, o_scr[...], proj(O_GG, O_GG + GLA_DV), proj(O_GA, O_GA + D_MODEL),
                          proj(O_GB, O_GB + D_MODEL), y_a, g1, sh2, sc2, gng_ref[...],
                          pa_ref, pb_ref, wo_ref, l1g_ref[...], l1b_ref[...], wr_ref, br_ref[...], between=pump)
    x1_ref[0] = x1
    _store_token_tiles(hf_ref.at[0], hf)
    route_ref[0] = route
    routet_ref[...] = route.T[0:SUBLANES, :]

    pump(len(pending))
    _gate_logits_to(gaten_scr, xcn, wrg_ref)
    pc_scr[:, O_RY:] = pn_scr[:, O_RY:]
    alrc_scr[...] = alrn_scr[...]
    hbc_scr[...] = hbn_scr[...]
    xcc_scr[...] = xcn_scr[...]
    gatec_scr[...] = gaten_scr[...]

    @pl.when((j1 == tiles_per_seq - 1) & (i >= 1))
    def _final_conv():
        convn_ref[0] = rxbuf[SUBLANES - (CONV_W - 1):SUBLANES, :]

    @pl.when((j == tiles_per_seq - 1) & (i >= PIPE_DEPTH))
    def _final():
        hlast_ref[0] = hcar[0:1, :]
        for hh in range(GLA_HEADS):
            sfin_ref[0, hh] = st_ref[hh].T


def _mixer_call(x, mod3, wts):
    b, t, _ = x.shape
    tt = MIX_TT
    assert t % tt == 0 and tt % GLA_CHUNK == 0
    nt = t // tt
    n_tiles = b * nt
    assert nt > 1
    cur = lambda i: jnp.maximum(i - PIPE_DEPTH, 0)
    nxt = lambda i: jnp.minimum(i, n_tiles - 1)
    tok = lambda i: (cur(i) // nt, cur(i) % nt, 0)
    tok_n = lambda i: (nxt(i) // nt, nxt(i) % nt, 0)
    per_b = lambda i: (cur(i) // nt, 0, 0)
    in_specs = [pl.BlockSpec((1, tt, D_MODEL), tok_n),
                pl.BlockSpec((1, 6, D_MODEL), lambda i: (nxt(i) // nt, 0, 0)),
                pl.BlockSpec((1, tt, D_MODEL), tok),
                pl.BlockSpec((1, 6, D_MODEL), per_b)] + [_const_spec(w.shape) for w in wts]
    out_specs = [pl.BlockSpec((1, tt, D_MODEL), tok),
                 pl.BlockSpec((1, tt * SUBLANES, LANES), tok),
                 pl.BlockSpec((1, tt, LANES), tok),
                 pl.BlockSpec((SUBLANES, tt), lambda i: (0, cur(i))),
                 pl.BlockSpec((1, CONV_W - 1, D_RNN), per_b),
                 pl.BlockSpec((1, 1, D_RNN), per_b),
                 pl.BlockSpec((1, GLA_HEADS, GLA_DKH, GLA_DVH), lambda i: (cur(i) // nt, 0, 0, 0))]
    out_shape = [jax.ShapeDtypeStruct((b, t, D_MODEL), F32),
                 jax.ShapeDtypeStruct((b, t * SUBLANES, LANES), F32),
                 jax.ShapeDtypeStruct((b, t, LANES), F32),
                 jax.ShapeDtypeStruct((SUBLANES, b * t), F32),
                 jax.ShapeDtypeStruct((b, CONV_W - 1, D_RNN), F32),
                 jax.ShapeDtypeStruct((b, 1, D_RNN), F32),
                 jax.ShapeDtypeStruct((b, GLA_HEADS, GLA_DKH, GLA_DVH), F32)]
    scratch = [pltpu.VMEM((tt, D_MODEL), BF16), pltpu.VMEM((tt, D_MODEL), BF16),
               pltpu.VMEM((tt, N_MAIN), F32), pltpu.VMEM((tt, N_MAIN), F32),
               pltpu.VMEM((tt, LANES), F32), pltpu.VMEM((tt, LANES), F32),
               pltpu.VMEM((tt, D_RNN), F32), pltpu.VMEM((tt, D_RNN), F32),
               pltpu.VMEM((tt, 2 * D_RNN), F32), pltpu.VMEM((tt, 2 * D_RNN), F32),
               pltpu.VMEM((tt + SUBLANES, D_RNN), F32),
               pltpu.VMEM((SUBLANES, D_RNN), F32),
               pltpu.VMEM((GLA_HEADS, GLA_DVH, GLA_DKH), F32),
               pltpu.VMEM((tt, GLA_DV), F32)]
    return pl.pallas_call(
        functools.partial(_mixer_kernel, tiles_per_seq=nt),
        grid=(n_tiles + PIPE_DEPTH,),
        in_specs=in_specs,
        out_specs=out_specs,
        out_shape=out_shape,
        scratch_shapes=scratch,
        compiler_params=pltpu.CompilerParams(dimension_semantics=("arbitrary",),
                                             vmem_limit_bytes=VMEM_LIMIT_BYTES),
        name="mixer",
    )(x, mod3, x, mod3, *wts)


def _s_pre_kernel(x_ref, mod_ref, sconv_ref, h0_ref, win_ref, winb_ref, bin_ref, walr_ref, balr_ref, wa2_ref, gba_ref,
                  cw_ref, cb_ref, wrg_ref, rba_ref, rbx_ref, lam_ref,
                  convn_ref, hnew_ref, ya_ref, q_ref, k_ref, v_ref, la_ref, gg_ref, ga_ref, gb_ref):
    x = x_ref[...]
    hb = (_ln(x) * (1.0 + mod_ref[1]) + mod_ref[0]).astype(BF16)

    def proj(lo, hi):
        return _dot(hb, _w(_win(win_ref, winb_ref, lo, hi))) + bin_ref[:, lo:hi]

    rx = proj(O_RX, O_RX + D_RNN)
    cw = cw_ref[...]
    xc = cb_ref[...] + sconv_ref[0] * cw[0:1]
    xc = xc + sconv_ref[1] * cw[1:2]
    xc = xc + sconv_ref[2] * cw[2:3]
    xc = xc + rx * cw[3:4]
    convn_ref[0] = sconv_ref[1]
    convn_ref[1] = sconv_ref[2]
    convn_ref[2] = rx
    a, u = _gates(xc, *_gate_logits(xc, wrg_ref, rba_ref[...], rbx_ref[...]), lam_ref[...])
    hnew = u + a * h0_ref[...]
    hnew_ref[...] = hnew
    ya_ref[...] = hnew * jax.nn.gelu(proj(O_RY, O_RY + D_RNN))
    q_ref[...] = proj(O_Q, O_Q + GLA_DK) * (GLA_DKH ** -0.5)
    k_ref[...] = proj(O_K, O_K + GLA_DK)
    v_ref[...] = proj(O_V, O_V + GLA_DV)
    la_ref[...] = _log_decay(hb, walr_ref, balr_ref, wa2_ref, gba_ref)
    gg_ref[...] = proj(O_GG, O_GG + GLA_DV)
    ga_ref[...] = proj(O_GA, O_GA + D_MODEL)
    gb_ref[...] = proj(O_GB, O_GB + D_MODEL)


def _s_pre_call(x_s, mod_s, sconv, h0, wts):
    n = x_s.shape[0]
    full = lambda shape: pl.BlockSpec(shape, lambda i: (0,) * len(shape))
    ins = [x_s, mod_s, sconv, h0] + list(wts)
    shapes = [(CONV_W - 1, n, D_RNN), (n, D_RNN), (n, D_RNN), (n, GLA_DK), (n, GLA_DK), (n, GLA_DV),
              (n, GLA_DK), (n, GLA_DV), (n, D_MODEL), (n, D_MODEL)]
    return pl.pallas_call(
        _s_pre_kernel,
        grid=(1,),
        in_specs=[full(a.shape) for a in ins],
        out_specs=[full(s) for s in shapes],
        out_shape=[jax.ShapeDtypeStruct(s, F32) for s in shapes],
        compiler_params=pltpu.CompilerParams(dimension_semantics=("arbitrary",),
                                             vmem_limit_bytes=VMEM_LIMIT_BYTES),
        name="s_pre",
    )(*ins)


def _to_column(row, n):
    eye = lax.broadcasted_iota(I32, (n, n), 0) == lax.broadcasted_iota(I32, (n, n), 1)
    return jnp.sum(jnp.where(eye, jnp.broadcast_to(row, (n, n)), 0.0), axis=1, keepdims=True)


def _s_state_kernel(s_ref, q_ref, k_ref, v_ref, la_ref, snew_ref, o_ref):
    sb = s_ref.shape[0]
    qb = q_ref[...].astype(BF16)
    dec = jnp.exp(la_ref[...])
    kk = k_ref[...]
    vv = v_ref[...]
    for si in range(sb):
        for hh in range(GLA_HEADS):
            kc = slice(hh * GLA_DKH, (hh + 1) * GLA_DKH)
            vc = slice(hh * GLA_DVH, (hh + 1) * GLA_DVH)
            d_col = _to_column(dec[si:si + 1, kc], GLA_DKH)
            k_col = _to_column(kk[si:si + 1, kc], GLA_DKH)
            s_new = s_ref[si, hh] * d_col + k_col * vv[si:si + 1, vc]
            snew_ref[si, hh] = s_new
            o_all = _dot(qb[:, kc], s_new.astype(BF16))
            o_ref[si:si + 1, vc] = o_all[si:si + 1, :]


def _s_state_call(state, q, k, v, la):
    n = state.shape[0]
    sb = SAMPLE_SB
    assert n % sb == 0
    blk = lambda w: pl.BlockSpec((sb, w), lambda i: (i, 0))
    st_spec = pl.BlockSpec((sb, GLA_HEADS, GLA_DKH, GLA_DVH), lambda i: (i, 0, 0, 0))
    return pl.pallas_call(
        _s_state_kernel,
        grid=(n // sb,),
        in_specs=[st_spec, blk(GLA_DK), blk(GLA_DK), blk(GLA_DV), blk(GLA_DK)],
        out_specs=[st_spec, blk(GLA_DV)],
        out_shape=[jax.ShapeDtypeStruct(state.shape, F32), jax.ShapeDtypeStruct((n, GLA_DV), F32)],
        compiler_params=pltpu.CompilerParams(dimension_semantics=("arbitrary",),
                                             vmem_limit_bytes=VMEM_LIMIT_BYTES),
        name="s_state",
    )(state, q, k, v, la)


def _s_post_kernel(x_ref, mod_ref, o_ref, gg_ref, ga_ref, gb_ref, ya_ref, gng_ref, pa_ref, pb_ref, wo_ref,
                   l1g_ref, l1b_ref, wr_ref, br_ref, x1_ref, hf_ref, route_ref, routet_ref):
    x1, hf, route = _post(x_ref[...], o_ref[...], gg_ref[...], ga_ref[...], gb_ref[...], ya_ref[...],
                          mod_ref[2], mod_ref[3], mod_ref[4], gng_ref[...], pa_ref, pb_ref, wo_ref,
                          l1g_ref[...], l1b_ref[...], wr_ref, br_ref[...])
    x1_ref[...] = x1
    _store_token_tiles(hf_ref, hf)
    route_ref[...] = route
    routet_ref[...] = route.T[0:SUBLANES, :]


def _s_post_call(x_s, mod_s, o, gg, ga, gb, ya, wts):
    n = x_s.shape[0]
    full = lambda shape: pl.BlockSpec(shape, lambda i: (0,) * len(shape))
    ins = [x_s, mod_s, o, gg, ga, gb, ya] + list(wts)
    shapes = [(n, D_MODEL), (n * SUBLANES, LANES), (n, LANES), (SUBLANES, n)]
    return pl.pallas_call(
        _s_post_kernel,
        grid=(1,),
        in_specs=[full(a.shape) for a in ins],
        out_specs=[full(s) for s in shapes],
        out_shape=[jax.ShapeDtypeStruct(s, F32) for s in shapes],
        compiler_params=pltpu.CompilerParams(dimension_semantics=("arbitrary",),
                                             vmem_limit_bytes=VMEM_LIMIT_BYTES),
        name="s_post",
    )(*ins)


def _rank_kernel(eid_ref, rank_ref, cnt_ref, carry):
    n_chunks = eid_ref.shape[0]
    carry[...] = jnp.zeros_like(carry)
    e_iota = lax.broadcasted_iota(I32, (N_EXPERTS, LANES), 0)
    upper = (lax.broadcasted_iota(I32, (LANES, LANES), 0)
             < lax.broadcasted_iota(I32, (LANES, LANES), 1)).astype(BF16)

    def body(c, _):
        ids = eid_ref[c]
        oh0 = (e_iota == ids[0:1, :])
        oh1 = (e_iota == ids[1:2, :])
        both = (oh0 | oh1).astype(F32)
        before = _dot(both.astype(BF16), upper) + carry[:, 0:1]
        r0 = jnp.sum(jnp.where(oh0, before, 0.0), axis=0, keepdims=True)
        r1 = jnp.sum(jnp.where(oh1, before, 0.0), axis=0, keepdims=True)
        rank_ref[c] = jnp.concatenate([r0, r1], axis=0).astype(I32)
        carry[...] = carry[...] + jnp.sum(both, axis=1, keepdims=True)
        return 0

    lax.fori_loop(0, n_chunks, body, 0)
    cnt_ref[...] = carry[...].astype(I32)


def _rank_call(eid3):
    n_chunks = eid3.shape[0]
    full = lambda shape: pl.BlockSpec(shape, lambda i: (0,) * len(shape))
    return pl.pallas_call(
        _rank_kernel,
        grid=(1,),
        in_specs=[full(eid3.shape)],
        out_specs=[full(eid3.shape), full((N_EXPERTS, LANES))],
        out_shape=[jax.ShapeDtypeStruct(eid3.shape, I32), jax.ShapeDtypeStruct((N_EXPERTS, LANES), I32)],
        scratch_shapes=[pltpu.VMEM((N_EXPERTS, LANES), F32)],
        compiler_params=pltpu.CompilerParams(dimension_semantics=("arbitrary",)),
        name="rank",
    )(eid3)


def _dispatch_kernel(pends_ref, d0_ref, d1_ref, src_p_ref, src_s_ref, buf_ref, sem, zeros_ref, zsem, *,
                     n_p_chunks):
    i = pl.program_id(0)
    n_tiles = buf_ref.shape[0] // (MOE_TM * SUBLANES)

    @pl.when(i == 0)
    def _clear():
        zeros_ref[...] = jnp.zeros_like(zeros_ref)

        def tile_copy(row0):
            return pltpu.make_async_copy(zeros_ref, buf_ref.at[_tok_rows(row0, MOE_TM)], zsem)

        def nonempty(e):
            return pends_ref[e] > jnp.where(e == 0, 0, pends_ref[jnp.maximum(e - 1, 0)])

        def start_tail(e, _):
            @pl.when(nonempty(e))
            def _():
                tile_copy(pends_ref[e] - MOE_TM).start()
            return 0

        def wait_tail(e, _):
            @pl.when(nonempty(e))
            def _():
                tile_copy(0).wait()
            return 0

        def start_unused(j, _):
            tile_copy(j * MOE_TM).start()
            return 0

        def wait_unused(j, _):
            tile_copy(0).wait()
            return 0

        first_unused = pends_ref[N_EXPERTS - 1] // MOE_TM
        lax.fori_loop(0, N_EXPERTS, start_tail, 0)
        lax.fori_loop(first_unused, n_tiles, start_unused, 0)
        lax.fori_loop(0, N_EXPERTS, wait_tail, 0)
        lax.fori_loop(first_unused, n_tiles, wait_unused, 0)

    def scatter_rows(src_ref):
        n = src_ref.shape[0] // SUBLANES

        def start_rows(t, _):
            for k, dest_ref in enumerate((d0_ref, d1_ref)):
                pltpu.make_async_copy(src_ref.at[_tok_rows(t)], buf_ref.at[_tok_rows(dest_ref[t])],
                                      sem).start(priority=k)
            return 0

        lax.fori_loop(0, n, start_rows, 0, unroll=4)
        for _ in range(2):
            pltpu.make_async_copy(src_ref, buf_ref.at[_tok_rows(0, n)], sem).wait()

    @pl.when(i < n_p_chunks)
    def _prompt():
        scatter_rows(src_p_ref)

    @pl.when(i >= n_p_chunks)
    def _sample():
        scatter_rows(src_s_ref)


def _dispatch_call(pends, dest0, dest1, src_p, src_s, n_rows):
    chunk_rows = ROW_CHUNK * SUBLANES
    assert src_p.shape[0] % chunk_rows == 0 and src_s.shape[0] % chunk_rows == 0 and n_rows % MOE_TM == 0
    n_p_chunks = src_p.shape[0] // chunk_rows
    n_s_chunks = src_s.shape[0] // chunk_rows
    tile = (chunk_rows, LANES)
    in_specs = [pl.BlockSpec((ROW_CHUNK,), lambda i, pe: (i,), memory_space=pltpu.SMEM),
                pl.BlockSpec((ROW_CHUNK,), lambda i, pe: (i,), memory_space=pltpu.SMEM),
                pl.BlockSpec(tile, lambda i, pe: (jnp.minimum(i, n_p_chunks - 1), 0)),
                pl.BlockSpec(tile, lambda i, pe: (jnp.maximum(i - n_p_chunks, 0), 0))]
    scratch = [pltpu.SemaphoreType.DMA(()), pltpu.VMEM((MOE_TM * SUBLANES, LANES), F32),
               pltpu.SemaphoreType.DMA(())]
    return pl.pallas_call(
        functools.partial(_dispatch_kernel, n_p_chunks=n_p_chunks),
        grid_spec=pltpu.PrefetchScalarGridSpec(
            num_scalar_prefetch=1, grid=(n_p_chunks + n_s_chunks,), in_specs=in_specs,
            out_specs=pl.BlockSpec(memory_space=pl.ANY), scratch_shapes=scratch),
        out_shape=jax.ShapeDtypeStruct((n_rows * SUBLANES, LANES), F32),
        compiler_params=pltpu.CompilerParams(dimension_semantics=("arbitrary",), has_side_effects=True),
        name="dispatch",
    )(pends, dest0, dest1, src_p, src_s)


def _expert_kernel(blk_e_ref, n_used_ref, x_ref, wg_ref, wu_ref, wd_ref, y_ref, wgb, wub, wdb):
    i = pl.program_id(0)
    new_expert = (i == 0) | (blk_e_ref[i] != blk_e_ref[jnp.maximum(i - 1, 0)])

    @pl.when(new_expert)
    def _cast_weights():
        wgb[...] = wg_ref[0].astype(BF16)
        wub[...] = wu_ref[0].astype(BF16)
        wdb[...] = wd_ref[0].astype(BF16)

    @pl.when(i < n_used_ref[0])
    def _run():
        xb = _load_token_tiles(x_ref).astype(BF16)
        g = _dot(xb, wgb[...])
        u = _dot(xb, wub[...])
        act = (jax.nn.silu(g) * u).astype(BF16)
        _store_token_tiles(y_ref, _dot(act, wdb[...]))

    @pl.when(i >= n_used_ref[0])
    def _skip():
        y_ref[...] = jnp.zeros_like(y_ref)


def _expert_call(blk_e, n_used, xbuf, we_gate, we_up, we_down):
    p = xbuf.shape[0] // SUBLANES
    assert p % MOE_TM == 0
    tile_spec = pl.BlockSpec((MOE_TM * SUBLANES, LANES), lambda i, be, nu: (i, 0))
    grid_spec = pltpu.PrefetchScalarGridSpec(
        num_scalar_prefetch=2,
        grid=(p // MOE_TM,),
        in_specs=[tile_spec,
                  pl.BlockSpec((1, D_MODEL, D_EXPERT), lambda i, be, nu: (be[i], 0, 0)),
                  pl.BlockSpec((1, D_MODEL, D_EXPERT), lambda i, be, nu: (be[i], 0, 0)),
                  pl.BlockSpec((1, D_EXPERT, D_MODEL), lambda i, be, nu: (be[i], 0, 0))],
        out_specs=tile_spec,
        scratch_shapes=[pltpu.VMEM((D_MODEL, D_EXPERT), BF16), pltpu.VMEM((D_MODEL, D_EXPERT), BF16),
                        pltpu.VMEM((D_EXPERT, D_MODEL), BF16)],
    )
    return pl.pallas_call(
        _expert_kernel,
        grid_spec=grid_spec,
        out_shape=jax.ShapeDtypeStruct((p * SUBLANES, LANES), F32),
        compiler_params=pltpu.CompilerParams(dimension_semantics=("arbitrary",),
                                             vmem_limit_bytes=VMEM_LIMIT_BYTES),
        name="experts",
    )(blk_e, n_used, xbuf, we_gate, we_up, we_down)


def _final_kernel(d0_ref, d1_ref, d0n_ref, d1n_ref, x1_ref, ybuf_ref, route_ref, g2_ref, l2g_ref, l2b_ref, out_ref,
                  yg, sems):
    n = x1_ref.shape[0]
    i = pl.program_id(0)
    slot = i % 2

    def gather(idx_refs, s):
        def start_rows(t, _):
            for k, idx_ref in enumerate(idx_refs):
                pltpu.make_async_copy(ybuf_ref.at[_tok_rows(idx_ref[t])], yg.at[s, k, _tok_rows(t)],
                                      sems.at[s]).start(priority=k)
            return 0
        lax.fori_loop(0, n, start_rows, 0, unroll=4)

    @pl.when(i == 0)
    def _first():
        gather((d0_ref, d1_ref), 0)

    @pl.when(i + 1 < pl.num_programs(0))
    def _prefetch():
        gather((d0n_ref, d1n_ref), 1 - slot)

    for k in range(2):
        pltpu.make_async_copy(ybuf_ref.at[_tok_rows(0, n)], yg.at[slot, k], sems.at[slot]).wait()
    route = route_ref[...]
    ff = (_load_token_tiles(yg.at[slot, 0]) * route[:, 2:3]
          + _load_token_tiles(yg.at[slot, 1]) * route[:, 3:4])
    out_ref[...] = _ln(ALPHA * x1_ref[...] + g2_ref[0] * ff) * l2g_ref[...] + l2b_ref[...]


def _final_call(dest0, dest1, first_token, x1, ybuf, route, g2, ln2_g, ln2_b, rows_per_g2, tile):
    n = x1.shape[0]
    assert n % tile == 0 and rows_per_g2 % tile == 0 and first_token % tile == 0
    per = rows_per_g2 // tile
    g2_rows = g2.shape[1]
    steps = n // tile
    off = first_token // tile
    idx = lambda: pl.BlockSpec((tile,), lambda i: (off + i,), memory_space=pltpu.SMEM)
    idx_next = lambda: pl.BlockSpec((tile,), lambda i: (off + jnp.minimum(i + 1, steps - 1),),
                                    memory_space=pltpu.SMEM)
    return pl.pallas_call(
        _final_kernel,
        grid=(steps,),
        in_specs=[idx(), idx(), idx_next(), idx_next(),
                  pl.BlockSpec((tile, D_MODEL), lambda i: (i, 0)),
                  pl.BlockSpec(memory_space=pl.ANY),
                  pl.BlockSpec((tile, LANES), lambda i: (i, 0)),
                  pl.BlockSpec((1, g2_rows, D_MODEL), lambda i: (i // per, 0, 0)),
                  pl.BlockSpec((1, D_MODEL), lambda i: (0, 0)),
                  pl.BlockSpec((1, D_MODEL), lambda i: (0, 0))],
        out_specs=pl.BlockSpec((tile, D_MODEL), lambda i: (i, 0)),
        out_shape=jax.ShapeDtypeStruct((n, D_MODEL), F32),
        scratch_shapes=[pltpu.VMEM((2, 2, tile * SUBLANES, LANES), F32), pltpu.SemaphoreType.DMA((2,))],
        compiler_params=pltpu.CompilerParams(dimension_semantics=("arbitrary",)),
        name="final",
    )(dest0, dest1, dest0, dest1, x1, ybuf, route, g2, ln2_g, ln2_b)


def _block_diag_gate_weights(wa, wx):
    def bd(w):
        w = w.reshape(N_RG_TILES, RG_GROUP, RG_BW, RG_BW)
        eye = jnp.eye(RG_GROUP, dtype=w.dtype)
        return jnp.einsum('tgcd,gh->tgchd', w, eye).reshape(N_RG_TILES, MXU_DIM, MXU_DIM)
    return jnp.concatenate([bd(wa), bd(wx)], axis=2)


def kernel(x_prompt, x_sample, state_conv, state_rglru, state_gla, c_prompt, c_sample, w_mod, b_mod, w_in, b_in,
           conv_w, conv_b, rg_wa, rg_ba, rg_wx, rg_bx, rg_lambda, gla_wa2, gla_ba, gla_norm_g, p_a, p_b, w_o,
           ln1_g, ln1_b, w_grp, b_grp, w_exp, b_exp, we_gate, we_up, we_down, ln2_g, ln2_b):
    assert w_mod.shape[0] == DEPTH == 1
    bp, tp, _ = x_prompt.shape
    ns = x_sample.shape[0]
    n_p = bp * tp
    n_tok = n_p + ns
    row = lambda a: a.reshape(1, -1)

    lo = 2 * D_RNN + 2 * GLA_DK + 2 * GLA_DV
    w_in0 = w_in[0]
    assert lo == O_GA
    win_a = _pack_rows(w_in0, lo)
    win_b = _pack_rows(w_in0[:, lo + GLA_RANK:])
    bin_main = row(jnp.concatenate([b_in[0, :lo], b_in[0, lo + GLA_RANK:]]))
    walr = _pack_rows(jnp.pad(w_in0[:, lo:lo + GLA_RANK], ((0, 0), (0, LANES - GLA_RANK))))
    balr = row(jnp.pad(b_in[0, lo:lo + GLA_RANK], (0, LANES - GLA_RANK)))
    wa2 = _pack_rows(jnp.pad(gla_wa2[0], ((0, LANES - GLA_RANK), (0, 0))))
    wrg = _pack_rows(_block_diag_gate_weights(rg_wa[0], rg_wx[0]).reshape(RG_BLOCKS * RG_BW, 2 * MXU_DIM))
    wrg = wrg.reshape(N_RG_TILES, MXU_DIM // 2, 2 * MXU_DIM)
    w_route = _pack_rows(jnp.pad(jnp.concatenate([w_exp[0], w_grp[0]], axis=1),
                                 ((0, 0), (0, LANES - N_EXPERTS - N_GROUPS))))
    b_route = row(jnp.pad(jnp.concatenate([b_exp[0], b_grp[0]]), (0, LANES - N_EXPERTS - N_GROUPS)))
    pa, pb, wo = _pack_rows(p_a[0]), _pack_rows(p_b[0]), _pack_rows(w_o[0])
    pre_w = [win_a, win_b, bin_main, walr, balr, wa2, row(gla_ba[0])]
    rec_w = [conv_w[0], row(conv_b[0]), wrg, row(rg_ba[0]), row(rg_bx[0]), row(rg_lambda[0])]
    post_w = [row(gla_norm_g[0]), pa, pb, wo, row(ln1_g[0]), row(ln1_b[0]), w_route, b_route]

    mod = _mod_call(jnp.concatenate([c_prompt, c_sample], axis=0), w_mod[0], row(b_mod[0]))
    mod_p = mod[:bp].reshape(bp, 6, D_MODEL)
    mod_s = mod[bp:].reshape(ns, 6, D_MODEL).transpose(1, 0, 2)

    mix_w = pre_w + [post_w[0]] + rec_w + post_w[1:]
    x1_p, hf_p, route_p, routet_p, conv_p, h_p, s_p = _mixer_call(x_prompt, mod_p, mix_w)

    xs = x_sample.reshape(ns, D_MODEL)
    sconv = state_conv[0].transpose(1, 0, 2)
    conv_s, h_s, ya_s, q_s, k_s, v_s, la_s, gg_s, ga_s, gb_s = _s_pre_call(
        xs, mod_s, sconv, state_rglru[0], pre_w + rec_w)
    s_s, o_s = _s_state_call(state_gla[0], q_s, k_s, v_s, la_s)
    x1_s, hf_s, route_s, routet_s = _s_post_call(xs, mod_s, o_s, gg_s, ga_s, gb_s, ya_s, post_w)

    route_pf = route_p.reshape(n_p, LANES)
    eid = jnp.concatenate([routet_p[:2], routet_s[:2]], axis=1).astype(I32)
    assert n_tok % LANES == 0
    n_chunks = n_tok // LANES
    rank3, cnt = _rank_call(eid.reshape(2, n_chunks, LANES).transpose(1, 0, 2))
    rank = rank3.transpose(1, 0, 2).reshape(2, n_tok)
    counts = cnt[:, 0]
    pcounts = (counts + MOE_TM - 1) // MOE_TM * MOE_TM
    pends = jnp.cumsum(pcounts)
    pstarts = pends - pcounts
    dest = (pstarts[eid] + rank).astype(I32)
    dest0, dest1 = dest[0], dest[1]
    n_tiles = -(-(2 * n_tok + N_EXPERTS * (MOE_TM - 1)) // MOE_TM)
    tile_start = jnp.arange(n_tiles, dtype=I32) * MOE_TM
    blk_e = jnp.minimum(jnp.sum(pends[None, :] <= tile_start[:, None], axis=1), N_EXPERTS - 1).astype(I32)
    n_used = (pends[-1] // MOE_TM).astype(I32).reshape(1)

    pends32 = pends.astype(I32)
    xbuf = _dispatch_call(pends32, dest0, dest1, hf_p.reshape(n_p * SUBLANES, LANES), hf_s, n_tiles * MOE_TM)
    ybuf = _expert_call(blk_e, n_used, xbuf, we_gate[0], we_up[0], we_down[0])

    y_p = _final_call(dest0, dest1, 0, x1_p.reshape(n_p, D_MODEL), ybuf, route_pf, mod_p[:, 5:6, :],
                      row(ln2_g[0]), row(ln2_b[0]), tp, ROW_CHUNK)
    y_s = _final_call(dest0, dest1, n_p, x1_s, ybuf, route_s, mod_s[5][None], row(ln2_g[0]), row(ln2_b[0]), ns, ns)

    return (y_p.reshape(bp, tp, D_MODEL), y_s.reshape(ns, 1, D_MODEL),
            conv_p[None], h_p.reshape(1, bp, D_RNN), s_p[None],
            conv_s.transpose(1, 0, 2)[None], h_s[None], s_s[None])
```

```python
import functools

import jax
import jax.numpy as jnp
import numpy as np
from jax import lax
from jax.experimental import pallas as pl
from jax.experimental.pallas import tpu as pltpu

F32 = jnp.float32
BF16 = jnp.bfloat16
I32 = jnp.int32

D_MODEL = 1024
D_RNN = D_MODEL
RG_BLOCKS = 16
RG_BW = D_RNN // RG_BLOCKS
CONV_W = 4
RG_C = 8.0
GLA_HEADS = 4
GLA_DK = D_MODEL // 2
GLA_DV = D_MODEL
GLA_DKH = GLA_DK // GLA_HEADS
GLA_DVH = GLA_DV // GLA_HEADS
GLA_RANK = 16
GLA_TAU = 16.0
GLA_CHUNK = 64
N_GROUPS = 4
EXP_PER_GROUP = 8
N_EXPERTS = N_GROUPS * EXP_PER_GROUP
D_EXPERT = 512
DEPTH = 1
ALPHA = (2.0 * DEPTH) ** 0.25
LN_EPS = 1e-5
RMS_EPS = 1e-6

LANES = 128
SUBLANES = 8
MXU_DIM = 256
VMEM_LIMIT_BYTES = 56 * 1024 * 1024

MIX_TT = 256
STAGE1_COLS = 512
PIPE_DEPTH = 2
MOE_TM = 256
ROW_CHUNK = 128
SAMPLE_SB = 8
RG_GROUP = MXU_DIM // RG_BW
N_RG_TILES = RG_BLOCKS // RG_GROUP
N_MAIN = 2 * D_RNN + 2 * GLA_DK + 2 * GLA_DV + 2 * D_MODEL
ROUTE_G0 = N_EXPERTS

O_RX, O_RY, O_Q, O_K, O_V, O_GG, O_GA, O_GB = 0, 1024, 2048, 2560, 3072, 4096, 5120, 6144


def _ln(x):
    mu = jnp.mean(x, -1, keepdims=True)
    xc = x - mu
    var = jnp.mean(xc * xc, -1, keepdims=True)
    return xc * lax.rsqrt(var + LN_EPS)


def _dot(a, b):
    return jnp.dot(a, b, preferred_element_type=F32)


def _pack_kernel(w_ref, o_ref):
    o_ref[...] = pltpu.bitcast(w_ref[...].astype(BF16), jnp.uint32)


def _pack_rows(w, n=None):
    k = w.shape[0]
    n = w.shape[1] if n is None else n
    cn = min(n, 4 * LANES)
    assert k % (2 * SUBLANES) == 0 and n % cn == 0
    return pl.pallas_call(
        _pack_kernel,
        grid=(n // cn,),
        in_specs=[pl.BlockSpec((k, cn), lambda i: (0, i))],
        out_specs=pl.BlockSpec((k // 2, cn), lambda i: (0, i)),
        out_shape=jax.ShapeDtypeStruct((k // 2, n), jnp.uint32),
        compiler_params=pltpu.CompilerParams(dimension_semantics=("arbitrary",)),
        name="pack",
    )(w)


def _win(win_ref, winb_ref, lo, hi):
    if hi <= O_GA:
        return win_ref[:, lo:hi]
    assert lo >= O_GA
    return winb_ref[:, lo - O_GA:hi - O_GA]


def _w(packed):
    return pltpu.bitcast(packed, BF16)


def _expm1(x):
    u = jnp.exp(x)
    small = (u - 1.0) * x / jnp.log(u)
    return jnp.where(u == 1.0, x, jnp.where(jnp.abs(x) < 0.5, small, u - 1.0))


def _const_spec(shape):
    nd = len(shape)
    return pl.BlockSpec(shape, lambda *_: (0,) * nd, pipeline_mode=pl.Buffered(1))


assert D_MODEL == SUBLANES * LANES


def _tok_rows(t, count=1):
    return pl.ds(pl.multiple_of(t * SUBLANES, SUBLANES), count * SUBLANES)


def _store_token_tiles(ref, rows):
    n = rows.shape[0]
    for s in range(SUBLANES):
        ref[pl.ds(s, n, stride=SUBLANES), :] = rows[:, s * LANES:(s + 1) * LANES]


def _load_token_tiles(ref):
    n = ref.shape[0] // SUBLANES
    return jnp.concatenate([ref[pl.ds(s, n, stride=SUBLANES), :] for s in range(SUBLANES)], axis=1)


def _mod_kernel(c_ref, w_ref, b_ref, o_ref):
    c = c_ref[...]
    s = jax.nn.silu(c)
    o_ref[...] = _dot(s.astype(BF16), w_ref[...].astype(BF16)) + b_ref[...]


def _mod_call(c_all, w_mod, b_mod):
    n = c_all.shape[0]
    tn = 512
    return pl.pallas_call(
        _mod_kernel,
        grid=(6 * D_MODEL // tn,),
        in_specs=[pl.BlockSpec((n, D_MODEL), lambda i: (0, 0)),
                  pl.BlockSpec((D_MODEL, tn), lambda i: (0, i)),
                  pl.BlockSpec((1, tn), lambda i: (0, i))],
        out_specs=pl.BlockSpec((n, tn), lambda i: (0, i)),
        out_shape=jax.ShapeDtypeStruct((n, 6 * D_MODEL), F32),
        compiler_params=pltpu.CompilerParams(dimension_semantics=("arbitrary",)),
        name="mod",
    )(c_all, w_mod, b_mod)


def _no_op():
    pass


def _gate_logits(xc, wrg_ref, rba, rbx):
    xcb = xc.astype(BF16)
    parts = [_dot(xcb[:, c * MXU_DIM:(c + 1) * MXU_DIM], _w(wrg_ref[c])) for c in range(N_RG_TILES)]
    r_pre = jnp.concatenate([p[:, :MXU_DIM] for p in parts], axis=1)
    i_pre = jnp.concatenate([p[:, MXU_DIM:] for p in parts], axis=1)
    return r_pre, i_pre, rba, rbx


def _gate_logits_to(out_ref, xc, wrg_ref):
    xcb = xc.astype(BF16)
    for c in range(N_RG_TILES):
        p = _dot(xcb[:, c * MXU_DIM:(c + 1) * MXU_DIM], _w(wrg_ref[c]))
        out_ref[:, c * MXU_DIM:(c + 1) * MXU_DIM] = p[:, :MXU_DIM]
        out_ref[:, D_RNN + c * MXU_DIM:D_RNN + (c + 1) * MXU_DIM] = p[:, MXU_DIM:]


def _gates(xc, r_pre, i_pre, rba, rbx, lam, between=_no_op):
    r = jax.nn.sigmoid(r_pre + rba)
    between()
    ig = jax.nn.sigmoid(i_pre + rbx)
    log_a = -RG_C * r * jax.nn.softplus(-lam)
    a = jnp.exp(log_a)
    between()
    u = jnp.sqrt(-_expm1(2.0 * log_a)) * (ig * xc)
    between()
    return a, u


def _low_rank(hb, walr_ref, balr_ref):
    return _dot(hb, _w(walr_ref[...])) + balr_ref[...]


def _log_decay_from(alr, wa2_ref, gba_ref):
    return jax.nn.log_sigmoid(_dot(alr.astype(BF16), _w(wa2_ref[...])) + gba_ref[...]) / GLA_TAU


def _log_decay(hb, walr_ref, balr_ref, wa2_ref, gba_ref):
    return _log_decay_from(_low_rank(hb, walr_ref, balr_ref), wa2_ref, gba_ref)


def _post(x, o, gg, ga, gb, y_a, g1, sh2, sc2, gng, pa_ref, pb_ref, wo_ref, l1g, l1b, wr_ref, br,
          between=_no_op):
    rows = x.shape[0]
    heads = []
    for hh in range(GLA_HEADS):
        oh = o[:, hh * GLA_DVH:(hh + 1) * GLA_DVH]
        ms = jnp.mean(oh * oh, -1, keepdims=True)
        heads.append(oh * lax.rsqrt(ms + RMS_EPS) * gng)
    y_b = jnp.concatenate(heads, axis=1) * jax.nn.silu(gg)
    merged = (jax.nn.sigmoid(ga) * _dot(y_a.astype(BF16), _w(pa_ref[...]))
              + jax.nn.sigmoid(gb) * _dot(y_b.astype(BF16), _w(pb_ref[...])))
    mix = _dot(merged.astype(BF16), _w(wo_ref[...]))
    between()
    x1 = _ln(ALPHA * x + g1 * mix) * l1g + l1b
    between()
    hf = _ln(x1) * (1.0 + sc2) + sh2
    logits = _dot(hf.astype(BF16), _w(wr_ref[...])) + br
    lane = lax.broadcasted_iota(I32, (rows, LANES), 1).astype(F32)
    neg = jnp.float32(-jnp.inf)
    big = jnp.float32(LANES)
    g_valid = (lane >= ROUTE_G0) & (lane < ROUTE_G0 + N_GROUPS)
    gl = jnp.where(g_valid, logits, neg)
    gmax = jnp.max(gl, -1, keepdims=True)
    g_lane = jnp.min(jnp.where(gl == gmax, lane, big), -1, keepdims=True)
    g_w = 1.0 / jnp.sum(jnp.exp(gl - gmax), -1, keepdims=True)
    e_lo = (g_lane - ROUTE_G0) * EXP_PER_GROUP
    el = jnp.where((lane >= e_lo) & (lane < e_lo + EXP_PER_GROUP), logits, neg)
    t1 = jnp.max(el, -1, keepdims=True)
    i1 = jnp.min(jnp.where(el == t1, lane, big), -1, keepdims=True)
    el2 = jnp.where(lane == i1, neg, el)
    t2 = jnp.max(el2, -1, keepdims=True)
    i2 = jnp.min(jnp.where(el2 == t2, lane, big), -1, keepdims=True)
    e2 = jnp.exp(t2 - t1)
    den = 1.0 + e2
    w1 = (1.0 / den) * g_w
    w2 = (e2 / den) * g_w
    route = jnp.where(lane == 0, i1.astype(F32),
                      jnp.where(lane == 1, i2.astype(F32),
                                jnp.where(lane == 2, w1, jnp.where(lane == 3, w2, 0.0))))
    return x1, hf, route


def _shift_rows(x, s):
    return pltpu.roll(x, s, 0)


def _mixer_kernel(xn_ref, modn_ref, x_ref, mod_ref, win_ref, winb_ref, bin_ref, walr_ref, balr_ref, wa2_ref, gba_ref,
                  gng_ref, cw_ref, cb_ref, wrg_ref, rba_ref, rbx_ref, lam_ref, pa_ref, pb_ref, wo_ref,
                  l1g_ref, l1b_ref, wr_ref, br_ref,
                  x1_ref, hf_ref, route_ref, routet_ref, convn_ref, hlast_ref, sfin_ref,
                  hbn_scr, hbc_scr, pn_scr, pc_scr, alrn_scr, alrc_scr, xcn_scr, xcc_scr, gaten_scr, gatec_scr,
                  rxbuf, hcar, st_ref, o_scr,
                  *, tiles_per_seq):
    i = pl.program_id(0)
    t = jnp.maximum(i - PIPE_DEPTH, 0)
    j = t % tiles_per_seq
    j1 = jnp.maximum(i - 1, 0) % tiles_per_seq
    tt = x_ref.shape[1]

    @pl.when(i == 0)
    def _no_tile_yet():
        hbc_scr[...] = jnp.zeros_like(hbc_scr)
        pc_scr[...] = jnp.zeros_like(pc_scr)
        alrc_scr[...] = jnp.zeros_like(alrc_scr)
        xcc_scr[...] = jnp.zeros_like(xcc_scr)
        gatec_scr[...] = jnp.zeros_like(gatec_scr)

    @pl.when(j1 == 0)
    def _init_conv():
        rxbuf[0:SUBLANES, :] = jnp.zeros((SUBLANES, D_RNN), F32)

    @pl.when(j == 0)
    def _init():
        hcar[...] = jnp.zeros_like(hcar)
        st_ref[...] = jnp.zeros_like(st_ref)

    def stage0():
        modn = modn_ref[0]
        hbn_scr[...] = (_ln(xn_ref[0]) * (1.0 + modn[1:2]) + modn[0:1]).astype(BF16)

    hbn = hbc_scr[...]

    def _proj_block(lo):
        def run():
            hi = lo + STAGE1_COLS
            pn_scr[:, lo:hi] = _dot(hbn, _w(_win(win_ref, winb_ref, lo, hi)))
        return run

    def _alr_block():
        alrn_scr[...] = _dot(hbn, _w(walr_ref[...]))

    pending = [_proj_block(lo) for lo in range(0, N_MAIN, STAGE1_COLS)] + [_alr_block]

    def pump(count=1):
        for _ in range(min(count, len(pending))):
            pending.pop(0)()

    x = x_ref[0]
    mod = mod_ref[0]
    g1, sh2, sc2 = mod[2:3], mod[3:4], mod[4:5]

    def proj(lo, hi):
        return pc_scr[:, lo:hi] + bin_ref[:, lo:hi]

    pump(O_RY // STAGE1_COLS)

    rx = pn_scr[:, O_RX:O_RX + D_RNN] + bin_ref[:, O_RX:O_RX + D_RNN]
    rxbuf[SUBLANES:SUBLANES + tt, :] = rx
    cw = cw_ref[...]
    xcn = cb_ref[...] + rxbuf[SUBLANES - 3:SUBLANES - 3 + tt, :] * cw[0:1]
    xcn = xcn + rxbuf[SUBLANES - 2:SUBLANES - 2 + tt, :] * cw[1:2]
    xcn = xcn + rxbuf[SUBLANES - 1:SUBLANES - 1 + tt, :] * cw[2:3]
    xcn = xcn + rx * cw[3:4]
    rxbuf[0:SUBLANES, :] = rxbuf[tt:tt + SUBLANES, :]
    xcn_scr[...] = xcn
    xc = xcc_scr[...]

    q = proj(O_Q, O_Q + GLA_DK) * (GLA_DKH ** -0.5)
    k = proj(O_K, O_K + GLA_DK)
    v = proj(O_V, O_V + GLA_DV)
    la = _log_decay_from(alrc_scr[...] + balr_ref[...], wa2_ref, gba_ref)
    pump()
    rowk = lax.broadcasted_iota(I32, (tt, GLA_DK), 0) % GLA_CHUNK
    bcum = la
    s = 1
    while s < GLA_CHUNK:
        bcum = bcum + jnp.where(rowk >= s, _shift_rows(bcum, s), 0.0)
        s *= 2
    eb = jnp.exp(bcum)
    q_in = (q * eb).astype(BF16)
    k_in = (k * jnp.exp(-bcum)).astype(BF16)
    pump()
    stage0()
    gelu_ry = jax.nn.gelu(proj(O_RY, O_RY + D_RNN))
    pump()

    a, u = _gates(xc, gatec_scr[:, :D_RNN], gatec_scr[:, D_RNN:], rba_ref[...], rbx_ref[...], lam_ref[...],
                  between=pump)
    row = lax.broadcasted_iota(I32, (tt, D_RNN), 0)
    s = 1
    while s < tt:
        keep = row >= s
        a_s = jnp.where(keep, _shift_rows(a, s), 1.0)
        u_s = jnp.where(keep, _shift_rows(u, s), 0.0)
        u = a * u_s + u
        a = a * a_s
        if s < SUBLANES:
            pump()
        s *= 2
    hseq = a * hcar[0:1, :] + u
    hcar[0:1, :] = hseq[tt - 1:tt, :]
    y_a = hseq * gelu_ry

    tri = (lax.broadcasted_iota(I32, (GLA_CHUNK, GLA_CHUNK), 0)
           >= lax.broadcasted_iota(I32, (GLA_CHUNK, GLA_CHUNK), 1))
    n_chunks = tt // GLA_CHUNK
    nt_dims = (((1,), (1,)), ((), ()))
    attn, d_st, dec, vh_all = {}, {}, {}, {}
    for c in range(n_chunks):
        r0 = c * GLA_CHUNK
        for hh in range(GLA_HEADS):
            kc = slice(hh * GLA_DKH, (hh + 1) * GLA_DKH)
            b = bcum[r0:r0 + GLA_CHUNK, kc]
            btot = b[GLA_CHUNK - 1:GLA_CHUNK, :]
            k_out = (k[r0:r0 + GLA_CHUNK, kc] * jnp.exp(btot - b)).astype(BF16)
            vh = v[r0:r0 + GLA_CHUNK, hh * GLA_DVH:(hh + 1) * GLA_DVH].astype(BF16)
            scores = lax.dot_general(q_in[r0:r0 + GLA_CHUNK, kc], k_in[r0:r0 + GLA_CHUNK, kc], nt_dims,
                                     preferred_element_type=F32)
            attn[c, hh] = jnp.where(tri, scores, 0.0).astype(BF16)
            d_st[c, hh] = lax.dot_general(vh, k_out, (((0,), (0,)), ((), ())), preferred_element_type=F32)
            dec[c, hh] = jnp.exp(btot)
            vh_all[c, hh] = vh
        pump()
    starts = {}
    for hh in range(GLA_HEADS):
        st = st_ref[hh]
        for c in range(n_chunks):
            starts[c, hh] = st.astype(BF16)
            st = st * dec[c, hh] + d_st[c, hh]
        st_ref[hh] = st
    for c in range(n_chunks):
        r0 = c * GLA_CHUNK
        for hh in range(GLA_HEADS):
            qi = q_in[r0:r0 + GLA_CHUNK, hh * GLA_DKH:(hh + 1) * GLA_DKH]
            inter = lax.dot_general(qi, starts[c, hh], nt_dims, preferred_element_type=F32)
            o_scr[r0:r0 + GLA_CHUNK, hh * GLA_DVH:(hh + 1) * GLA_DVH] = _dot(attn[c, hh], vh_all[c, hh]) + inter

    x1, hf, route = _post(x, o_scr[...], proj(O_GG, O_GG + GLA_DV), proj(O_GA, O_GA + D_MODEL),
                          proj(O_GB, O_GB + D_MODEL), y_a, g1, sh2, sc2, gng_ref[...],
                          pa_ref, pb_ref, wo_ref, l1g_ref[...], l1b_ref[...], wr_ref, br_ref[...], between=pump)
    x1_ref[0] = x1
    _store_token_tiles(hf_ref.at[0], hf)
    route_ref[0] = route
    routet_ref[...] = route.T[0:SUBLANES, :]

    pump(len(pending))
    _gate_logits_to(gaten_scr, xcn, wrg_ref)
    pc_scr[:, O_RY:] = pn_scr[:, O_RY:]
    alrc_scr[...] = alrn_scr[...]
    hbc_scr[...] = hbn_scr[...]
    xcc_scr[...] = xcn_scr[...]
    gatec_scr[...] = gaten_scr[...]

    @pl.when((j1 == tiles_per_seq - 1) & (i >= 1))
    def _final_conv():
        convn_ref[0] = rxbuf[SUBLANES - (CONV_W - 1):SUBLANES, :]

    @pl.when((j == tiles_per_seq - 1) & (i >= PIPE_DEPTH))
    def _final():
        hlast_ref[0] = hcar[0:1, :]
        for hh in range(GLA_HEADS):
            sfin_ref[0, hh] = st_ref[hh].T


def _mixer_call(x, mod3, wts):
    b, t, _ = x.shape
    tt = MIX_TT
    assert t % tt == 0 and tt % GLA_CHUNK == 0
    nt = t // tt
    n_tiles = b * nt
    assert nt > 1
    cur = lambda i: jnp.maximum(i - PIPE_DEPTH, 0)
    nxt = lambda i: jnp.minimum(i, n_tiles - 1)
    tok = lambda i: (cur(i) // nt, cur(i) % nt, 0)
    tok_n = lambda i: (nxt(i) // nt, nxt(i) % nt, 0)
    per_b = lambda i: (cur(i) // nt, 0, 0)
    in_specs = [pl.BlockSpec((1, tt, D_MODEL), tok_n),
                pl.BlockSpec((1, 6, D_MODEL), lambda i: (nxt(i) // nt, 0, 0)),
                pl.BlockSpec((1, tt, D_MODEL), tok),
                pl.BlockSpec((1, 6, D_MODEL), per_b)] + [_const_spec(w.shape) for w in wts]
    out_specs = [pl.BlockSpec((1, tt, D_MODEL), tok),
                 pl.BlockSpec((1, tt * SUBLANES, LANES), tok),
                 pl.BlockSpec((1, tt, LANES), tok),
                 pl.BlockSpec((SUBLANES, tt), lambda i: (0, cur(i))),
                 pl.BlockSpec((1, CONV_W - 1, D_RNN), per_b),
                 pl.BlockSpec((1, 1, D_RNN), per_b),
                 pl.BlockSpec((1, GLA_HEADS, GLA_DKH, GLA_DVH), lambda i: (cur(i) // nt, 0, 0, 0))]
    out_shape = [jax.ShapeDtypeStruct((b, t, D_MODEL), F32),
                 jax.ShapeDtypeStruct((b, t * SUBLANES, LANES), F32),
                 jax.ShapeDtypeStruct((b, t, LANES), F32),
                 jax.ShapeDtypeStruct((SUBLANES, b * t), F32),
                 jax.ShapeDtypeStruct((b, CONV_W - 1, D_RNN), F32),
                 jax.ShapeDtypeStruct((b, 1, D_RNN), F32),
                 jax.ShapeDtypeStruct((b, GLA_HEADS, GLA_DKH, GLA_DVH), F32)]
    scratch = [pltpu.VMEM((tt, D_MODEL), BF16), pltpu.VMEM((tt, D_MODEL), BF16),
               pltpu.VMEM((tt, N_MAIN), F32), pltpu.VMEM((tt, N_MAIN), F32),
               pltpu.VMEM((tt, LANES), F32), pltpu.VMEM((tt, LANES), F32),
               pltpu.VMEM((tt, D_RNN), F32), pltpu.VMEM((tt, D_RNN), F32),
               pltpu.VMEM((tt, 2 * D_RNN), F32), pltpu.VMEM((tt, 2 * D_RNN), F32),
               pltpu.VMEM((tt + SUBLANES, D_RNN), F32),
               pltpu.VMEM((SUBLANES, D_RNN), F32),
               pltpu.VMEM((GLA_HEADS, GLA_DVH, GLA_DKH), F32),
               pltpu.VMEM((tt, GLA_DV), F32)]
    return pl.pallas_call(
        functools.partial(_mixer_kernel, tiles_per_seq=nt),
        grid=(n_tiles + PIPE_DEPTH,),
        in_specs=in_specs,
        out_specs=out_specs,
        out_shape=out_shape,
        scratch_shapes=scratch,
        compiler_params=pltpu.CompilerParams(dimension_semantics=("arbitrary",),
                                             vmem_limit_bytes=VMEM_LIMIT_BYTES),
        name="mixer",
    )(x, mod3, x, mod3, *wts)


def _s_pre_kernel(x_ref, mod_ref, sconv_ref, h0_ref, win_ref, winb_ref, bin_ref, walr_ref, balr_ref, wa2_ref, gba_ref,
                  cw_ref, cb_ref, wrg_ref, rba_ref, rbx_ref, lam_ref,
                  convn_ref, hnew_ref, ya_ref, q_ref, k_ref, v_ref, la_ref, gg_ref, ga_ref, gb_ref):
    x = x_ref[...]
    hb = (_ln(x) * (1.0 + mod_ref[1]) + mod_ref[0]).astype(BF16)

    def proj(lo, hi):
        return _dot(hb, _w(_win(win_ref, winb_ref, lo, hi))) + bin_ref[:, lo:hi]

    rx = proj(O_RX, O_RX + D_RNN)
    cw = cw_ref[...]
    xc = cb_ref[...] + sconv_ref[0] * cw[0:1]
    xc = xc + sconv_ref[1] * cw[1:2]
    xc = xc + sconv_ref[2] * cw[2:3]
    xc = xc + rx * cw[3:4]
    convn_ref[0] = sconv_ref[1]
    convn_ref[1] = sconv_ref[2]
    convn_ref[2] = rx
    a, u = _gates(xc, *_gate_logits(xc, wrg_ref, rba_ref[...], rbx_ref[...]), lam_ref[...])
    hnew = u + a * h0_ref[...]
    hnew_ref[...] = hnew
    ya_ref[...] = hnew * jax.nn.gelu(proj(O_RY, O_RY + D_RNN))
    q_ref[...] = proj(O_Q, O_Q + GLA_DK) * (GLA_DKH ** -0.5)
    k_ref[...] = proj(O_K, O_K + GLA_DK)
    v_ref[...] = proj(O_V, O_V + GLA_DV)
    la_ref[...] = _log_decay(hb, walr_ref, balr_ref, wa2_ref, gba_ref)
    gg_ref[...] = proj(O_GG, O_GG + GLA_DV)
    ga_ref[...] = proj(O_GA, O_GA + D_MODEL)
    gb_ref[...] = proj(O_GB, O_GB + D_MODEL)


def _s_pre_call(x_s, mod_s, sconv, h0, wts):
    n = x_s.shape[0]
    full = lambda shape: pl.BlockSpec(shape, lambda i: (0,) * len(shape))
    ins = [x_s, mod_s, sconv, h0] + list(wts)
    shapes = [(CONV_W - 1, n, D_RNN), (n, D_RNN), (n, D_RNN), (n, GLA_DK), (n, GLA_DK), (n, GLA_DV),
              (n, GLA_DK), (n, GLA_DV), (n, D_MODEL), (n, D_MODEL)]
    return pl.pallas_call(
        _s_pre_kernel,
        grid=(1,),
        in_specs=[full(a.shape) for a in ins],
        out_specs=[full(s) for s in shapes],
        out_shape=[jax.ShapeDtypeStruct(s, F32) for s in shapes],
        compiler_params=pltpu.CompilerParams(dimension_semantics=("arbitrary",),
                                             vmem_limit_bytes=VMEM_LIMIT_BYTES),
        name="s_pre",
    )(*ins)


def _to_column(row, n):
    eye = lax.broadcasted_iota(I32, (n, n), 0) == lax.broadcasted_iota(I32, (n, n), 1)
    return jnp.sum(jnp.where(eye, jnp.broadcast_to(row, (n, n)), 0.0), axis=1, keepdims=True)


def _s_state_kernel(s_ref, q_ref, k_ref, v_ref, la_ref, snew_ref, o_ref):
    sb = s_ref.shape[0]
    qb = q_ref[...].astype(BF16)
    dec = jnp.exp(la_ref[...])
    kk = k_ref[...]
    vv = v_ref[...]
    for si in range(sb):
        for hh in range(GLA_HEADS):
            kc = slice(hh * GLA_DKH, (hh + 1) * GLA_DKH)
            vc = slice(hh * GLA_DVH, (hh + 1) * GLA_DVH)
            d_col = _to_column(dec[si:si + 1, kc], GLA_DKH)
            k_col = _to_column(kk[si:si + 1, kc], GLA_DKH)
            s_new = s_ref[si, hh] * d_col + k_col * vv[si:si + 1, vc]
            snew_ref[si, hh] = s_new
            o_all = _dot(qb[:, kc], s_new.astype(BF16))
            o_ref[si:si + 1, vc] = o_all[si:si + 1, :]


def _s_state_call(state, q, k, v, la):
    n = state.shape[0]
    sb = SAMPLE_SB
    assert n % sb == 0
    blk = lambda w: pl.BlockSpec((sb, w), lambda i: (i, 0))
    st_spec = pl.BlockSpec((sb, GLA_HEADS, GLA_DKH, GLA_DVH), lambda i: (i, 0, 0, 0))
    return pl.pallas_call(
        _s_state_kernel,
        grid=(n // sb,),
        in_specs=[st_spec, blk(GLA_DK), blk(GLA_DK), blk(GLA_DV), blk(GLA_DK)],
        out_specs=[st_spec, blk(GLA_DV)],
        out_shape=[jax.ShapeDtypeStruct(state.shape, F32), jax.ShapeDtypeStruct((n, GLA_DV), F32)],
        compiler_params=pltpu.CompilerParams(dimension_semantics=("arbitrary",),
                                             vmem_limit_bytes=VMEM_LIMIT_BYTES),
        name="s_state",
    )(state, q, k, v, la)


def _s_post_kernel(x_ref, mod_ref, o_ref, gg_ref, ga_ref, gb_ref, ya_ref, gng_ref, pa_ref, pb_ref, wo_ref,
                   l1g_ref, l1b_ref, wr_ref, br_ref, x1_ref, hf_ref, route_ref, routet_ref):
    x1, hf, route = _post(x_ref[...], o_ref[...], gg_ref[...], ga_ref[...], gb_ref[...], ya_ref[...],
                          mod_ref[2], mod_ref[3], mod_ref[4], gng_ref[...], pa_ref, pb_ref, wo_ref,
                          l1g_ref[...], l1b_ref[...], wr_ref, br_ref[...])
    x1_ref[...] = x1
    _store_token_tiles(hf_ref, hf)
    route_ref[...] = route
    routet_ref[...] = route.T[0:SUBLANES, :]


def _s_post_call(x_s, mod_s, o, gg, ga, gb, ya, wts):
    n = x_s.shape[0]
    full = lambda shape: pl.BlockSpec(shape, lambda i: (0,) * len(shape))
    ins = [x_s, mod_s, o, gg, ga, gb, ya] + list(wts)
    shapes = [(n, D_MODEL), (n * SUBLANES, LANES), (n, LANES), (SUBLANES, n)]
    return pl.pallas_call(
        _s_post_kernel,
        grid=(1,),
        in_specs=[full(a.shape) for a in ins],
        out_specs=[full(s) for s in shapes],
        out_shape=[jax.ShapeDtypeStruct(s, F32) for s in shapes],
        compiler_params=pltpu.CompilerParams(dimension_semantics=("arbitrary",),
                                             vmem_limit_bytes=VMEM_LIMIT_BYTES),
        name="s_post",
    )(*ins)


def _rank_kernel(eid_ref, rank_ref, cnt_ref, carry):
    n_chunks = eid_ref.shape[0]
    carry[...] = jnp.zeros_like(carry)
    e_iota = lax.broadcasted_iota(I32, (N_EXPERTS, LANES), 0)
    upper = (lax.broadcasted_iota(I32, (LANES, LANES), 0)
             < lax.broadcasted_iota(I32, (LANES, LANES), 1)).astype(BF16)

    def body(c, _):
        ids = eid_ref[c]
        oh0 = (e_iota == ids[0:1, :])
        oh1 = (e_iota == ids[1:2, :])
        both = (oh0 | oh1).astype(F32)
        before = _dot(both.astype(BF16), upper) + carry[:, 0:1]
        r0 = jnp.sum(jnp.where(oh0, before, 0.0), axis=0, keepdims=True)
        r1 = jnp.sum(jnp.where(oh1, before, 0.0), axis=0, keepdims=True)
        rank_ref[c] = jnp.concatenate([r0, r1], axis=0).astype(I32)
        carry[...] = carry[...] + jnp.sum(both, axis=1, keepdims=True)
        return 0

    lax.fori_loop(0, n_chunks, body, 0)
    cnt_ref[...] = carry[...].astype(I32)


def _rank_call(eid3):
    n_chunks = eid3.shape[0]
    full = lambda shape: pl.BlockSpec(shape, lambda i: (0,) * len(shape))
    return pl.pallas_call(
        _rank_kernel,
        grid=(1,),
        in_specs=[full(eid3.shape)],
        out_specs=[full(eid3.shape), full((N_EXPERTS, LANES))],
        out_shape=[jax.ShapeDtypeStruct(eid3.shape, I32), jax.ShapeDtypeStruct((N_EXPERTS, LANES), I32)],
        scratch_shapes=[pltpu.VMEM((N_EXPERTS, LANES), F32)],
        compiler_params=pltpu.CompilerParams(dimension_semantics=("arbitrary",)),
        name="rank",
    )(eid3)


def _dispatch_kernel(pends_ref, d0_ref, d1_ref, src_p_ref, src_s_ref, buf_ref, sem, zeros_ref, zsem, *,
                     n_p_chunks):
    i = pl.program_id(0)
    n_tiles = buf_ref.shape[0] // (MOE_TM * SUBLANES)

    @pl.when(i == 0)
    def _clear():
        zeros_ref[...] = jnp.zeros_like(zeros_ref)

        def tile_copy(row0):
            return pltpu.make_async_copy(zeros_ref, buf_ref.at[_tok_rows(row0, MOE_TM)], zsem)

        def nonempty(e):
            return pends_ref[e] > jnp.where(e == 0, 0, pends_ref[jnp.maximum(e - 1, 0)])

        def start_tail(e, _):
            @pl.when(nonempty(e))
            def _():
                tile_copy(pends_ref[e] - MOE_TM).start()
            return 0

        def wait_tail(e, _):
            @pl.when(nonempty(e))
            def _():
                tile_copy(0).wait()
            return 0

        def start_unused(j, _):
            tile_copy(j * MOE_TM).start()
            return 0

        def wait_unused(j, _):
            tile_copy(0).wait()
            return 0

        first_unused = pends_ref[N_EXPERTS - 1] // MOE_TM
        lax.fori_loop(0, N_EXPERTS, start_tail, 0)
        lax.fori_loop(first_unused, n_tiles, start_unused, 0)
        lax.fori_loop(0, N_EXPERTS, wait_tail, 0)
        lax.fori_loop(first_unused, n_tiles, wait_unused, 0)

    def scatter_rows(src_ref):
        n = src_ref.shape[0] // SUBLANES

        def start_rows(t, _):
            for k, dest_ref in enumerate((d0_ref, d1_ref)):
                pltpu.make_async_copy(src_ref.at[_tok_rows(t)], buf_ref.at[_tok_rows(dest_ref[t])],
                                      sem).start(priority=k)
            return 0

        lax.fori_loop(0, n, start_rows, 0, unroll=4)
        for _ in range(2):
            pltpu.make_async_copy(src_ref, buf_ref.at[_tok_rows(0, n)], sem).wait()

    @pl.when(i < n_p_chunks)
    def _prompt():
        scatter_rows(src_p_ref)

    @pl.when(i >= n_p_chunks)
    def _sample():
        scatter_rows(src_s_ref)


def _dispatch_call(pends, dest0, dest1, src_p, src_s, n_rows):
    chunk_rows = ROW_CHUNK * SUBLANES
    assert src_p.shape[0] % chunk_rows == 0 and src_s.shape[0] % chunk_rows == 0 and n_rows % MOE_TM == 0
    n_p_chunks = src_p.shape[0] // chunk_rows
    n_s_chunks = src_s.shape[0] // chunk_rows
    tile = (chunk_rows, LANES)
    in_specs = [pl.BlockSpec((ROW_CHUNK,), lambda i, pe: (i,), memory_space=pltpu.SMEM),
                pl.BlockSpec((ROW_CHUNK,), lambda i, pe: (i,), memory_space=pltpu.SMEM),
                pl.BlockSpec(tile, lambda i, pe: (jnp.minimum(i, n_p_chunks - 1), 0)),
                pl.BlockSpec(tile, lambda i, pe: (jnp.maximum(i - n_p_chunks, 0), 0))]
    scratch = [pltpu.SemaphoreType.DMA(()), pltpu.VMEM((MOE_TM * SUBLANES, LANES), F32),
               pltpu.SemaphoreType.DMA(())]
    return pl.pallas_call(
        functools.partial(_dispatch_kernel, n_p_chunks=n_p_chunks),
        grid_spec=pltpu.PrefetchScalarGridSpec(
            num_scalar_prefetch=1, grid=(n_p_chunks + n_s_chunks,), in_specs=in_specs,
            out_specs=pl.BlockSpec(memory_space=pl.ANY), scratch_shapes=scratch),
        out_shape=jax.ShapeDtypeStruct((n_rows * SUBLANES, LANES), F32),
        compiler_params=pltpu.CompilerParams(dimension_semantics=("arbitrary",), has_side_effects=True),
        name="dispatch",
    )(pends, dest0, dest1, src_p, src_s)


def _expert_kernel(blk_e_ref, n_used_ref, x_ref, wg_ref, wu_ref, wd_ref, y_ref, wgb, wub, wdb):
    i = pl.program_id(0)
    new_expert = (i == 0) | (blk_e_ref[i] != blk_e_ref[jnp.maximum(i - 1, 0)])

    @pl.when(new_expert)
    def _cast_weights():
        wgb[...] = wg_ref[0].astype(BF16)
        wub[...] = wu_ref[0].astype(BF16)
        wdb[...] = wd_ref[0].astype(BF16)

    @pl.when(i < n_used_ref[0])
    def _run():
        xb = _load_token_tiles(x_ref).astype(BF16)
        g = _dot(xb, wgb[...])
        u = _dot(xb, wub[...])
        act = (jax.nn.silu(g) * u).astype(BF16)
        _store_token_tiles(y_ref, _dot(act, wdb[...]))

    @pl.when(i >= n_used_ref[0])
    def _skip():
        y_ref[...] = jnp.zeros_like(y_ref)


def _expert_call(blk_e, n_used, xbuf, we_gate, we_up, we_down):
    p = xbuf.shape[0] // SUBLANES
    assert p % MOE_TM == 0
    tile_spec = pl.BlockSpec((MOE_TM * SUBLANES, LANES), lambda i, be, nu: (i, 0))
    grid_spec = pltpu.PrefetchScalarGridSpec(
        num_scalar_prefetch=2,
        grid=(p // MOE_TM,),
        in_specs=[tile_spec,
                  pl.BlockSpec((1, D_MODEL, D_EXPERT), lambda i, be, nu: (be[i], 0, 0)),
                  pl.BlockSpec((1, D_MODEL, D_EXPERT), lambda i, be, nu: (be[i], 0, 0)),
                  pl.BlockSpec((1, D_EXPERT, D_MODEL), lambda i, be, nu: (be[i], 0, 0))],
        out_specs=tile_spec,
        scratch_shapes=[pltpu.VMEM((D_MODEL, D_EXPERT), BF16), pltpu.VMEM((D_MODEL, D_EXPERT), BF16),
                        pltpu.VMEM((D_EXPERT, D_MODEL), BF16)],
    )
    return pl.pallas_call(
        _expert_kernel,
        grid_spec=grid_spec,
        out_shape=jax.ShapeDtypeStruct((p * SUBLANES, LANES), F32),
        compiler_params=pltpu.CompilerParams(dimension_semantics=("arbitrary",),
                                             vmem_limit_bytes=VMEM_LIMIT_BYTES),
        name="experts",
    )(blk_e, n_used, xbuf, we_gate, we_up, we_down)


def _final_kernel(d0_ref, d1_ref, d0n_ref, d1n_ref, x1_ref, ybuf_ref, route_ref, g2_ref, l2g_ref, l2b_ref, out_ref,
                  yg, sems):
    n = x1_ref.shape[0]
    i = pl.program_id(0)
    slot = i % 2

    def gather(idx_refs, s):
        def start_rows(t, _):
            for k, idx_ref in enumerate(idx_refs):
                pltpu.make_async_copy(ybuf_ref.at[_tok_rows(idx_ref[t])], yg.at[s, k, _tok_rows(t)],
                                      sems.at[s]).start(priority=k)
            return 0
        lax.fori_loop(0, n, start_rows, 0, unroll=4)

    @pl.when(i == 0)
    def _first():
        gather((d0_ref, d1_ref), 0)

    @pl.when(i + 1 < pl.num_programs(0))
    def _prefetch():
        gather((d0n_ref, d1n_ref), 1 - slot)

    for k in range(2):
        pltpu.make_async_copy(ybuf_ref.at[_tok_rows(0, n)], yg.at[slot, k], sems.at[slot]).wait()
    route = route_ref[...]
    ff = (_load_token_tiles(yg.at[slot, 0]) * route[:, 2:3]
          + _load_token_tiles(yg.at[slot, 1]) * route[:, 3:4])
    out_ref[...] = _ln(ALPHA * x1_ref[...] + g2_ref[0] * ff) * l2g_ref[...] + l2b_ref[...]


def _final_call(dest0, dest1, first_token, x1, ybuf, route, g2, ln2_g, ln2_b, rows_per_g2, tile):
    n = x1.shape[0]
    assert n % tile == 0 and rows_per_g2 % tile == 0 and first_token % tile == 0
    per = rows_per_g2 // tile
    g2_rows = g2.shape[1]
    steps = n // tile
    off = first_token // tile
    idx = lambda: pl.BlockSpec((tile,), lambda i: (off + i,), memory_space=pltpu.SMEM)
    idx_next = lambda: pl.BlockSpec((tile,), lambda i: (off + jnp.minimum(i + 1, steps - 1),),
                                    memory_space=pltpu.SMEM)
    return pl.pallas_call(
        _final_kernel,
        grid=(steps,),
        in_specs=[idx(), idx(), idx_next(), idx_next(),
                  pl.BlockSpec((tile, D_MODEL), lambda i: (i, 0)),
                  pl.BlockSpec(memory_space=pl.ANY),
                  pl.BlockSpec((tile, LANES), lambda i: (i, 0)),
                  pl.BlockSpec((1, g2_rows, D_MODEL), lambda i: (i // per, 0, 0)),
                  pl.BlockSpec((1, D_MODEL), lambda i: (0, 0)),
                  pl.BlockSpec((1, D_MODEL), lambda i: (0, 0))],
        out_specs=pl.BlockSpec((tile, D_MODEL), lambda i: (i, 0)),
        out_shape=jax.ShapeDtypeStruct((n, D_MODEL), F32),
        scratch_shapes=[pltpu.VMEM((2, 2, tile * SUBLANES, LANES), F32), pltpu.SemaphoreType.DMA((2,))],
        compiler_params=pltpu.CompilerParams(dimension_semantics=("arbitrary",)),
        name="final",
    )(dest0, dest1, dest0, dest1, x1, ybuf, route, g2, ln2_g, ln2_b)


def _block_diag_gate_weights(wa, wx):
    def bd(w):
        w = w.reshape(N_RG_TILES, RG_GROUP, RG_BW, RG_BW)
        eye = jnp.eye(RG_GROUP, dtype=w.dtype)
        return jnp.einsum('tgcd,gh->tgchd', w, eye).reshape(N_RG_TILES, MXU_DIM, MXU_DIM)
    return jnp.concatenate([bd(wa), bd(wx)], axis=2)


def kernel(x_prompt, x_sample, state_conv, state_rglru, state_gla, c_prompt, c_sample, w_mod, b_mod, w_in, b_in,
           conv_w, conv_b, rg_wa, rg_ba, rg_wx, rg_bx, rg_lambda, gla_wa2, gla_ba, gla_norm_g, p_a, p_b, w_o,
           ln1_g, ln1_b, w_grp, b_grp, w_exp, b_exp, we_gate, we_up, we_down, ln2_g, ln2_b):
    assert w_mod.shape[0] == DEPTH == 1
    bp, tp, _ = x_prompt.shape
    ns = x_sample.shape[0]
    n_p = bp * tp
    n_tok = n_p + ns
    row = lambda a: a.reshape(1, -1)

    lo = 2 * D_RNN + 2 * GLA_DK + 2 * GLA_DV
    w_in0 = w_in[0]
    assert lo == O_GA
    win_a = _pack_rows(w_in0, lo)
    win_b = _pack_rows(w_in0[:, lo + GLA_RANK:])
    bin_main = row(jnp.concatenate([b_in[0, :lo], b_in[0, lo + GLA_RANK:]]))
    walr = _pack_rows(jnp.pad(w_in0[:, lo:lo + GLA_RANK], ((0, 0), (0, LANES - GLA_RANK))))
    balr = row(jnp.pad(b_in[0, lo:lo + GLA_RANK], (0, LANES - GLA_RANK)))
    wa2 = _pack_rows(jnp.pad(gla_wa2[0], ((0, LANES - GLA_RANK), (0, 0))))
    wrg = _pack_rows(_block_diag_gate_weights(rg_wa[0], rg_wx[0]).reshape(RG_BLOCKS * RG_BW, 2 * MXU_DIM))
    wrg = wrg.reshape(N_RG_TILES, MXU_DIM // 2, 2 * MXU_DIM)
    w_route = _pack_rows(jnp.pad(jnp.concatenate([w_exp[0], w_grp[0]], axis=1),
                                 ((0, 0), (0, LANES - N_EXPERTS - N_GROUPS))))
    b_route = row(jnp.pad(jnp.concatenate([b_exp[0], b_grp[0]]), (0, LANES - N_EXPERTS - N_GROUPS)))
    pa, pb, wo = _pack_rows(p_a[0]), _pack_rows(p_b[0]), _pack_rows(w_o[0])
    pre_w = [win_a, win_b, bin_main, walr, balr, wa2, row(gla_ba[0])]
    rec_w = [conv_w[0], row(conv_b[0]), wrg, row(rg_ba[0]), row(rg_bx[0]), row(rg_lambda[0])]
    post_w = [row(gla_norm_g[0]), pa, pb, wo, row(ln1_g[0]), row(ln1_b[0]), w_route, b_route]

    mod = _mod_call(jnp.concatenate([c_prompt, c_sample], axis=0), w_mod[0], row(b_mod[0]))
    mod_p = mod[:bp].reshape(bp, 6, D_MODEL)
    mod_s = mod[bp:].reshape(ns, 6, D_MODEL).transpose(1, 0, 2)

    mix_w = pre_w + [post_w[0]] + rec_w + post_w[1:]
    x1_p, hf_p, route_p, routet_p, conv_p, h_p, s_p = _mixer_call(x_prompt, mod_p, mix_w)

    xs = x_sample.reshape(ns, D_MODEL)
    sconv = state_conv[0].transpose(1, 0, 2)
    conv_s, h_s, ya_s, q_s, k_s, v_s, la_s, gg_s, ga_s, gb_s = _s_pre_call(
        xs, mod_s, sconv, state_rglru[0], pre_w + rec_w)
    s_s, o_s = _s_state_call(state_gla[0], q_s, k_s, v_s, la_s)
    x1_s, hf_s, route_s, routet_s = _s_post_call(xs, mod_s, o_s, gg_s, ga_s, gb_s, ya_s, post_w)

    route_pf = route_p.reshape(n_p, LANES)
    eid = jnp.concatenate([routet_p[:2], routet_s[:2]], axis=1).astype(I32)
    assert n_tok % LANES == 0
    n_chunks = n_tok // LANES
    rank3, cnt = _rank_call(eid.reshape(2, n_chunks, LANES).transpose(1, 0, 2))
    rank = rank3.transpose(1, 0, 2).reshape(2, n_tok)
    counts = cnt[:, 0]
    pcounts = (counts + MOE_TM - 1) // MOE_TM * MOE_TM
    pends = jnp.cumsum(pcounts)
    pstarts = pends - pcounts
    experts = jnp.arange(N_EXPERTS, dtype=I32)[:, None, None]
    start_of = jnp.sum(jnp.where(eid[None] == experts, pstarts.astype(I32)[:, None, None], 0), axis=0)
    dest = start_of + rank
    dest0, dest1 = dest[0], dest[1]
    n_tiles = -(-(2 * n_tok + N_EXPERTS * (MOE_TM - 1)) // MOE_TM)
    tile_start = jnp.arange(n_tiles, dtype=I32) * MOE_TM
    blk_e = jnp.minimum(jnp.sum(pends[None, :] <= tile_start[:, None], axis=1), N_EXPERTS - 1).astype(I32)
    n_used = (pends[-1] // MOE_TM).astype(I32).reshape(1)

    pends32 = pends.astype(I32)
    xbuf = _dispatch_call(pends32, dest0, dest1, hf_p.reshape(n_p * SUBLANES, LANES), hf_s, n_tiles * MOE_TM)
    ybuf = _expert_call(blk_e, n_used, xbuf, we_gate[0], we_up[0], we_down[0])

    y_p = _final_call(dest0, dest1, 0, x1_p.reshape(n_p, D_MODEL), ybuf, route_pf, mod_p[:, 5:6, :],
                      row(ln2_g[0]), row(ln2_b[0]), tp, ROW_CHUNK)
    y_s = _final_call(dest0, dest1, n_p, x1_s, ybuf, route_s, mod_s[5][None], row(ln2_g[0]), row(ln2_b[0]), ns, ns)

    return (y_p.reshape(bp, tp, D_MODEL), y_s.reshape(ns, 1, D_MODEL),
            conv_p[None], h_p.reshape(1, bp, D_RNN), s_p[None],
            conv_s.transpose(1, 0, 2)[None], h_s[None], s_s[None])
```

```python
import functools

import jax
import jax.numpy as jnp
import numpy as np
from jax import lax
from jax.experimental import pallas as pl
from jax.experimental.pallas import tpu as pltpu

F32 = jnp.float32
BF16 = jnp.bfloat16
I32 = jnp.int32

D_MODEL = 1024
D_RNN = D_MODEL
RG_BLOCKS = 16
RG_BW = D_RNN // RG_BLOCKS
CONV_W = 4
RG_C = 8.0
GLA_HEADS = 4
GLA_DK = D_MODEL // 2
GLA_DV = D_MODEL
GLA_DKH = GLA_DK // GLA_HEADS
GLA_DVH = GLA_DV // GLA_HEADS
GLA_RANK = 16
GLA_TAU = 16.0
GLA_CHUNK = 64
N_GROUPS = 4
EXP_PER_GROUP = 8
N_EXPERTS = N_GROUPS * EXP_PER_GROUP
D_EXPERT = 512
DEPTH = 1
ALPHA = (2.0 * DEPTH) ** 0.25
LN_EPS = 1e-5
RMS_EPS = 1e-6

LANES = 128
SUBLANES = 8
MXU_DIM = 256
VMEM_LIMIT_BYTES = 56 * 1024 * 1024

MIX_TT = 256
STAGE1_COLS = 512
PIPE_DEPTH = 2
MOE_TM = 256
ROW_CHUNK = 128
SAMPLE_SB = 8
RG_GROUP = MXU_DIM // RG_BW
N_RG_TILES = RG_BLOCKS // RG_GROUP
N_MAIN = 2 * D_RNN + 2 * GLA_DK + 2 * GLA_DV + 2 * D_MODEL
ROUTE_G0 = N_EXPERTS

O_RX, O_RY, O_Q, O_K, O_V, O_GG, O_GA, O_GB = 0, 1024, 2048, 2560, 3072, 4096, 5120, 6144


def _ln(x):
    mu = jnp.mean(x, -1, keepdims=True)
    xc = x - mu
    var = jnp.mean(xc * xc, -1, keepdims=True)
    return xc * lax.rsqrt(var + LN_EPS)


def _dot(a, b):
    return jnp.dot(a, b, preferred_element_type=F32)


def _pack_kernel(w_ref, o_ref):
    o_ref[...] = pltpu.bitcast(w_ref[...].astype(BF16), jnp.uint32)


def _pack_rows(w, n=None):
    k = w.shape[0]
    n = w.shape[1] if n is None else n
    cn = min(n, 4 * LANES)
    assert k % (2 * SUBLANES) == 0 and n % cn == 0
    return pl.pallas_call(
        _pack_kernel,
        grid=(n // cn,),
        in_specs=[pl.BlockSpec((k, cn), lambda i: (0, i))],
        out_specs=pl.BlockSpec((k // 2, cn), lambda i: (0, i)),
        out_shape=jax.ShapeDtypeStruct((k // 2, n), jnp.uint32),
        compiler_params=pltpu.CompilerParams(dimension_semantics=("arbitrary",)),
        name="pack",
    )(w)


def _win(win_ref, winb_ref, lo, hi):
    if hi <= O_GA:
        return win_ref[:, lo:hi]
    assert lo >= O_GA
    return winb_ref[:, lo - O_GA:hi - O_GA]


def _w(packed):
    return pltpu.bitcast(packed, BF16)


def _expm1(x):
    u = jnp.exp(x)
    small = (u - 1.0) * x / jnp.log(u)
    return jnp.where(u == 1.0, x, jnp.where(jnp.abs(x) < 0.5, small, u - 1.0))


def _const_spec(shape):
    nd = len(shape)
    return pl.BlockSpec(shape, lambda *_: (0,) * nd, pipeline_mode=pl.Buffered(1))


assert D_MODEL == SUBLANES * LANES


def _tok_rows(t, count=1):
    return pl.ds(pl.multiple_of(t * SUBLANES, SUBLANES), count * SUBLANES)


def _store_token_tiles(ref, rows):
    n = rows.shape[0]
    for s in range(SUBLANES):
        ref[pl.ds(s, n, stride=SUBLANES), :] = rows[:, s * LANES:(s + 1) * LANES]


def _load_token_tiles(ref):
    n = ref.shape[0] // SUBLANES
    return jnp.concatenate([ref[pl.ds(s, n, stride=SUBLANES), :] for s in range(SUBLANES)], axis=1)


def _mod_kernel(c_ref, w_ref, b_ref, o_ref):
    c = c_ref[...]
    s = jax.nn.silu(c)
    o_ref[...] = _dot(s.astype(BF16), w_ref[...].astype(BF16)) + b_ref[...]


def _mod_call(c_all, w_mod, b_mod):
    n = c_all.shape[0]
    tn = 512
    return pl.pallas_call(
        _mod_kernel,
        grid=(6 * D_MODEL // tn,),
        in_specs=[pl.BlockSpec((n, D_MODEL), lambda i: (0, 0)),
                  pl.BlockSpec((D_MODEL, tn), lambda i: (0, i)),
                  pl.BlockSpec((1, tn), lambda i: (0, i))],
        out_specs=pl.BlockSpec((n, tn), lambda i: (0, i)),
        out_shape=jax.ShapeDtypeStruct((n, 6 * D_MODEL), F32),
        compiler_params=pltpu.CompilerParams(dimension_semantics=("arbitrary",)),
        name="mod",
    )(c_all, w_mod, b_mod)


def _no_op():
    pass


def _gate_logits(xc, wrg_ref, rba, rbx):
    xcb = xc.astype(BF16)
    parts = [_dot(xcb[:, c * MXU_DIM:(c + 1) * MXU_DIM], _w(wrg_ref[c])) for c in range(N_RG_TILES)]
    r_pre = jnp.concatenate([p[:, :MXU_DIM] for p in parts], axis=1)
    i_pre = jnp.concatenate([p[:, MXU_DIM:] for p in parts], axis=1)
    return r_pre, i_pre, rba, rbx


def _gate_logits_to(out_ref, xc, wrg_ref):
    xcb = xc.astype(BF16)
    for c in range(N_RG_TILES):
        p = _dot(xcb[:, c * MXU_DIM:(c + 1) * MXU_DIM], _w(wrg_ref[c]))
        out_ref[:, c * MXU_DIM:(c + 1) * MXU_DIM] = p[:, :MXU_DIM]
        out_ref[:, D_RNN + c * MXU_DIM:D_RNN + (c + 1) * MXU_DIM] = p[:, MXU_DIM:]


def _gates(xc, r_pre, i_pre, rba, rbx, lam, between=_no_op):
    r = jax.nn.sigmoid(r_pre + rba)
    between()
    ig = jax.nn.sigmoid(i_pre + rbx)
    log_a = -RG_C * r * jax.nn.softplus(-lam)
    a = jnp.exp(log_a)
    between()
    u = jnp.sqrt(-_expm1(2.0 * log_a)) * (ig * xc)
    between()
    return a, u


def _low_rank(hb, walr_ref, balr_ref):
    return _dot(hb, _w(walr_ref[...])) + balr_ref[...]


def _log_decay_from(alr, wa2_ref, gba_ref):
    return jax.nn.log_sigmoid(_dot(alr.astype(BF16), _w(wa2_ref[...])) + gba_ref[...]) / GLA_TAU


def _log_decay(hb, walr_ref, balr_ref, wa2_ref, gba_ref):
    return _log_decay_from(_low_rank(hb, walr_ref, balr_ref), wa2_ref, gba_ref)


def _post(x, o, gg, ga, gb, y_a, g1, sh2, sc2, gng, pa_ref, pb_ref, wo_ref, l1g, l1b, wr_ref, br,
          between=_no_op):
    rows = x.shape[0]
    heads = []
    for hh in range(GLA_HEADS):
        oh = o[:, hh * GLA_DVH:(hh + 1) * GLA_DVH]
        ms = jnp.mean(oh * oh, -1, keepdims=True)
        heads.append(oh * lax.rsqrt(ms + RMS_EPS) * gng)
    y_b = jnp.concatenate(heads, axis=1) * jax.nn.silu(gg)
    merged = (jax.nn.sigmoid(ga) * _dot(y_a.astype(BF16), _w(pa_ref[...]))
              + jax.nn.sigmoid(gb) * _dot(y_b.astype(BF16), _w(pb_ref[...])))
    mix = _dot(merged.astype(BF16), _w(wo_ref[...]))
    between()
    x1 = _ln(ALPHA * x + g1 * mix) * l1g + l1b
    between()
    hf = _ln(x1) * (1.0 + sc2) + sh2
    logits = _dot(hf.astype(BF16), _w(wr_ref[...])) + br
    lane = lax.broadcasted_iota(I32, (rows, LANES), 1).astype(F32)
    neg = jnp.float32(-jnp.inf)
    big = jnp.float32(LANES)
    g_valid = (lane >= ROUTE_G0) & (lane < ROUTE_G0 + N_GROUPS)
    gl = jnp.where(g_valid, logits, neg)
    gmax = jnp.max(gl, -1, keepdims=True)
    g_lane = jnp.min(jnp.where(gl == gmax, lane, big), -1, keepdims=True)
    g_w = 1.0 / jnp.sum(jnp.exp(gl - gmax), -1, keepdims=True)
    e_lo = (g_lane - ROUTE_G0) * EXP_PER_GROUP
    el = jnp.where((lane >= e_lo) & (lane < e_lo + EXP_PER_GROUP), logits, neg)
    t1 = jnp.max(el, -1, keepdims=True)
    i1 = jnp.min(jnp.where(el == t1, lane, big), -1, keepdims=True)
    el2 = jnp.where(lane == i1, neg, el)
    t2 = jnp.max(el2, -1, keepdims=True)
    i2 = jnp.min(jnp.where(el2 == t2, lane, big), -1, keepdims=True)
    e2 = jnp.exp(t2 - t1)
    den = 1.0 + e2
    w1 = (1.0 / den) * g_w
    w2 = (e2 / den) * g_w
    route = jnp.where(lane == 0, i1.astype(F32),
                      jnp.where(lane == 1, i2.astype(F32),
                                jnp.where(lane == 2, w1, jnp.where(lane == 3, w2, 0.0))))
    return x1, hf, route


def _mixer_kernel(xn_ref, modn_ref, x_ref, mod_ref, win_ref, winb_ref, bin_ref, walr_ref, balr_ref, wa2_ref, gba_ref,
                  gng_ref, cw_ref, cb_ref, wrg_ref, rba_ref, rbx_ref, lam_ref, pa_ref, pb_ref, wo_ref,
                  l1g_ref, l1b_ref, wr_ref, br_ref,
                  x1_ref, hf_ref, route_ref, routet_ref, convn_ref, hlast_ref, sfin_ref,
                  hbn_scr, hbc_scr, pn_scr, pc_scr, alrn_scr, alrc_scr, xcn_scr, xcc_scr, gaten_scr, gatec_scr,
                  rxbuf, hcar, st_ref, o_scr,
                  *, tiles_per_seq):
    i = pl.program_id(0)
    t = jnp.maximum(i - PIPE_DEPTH, 0)
    j = t % tiles_per_seq
    j1 = jnp.maximum(i - 1, 0) % tiles_per_seq
    tt = x_ref.shape[1]

    @pl.when(i == 0)
    def _no_tile_yet():
        hbc_scr[...] = jnp.zeros_like(hbc_scr)
        pc_scr[...] = jnp.zeros_like(pc_scr)
        alrc_scr[...] = jnp.zeros_like(alrc_scr)
        xcc_scr[...] = jnp.zeros_like(xcc_scr)
        gatec_scr[...] = jnp.zeros_like(gatec_scr)

    @pl.when(j1 == 0)
    def _init_conv():
        rxbuf[0:SUBLANES, :] = jnp.zeros((SUBLANES, D_RNN), F32)

    @pl.when(j == 0)
    def _init():
        hcar[...] = jnp.zeros_like(hcar)
        st_ref[...] = jnp.zeros_like(st_ref)

    def stage0():
        modn = modn_ref[0]
        hbn_scr[...] = (_ln(xn_ref[0]) * (1.0 + modn[1:2]) + modn[0:1]).astype(BF16)

    hbn = hbc_scr[...]

    def _proj_block(lo):
        def run():
            hi = lo + STAGE1_COLS
            pn_scr[:, lo:hi] = _dot(hbn, _w(_win(win_ref, winb_ref, lo, hi)))
        return run

    def _alr_block():
        alrn_scr[...] = _dot(hbn, _w(walr_ref[...]))

    pending = [_proj_block(lo) for lo in range(0, N_MAIN, STAGE1_COLS)] + [_alr_block]

    def pump(count=1):
        for _ in range(min(count, len(pending))):
            pending.pop(0)()

    x = x_ref[0]
    mod = mod_ref[0]
    g1, sh2, sc2 = mod[2:3], mod[3:4], mod[4:5]

    def proj(lo, hi):
        return pc_scr[:, lo:hi] + bin_ref[:, lo:hi]

    pump(O_RY // STAGE1_COLS)

    rx = pn_scr[:, O_RX:O_RX + D_RNN] + bin_ref[:, O_RX:O_RX + D_RNN]
    rxbuf[SUBLANES:SUBLANES + tt, :] = rx
    cw = cw_ref[...]
    xcn = cb_ref[...] + rxbuf[SUBLANES - 3:SUBLANES - 3 + tt, :] * cw[0:1]
    xcn = xcn + rxbuf[SUBLANES - 2:SUBLANES - 2 + tt, :] * cw[1:2]
    xcn = xcn + rxbuf[SUBLANES - 1:SUBLANES - 1 + tt, :] * cw[2:3]
    xcn = xcn + rx * cw[3:4]
    rxbuf[0:SUBLANES, :] = rxbuf[tt:tt + SUBLANES, :]
    xcn_scr[...] = xcn
    xc = xcc_scr[...]

    q = proj(O_Q, O_Q + GLA_DK) * (GLA_DKH ** -0.5)
    k = proj(O_K, O_K + GLA_DK)
    v = proj(O_V, O_V + GLA_DV)
    la = _log_decay_from(alrc_scr[...] + balr_ref[...], wa2_ref, gba_ref)
    pump()
    la = la.reshape(tt // SUBLANES, SUBLANES, GLA_DK)
    subk = lax.broadcasted_iota(I32, (1, SUBLANES, GLA_DK), 1)
    s = 1
    while s < SUBLANES:
        la = la + jnp.where(subk >= s, pltpu.roll(la, s, 1), 0.0)
        s *= 2
    groups = []
    for g in range(tt // SUBLANES):
        blk = la[g] if g % (GLA_CHUNK // SUBLANES) == 0 else la[g] + groups[-1][SUBLANES - 1:SUBLANES]
        groups.append(blk)
    bcum = jnp.concatenate(groups, axis=0)
    eb = jnp.exp(bcum)
    q_in = (q * eb).astype(BF16)
    k_in = (k * jnp.exp(-bcum)).astype(BF16)
    pump()
    stage0()
    gelu_ry = jax.nn.gelu(proj(O_RY, O_RY + D_RNN))
    pump()

    a, u = _gates(xc, gatec_scr[:, :D_RNN], gatec_scr[:, D_RNN:], rba_ref[...], rbx_ref[...], lam_ref[...],
                  between=pump)
    s = 1
    a = a.reshape(tt // SUBLANES, SUBLANES, D_RNN)
    u = u.reshape(tt // SUBLANES, SUBLANES, D_RNN)
    sub = lax.broadcasted_iota(I32, (1, SUBLANES, D_RNN), 1)
    while s < SUBLANES:
        keep = sub >= s
        a_s = jnp.where(keep, pltpu.roll(a, s, 1), 1.0)
        u_s = jnp.where(keep, pltpu.roll(u, s, 1), 0.0)
        u = a * u_s + u
        a = a * a_s
        pump()
        s *= 2
    carry = hcar[0:1, :]
    groups = []
    for g in range(tt // SUBLANES):
        hg = a[g] * carry + u[g]
        groups.append(hg)
        carry = hg[SUBLANES - 1:SUBLANES]
    hseq = jnp.concatenate(groups, axis=0)
    hcar[0:1, :] = carry
    y_a = hseq * gelu_ry

    tri = (lax.broadcasted_iota(I32, (GLA_CHUNK, GLA_CHUNK), 0)
           >= lax.broadcasted_iota(I32, (GLA_CHUNK, GLA_CHUNK), 1))
    n_chunks = tt // GLA_CHUNK
    nt_dims = (((1,), (1,)), ((), ()))
    attn, d_st, dec, vh_all = {}, {}, {}, {}
    for c in range(n_chunks):
        r0 = c * GLA_CHUNK
        for hh in range(GLA_HEADS):
            kc = slice(hh * GLA_DKH, (hh + 1) * GLA_DKH)
            b = bcum[r0:r0 + GLA_CHUNK, kc]
            btot = b[GLA_CHUNK - 1:GLA_CHUNK, :]
            k_out = (k[r0:r0 + GLA_CHUNK, kc] * jnp.exp(btot - b)).astype(BF16)
            vh = v[r0:r0 + GLA_CHUNK, hh * GLA_DVH:(hh + 1) * GLA_DVH].astype(BF16)
            scores = lax.dot_general(q_in[r0:r0 + GLA_CHUNK, kc], k_in[r0:r0 + GLA_CHUNK, kc], nt_dims,
                                     preferred_element_type=F32)
            attn[c, hh] = jnp.where(tri, scores, 0.0).astype(BF16)
            d_st[c, hh] = lax.dot_general(vh, k_out, (((0,), (0,)), ((), ())), preferred_element_type=F32)
            dec[c, hh] = jnp.exp(btot)
            vh_all[c, hh] = vh
        pump()
    starts = {}
    for hh in range(GLA_HEADS):
        st = st_ref[hh]
        for c in range(n_chunks):
            starts[c, hh] = st.astype(BF16)
            st = st * dec[c, hh] + d_st[c, hh]
        st_ref[hh] = st
    for c in range(n_chunks):
        r0 = c * GLA_CHUNK
        for hh in range(GLA_HEADS):
            qi = q_in[r0:r0 + GLA_CHUNK, hh * GLA_DKH:(hh + 1) * GLA_DKH]
            inter = lax.dot_general(qi, starts[c, hh], nt_dims, preferred_element_type=F32)
            o_scr[r0:r0 + GLA_CHUNK, hh * GLA_DVH:(hh + 1) * GLA_DVH] = _dot(attn[c, hh], vh_all[c, hh]) + inter

    x1, hf, route = _post(x, o_scr[...], proj(O_GG, O_GG + GLA_DV), proj(O_GA, O_GA + D_MODEL),
                          proj(O_GB, O_GB + D_MODEL), y_a, g1, sh2, sc2, gng_ref[...],
                          pa_ref, pb_ref, wo_ref, l1g_ref[...], l1b_ref[...], wr_ref, br_ref[...], between=pump)
    x1_ref[0] = x1
    _store_token_tiles(hf_ref.at[0], hf)
    route_ref[0] = route
    routet_ref[...] = route.T[0:SUBLANES, :]

    pump(len(pending))
    _gate_logits_to(gaten_scr, xcn, wrg_ref)
    pc_scr[:, O_RY:] = pn_scr[:, O_RY:]
    alrc_scr[...] = alrn_scr[...]
    hbc_scr[...] = hbn_scr[...]
    xcc_scr[...] = xcn_scr[...]
    gatec_scr[...] = gaten_scr[...]

    @pl.when((j1 == tiles_per_seq - 1) & (i >= 1))
    def _final_conv():
        convn_ref[0] = rxbuf[SUBLANES - (CONV_W - 1):SUBLANES, :]

    @pl.when((j == tiles_per_seq - 1) & (i >= PIPE_DEPTH))
    def _final():
        hlast_ref[0] = hcar[0:1, :]
        for hh in range(GLA_HEADS):
            sfin_ref[0, hh] = st_ref[hh].T


def _mixer_call(x, mod3, wts):
    b, t, _ = x.shape
    tt = MIX_TT
    assert t % tt == 0 and tt % GLA_CHUNK == 0
    nt = t // tt
    n_tiles = b * nt
    assert nt > 1
    cur = lambda i: jnp.maximum(i - PIPE_DEPTH, 0)
    nxt = lambda i: jnp.minimum(i, n_tiles - 1)
    tok = lambda i: (cur(i) // nt, cur(i) % nt, 0)
    tok_n = lambda i: (nxt(i) // nt, nxt(i) % nt, 0)
    per_b = lambda i: (cur(i) // nt, 0, 0)
    in_specs = [pl.BlockSpec((1, tt, D_MODEL), tok_n),
                pl.BlockSpec((1, 6, D_MODEL), lambda i: (nxt(i) // nt, 0, 0)),
                pl.BlockSpec((1, tt, D_MODEL), tok),
                pl.BlockSpec((1, 6, D_MODEL), per_b)] + [_const_spec(w.shape) for w in wts]
    out_specs = [pl.BlockSpec((1, tt, D_MODEL), tok),
                 pl.BlockSpec((1, tt * SUBLANES, LANES), tok),
                 pl.BlockSpec((1, tt, LANES), tok),
                 pl.BlockSpec((SUBLANES, tt), lambda i: (0, cur(i))),
                 pl.BlockSpec((1, CONV_W - 1, D_RNN), per_b),
                 pl.BlockSpec((1, 1, D_RNN), per_b),
                 pl.BlockSpec((1, GLA_HEADS, GLA_DKH, GLA_DVH), lambda i: (cur(i) // nt, 0, 0, 0))]
    out_shape = [jax.ShapeDtypeStruct((b, t, D_MODEL), F32),
                 jax.ShapeDtypeStruct((b, t * SUBLANES, LANES), F32),
                 jax.ShapeDtypeStruct((b, t, LANES), F32),
                 jax.ShapeDtypeStruct((SUBLANES, b * t), F32),
                 jax.ShapeDtypeStruct((b, CONV_W - 1, D_RNN), F32),
                 jax.ShapeDtypeStruct((b, 1, D_RNN), F32),
                 jax.ShapeDtypeStruct((b, GLA_HEADS, GLA_DKH, GLA_DVH), F32)]
    scratch = [pltpu.VMEM((tt, D_MODEL), BF16), pltpu.VMEM((tt, D_MODEL), BF16),
               pltpu.VMEM((tt, N_MAIN), F32), pltpu.VMEM((tt, N_MAIN), F32),
               pltpu.VMEM((tt, LANES), F32), pltpu.VMEM((tt, LANES), F32),
               pltpu.VMEM((tt, D_RNN), F32), pltpu.VMEM((tt, D_RNN), F32),
               pltpu.VMEM((tt, 2 * D_RNN), F32), pltpu.VMEM((tt, 2 * D_RNN), F32),
               pltpu.VMEM((tt + SUBLANES, D_RNN), F32),
               pltpu.VMEM((SUBLANES, D_RNN), F32),
               pltpu.VMEM((GLA_HEADS, GLA_DVH, GLA_DKH), F32),
               pltpu.VMEM((tt, GLA_DV), F32)]
    return pl.pallas_call(
        functools.partial(_mixer_kernel, tiles_per_seq=nt),
        grid=(n_tiles + PIPE_DEPTH,),
        in_specs=in_specs,
        out_specs=out_specs,
        out_shape=out_shape,
        scratch_shapes=scratch,
        compiler_params=pltpu.CompilerParams(dimension_semantics=("arbitrary",),
                                             vmem_limit_bytes=VMEM_LIMIT_BYTES),
        name="mixer",
    )(x, mod3, x, mod3, *wts)


def _s_pre_kernel(x_ref, mod_ref, sconv_ref, h0_ref, win_ref, winb_ref, bin_ref, walr_ref, balr_ref, wa2_ref, gba_ref,
                  cw_ref, cb_ref, wrg_ref, rba_ref, rbx_ref, lam_ref,
                  convn_ref, hnew_ref, ya_ref, q_ref, k_ref, v_ref, la_ref, gg_ref, ga_ref, gb_ref):
    x = x_ref[...]
    hb = (_ln(x) * (1.0 + mod_ref[1]) + mod_ref[0]).astype(BF16)

    def proj(lo, hi):
        return _dot(hb, _w(_win(win_ref, winb_ref, lo, hi))) + bin_ref[:, lo:hi]

    rx = proj(O_RX, O_RX + D_RNN)
    cw = cw_ref[...]
    xc = cb_ref[...] + sconv_ref[0] * cw[0:1]
    xc = xc + sconv_ref[1] * cw[1:2]
    xc = xc + sconv_ref[2] * cw[2:3]
    xc = xc + rx * cw[3:4]
    convn_ref[0] = sconv_ref[1]
    convn_ref[1] = sconv_ref[2]
    convn_ref[2] = rx
    a, u = _gates(xc, *_gate_logits(xc, wrg_ref, rba_ref[...], rbx_ref[...]), lam_ref[...])
    hnew = u + a * h0_ref[...]
    hnew_ref[...] = hnew
    ya_ref[...] = hnew * jax.nn.gelu(proj(O_RY, O_RY + D_RNN))
    q_ref[...] = proj(O_Q, O_Q + GLA_DK) * (GLA_DKH ** -0.5)
    k_ref[...] = proj(O_K, O_K + GLA_DK)
    v_ref[...] = proj(O_V, O_V + GLA_DV)
    la_ref[...] = _log_decay(hb, walr_ref, balr_ref, wa2_ref, gba_ref)
    gg_ref[...] = proj(O_GG, O_GG + GLA_DV)
    ga_ref[...] = proj(O_GA, O_GA + D_MODEL)
    gb_ref[...] = proj(O_GB, O_GB + D_MODEL)


def _s_pre_call(x_s, mod_s, sconv, h0, wts):
    n = x_s.shape[0]
    full = lambda shape: pl.BlockSpec(shape, lambda i: (0,) * len(shape))
    ins = [x_s, mod_s, sconv, h0] + list(wts)
    shapes = [(CONV_W - 1, n, D_RNN), (n, D_RNN), (n, D_RNN), (n, GLA_DK), (n, GLA_DK), (n, GLA_DV),
              (n, GLA_DK), (n, GLA_DV), (n, D_MODEL), (n, D_MODEL)]
    return pl.pallas_call(
        _s_pre_kernel,
        grid=(1,),
        in_specs=[full(a.shape) for a in ins],
        out_specs=[full(s) for s in shapes],
        out_shape=[jax.ShapeDtypeStruct(s, F32) for s in shapes],
        compiler_params=pltpu.CompilerParams(dimension_semantics=("arbitrary",),
                                             vmem_limit_bytes=VMEM_LIMIT_BYTES),
        name="s_pre",
    )(*ins)


def _to_column(row, n):
    eye = lax.broadcasted_iota(I32, (n, n), 0) == lax.broadcasted_iota(I32, (n, n), 1)
    return jnp.sum(jnp.where(eye, jnp.broadcast_to(row, (n, n)), 0.0), axis=1, keepdims=True)


def _s_state_kernel(s_ref, q_ref, k_ref, v_ref, la_ref, snew_ref, o_ref):
    sb = s_ref.shape[0]
    qb = q_ref[...].astype(BF16)
    dec = jnp.exp(la_ref[...])
    kk = k_ref[...]
    vv = v_ref[...]
    for si in range(sb):
        for hh in range(GLA_HEADS):
            kc = slice(hh * GLA_DKH, (hh + 1) * GLA_DKH)
            vc = slice(hh * GLA_DVH, (hh + 1) * GLA_DVH)
            d_col = _to_column(dec[si:si + 1, kc], GLA_DKH)
            k_col = _to_column(kk[si:si + 1, kc], GLA_DKH)
            s_new = s_ref[si, hh] * d_col + k_col * vv[si:si + 1, vc]
            snew_ref[si, hh] = s_new
            o_all = _dot(qb[:, kc], s_new.astype(BF16))
            o_ref[si:si + 1, vc] = o_all[si:si + 1, :]


def _s_state_call(state, q, k, v, la):
    n = state.shape[0]
    sb = SAMPLE_SB
    assert n % sb == 0
    blk = lambda w: pl.BlockSpec((sb, w), lambda i: (i, 0))
    st_spec = pl.BlockSpec((sb, GLA_HEADS, GLA_DKH, GLA_DVH), lambda i: (i, 0, 0, 0))
    return pl.pallas_call(
        _s_state_kernel,
        grid=(n // sb,),
        in_specs=[st_spec, blk(GLA_DK), blk(GLA_DK), blk(GLA_DV), blk(GLA_DK)],
        out_specs=[st_spec, blk(GLA_DV)],
        out_shape=[jax.ShapeDtypeStruct(state.shape, F32), jax.ShapeDtypeStruct((n, GLA_DV), F32)],
        compiler_params=pltpu.CompilerParams(dimension_semantics=("arbitrary",),
                                             vmem_limit_bytes=VMEM_LIMIT_BYTES),
        name="s_state",
    )(state, q, k, v, la)


def _s_post_kernel(x_ref, mod_ref, o_ref, gg_ref, ga_ref, gb_ref, ya_ref, gng_ref, pa_ref, pb_ref, wo_ref,
                   l1g_ref, l1b_ref, wr_ref, br_ref, x1_ref, hf_ref, route_ref, routet_ref):
    x1, hf, route = _post(x_ref[...], o_ref[...], gg_ref[...], ga_ref[...], gb_ref[...], ya_ref[...],
                          mod_ref[2], mod_ref[3], mod_ref[4], gng_ref[...], pa_ref, pb_ref, wo_ref,
                          l1g_ref[...], l1b_ref[...], wr_ref, br_ref[...])
    x1_ref[...] = x1
    _store_token_tiles(hf_ref, hf)
    route_ref[...] = route
    routet_ref[...] = route.T[0:SUBLANES, :]


def _s_post_call(x_s, mod_s, o, gg, ga, gb, ya, wts):
    n = x_s.shape[0]
    full = lambda shape: pl.BlockSpec(shape, lambda i: (0,) * len(shape))
    ins = [x_s, mod_s, o, gg, ga, gb, ya] + list(wts)
    shapes = [(n, D_MODEL), (n * SUBLANES, LANES), (n, LANES), (SUBLANES, n)]
    return pl.pallas_call(
        _s_post_kernel,
        grid=(1,),
        in_specs=[full(a.shape) for a in ins],
        out_specs=[full(s) for s in shapes],
        out_shape=[jax.ShapeDtypeStruct(s, F32) for s in shapes],
        compiler_params=pltpu.CompilerParams(dimension_semantics=("arbitrary",),
                                             vmem_limit_bytes=VMEM_LIMIT_BYTES),
        name="s_post",
    )(*ins)


def _rank_kernel(eid_ref, rank_ref, cnt_ref, carry):
    n_chunks = eid_ref.shape[0]
    carry[...] = jnp.zeros_like(carry)
    e_iota = lax.broadcasted_iota(I32, (N_EXPERTS, LANES), 0)
    upper = (lax.broadcasted_iota(I32, (LANES, LANES), 0)
             < lax.broadcasted_iota(I32, (LANES, LANES), 1)).astype(BF16)

    def body(c, _):
        ids = eid_ref[c]
        oh0 = (e_iota == ids[0:1, :])
        oh1 = (e_iota == ids[1:2, :])
        both = (oh0 | oh1).astype(F32)
        before = _dot(both.astype(BF16), upper) + carry[:, 0:1]
        r0 = jnp.sum(jnp.where(oh0, before, 0.0), axis=0, keepdims=True)
        r1 = jnp.sum(jnp.where(oh1, before, 0.0), axis=0, keepdims=True)
        rank_ref[c] = jnp.concatenate([r0, r1], axis=0).astype(I32)
        carry[...] = carry[...] + jnp.sum(both, axis=1, keepdims=True)
        return 0

    lax.fori_loop(0, n_chunks, body, 0)
    cnt_ref[...] = carry[...].astype(I32)


def _rank_call(eid3):
    n_chunks = eid3.shape[0]
    full = lambda shape: pl.BlockSpec(shape, lambda i: (0,) * len(shape))
    return pl.pallas_call(
        _rank_kernel,
        grid=(1,),
        in_specs=[full(eid3.shape)],
        out_specs=[full(eid3.shape), full((N_EXPERTS, LANES))],
        out_shape=[jax.ShapeDtypeStruct(eid3.shape, I32), jax.ShapeDtypeStruct((N_EXPERTS, LANES), I32)],
        scratch_shapes=[pltpu.VMEM((N_EXPERTS, LANES), F32)],
        compiler_params=pltpu.CompilerParams(dimension_semantics=("arbitrary",)),
        name="rank",
    )(eid3)


def _dispatch_kernel(pends_ref, d0_ref, d1_ref, src_p_ref, src_s_ref, buf_ref, sem, zeros_ref, zsem, *,
                     n_p_chunks):
    i = pl.program_id(0)
    n_tiles = buf_ref.shape[0] // (MOE_TM * SUBLANES)

    @pl.when(i == 0)
    def _clear():
        zeros_ref[...] = jnp.zeros_like(zeros_ref)

        def tile_copy(row0):
            return pltpu.make_async_copy(zeros_ref, buf_ref.at[_tok_rows(row0, MOE_TM)], zsem)

        def nonempty(e):
            return pends_ref[e] > jnp.where(e == 0, 0, pends_ref[jnp.maximum(e - 1, 0)])

        def start_tail(e, _):
            @pl.when(nonempty(e))
            def _():
                tile_copy(pends_ref[e] - MOE_TM).start()
            return 0

        def wait_tail(e, _):
            @pl.when(nonempty(e))
            def _():
                tile_copy(0).wait()
            return 0

        def start_unused(j, _):
            tile_copy(j * MOE_TM).start()
            return 0

        def wait_unused(j, _):
            tile_copy(0).wait()
            return 0

        first_unused = pends_ref[N_EXPERTS - 1] // MOE_TM
        lax.fori_loop(0, N_EXPERTS, start_tail, 0)
        lax.fori_loop(first_unused, n_tiles, start_unused, 0)
        lax.fori_loop(0, N_EXPERTS, wait_tail, 0)
        lax.fori_loop(first_unused, n_tiles, wait_unused, 0)

    def scatter_rows(src_ref):
        n = src_ref.shape[0] // SUBLANES

        def start_rows(t, _):
            for k, dest_ref in enumerate((d0_ref, d1_ref)):
                pltpu.make_async_copy(src_ref.at[_tok_rows(t)], buf_ref.at[_tok_rows(dest_ref[t])],
                                      sem).start(priority=k)
            return 0

        lax.fori_loop(0, n, start_rows, 0, unroll=4)
        for _ in range(2):
            pltpu.make_async_copy(src_ref, buf_ref.at[_tok_rows(0, n)], sem).wait()

    @pl.when(i < n_p_chunks)
    def _prompt():
        scatter_rows(src_p_ref)

    @pl.when(i >= n_p_chunks)
    def _sample():
        scatter_rows(src_s_ref)


def _dispatch_call(pends, dest0, dest1, src_p, src_s, n_rows):
    chunk_rows = ROW_CHUNK * SUBLANES
    assert src_p.shape[0] % chunk_rows == 0 and src_s.shape[0] % chunk_rows == 0 and n_rows % MOE_TM == 0
    n_p_chunks = src_p.shape[0] // chunk_rows
    n_s_chunks = src_s.shape[0] // chunk_rows
    tile = (chunk_rows, LANES)
    in_specs = [pl.BlockSpec((ROW_CHUNK,), lambda i, pe: (i,), memory_space=pltpu.SMEM),
                pl.BlockSpec((ROW_CHUNK,), lambda i, pe: (i,), memory_space=pltpu.SMEM),
                pl.BlockSpec(tile, lambda i, pe: (jnp.minimum(i, n_p_chunks - 1), 0)),
                pl.BlockSpec(tile, lambda i, pe: (jnp.maximum(i - n_p_chunks, 0), 0))]
    scratch = [pltpu.SemaphoreType.DMA(()), pltpu.VMEM((MOE_TM * SUBLANES, LANES), F32),
               pltpu.SemaphoreType.DMA(())]
    return pl.pallas_call(
        functools.partial(_dispatch_kernel, n_p_chunks=n_p_chunks),
        grid_spec=pltpu.PrefetchScalarGridSpec(
            num_scalar_prefetch=1, grid=(n_p_chunks + n_s_chunks,), in_specs=in_specs,
            out_specs=pl.BlockSpec(memory_space=pl.ANY), scratch_shapes=scratch),
        out_shape=jax.ShapeDtypeStruct((n_rows * SUBLANES, LANES), F32),
        compiler_params=pltpu.CompilerParams(dimension_semantics=("arbitrary",), has_side_effects=True),
        name="dispatch",
    )(pends, dest0, dest1, src_p, src_s)


def _expert_kernel(blk_e_ref, n_used_ref, x_ref, wg_ref, wu_ref, wd_ref, y_ref, wgb, wub, wdb):
    i = pl.program_id(0)
    new_expert = (i == 0) | (blk_e_ref[i] != blk_e_ref[jnp.maximum(i - 1, 0)])

    @pl.when(new_expert)
    def _cast_weights():
        wgb[...] = wg_ref[0].astype(BF16)
        wub[...] = wu_ref[0].astype(BF16)
        wdb[...] = wd_ref[0].astype(BF16)

    @pl.when(i < n_used_ref[0])
    def _run():
        xb = _load_token_tiles(x_ref).astype(BF16)
        g = _dot(xb, wgb[...])
        u = _dot(xb, wub[...])
        act = (jax.nn.silu(g) * u).astype(BF16)
        _store_token_tiles(y_ref, _dot(act, wdb[...]))

    @pl.when(i >= n_used_ref[0])
    def _skip():
        y_ref[...] = jnp.zeros_like(y_ref)


def _expert_call(blk_e, n_used, xbuf, we_gate, we_up, we_down):
    p = xbuf.shape[0] // SUBLANES
    assert p % MOE_TM == 0
    tile_spec = pl.BlockSpec((MOE_TM * SUBLANES, LANES), lambda i, be, nu: (i, 0))
    grid_spec = pltpu.PrefetchScalarGridSpec(
        num_scalar_prefetch=2,
        grid=(p // MOE_TM,),
        in_specs=[tile_spec,
                  pl.BlockSpec((1, D_MODEL, D_EXPERT), lambda i, be, nu: (be[i], 0, 0)),
                  pl.BlockSpec((1, D_MODEL, D_EXPERT), lambda i, be, nu: (be[i], 0, 0)),
                  pl.BlockSpec((1, D_EXPERT, D_MODEL), lambda i, be, nu: (be[i], 0, 0))],
        out_specs=tile_spec,
        scratch_shapes=[pltpu.VMEM((D_MODEL, D_EXPERT), BF16), pltpu.VMEM((D_MODEL, D_EXPERT), BF16),
                        pltpu.VMEM((D_EXPERT, D_MODEL), BF16)],
    )
    return pl.pallas_call(
        _expert_kernel,
        grid_spec=grid_spec,
        out_shape=jax.ShapeDtypeStruct((p * SUBLANES, LANES), F32),
        compiler_params=pltpu.CompilerParams(dimension_semantics=("arbitrary",),
                                             vmem_limit_bytes=VMEM_LIMIT_BYTES),
        name="experts",
    )(blk_e, n_used, xbuf, we_gate, we_up, we_down)


def _final_kernel(d0_ref, d1_ref, d0n_ref, d1n_ref, x1_ref, ybuf_ref, route_ref, g2_ref, l2g_ref, l2b_ref, out_ref,
                  yg, sems):
    n = x1_ref.shape[0]
    i = pl.program_id(0)
    slot = i % 2

    def gather(idx_refs, s):
        def start_rows(t, _):
            for k, idx_ref in enumerate(idx_refs):
                pltpu.make_async_copy(ybuf_ref.at[_tok_rows(idx_ref[t])], yg.at[s, k, _tok_rows(t)],
                                      sems.at[s]).start(priority=k)
            return 0
        lax.fori_loop(0, n, start_rows, 0, unroll=4)

    @pl.when(i == 0)
    def _first():
        gather((d0_ref, d1_ref), 0)

    @pl.when(i + 1 < pl.num_programs(0))
    def _prefetch():
        gather((d0n_ref, d1n_ref), 1 - slot)

    for k in range(2):
        pltpu.make_async_copy(ybuf_ref.at[_tok_rows(0, n)], yg.at[slot, k], sems.at[slot]).wait()
    route = route_ref[...]
    ff = (_load_token_tiles(yg.at[slot, 0]) * route[:, 2:3]
          + _load_token_tiles(yg.at[slot, 1]) * route[:, 3:4])
    out_ref[...] = _ln(ALPHA * x1_ref[...] + g2_ref[0] * ff) * l2g_ref[...] + l2b_ref[...]


def _final_call(dest0, dest1, first_token, x1, ybuf, route, g2, ln2_g, ln2_b, rows_per_g2, tile):
    n = x1.shape[0]
    assert n % tile == 0 and rows_per_g2 % tile == 0 and first_token % tile == 0
    per = rows_per_g2 // tile
    g2_rows = g2.shape[1]
    steps = n // tile
    off = first_token // tile
    idx = lambda: pl.BlockSpec((tile,), lambda i: (off + i,), memory_space=pltpu.SMEM)
    idx_next = lambda: pl.BlockSpec((tile,), lambda i: (off + jnp.minimum(i + 1, steps - 1),),
                                    memory_space=pltpu.SMEM)
    return pl.pallas_call(
        _final_kernel,
        grid=(steps,),
        in_specs=[idx(), idx(), idx_next(), idx_next(),
                  pl.BlockSpec((tile, D_MODEL), lambda i: (i, 0)),
                  pl.BlockSpec(memory_space=pl.ANY),
                  pl.BlockSpec((tile, LANES), lambda i: (i, 0)),
                  pl.BlockSpec((1, g2_rows, D_MODEL), lambda i: (i // per, 0, 0)),
                  pl.BlockSpec((1, D_MODEL), lambda i: (0, 0)),
                  pl.BlockSpec((1, D_MODEL), lambda i: (0, 0))],
        out_specs=pl.BlockSpec((tile, D_MODEL), lambda i: (i, 0)),
        out_shape=jax.ShapeDtypeStruct((n, D_MODEL), F32),
        scratch_shapes=[pltpu.VMEM((2, 2, tile * SUBLANES, LANES), F32), pltpu.SemaphoreType.DMA((2,))],
        compiler_params=pltpu.CompilerParams(dimension_semantics=("arbitrary",)),
        name="final",
    )(dest0, dest1, dest0, dest1, x1, ybuf, route, g2, ln2_g, ln2_b)


def _block_diag_gate_weights(wa, wx):
    def bd(w):
        w = w.reshape(N_RG_TILES, RG_GROUP, RG_BW, RG_BW)
        eye = jnp.eye(RG_GROUP, dtype=w.dtype)
        return jnp.einsum('tgcd,gh->tgchd', w, eye).reshape(N_RG_TILES, MXU_DIM, MXU_DIM)
    return jnp.concatenate([bd(wa), bd(wx)], axis=2)


def kernel(x_prompt, x_sample, state_conv, state_rglru, state_gla, c_prompt, c_sample, w_mod, b_mod, w_in, b_in,
           conv_w, conv_b, rg_wa, rg_ba, rg_wx, rg_bx, rg_lambda, gla_wa2, gla_ba, gla_norm_g, p_a, p_b, w_o,
           ln1_g, ln1_b, w_grp, b_grp, w_exp, b_exp, we_gate, we_up, we_down, ln2_g, ln2_b):
    assert w_mod.shape[0] == DEPTH == 1
    bp, tp, _ = x_prompt.shape
    ns = x_sample.shape[0]
    n_p = bp * tp
    n_tok = n_p + ns
    row = lambda a: a.reshape(1, -1)

    lo = 2 * D_RNN + 2 * GLA_DK + 2 * GLA_DV
    w_in0 = w_in[0]
    assert lo == O_GA
    win_a = _pack_rows(w_in0, lo)
    win_b = _pack_rows(w_in0[:, lo + GLA_RANK:])
    bin_main = row(jnp.concatenate([b_in[0, :lo], b_in[0, lo + GLA_RANK:]]))
    walr = _pack_rows(jnp.pad(w_in0[:, lo:lo + GLA_RANK], ((0, 0), (0, LANES - GLA_RANK))))
    balr = row(jnp.pad(b_in[0, lo:lo + GLA_RANK], (0, LANES - GLA_RANK)))
    wa2 = _pack_rows(jnp.pad(gla_wa2[0], ((0, LANES - GLA_RANK), (0, 0))))
    wrg = _pack_rows(_block_diag_gate_weights(rg_wa[0], rg_wx[0]).reshape(RG_BLOCKS * RG_BW, 2 * MXU_DIM))
    wrg = wrg.reshape(N_RG_TILES, MXU_DIM // 2, 2 * MXU_DIM)
    w_route = _pack_rows(jnp.pad(jnp.concatenate([w_exp[0], w_grp[0]], axis=1),
                                 ((0, 0), (0, LANES - N_EXPERTS - N_GROUPS))))
    b_route = row(jnp.pad(jnp.concatenate([b_exp[0], b_grp[0]]), (0, LANES - N_EXPERTS - N_GROUPS)))
    pa, pb, wo = _pack_rows(p_a[0]), _pack_rows(p_b[0]), _pack_rows(w_o[0])
    pre_w = [win_a, win_b, bin_main, walr, balr, wa2, row(gla_ba[0])]
    rec_w = [conv_w[0], row(conv_b[0]), wrg, row(rg_ba[0]), row(rg_bx[0]), row(rg_lambda[0])]
    post_w = [row(gla_norm_g[0]), pa, pb, wo, row(ln1_g[0]), row(ln1_b[0]), w_route, b_route]

    mod = _mod_call(jnp.concatenate([c_prompt, c_sample], axis=0), w_mod[0], row(b_mod[0]))
    mod_p = mod[:bp].reshape(bp, 6, D_MODEL)
    mod_s = mod[bp:].reshape(ns, 6, D_MODEL).transpose(1, 0, 2)

    mix_w = pre_w + [post_w[0]] + rec_w + post_w[1:]
    x1_p, hf_p, route_p, routet_p, conv_p, h_p, s_p = _mixer_call(x_prompt, mod_p, mix_w)

    xs = x_sample.reshape(ns, D_MODEL)
    sconv = state_conv[0].transpose(1, 0, 2)
    conv_s, h_s, ya_s, q_s, k_s, v_s, la_s, gg_s, ga_s, gb_s = _s_pre_call(
        xs, mod_s, sconv, state_rglru[0], pre_w + rec_w)
    s_s, o_s = _s_state_call(state_gla[0], q_s, k_s, v_s, la_s)
    x1_s, hf_s, route_s, routet_s = _s_post_call(xs, mod_s, o_s, gg_s, ga_s, gb_s, ya_s, post_w)

    route_pf = route_p.reshape(n_p, LANES)
    eid = jnp.concatenate([routet_p[:2], routet_s[:2]], axis=1).astype(I32)
    assert n_tok % LANES == 0
    n_chunks = n_tok // LANES
    rank3, cnt = _rank_call(eid.reshape(2, n_chunks, LANES).transpose(1, 0, 2))
    rank = rank3.transpose(1, 0, 2).reshape(2, n_tok)
    counts = cnt[:, 0]
    pcounts = (counts + MOE_TM - 1) // MOE_TM * MOE_TM
    pends = jnp.cumsum(pcounts)
    pstarts = pends - pcounts
    experts = jnp.arange(N_EXPERTS, dtype=I32)[:, None, None]
    start_of = jnp.sum(jnp.where(eid[None] == experts, pstarts.astype(I32)[:, None, None], 0), axis=0)
    dest = start_of + rank
    dest0, dest1 = dest[0], dest[1]
    n_tiles = -(-(2 * n_tok + N_EXPERTS * (MOE_TM - 1)) // MOE_TM)
    tile_start = jnp.arange(n_tiles, dtype=I32) * MOE_TM
    blk_e = jnp.minimum(jnp.sum(pends[None, :] <= tile_start[:, None], axis=1), N_EXPERTS - 1).astype(I32)
    n_used = (pends[-1] // MOE_TM).astype(I32).reshape(1)

    pends32 = pends.astype(I32)
    xbuf = _dispatch_call(pends32, dest0, dest1, hf_p.reshape(n_p * SUBLANES, LANES), hf_s, n_tiles * MOE_TM)
    ybuf = _expert_call(blk_e, n_used, xbuf, we_gate[0], we_up[0], we_down[0])

    y_p = _final_call(dest0, dest1, 0, x1_p.reshape(n_p, D_MODEL), ybuf, route_pf, mod_p[:, 5:6, :],
                      row(ln2_g[0]), row(ln2_b[0]), tp, ROW_CHUNK)
    y_s = _final_call(dest0, dest1, n_p, x1_s, ybuf, route_s, mod_s[5][None], row(ln2_g[0]), row(ln2_b[0]), ns, ns)

    return (y_p.reshape(bp, tp, D_MODEL), y_s.reshape(ns, 1, D_MODEL),
            conv_p[None], h_p.reshape(1, bp, D_RNN), s_p[None],
            conv_s.transpose(1, 0, 2)[None], h_s[None], s_s[None])
```

```python
import functools

import jax
import jax.numpy as jnp
import numpy as np
from jax import lax
from jax.experimental import pallas as pl
from jax.experimental.pallas import tpu as pltpu

F32 = jnp.float32
BF16 = jnp.bfloat16
I32 = jnp.int32

D_MODEL = 1024
D_RNN = D_MODEL
RG_BLOCKS = 16
RG_BW = D_RNN // RG_BLOCKS
CONV_W = 4
RG_C = 8.0
GLA_HEADS = 4
GLA_DK = D_MODEL // 2
GLA_DV = D_MODEL
GLA_DKH = GLA_DK // GLA_HEADS
GLA_DVH = GLA_DV // GLA_HEADS
GLA_RANK = 16
GLA_TAU = 16.0
GLA_CHUNK = 64
N_GROUPS = 4
EXP_PER_GROUP = 8
N_EXPERTS = N_GROUPS * EXP_PER_GROUP
D_EXPERT = 512
DEPTH = 1
ALPHA = (2.0 * DEPTH) ** 0.25
LN_EPS = 1e-5
RMS_EPS = 1e-6

LANES = 128
SUBLANES = 8
MXU_DIM = 256
VMEM_LIMIT_BYTES = 56 * 1024 * 1024

MIX_TT = 256
STAGE1_COLS = 512
PIPE_DEPTH = 2
MOE_TM = 256
ROW_CHUNK = 128
SAMPLE_SB = 8
RG_GROUP = MXU_DIM // RG_BW
N_RG_TILES = RG_BLOCKS // RG_GROUP
N_MAIN = 2 * D_RNN + 2 * GLA_DK + 2 * GLA_DV + 2 * D_MODEL
ROUTE_G0 = N_EXPERTS

O_RX, O_RY, O_Q, O_K, O_V, O_GG, O_GA, O_GB = 0, 1024, 2048, 2560, 3072, 4096, 5120, 6144


def _ln(x):
    mu = jnp.mean(x, -1, keepdims=True)
    xc = x - mu
    var = jnp.mean(xc * xc, -1, keepdims=True)
    return xc * lax.rsqrt(var + LN_EPS)


def _dot(a, b):
    return jnp.dot(a, b, preferred_element_type=F32)


def _pack_kernel(w_ref, o_ref):
    o_ref[...] = pltpu.bitcast(w_ref[...].astype(BF16), jnp.uint32)


def _pack_rows(w, n=None):
    k = w.shape[0]
    n = w.shape[1] if n is None else n
    cn = min(n, 4 * LANES)
    assert k % (2 * SUBLANES) == 0 and n % cn == 0
    return pl.pallas_call(
        _pack_kernel,
        grid=(n // cn,),
        in_specs=[pl.BlockSpec((k, cn), lambda i: (0, i))],
        out_specs=pl.BlockSpec((k // 2, cn), lambda i: (0, i)),
        out_shape=jax.ShapeDtypeStruct((k // 2, n), jnp.uint32),
        compiler_params=pltpu.CompilerParams(dimension_semantics=("arbitrary",)),
        name="pack",
    )(w)


def _win(win_ref, winb_ref, lo, hi):
    if hi <= O_GA:
        return win_ref[:, lo:hi]
    assert lo >= O_GA
    return winb_ref[:, lo - O_GA:hi - O_GA]


def _w(packed):
    return pltpu.bitcast(packed, BF16)


def _expm1(x):
    u = jnp.exp(x)
    small = (u - 1.0) * x / jnp.log(u)
    return jnp.where(u == 1.0, x, jnp.where(jnp.abs(x) < 0.5, small, u - 1.0))


def _const_spec(shape):
    nd = len(shape)
    return pl.BlockSpec(shape, lambda *_: (0,) * nd, pipeline_mode=pl.Buffered(1))


assert D_MODEL == SUBLANES * LANES


def _tok_rows(t, count=1):
    return pl.ds(pl.multiple_of(t * SUBLANES, SUBLANES), count * SUBLANES)


def _store_token_tiles(ref, rows):
    n = rows.shape[0]
    for s in range(SUBLANES):
        ref[pl.ds(s, n, stride=SUBLANES), :] = rows[:, s * LANES:(s + 1) * LANES]


def _load_token_tiles(ref):
    n = ref.shape[0] // SUBLANES
    return jnp.concatenate([ref[pl.ds(s, n, stride=SUBLANES), :] for s in range(SUBLANES)], axis=1)


def _mod_kernel(c_ref, w_ref, b_ref, o_ref):
    c = c_ref[...]
    s = jax.nn.silu(c)
    o_ref[...] = _dot(s.astype(BF16), w_ref[...].astype(BF16)) + b_ref[...]


def _mod_call(c_all, w_mod, b_mod):
    n = c_all.shape[0]
    tn = 512
    return pl.pallas_call(
        _mod_kernel,
        grid=(6 * D_MODEL // tn,),
        in_specs=[pl.BlockSpec((n, D_MODEL), lambda i: (0, 0)),
                  pl.BlockSpec((D_MODEL, tn), lambda i: (0, i)),
                  pl.BlockSpec((1, tn), lambda i: (0, i))],
        out_specs=pl.BlockSpec((n, tn), lambda i: (0, i)),
        out_shape=jax.ShapeDtypeStruct((n, 6 * D_MODEL), F32),
        compiler_params=pltpu.CompilerParams(dimension_semantics=("arbitrary",)),
        name="mod",
    )(c_all, w_mod, b_mod)


def _no_op():
    pass


def _gate_logits(xc, wrg_ref, rba, rbx):
    xcb = xc.astype(BF16)
    parts = [_dot(xcb[:, c * MXU_DIM:(c + 1) * MXU_DIM], _w(wrg_ref[c])) for c in range(N_RG_TILES)]
    r_pre = jnp.concatenate([p[:, :MXU_DIM] for p in parts], axis=1)
    i_pre = jnp.concatenate([p[:, MXU_DIM:] for p in parts], axis=1)
    return r_pre, i_pre, rba, rbx


def _gate_logits_to(out_ref, xc, wrg_ref):
    xcb = xc.astype(BF16)
    for c in range(N_RG_TILES):
        p = _dot(xcb[:, c * MXU_DIM:(c + 1) * MXU_DIM], _w(wrg_ref[c]))
        out_ref[:, c * MXU_DIM:(c + 1) * MXU_DIM] = p[:, :MXU_DIM]
        out_ref[:, D_RNN + c * MXU_DIM:D_RNN + (c + 1) * MXU_DIM] = p[:, MXU_DIM:]


def _gates(xc, r_pre, i_pre, rba, rbx, lam, between=_no_op):
    r = jax.nn.sigmoid(r_pre + rba)
    between()
    ig = jax.nn.sigmoid(i_pre + rbx)
    log_a = -RG_C * r * jax.nn.softplus(-lam)
    a = jnp.exp(log_a)
    between()
    u = jnp.sqrt(-_expm1(2.0 * log_a)) * (ig * xc)
    between()
    return a, u


def _low_rank(hb, walr_ref, balr_ref):
    return _dot(hb, _w(walr_ref[...])) + balr_ref[...]


def _log_decay_from(alr, wa2_ref, gba_ref):
    return jax.nn.log_sigmoid(_dot(alr.astype(BF16), _w(wa2_ref[...])) + gba_ref[...]) / GLA_TAU


def _log_decay(hb, walr_ref, balr_ref, wa2_ref, gba_ref):
    return _log_decay_from(_low_rank(hb, walr_ref, balr_ref), wa2_ref, gba_ref)


def _post(x, o, gg, ga, gb, y_a, g1, sh2, sc2, gng, pa_ref, pb_ref, wo_ref, l1g, l1b, wr_ref, br,
          between=_no_op):
    rows = x.shape[0]
    heads = []
    for hh in range(GLA_HEADS):
        oh = o[:, hh * GLA_DVH:(hh + 1) * GLA_DVH]
        ms = jnp.mean(oh * oh, -1, keepdims=True)
        heads.append(oh * lax.rsqrt(ms + RMS_EPS) * gng)
    y_b = jnp.concatenate(heads, axis=1) * jax.nn.silu(gg)
    merged = (jax.nn.sigmoid(ga) * _dot(y_a.astype(BF16), _w(pa_ref[...]))
              + jax.nn.sigmoid(gb) * _dot(y_b.astype(BF16), _w(pb_ref[...])))
    mix = _dot(merged.astype(BF16), _w(wo_ref[...]))
    between()
    x1 = _ln(ALPHA * x + g1 * mix) * l1g + l1b
    between()
    hf = _ln(x1) * (1.0 + sc2) + sh2
    logits = _dot(hf.astype(BF16), _w(wr_ref[...])) + br
    lane = lax.broadcasted_iota(I32, (rows, LANES), 1).astype(F32)
    neg = jnp.float32(-jnp.inf)
    big = jnp.float32(LANES)
    g_valid = (lane >= ROUTE_G0) & (lane < ROUTE_G0 + N_GROUPS)
    gl = jnp.where(g_valid, logits, neg)
    gmax = jnp.max(gl, -1, keepdims=True)
    g_lane = jnp.min(jnp.where(gl == gmax, lane, big), -1, keepdims=True)
    g_w = 1.0 / jnp.sum(jnp.exp(gl - gmax), -1, keepdims=True)
    e_lo = (g_lane - ROUTE_G0) * EXP_PER_GROUP
    el = jnp.where((lane >= e_lo) & (lane < e_lo + EXP_PER_GROUP), logits, neg)
    t1 = jnp.max(el, -1, keepdims=True)
    i1 = jnp.min(jnp.where(el == t1, lane, big), -1, keepdims=True)
    el2 = jnp.where(lane == i1, neg, el)
    t2 = jnp.max(el2, -1, keepdims=True)
    i2 = jnp.min(jnp.where(el2 == t2, lane, big), -1, keepdims=True)
    e2 = jnp.exp(t2 - t1)
    den = 1.0 + e2
    w1 = (1.0 / den) * g_w
    w2 = (e2 / den) * g_w
    route = jnp.where(lane == 0, i1.astype(F32),
                      jnp.where(lane == 1, i2.astype(F32),
                                jnp.where(lane == 2, w1, jnp.where(lane == 3, w2, 0.0))))
    return x1, hf, route


def _mixer_kernel(xn_ref, modn_ref, x_ref, mod_ref, win_ref, winb_ref, bin_ref, walr_ref, balr_ref, wa2_ref, gba_ref,
                  gng_ref, cw_ref, cb_ref, wrg_ref, rba_ref, rbx_ref, lam_ref, pa_ref, pb_ref, wo_ref,
                  l1g_ref, l1b_ref, wr_ref, br_ref,
                  x1_ref, hf_ref, route_ref, routet_ref, convn_ref, hlast_ref, sfin_ref,
                  hbn_scr, hbc_scr, pn_scr, pc_scr, alrn_scr, alrc_scr, xcn_scr, xcc_scr, gaten_scr, gatec_scr,
                  rxbuf, hcar, st_ref, o_scr,
                  *, tiles_per_seq):
    i = pl.program_id(0)
    t = jnp.maximum(i - PIPE_DEPTH, 0)
    j = t % tiles_per_seq
    j1 = jnp.maximum(i - 1, 0) % tiles_per_seq
    tt = x_ref.shape[1]

    @pl.when(i == 0)
    def _no_tile_yet():
        hbc_scr[...] = jnp.zeros_like(hbc_scr)
        pc_scr[...] = jnp.zeros_like(pc_scr)
        alrc_scr[...] = jnp.zeros_like(alrc_scr)
        xcc_scr[...] = jnp.zeros_like(xcc_scr)
        gatec_scr[...] = jnp.zeros_like(gatec_scr)

    @pl.when(j1 == 0)
    def _init_conv():
        rxbuf[0:SUBLANES, :] = jnp.zeros((SUBLANES, D_RNN), F32)

    @pl.when(j == 0)
    def _init():
        hcar[...] = jnp.zeros_like(hcar)
        st_ref[...] = jnp.zeros_like(st_ref)

    def stage0():
        modn = modn_ref[0]
        hbn_scr[...] = (_ln(xn_ref[0]) * (1.0 + modn[1:2]) + modn[0:1]).astype(BF16)

    hbn = hbc_scr[...]

    def _proj_block(lo):
        def run():
            hi = lo + STAGE1_COLS
            pn_scr[:, lo:hi] = _dot(hbn, _w(_win(win_ref, winb_ref, lo, hi)))
        return run

    def _alr_block():
        alrn_scr[...] = _dot(hbn, _w(walr_ref[...]))

    pending = [_proj_block(lo) for lo in range(0, N_MAIN, STAGE1_COLS)] + [_alr_block]

    def pump(count=1):
        for _ in range(min(count, len(pending))):
            pending.pop(0)()

    x = x_ref[0]
    mod = mod_ref[0]
    g1, sh2, sc2 = mod[2:3], mod[3:4], mod[4:5]

    def proj(lo, hi):
        return pc_scr[:, lo:hi] + bin_ref[:, lo:hi]

    pump(O_RY // STAGE1_COLS)

    rx = pn_scr[:, O_RX:O_RX + D_RNN] + bin_ref[:, O_RX:O_RX + D_RNN]
    rxbuf[SUBLANES:SUBLANES + tt, :] = rx
    cw = cw_ref[...]
    xcn = cb_ref[...] + rxbuf[SUBLANES - 3:SUBLANES - 3 + tt, :] * cw[0:1]
    xcn = xcn + rxbuf[SUBLANES - 2:SUBLANES - 2 + tt, :] * cw[1:2]
    xcn = xcn + rxbuf[SUBLANES - 1:SUBLANES - 1 + tt, :] * cw[2:3]
    xcn = xcn + rx * cw[3:4]
    rxbuf[0:SUBLANES, :] = rxbuf[tt:tt + SUBLANES, :]
    xcn_scr[...] = xcn
    xc = xcc_scr[...]

    q = proj(O_Q, O_Q + GLA_DK) * (GLA_DKH ** -0.5)
    k = proj(O_K, O_K + GLA_DK)
    v = proj(O_V, O_V + GLA_DV)
    la = _log_decay_from(alrc_scr[...] + balr_ref[...], wa2_ref, gba_ref)
    pump()
    la = la.reshape(tt // SUBLANES, SUBLANES, GLA_DK)
    subk = lax.broadcasted_iota(I32, (1, SUBLANES, GLA_DK), 1)
    s = 1
    while s < SUBLANES:
        la = la + jnp.where(subk >= s, pltpu.roll(la, s, 1), 0.0)
        s *= 2
    groups = []
    for g in range(tt // SUBLANES):
        blk = la[g] if g % (GLA_CHUNK // SUBLANES) == 0 else la[g] + groups[-1][SUBLANES - 1:SUBLANES]
        groups.append(blk)
    bcum = jnp.concatenate(groups, axis=0)
    eb = jnp.exp(bcum)
    q_in = (q * eb).astype(BF16)
    k_in = (k * jnp.exp(-bcum)).astype(BF16)
    pump()
    stage0()
    gelu_ry = jax.nn.gelu(proj(O_RY, O_RY + D_RNN))
    pump()

    a, u = _gates(xc, gatec_scr[:, :D_RNN], gatec_scr[:, D_RNN:], rba_ref[...], rbx_ref[...], lam_ref[...],
                  between=pump)
    s = 1
    a = a.reshape(tt // SUBLANES, SUBLANES, D_RNN)
    u = u.reshape(tt // SUBLANES, SUBLANES, D_RNN)
    sub = lax.broadcasted_iota(I32, (1, SUBLANES, D_RNN), 1)
    while s < SUBLANES:
        keep = sub >= s
        a_s = jnp.where(keep, pltpu.roll(a, s, 1), 1.0)
        u_s = jnp.where(keep, pltpu.roll(u, s, 1), 0.0)
        u = a * u_s + u
        a = a * a_s
        pump()
        s *= 2
    carry = hcar[0:1, :]
    groups = []
    for g in range(tt // SUBLANES):
        hg = a[g] * carry + u[g]
        groups.append(hg)
        carry = hg[SUBLANES - 1:SUBLANES]
    hseq = jnp.concatenate(groups, axis=0)
    hcar[0:1, :] = carry
    y_a = hseq * gelu_ry

    tri = (lax.broadcasted_iota(I32, (GLA_CHUNK, GLA_CHUNK), 0)
           >= lax.broadcasted_iota(I32, (GLA_CHUNK, GLA_CHUNK), 1))
    n_chunks = tt // GLA_CHUNK
    nt_dims = (((1,), (1,)), ((), ()))
    attn, d_st, dec, vh_all = {}, {}, {}, {}
    for c in range(n_chunks):
        r0 = c * GLA_CHUNK
        for hh in range(GLA_HEADS):
            kc = slice(hh * GLA_DKH, (hh + 1) * GLA_DKH)
            b = bcum[r0:r0 + GLA_CHUNK, kc]
            btot = b[GLA_CHUNK - 1:GLA_CHUNK, :]
            k_out = (k[r0:r0 + GLA_CHUNK, kc] * jnp.exp(btot - b)).astype(BF16)
            vh = v[r0:r0 + GLA_CHUNK, hh * GLA_DVH:(hh + 1) * GLA_DVH].astype(BF16)
            scores = lax.dot_general(q_in[r0:r0 + GLA_CHUNK, kc], k_in[r0:r0 + GLA_CHUNK, kc], nt_dims,
                                     preferred_element_type=F32)
            attn[c, hh] = jnp.where(tri, scores, 0.0).astype(BF16)
            d_st[c, hh] = lax.dot_general(vh, k_out, (((0,), (0,)), ((), ())), preferred_element_type=F32)
            dec[c, hh] = jnp.exp(btot)
            vh_all[c, hh] = vh
        pump()
    starts = {}
    for hh in range(GLA_HEADS):
        st = st_ref[hh]
        for c in range(n_chunks):
            starts[c, hh] = st.astype(BF16)
            st = st * dec[c, hh] + d_st[c, hh]
        st_ref[hh] = st
    for c in range(n_chunks):
        r0 = c * GLA_CHUNK
        for hh in range(GLA_HEADS):
            qi = q_in[r0:r0 + GLA_CHUNK, hh * GLA_DKH:(hh + 1) * GLA_DKH]
            inter = lax.dot_general(qi, starts[c, hh], nt_dims, preferred_element_type=F32)
            o_scr[r0:r0 + GLA_CHUNK, hh * GLA_DVH:(hh + 1) * GLA_DVH] = _dot(attn[c, hh], vh_all[c, hh]) + inter

    x1, hf, route = _post(x, o_scr[...], proj(O_GG, O_GG + GLA_DV), proj(O_GA, O_GA + D_MODEL),
                          proj(O_GB, O_GB + D_MODEL), y_a, g1, sh2, sc2, gng_ref[...],
                          pa_ref, pb_ref, wo_ref, l1g_ref[...], l1b_ref[...], wr_ref, br_ref[...], between=pump)
    x1_ref[0] = x1
    _store_token_tiles(hf_ref.at[0], hf)
    route_ref[0] = route
    routet_ref[...] = route.T[0:SUBLANES, :]

    pump(len(pending))
    _gate_logits_to(gaten_scr, xcn, wrg_ref)
    pc_scr[:, O_RY:] = pn_scr[:, O_RY:]
    alrc_scr[...] = alrn_scr[...]
    hbc_scr[...] = hbn_scr[...]
    xcc_scr[...] = xcn_scr[...]
    gatec_scr[...] = gaten_scr[...]

    @pl.when((j1 == tiles_per_seq - 1) & (i >= 1))
    def _final_conv():
        convn_ref[0] = rxbuf[SUBLANES - (CONV_W - 1):SUBLANES, :]

    @pl.when((j == tiles_per_seq - 1) & (i >= PIPE_DEPTH))
    def _final():
        hlast_ref[0] = hcar[0:1, :]
        for hh in range(GLA_HEADS):
            sfin_ref[0, hh] = st_ref[hh].T


def _mixer_call(x, mod3, wts):
    b, t, _ = x.shape
    tt = MIX_TT
    assert t % tt == 0 and tt % GLA_CHUNK == 0
    nt = t // tt
    n_tiles = b * nt
    assert nt > 1
    cur = lambda i: jnp.maximum(i - PIPE_DEPTH, 0)
    nxt = lambda i: jnp.minimum(i, n_tiles - 1)
    tok = lambda i: (cur(i) // nt, cur(i) % nt, 0)
    tok_n = lambda i: (nxt(i) // nt, nxt(i) % nt, 0)
    per_b = lambda i: (cur(i) // nt, 0, 0)
    in_specs = [pl.BlockSpec((1, tt, D_MODEL), tok_n),
                pl.BlockSpec((1, 6, D_MODEL), lambda i: (nxt(i) // nt, 0, 0)),
                pl.BlockSpec((1, tt, D_MODEL), tok),
                pl.BlockSpec((1, 6, D_MODEL), per_b)] + [_const_spec(w.shape) for w in wts]
    out_specs = [pl.BlockSpec((1, tt, D_MODEL), tok),
                 pl.BlockSpec((1, tt * SUBLANES, LANES), tok),
                 pl.BlockSpec((1, tt, LANES), tok),
                 pl.BlockSpec((SUBLANES, tt), lambda i: (0, cur(i))),
                 pl.BlockSpec((1, CONV_W - 1, D_RNN), per_b),
                 pl.BlockSpec((1, 1, D_RNN), per_b),
                 pl.BlockSpec((1, GLA_HEADS, GLA_DKH, GLA_DVH), lambda i: (cur(i) // nt, 0, 0, 0))]
    out_shape = [jax.ShapeDtypeStruct((b, t, D_MODEL), F32),
                 jax.ShapeDtypeStruct((b, t * SUBLANES, LANES), F32),
                 jax.ShapeDtypeStruct((b, t, LANES), F32),
                 jax.ShapeDtypeStruct((SUBLANES, b * t), F32),
                 jax.ShapeDtypeStruct((b, CONV_W - 1, D_RNN), F32),
                 jax.ShapeDtypeStruct((b, 1, D_RNN), F32),
                 jax.ShapeDtypeStruct((b, GLA_HEADS, GLA_DKH, GLA_DVH), F32)]
    scratch = [pltpu.VMEM((tt, D_MODEL), BF16), pltpu.VMEM((tt, D_MODEL), BF16),
               pltpu.VMEM((tt, N_MAIN), F32), pltpu.VMEM((tt, N_MAIN), F32),
               pltpu.VMEM((tt, LANES), F32), pltpu.VMEM((tt, LANES), F32),
               pltpu.VMEM((tt, D_RNN), F32), pltpu.VMEM((tt, D_RNN), F32),
               pltpu.VMEM((tt, 2 * D_RNN), F32), pltpu.VMEM((tt, 2 * D_RNN), F32),
               pltpu.VMEM((tt + SUBLANES, D_RNN), F32),
               pltpu.VMEM((SUBLANES, D_RNN), F32),
               pltpu.VMEM((GLA_HEADS, GLA_DVH, GLA_DKH), F32),
               pltpu.VMEM((tt, GLA_DV), F32)]
    return pl.pallas_call(
        functools.partial(_mixer_kernel, tiles_per_seq=nt),
        grid=(n_tiles + PIPE_DEPTH,),
        in_specs=in_specs,
        out_specs=out_specs,
        out_shape=out_shape,
        scratch_shapes=scratch,
        compiler_params=pltpu.CompilerParams(dimension_semantics=("arbitrary",),
                                             vmem_limit_bytes=VMEM_LIMIT_BYTES),
        name="mixer",
    )(x, mod3, x, mod3, *wts)


def _s_pre_kernel(x_ref, mod_ref, sconv_ref, h0_ref, win_ref, winb_ref, bin_ref, walr_ref, balr_ref, wa2_ref, gba_ref,
                  cw_ref, cb_ref, wrg_ref, rba_ref, rbx_ref, lam_ref,
                  convn_ref, hnew_ref, ya_ref, q_ref, k_ref, v_ref, la_ref, gg_ref, ga_ref, gb_ref):
    x = x_ref[...]
    hb = (_ln(x) * (1.0 + mod_ref[1]) + mod_ref[0]).astype(BF16)

    def proj(lo, hi):
        return _dot(hb, _w(_win(win_ref, winb_ref, lo, hi))) + bin_ref[:, lo:hi]

    rx = proj(O_RX, O_RX + D_RNN)
    cw = cw_ref[...]
    xc = cb_ref[...] + sconv_ref[0] * cw[0:1]
    xc = xc + sconv_ref[1] * cw[1:2]
    xc = xc + sconv_ref[2] * cw[2:3]
    xc = xc + rx * cw[3:4]
    convn_ref[0] = sconv_ref[1]
    convn_ref[1] = sconv_ref[2]
    convn_ref[2] = rx
    a, u = _gates(xc, *_gate_logits(xc, wrg_ref, rba_ref[...], rbx_ref[...]), lam_ref[...])
    hnew = u + a * h0_ref[...]
    hnew_ref[...] = hnew
    ya_ref[...] = hnew * jax.nn.gelu(proj(O_RY, O_RY + D_RNN))
    q_ref[...] = proj(O_Q, O_Q + GLA_DK) * (GLA_DKH ** -0.5)
    k_ref[...] = proj(O_K, O_K + GLA_DK)
    v_ref[...] = proj(O_V, O_V + GLA_DV)
    la_ref[...] = _log_decay(hb, walr_ref, balr_ref, wa2_ref, gba_ref)
    gg_ref[...] = proj(O_GG, O_GG + GLA_DV)
    ga_ref[...] = proj(O_GA, O_GA + D_MODEL)
    gb_ref[...] = proj(O_GB, O_GB + D_MODEL)


def _s_pre_call(x_s, mod_s, sconv, h0, wts):
    n = x_s.shape[0]
    full = lambda shape: pl.BlockSpec(shape, lambda i: (0,) * len(shape))
    ins = [x_s, mod_s, sconv, h0] + list(wts)
    shapes = [(CONV_W - 1, n, D_RNN), (n, D_RNN), (n, D_RNN), (n, GLA_DK), (n, GLA_DK), (n, GLA_DV),
              (n, GLA_DK), (n, GLA_DV), (n, D_MODEL), (n, D_MODEL)]
    return pl.pallas_call(
        _s_pre_kernel,
        grid=(1,),
        in_specs=[full(a.shape) for a in ins],
        out_specs=[full(s) for s in shapes],
        out_shape=[jax.ShapeDtypeStruct(s, F32) for s in shapes],
        compiler_params=pltpu.CompilerParams(dimension_semantics=("arbitrary",),
                                             vmem_limit_bytes=VMEM_LIMIT_BYTES),
        name="s_pre",
    )(*ins)


def _to_column(row, n):
    eye = lax.broadcasted_iota(I32, (n, n), 0) == lax.broadcasted_iota(I32, (n, n), 1)
    return jnp.sum(jnp.where(eye, jnp.broadcast_to(row, (n, n)), 0.0), axis=1, keepdims=True)


def _s_state_kernel(s_ref, q_ref, k_ref, v_ref, la_ref, snew_ref, o_ref):
    sb = s_ref.shape[0]
    qb = q_ref[...].astype(BF16)
    dec = jnp.exp(la_ref[...])
    kk = k_ref[...]
    vv = v_ref[...]
    for si in range(sb):
        for hh in range(GLA_HEADS):
            kc = slice(hh * GLA_DKH, (hh + 1) * GLA_DKH)
            vc = slice(hh * GLA_DVH, (hh + 1) * GLA_DVH)
            d_col = _to_column(dec[si:si + 1, kc], GLA_DKH)
            k_col = _to_column(kk[si:si + 1, kc], GLA_DKH)
            s_new = s_ref[si, hh] * d_col + k_col * vv[si:si + 1, vc]
            snew_ref[si, hh] = s_new
            o_all = _dot(qb[:, kc], s_new.astype(BF16))
            o_ref[si:si + 1, vc] = o_all[si:si + 1, :]


def _s_state_call(state, q, k, v, la):
    n = state.shape[0]
    sb = SAMPLE_SB
    assert n % sb == 0
    blk = lambda w: pl.BlockSpec((sb, w), lambda i: (i, 0))
    st_spec = pl.BlockSpec((sb, GLA_HEADS, GLA_DKH, GLA_DVH), lambda i: (i, 0, 0, 0))
    return pl.pallas_call(
        _s_state_kernel,
        grid=(n // sb,),
        in_specs=[st_spec, blk(GLA_DK), blk(GLA_DK), blk(GLA_DV), blk(GLA_DK)],
        out_specs=[st_spec, blk(GLA_DV)],
        out_shape=[jax.ShapeDtypeStruct(state.shape, F32), jax.ShapeDtypeStruct((n, GLA_DV), F32)],
        compiler_params=pltpu.CompilerParams(dimension_semantics=("arbitrary",),
                                             vmem_limit_bytes=VMEM_LIMIT_BYTES),
        name="s_state",
    )(state, q, k, v, la)


def _s_post_kernel(x_ref, mod_ref, o_ref, gg_ref, ga_ref, gb_ref, ya_ref, gng_ref, pa_ref, pb_ref, wo_ref,
                   l1g_ref, l1b_ref, wr_ref, br_ref, x1_ref, hf_ref, route_ref, routet_ref):
    x1, hf, route = _post(x_ref[...], o_ref[...], gg_ref[...], ga_ref[...], gb_ref[...], ya_ref[...],
                          mod_ref[2], mod_ref[3], mod_ref[4], gng_ref[...], pa_ref, pb_ref, wo_ref,
                          l1g_ref[...], l1b_ref[...], wr_ref, br_ref[...])
    x1_ref[...] = x1
    _store_token_tiles(hf_ref, hf)
    route_ref[...] = route
    routet_ref[...] = route.T[0:SUBLANES, :]


def _s_post_call(x_s, mod_s, o, gg, ga, gb, ya, wts):
    n = x_s.shape[0]
    full = lambda shape: pl.BlockSpec(shape, lambda i: (0,) * len(shape))
    ins = [x_s, mod_s, o, gg, ga, gb, ya] + list(wts)
    shapes = [(n, D_MODEL), (n * SUBLANES, LANES), (n, LANES), (SUBLANES, n)]
    return pl.pallas_call(
        _s_post_kernel,
        grid=(1,),
        in_specs=[full(a.shape) for a in ins],
        out_specs=[full(s) for s in shapes],
        out_shape=[jax.ShapeDtypeStruct(s, F32) for s in shapes],
        compiler_params=pltpu.CompilerParams(dimension_semantics=("arbitrary",),
                                             vmem_limit_bytes=VMEM_LIMIT_BYTES),
        name="s_post",
    )(*ins)


def _rank_kernel(eid_ref, rank_ref, cnt_ref, carry):
    n_chunks = eid_ref.shape[0]
    carry[...] = jnp.zeros_like(carry)
    e_iota = lax.broadcasted_iota(I32, (N_EXPERTS, LANES), 0)
    upper = (lax.broadcasted_iota(I32, (LANES, LANES), 0)
             < lax.broadcasted_iota(I32, (LANES, LANES), 1)).astype(BF16)

    def body(c, _):
        ids = eid_ref[c]
        oh0 = (e_iota == ids[0:1, :])
        oh1 = (e_iota == ids[1:2, :])
        both = (oh0 | oh1).astype(F32)
        before = _dot(both.astype(BF16), upper) + carry[:, 0:1]
        r0 = jnp.sum(jnp.where(oh0, before, 0.0), axis=0, keepdims=True)
        r1 = jnp.sum(jnp.where(oh1, before, 0.0), axis=0, keepdims=True)
        rank_ref[c] = jnp.concatenate([r0, r1], axis=0).astype(I32)
        carry[...] = carry[...] + jnp.sum(both, axis=1, keepdims=True)
        return 0

    lax.fori_loop(0, n_chunks, body, 0)
    cnt_ref[...] = carry[...].astype(I32)


def _rank_call(eid3):
    n_chunks = eid3.shape[0]
    full = lambda shape: pl.BlockSpec(shape, lambda i: (0,) * len(shape))
    return pl.pallas_call(
        _rank_kernel,
        grid=(1,),
        in_specs=[full(eid3.shape)],
        out_specs=[full(eid3.shape), full((N_EXPERTS, LANES))],
        out_shape=[jax.ShapeDtypeStruct(eid3.shape, I32), jax.ShapeDtypeStruct((N_EXPERTS, LANES), I32)],
        scratch_shapes=[pltpu.VMEM((N_EXPERTS, LANES), F32)],
        compiler_params=pltpu.CompilerParams(dimension_semantics=("arbitrary",)),
        name="rank",
    )(eid3)


def _dispatch_kernel(pends_ref, d0_ref, d1_ref, src_p_ref, src_s_ref, buf_ref, sem, zeros_ref, zsem, *,
                     n_p_chunks):
    i = pl.program_id(0)
    n_tiles = buf_ref.shape[0] // (MOE_TM * SUBLANES)

    @pl.when(i == 0)
    def _clear():
        zeros_ref[...] = jnp.zeros_like(zeros_ref)

        def tile_copy(row0):
            return pltpu.make_async_copy(zeros_ref, buf_ref.at[_tok_rows(row0, MOE_TM)], zsem)

        def nonempty(e):
            return pends_ref[e] > jnp.where(e == 0, 0, pends_ref[jnp.maximum(e - 1, 0)])

        def start_tail(e, _):
            @pl.when(nonempty(e))
            def _():
                tile_copy(pends_ref[e] - MOE_TM).start()
            return 0

        def wait_tail(e, _):
            @pl.when(nonempty(e))
            def _():
                tile_copy(0).wait()
            return 0

        def start_unused(j, _):
            tile_copy(j * MOE_TM).start()
            return 0

        def wait_unused(j, _):
            tile_copy(0).wait()
            return 0

        first_unused = pends_ref[N_EXPERTS - 1] // MOE_TM
        lax.fori_loop(0, N_EXPERTS, start_tail, 0)
        lax.fori_loop(first_unused, n_tiles, start_unused, 0)
        lax.fori_loop(0, N_EXPERTS, wait_tail, 0)
        lax.fori_loop(first_unused, n_tiles, wait_unused, 0)

    def scatter_rows(src_ref):
        n = src_ref.shape[0] // SUBLANES

        def start_rows(t, _):
            for k, dest_ref in enumerate((d0_ref, d1_ref)):
                pltpu.make_async_copy(src_ref.at[_tok_rows(t)], buf_ref.at[_tok_rows(dest_ref[t])],
                                      sem).start(priority=k)
            return 0

        lax.fori_loop(0, n, start_rows, 0, unroll=4)
        for _ in range(2):
            pltpu.make_async_copy(src_ref, buf_ref.at[_tok_rows(0, n)], sem).wait()

    @pl.when(i < n_p_chunks)
    def _prompt():
        scatter_rows(src_p_ref)

    @pl.when(i >= n_p_chunks)
    def _sample():
        scatter_rows(src_s_ref)


def _dispatch_call(pends, dest0, dest1, src_p, src_s, n_rows):
    chunk_rows = ROW_CHUNK * SUBLANES
    assert src_p.shape[0] % chunk_rows == 0 and src_s.shape[0] % chunk_rows == 0 and n_rows % MOE_TM == 0
    n_p_chunks = src_p.shape[0] // chunk_rows
    n_s_chunks = src_s.shape[0] // chunk_rows
    tile = (chunk_rows, LANES)
    in_specs = [pl.BlockSpec((ROW_CHUNK,), lambda i, pe: (i,), memory_space=pltpu.SMEM),
                pl.BlockSpec((ROW_CHUNK,), lambda i, pe: (i,), memory_space=pltpu.SMEM),
                pl.BlockSpec(tile, lambda i, pe: (jnp.minimum(i, n_p_chunks - 1), 0)),
                pl.BlockSpec(tile, lambda i, pe: (jnp.maximum(i - n_p_chunks, 0), 0))]
    scratch = [pltpu.SemaphoreType.DMA(()), pltpu.VMEM((MOE_TM * SUBLANES, LANES), F32),
               pltpu.SemaphoreType.DMA(())]
    return pl.pallas_call(
        functools.partial(_dispatch_kernel, n_p_chunks=n_p_chunks),
        grid_spec=pltpu.PrefetchScalarGridSpec(
            num_scalar_prefetch=1, grid=(n_p_chunks + n_s_chunks,), in_specs=in_specs,
            out_specs=pl.BlockSpec(memory_space=pl.ANY), scratch_shapes=scratch),
        out_shape=jax.ShapeDtypeStruct((n_rows * SUBLANES, LANES), F32),
        compiler_params=pltpu.CompilerParams(dimension_semantics=("arbitrary",), has_side_effects=True),
        name="dispatch",
    )(pends, dest0, dest1, src_p, src_s)


def _expert_kernel(blk_e_ref, n_used_ref, x_ref, wg0_ref, wu0_ref, wd0_ref, wg1_ref, wu1_ref, wd1_ref, y_ref,
                   wgb, wub, wdb):
    i = pl.program_id(0)
    rows = MOE_TM * SUBLANES
    halves = ((0, wg0_ref, wu0_ref, wd0_ref), (1, wg1_ref, wu1_ref, wd1_ref))

    for h, wg_ref, wu_ref, wd_ref in halves:
        e_now = blk_e_ref[2 * i + h]
        e_before = blk_e_ref[jnp.maximum(2 * i + h - 2, 0)]

        @pl.when((i == 0) | (e_now != e_before))
        def _cast_weights(h=h, wg_ref=wg_ref, wu_ref=wu_ref, wd_ref=wd_ref):
            wgb[h] = wg_ref[0].astype(BF16)
            wub[h] = wu_ref[0].astype(BF16)
            wdb[h] = wd_ref[0].astype(BF16)

    def swiglu(x_view, y_view, h):
        xb = _load_token_tiles(x_view).astype(BF16)
        g = _dot(xb, wgb[h])
        u = _dot(xb, wub[h])
        act = (jax.nn.silu(g) * u).astype(BF16)
        _store_token_tiles(y_view, _dot(act, wdb[h]))

    used = 2 * i < n_used_ref[0]
    same = blk_e_ref[2 * i] == blk_e_ref[2 * i + 1]

    @pl.when(used & same)
    def _one_expert():
        swiglu(x_ref, y_ref, 0)

    @pl.when(used & jnp.logical_not(same))
    def _two_experts():
        for h in range(2):
            swiglu(x_ref.at[pl.ds(h * rows, rows)], y_ref.at[pl.ds(h * rows, rows)], h)

    @pl.when(jnp.logical_not(used))
    def _skip():
        y_ref[...] = jnp.zeros_like(y_ref)


def _expert_call(blk_e, n_used, xbuf, we_gate, we_up, we_down):
    p = xbuf.shape[0] // SUBLANES
    assert p % (2 * MOE_TM) == 0 and blk_e.shape[0] == p // MOE_TM
    tile_spec = pl.BlockSpec((2 * MOE_TM * SUBLANES, LANES), lambda i, be, nu: (i, 0))
    w_in_spec = lambda h: pl.BlockSpec((1, D_MODEL, D_EXPERT), lambda i, be, nu: (be[2 * i + h], 0, 0))
    w_out_spec = lambda h: pl.BlockSpec((1, D_EXPERT, D_MODEL), lambda i, be, nu: (be[2 * i + h], 0, 0))
    grid_spec = pltpu.PrefetchScalarGridSpec(
        num_scalar_prefetch=2,
        grid=(p // (2 * MOE_TM),),
        in_specs=[tile_spec, w_in_spec(0), w_in_spec(0), w_out_spec(0), w_in_spec(1), w_in_spec(1), w_out_spec(1)],
        out_specs=tile_spec,
        scratch_shapes=[pltpu.VMEM((2, D_MODEL, D_EXPERT), BF16), pltpu.VMEM((2, D_MODEL, D_EXPERT), BF16),
                        pltpu.VMEM((2, D_EXPERT, D_MODEL), BF16)],
    )
    return pl.pallas_call(
        _expert_kernel,
        grid_spec=grid_spec,
        out_shape=jax.ShapeDtypeStruct((p * SUBLANES, LANES), F32),
        compiler_params=pltpu.CompilerParams(dimension_semantics=("arbitrary",),
                                             vmem_limit_bytes=VMEM_LIMIT_BYTES),
        name="experts",
    )(blk_e, n_used, xbuf, we_gate, we_up, we_down, we_gate, we_up, we_down)


def _final_kernel(d0_ref, d1_ref, d0n_ref, d1n_ref, x1_ref, ybuf_ref, route_ref, g2_ref, l2g_ref, l2b_ref, out_ref,
                  yg, sems):
    n = x1_ref.shape[0]
    i = pl.program_id(0)
    slot = i % 2

    def gather(idx_refs, s):
        def start_rows(t, _):
            for k, idx_ref in enumerate(idx_refs):
                pltpu.make_async_copy(ybuf_ref.at[_tok_rows(idx_ref[t])], yg.at[s, k, _tok_rows(t)],
                                      sems.at[s]).start(priority=k)
            return 0
        lax.fori_loop(0, n, start_rows, 0, unroll=4)

    @pl.when(i == 0)
    def _first():
        gather((d0_ref, d1_ref), 0)

    @pl.when(i + 1 < pl.num_programs(0))
    def _prefetch():
        gather((d0n_ref, d1n_ref), 1 - slot)

    for k in range(2):
        pltpu.make_async_copy(ybuf_ref.at[_tok_rows(0, n)], yg.at[slot, k], sems.at[slot]).wait()
    route = route_ref[...]
    ff = (_load_token_tiles(yg.at[slot, 0]) * route[:, 2:3]
          + _load_token_tiles(yg.at[slot, 1]) * route[:, 3:4])
    out_ref[...] = _ln(ALPHA * x1_ref[...] + g2_ref[0] * ff) * l2g_ref[...] + l2b_ref[...]


def _final_call(dest0, dest1, first_token, x1, ybuf, route, g2, ln2_g, ln2_b, rows_per_g2, tile):
    n = x1.shape[0]
    assert n % tile == 0 and rows_per_g2 % tile == 0 and first_token % tile == 0
    per = rows_per_g2 // tile
    g2_rows = g2.shape[1]
    steps = n // tile
    off = first_token // tile
    idx = lambda: pl.BlockSpec((tile,), lambda i: (off + i,), memory_space=pltpu.SMEM)
    idx_next = lambda: pl.BlockSpec((tile,), lambda i: (off + jnp.minimum(i + 1, steps - 1),),
                                    memory_space=pltpu.SMEM)
    return pl.pallas_call(
        _final_kernel,
        grid=(steps,),
        in_specs=[idx(), idx(), idx_next(), idx_next(),
                  pl.BlockSpec((tile, D_MODEL), lambda i: (i, 0)),
                  pl.BlockSpec(memory_space=pl.ANY),
                  pl.BlockSpec((tile, LANES), lambda i: (i, 0)),
                  pl.BlockSpec((1, g2_rows, D_MODEL), lambda i: (i // per, 0, 0)),
                  pl.BlockSpec((1, D_MODEL), lambda i: (0, 0)),
                  pl.BlockSpec((1, D_MODEL), lambda i: (0, 0))],
        out_specs=pl.BlockSpec((tile, D_MODEL), lambda i: (i, 0)),
        out_shape=jax.ShapeDtypeStruct((n, D_MODEL), F32),
        scratch_shapes=[pltpu.VMEM((2, 2, tile * SUBLANES, LANES), F32), pltpu.SemaphoreType.DMA((2,))],
        compiler_params=pltpu.CompilerParams(dimension_semantics=("arbitrary",)),
        name="final",
    )(dest0, dest1, dest0, dest1, x1, ybuf, route, g2, ln2_g, ln2_b)


def _block_diag_gate_weights(wa, wx):
    def bd(w):
        w = w.reshape(N_RG_TILES, RG_GROUP, RG_BW, RG_BW)
        eye = jnp.eye(RG_GROUP, dtype=w.dtype)
        return jnp.einsum('tgcd,gh->tgchd', w, eye).reshape(N_RG_TILES, MXU_DIM, MXU_DIM)
    return jnp.concatenate([bd(wa), bd(wx)], axis=2)


def kernel(x_prompt, x_sample, state_conv, state_rglru, state_gla, c_prompt, c_sample, w_mod, b_mod, w_in, b_in,
           conv_w, conv_b, rg_wa, rg_ba, rg_wx, rg_bx, rg_lambda, gla_wa2, gla_ba, gla_norm_g, p_a, p_b, w_o,
           ln1_g, ln1_b, w_grp, b_grp, w_exp, b_exp, we_gate, we_up, we_down, ln2_g, ln2_b):
    assert w_mod.shape[0] == DEPTH == 1
    bp, tp, _ = x_prompt.shape
    ns = x_sample.shape[0]
    n_p = bp * tp
    n_tok = n_p + ns
    row = lambda a: a.reshape(1, -1)

    lo = 2 * D_RNN + 2 * GLA_DK + 2 * GLA_DV
    w_in0 = w_in[0]
    assert lo == O_GA
    win_a = _pack_rows(w_in0, lo)
    win_b = _pack_rows(w_in0[:, lo + GLA_RANK:])
    bin_main = row(jnp.concatenate([b_in[0, :lo], b_in[0, lo + GLA_RANK:]]))
    walr = _pack_rows(jnp.pad(w_in0[:, lo:lo + GLA_RANK], ((0, 0), (0, LANES - GLA_RANK))))
    balr = row(jnp.pad(b_in[0, lo:lo + GLA_RANK], (0, LANES - GLA_RANK)))
    wa2 = _pack_rows(jnp.pad(gla_wa2[0], ((0, LANES - GLA_RANK), (0, 0))))
    wrg = _pack_rows(_block_diag_gate_weights(rg_wa[0], rg_wx[0]).reshape(RG_BLOCKS * RG_BW, 2 * MXU_DIM))
    wrg = wrg.reshape(N_RG_TILES, MXU_DIM // 2, 2 * MXU_DIM)
    w_route = _pack_rows(jnp.pad(jnp.concatenate([w_exp[0], w_grp[0]], axis=1),
                                 ((0, 0), (0, LANES - N_EXPERTS - N_GROUPS))))
    b_route = row(jnp.pad(jnp.concatenate([b_exp[0], b_grp[0]]), (0, LANES - N_EXPERTS - N_GROUPS)))
    pa, pb, wo = _pack_rows(p_a[0]), _pack_rows(p_b[0]), _pack_rows(w_o[0])
    pre_w = [win_a, win_b, bin_main, walr, balr, wa2, row(gla_ba[0])]
    rec_w = [conv_w[0], row(conv_b[0]), wrg, row(rg_ba[0]), row(rg_bx[0]), row(rg_lambda[0])]
    post_w = [row(gla_norm_g[0]), pa, pb, wo, row(ln1_g[0]), row(ln1_b[0]), w_route, b_route]

    mod = _mod_call(jnp.concatenate([c_prompt, c_sample], axis=0), w_mod[0], row(b_mod[0]))
    mod_p = mod[:bp].reshape(bp, 6, D_MODEL)
    mod_s = mod[bp:].reshape(ns, 6, D_MODEL).transpose(1, 0, 2)

    mix_w = pre_w + [post_w[0]] + rec_w + post_w[1:]
    x1_p, hf_p, route_p, routet_p, conv_p, h_p, s_p = _mixer_call(x_prompt, mod_p, mix_w)

    xs = x_sample.reshape(ns, D_MODEL)
    sconv = state_conv[0].transpose(1, 0, 2)
    conv_s, h_s, ya_s, q_s, k_s, v_s, la_s, gg_s, ga_s, gb_s = _s_pre_call(
        xs, mod_s, sconv, state_rglru[0], pre_w + rec_w)
    s_s, o_s = _s_state_call(state_gla[0], q_s, k_s, v_s, la_s)
    x1_s, hf_s, route_s, routet_s = _s_post_call(xs, mod_s, o_s, gg_s, ga_s, gb_s, ya_s, post_w)

    route_pf = route_p.reshape(n_p, LANES)
    eid = jnp.concatenate([routet_p[:2], routet_s[:2]], axis=1).astype(I32)
    assert n_tok % LANES == 0
    n_chunks = n_tok // LANES
    rank3, cnt = _rank_call(eid.reshape(2, n_chunks, LANES).transpose(1, 0, 2))
    rank = rank3.transpose(1, 0, 2).reshape(2, n_tok)
    counts = cnt[:, 0]
    pcounts = (counts + MOE_TM - 1) // MOE_TM * MOE_TM
    pends = jnp.cumsum(pcounts)
    pstarts = pends - pcounts
    experts = jnp.arange(N_EXPERTS, dtype=I32)[:, None, None]
    start_of = jnp.sum(jnp.where(eid[None] == experts, pstarts.astype(I32)[:, None, None], 0), axis=0)
    dest = start_of + rank
    dest0, dest1 = dest[0], dest[1]
    n_tiles = -(-(2 * n_tok + N_EXPERTS * (MOE_TM - 1)) // (2 * MOE_TM)) * 2
    tile_start = jnp.arange(n_tiles, dtype=I32) * MOE_TM
    blk_e = jnp.minimum(jnp.sum(pends[None, :] <= tile_start[:, None], axis=1), N_EXPERTS - 1).astype(I32)
    n_used = (pends[-1] // MOE_TM).astype(I32).reshape(1)

    pends32 = pends.astype(I32)
    xbuf = _dispatch_call(pends32, dest0, dest1, hf_p.reshape(n_p * SUBLANES, LANES), hf_s, n_tiles * MOE_TM)
    ybuf = _expert_call(blk_e, n_used, xbuf, we_gate[0], we_up[0], we_down[0])

    y_p = _final_call(dest0, dest1, 0, x1_p.reshape(n_p, D_MODEL), ybuf, route_pf, mod_p[:, 5:6, :],
                      row(ln2_g[0]), row(ln2_b[0]), tp, ROW_CHUNK)
    y_s = _final_call(dest0, dest1, n_p, x1_s, ybuf, route_s, mod_s[5][None], row(ln2_g[0]), row(ln2_b[0]), ns, ns)

    return (y_p.reshape(bp, tp, D_MODEL), y_s.reshape(ns, 1, D_MODEL),
            conv_p[None], h_p.reshape(1, bp, D_RNN), s_p[None],
            conv_s.transpose(1, 0, 2)[None], h_s[None], s_s[None])
```

```python
import functools

import jax
import jax.numpy as jnp
import numpy as np
from jax import lax
from jax.experimental import pallas as pl
from jax.experimental.pallas import tpu as pltpu

F32 = jnp.float32
BF16 = jnp.bfloat16
I32 = jnp.int32

D_MODEL = 1024
D_RNN = D_MODEL
RG_BLOCKS = 16
RG_BW = D_RNN // RG_BLOCKS
CONV_W = 4
RG_C = 8.0
GLA_HEADS = 4
GLA_DK = D_MODEL // 2
GLA_DV = D_MODEL
GLA_DKH = GLA_DK // GLA_HEADS
GLA_DVH = GLA_DV // GLA_HEADS
GLA_RANK = 16
GLA_TAU = 16.0
GLA_CHUNK = 64
N_GROUPS = 4
EXP_PER_GROUP = 8
N_EXPERTS = N_GROUPS * EXP_PER_GROUP
D_EXPERT = 512
DEPTH = 1
ALPHA = (2.0 * DEPTH) ** 0.25
LN_EPS = 1e-5
RMS_EPS = 1e-6

LANES = 128
SUBLANES = 8
MXU_DIM = 256
VMEM_LIMIT_BYTES = 56 * 1024 * 1024

MIX_TT = 256
STAGE1_COLS = 512
PIPE_DEPTH = 2
MOE_TM = 256
ROW_CHUNK = 128
SAMPLE_SB = 8
RG_GROUP = MXU_DIM // RG_BW
N_RG_TILES = RG_BLOCKS // RG_GROUP
N_MAIN = 2 * D_RNN + 2 * GLA_DK + 2 * GLA_DV + 2 * D_MODEL
ROUTE_G0 = N_EXPERTS

O_RX, O_RY, O_Q, O_K, O_V, O_GG, O_GA, O_GB = 0, 1024, 2048, 2560, 3072, 4096, 5120, 6144


def _ln(x):
    mu = jnp.mean(x, -1, keepdims=True)
    xc = x - mu
    var = jnp.mean(xc * xc, -1, keepdims=True)
    return xc * lax.rsqrt(var + LN_EPS)


def _dot(a, b):
    return jnp.dot(a, b, preferred_element_type=F32)


def _pack_kernel(w_ref, o_ref):
    o_ref[...] = pltpu.bitcast(w_ref[...].astype(BF16), jnp.uint32)


def _pack_rows(w, n=None):
    k = w.shape[0]
    n = w.shape[1] if n is None else n
    cn = min(n, 4 * LANES)
    assert k % (2 * SUBLANES) == 0 and n % cn == 0
    return pl.pallas_call(
        _pack_kernel,
        grid=(n // cn,),
        in_specs=[pl.BlockSpec((k, cn), lambda i: (0, i))],
        out_specs=pl.BlockSpec((k // 2, cn), lambda i: (0, i)),
        out_shape=jax.ShapeDtypeStruct((k // 2, n), jnp.uint32),
        compiler_params=pltpu.CompilerParams(dimension_semantics=("arbitrary",)),
        name="pack",
    )(w)


def _win(win_ref, winb_ref, lo, hi):
    if hi <= O_GA:
        return win_ref[:, lo:hi]
    assert lo >= O_GA
    return winb_ref[:, lo - O_GA:hi - O_GA]


def _w(packed):
    return pltpu.bitcast(packed, BF16)


def _expm1(x):
    u = jnp.exp(x)
    small = (u - 1.0) * x / jnp.log(u)
    return jnp.where(u == 1.0, x, jnp.where(jnp.abs(x) < 0.5, small, u - 1.0))


def _const_spec(shape):
    nd = len(shape)
    return pl.BlockSpec(shape, lambda *_: (0,) * nd, pipeline_mode=pl.Buffered(1))


assert D_MODEL == SUBLANES * LANES


def _tok_rows(t, count=1):
    return pl.ds(pl.multiple_of(t * SUBLANES, SUBLANES), count * SUBLANES)


def _store_token_tiles(ref, rows):
    n = rows.shape[0]
    for s in range(SUBLANES):
        ref[pl.ds(s, n, stride=SUBLANES), :] = rows[:, s * LANES:(s + 1) * LANES]


def _load_token_tiles(ref):
    n = ref.shape[0] // SUBLANES
    return jnp.concatenate([ref[pl.ds(s, n, stride=SUBLANES), :] for s in range(SUBLANES)], axis=1)


def _mod_kernel(c_ref, w_ref, b_ref, o_ref):
    c = c_ref[...]
    s = jax.nn.silu(c)
    o_ref[...] = _dot(s.astype(BF16), w_ref[...].astype(BF16)) + b_ref[...]


def _mod_call(c_all, w_mod, b_mod):
    n = c_all.shape[0]
    tn = 512
    return pl.pallas_call(
        _mod_kernel,
        grid=(6 * D_MODEL // tn,),
        in_specs=[pl.BlockSpec((n, D_MODEL), lambda i: (0, 0)),
                  pl.BlockSpec((D_MODEL, tn), lambda i: (0, i)),
                  pl.BlockSpec((1, tn), lambda i: (0, i))],
        out_specs=pl.BlockSpec((n, tn), lambda i: (0, i)),
        out_shape=jax.ShapeDtypeStruct((n, 6 * D_MODEL), F32),
        compiler_params=pltpu.CompilerParams(dimension_semantics=("arbitrary",)),
        name="mod",
    )(c_all, w_mod, b_mod)


def _no_op():
    pass


def _gate_logits(xc, wrg_ref, rba, rbx):
    xcb = xc.astype(BF16)
    parts = [_dot(xcb[:, c * MXU_DIM:(c + 1) * MXU_DIM], _w(wrg_ref[c])) for c in range(N_RG_TILES)]
    r_pre = jnp.concatenate([p[:, :MXU_DIM] for p in parts], axis=1)
    i_pre = jnp.concatenate([p[:, MXU_DIM:] for p in parts], axis=1)
    return r_pre, i_pre, rba, rbx


def _gate_logits_to(out_ref, xc, wrg_ref):
    xcb = xc.astype(BF16)
    for c in range(N_RG_TILES):
        p = _dot(xcb[:, c * MXU_DIM:(c + 1) * MXU_DIM], _w(wrg_ref[c]))
        out_ref[:, c * MXU_DIM:(c + 1) * MXU_DIM] = p[:, :MXU_DIM]
        out_ref[:, D_RNN + c * MXU_DIM:D_RNN + (c + 1) * MXU_DIM] = p[:, MXU_DIM:]


def _gates(xc, r_pre, i_pre, rba, rbx, lam, between=_no_op):
    r = jax.nn.sigmoid(r_pre + rba)
    between()
    ig = jax.nn.sigmoid(i_pre + rbx)
    log_a = -RG_C * r * jax.nn.softplus(-lam)
    a = jnp.exp(log_a)
    between()
    u = jnp.sqrt(-_expm1(2.0 * log_a)) * (ig * xc)
    between()
    return a, u


def _low_rank(hb, walr_ref, balr_ref):
    return _dot(hb, _w(walr_ref[...])) + balr_ref[...]


def _log_decay_from(alr, wa2_ref, gba_ref):
    return jax.nn.log_sigmoid(_dot(alr.astype(BF16), _w(wa2_ref[...])) + gba_ref[...]) / GLA_TAU


def _log_decay(hb, walr_ref, balr_ref, wa2_ref, gba_ref):
    return _log_decay_from(_low_rank(hb, walr_ref, balr_ref), wa2_ref, gba_ref)


def _post(x, o, gg, ga, gb, y_a, g1, sh2, sc2, gng, pa_ref, pb_ref, wo_ref, l1g, l1b, wr_ref, br,
          between=_no_op):
    rows = x.shape[0]
    heads = []
    for hh in range(GLA_HEADS):
        oh = o[:, hh * GLA_DVH:(hh + 1) * GLA_DVH]
        ms = jnp.mean(oh * oh, -1, keepdims=True)
        heads.append(oh * lax.rsqrt(ms + RMS_EPS) * gng)
    y_b = jnp.concatenate(heads, axis=1) * jax.nn.silu(gg)
    merged = (jax.nn.sigmoid(ga) * _dot(y_a.astype(BF16), _w(pa_ref[...]))
              + jax.nn.sigmoid(gb) * _dot(y_b.astype(BF16), _w(pb_ref[...])))
    mix = _dot(merged.astype(BF16), _w(wo_ref[...]))
    between()
    x1 = _ln(ALPHA * x + g1 * mix) * l1g + l1b
    between()
    hf = _ln(x1) * (1.0 + sc2) + sh2
    logits = _dot(hf.astype(BF16), _w(wr_ref[...])) + br
    lane = lax.broadcasted_iota(I32, (rows, LANES), 1).astype(F32)
    neg = jnp.float32(-jnp.inf)
    big = jnp.float32(LANES)
    g_valid = (lane >= ROUTE_G0) & (lane < ROUTE_G0 + N_GROUPS)
    gl = jnp.where(g_valid, logits, neg)
    gmax = jnp.max(gl, -1, keepdims=True)
    g_lane = jnp.min(jnp.where(gl == gmax, lane, big), -1, keepdims=True)
    g_w = 1.0 / jnp.sum(jnp.exp(gl - gmax), -1, keepdims=True)
    e_lo = (g_lane - ROUTE_G0) * EXP_PER_GROUP
    el = jnp.where((lane >= e_lo) & (lane < e_lo + EXP_PER_GROUP), logits, neg)
    t1 = jnp.max(el, -1, keepdims=True)
    i1 = jnp.min(jnp.where(el == t1, lane, big), -1, keepdims=True)
    el2 = jnp.where(lane == i1, neg, el)
    t2 = jnp.max(el2, -1, keepdims=True)
    i2 = jnp.min(jnp.where(el2 == t2, lane, big), -1, keepdims=True)
    e2 = jnp.exp(t2 - t1)
    den = 1.0 + e2
    w1 = (1.0 / den) * g_w
    w2 = (e2 / den) * g_w
    route = jnp.where(lane == 0, i1.astype(F32),
                      jnp.where(lane == 1, i2.astype(F32),
                                jnp.where(lane == 2, w1, jnp.where(lane == 3, w2, 0.0))))
    return x1, hf, route


def _mixer_kernel(xn_ref, modn_ref, x_ref, mod_ref, win_ref, winb_ref, bin_ref, walr_ref, balr_ref, wa2_ref, gba_ref,
                  gng_ref, cw_ref, cb_ref, wrg_ref, rba_ref, rbx_ref, lam_ref, pa_ref, pb_ref, wo_ref,
                  l1g_ref, l1b_ref, wr_ref, br_ref,
                  x1_ref, hf_ref, route_ref, routet_ref, convn_ref, hlast_ref, sfin_ref,
                  hbn_scr, hbc_scr, pn_scr, pc_scr, alrn_scr, alrc_scr, xcn_scr, xcc_scr, gaten_scr, gatec_scr,
                  rxbuf, hcar, st_ref, o_scr,
                  *, tiles_per_seq):
    i = pl.program_id(0)
    t = jnp.maximum(i - PIPE_DEPTH, 0)
    j = t % tiles_per_seq
    j1 = jnp.maximum(i - 1, 0) % tiles_per_seq
    tt = x_ref.shape[1]

    @pl.when(i == 0)
    def _no_tile_yet():
        hbc_scr[...] = jnp.zeros_like(hbc_scr)
        pc_scr[...] = jnp.zeros_like(pc_scr)
        alrc_scr[...] = jnp.zeros_like(alrc_scr)
        xcc_scr[...] = jnp.zeros_like(xcc_scr)
        gatec_scr[...] = jnp.zeros_like(gatec_scr)

    @pl.when(j1 == 0)
    def _init_conv():
        rxbuf[0:SUBLANES, :] = jnp.zeros((SUBLANES, D_RNN), F32)

    @pl.when(j == 0)
    def _init():
        hcar[...] = jnp.zeros_like(hcar)
        st_ref[...] = jnp.zeros_like(st_ref)

    def stage0():
        modn = modn_ref[0]
        hbn_scr[...] = (_ln(xn_ref[0]) * (1.0 + modn[1:2]) + modn[0:1]).astype(BF16)

    hbn = hbc_scr[...]

    def _proj_block(lo):
        def run():
            hi = lo + STAGE1_COLS
            pn_scr[:, lo:hi] = _dot(hbn, _w(_win(win_ref, winb_ref, lo, hi)))
        return run

    def _alr_block():
        alrn_scr[...] = _dot(hbn, _w(walr_ref[...]))

    pending = [_proj_block(lo) for lo in range(0, N_MAIN, STAGE1_COLS)] + [_alr_block]

    def pump(count=1):
        for _ in range(min(count, len(pending))):
            pending.pop(0)()

    x = x_ref[0]
    mod = mod_ref[0]
    g1, sh2, sc2 = mod[2:3], mod[3:4], mod[4:5]

    def proj(lo, hi):
        return pc_scr[:, lo:hi] + bin_ref[:, lo:hi]

    pump(O_RY // STAGE1_COLS)

    rx = pn_scr[:, O_RX:O_RX + D_RNN] + bin_ref[:, O_RX:O_RX + D_RNN]
    rxbuf[SUBLANES:SUBLANES + tt, :] = rx
    cw = cw_ref[...]
    xcn = cb_ref[...] + rxbuf[SUBLANES - 3:SUBLANES - 3 + tt, :] * cw[0:1]
    xcn = xcn + rxbuf[SUBLANES - 2:SUBLANES - 2 + tt, :] * cw[1:2]
    xcn = xcn + rxbuf[SUBLANES - 1:SUBLANES - 1 + tt, :] * cw[2:3]
    xcn = xcn + rx * cw[3:4]
    rxbuf[0:SUBLANES, :] = rxbuf[tt:tt + SUBLANES, :]
    xcn_scr[...] = xcn
    xc = xcc_scr[...]

    q = proj(O_Q, O_Q + GLA_DK) * (GLA_DKH ** -0.5)
    k = proj(O_K, O_K + GLA_DK)
    v = proj(O_V, O_V + GLA_DV)
    la = _log_decay_from(alrc_scr[...] + balr_ref[...], wa2_ref, gba_ref)
    pump()
    la = la.reshape(tt // SUBLANES, SUBLANES, GLA_DK)
    subk = lax.broadcasted_iota(I32, (1, SUBLANES, GLA_DK), 1)
    s = 1
    while s < SUBLANES:
        la = la + jnp.where(subk >= s, pltpu.roll(la, s, 1), 0.0)
        s *= 2
    groups = []
    for g in range(tt // SUBLANES):
        blk = la[g] if g % (GLA_CHUNK // SUBLANES) == 0 else la[g] + groups[-1][SUBLANES - 1:SUBLANES]
        groups.append(blk)
    bcum = jnp.concatenate(groups, axis=0)
    eb = jnp.exp(bcum)
    q_in = (q * eb).astype(BF16)
    k_in = (k * jnp.exp(-bcum)).astype(BF16)
    pump()
    stage0()
    gelu_ry = jax.nn.gelu(proj(O_RY, O_RY + D_RNN))
    pump()

    a, u = _gates(xc, gatec_scr[:, :D_RNN], gatec_scr[:, D_RNN:], rba_ref[...], rbx_ref[...], lam_ref[...],
                  between=pump)
    s = 1
    a = a.reshape(tt // SUBLANES, SUBLANES, D_RNN)
    u = u.reshape(tt // SUBLANES, SUBLANES, D_RNN)
    sub = lax.broadcasted_iota(I32, (1, SUBLANES, D_RNN), 1)
    while s < SUBLANES:
        keep = sub >= s
        a_s = jnp.where(keep, pltpu.roll(a, s, 1), 1.0)
        u_s = jnp.where(keep, pltpu.roll(u, s, 1), 0.0)
        u = a * u_s + u
        a = a * a_s
        pump()
        s *= 2
    carry = hcar[0:1, :]
    groups = []
    for g in range(tt // SUBLANES):
        hg = a[g] * carry + u[g]
        groups.append(hg)
        carry = hg[SUBLANES - 1:SUBLANES]
    hseq = jnp.concatenate(groups, axis=0)
    hcar[0:1, :] = carry
    y_a = hseq * gelu_ry

    tri = (lax.broadcasted_iota(I32, (GLA_CHUNK, GLA_CHUNK), 0)
           >= lax.broadcasted_iota(I32, (GLA_CHUNK, GLA_CHUNK), 1))
    n_chunks = tt // GLA_CHUNK
    nt_dims = (((1,), (1,)), ((), ()))
    attn, d_st, dec, vh_all = {}, {}, {}, {}
    for c in range(n_chunks):
        r0 = c * GLA_CHUNK
        for hh in range(GLA_HEADS):
            kc = slice(hh * GLA_DKH, (hh + 1) * GLA_DKH)
            b = bcum[r0:r0 + GLA_CHUNK, kc]
            btot = b[GLA_CHUNK - 1:GLA_CHUNK, :]
            k_out = (k[r0:r0 + GLA_CHUNK, kc] * jnp.exp(btot - b)).astype(BF16)
            vh = v[r0:r0 + GLA_CHUNK, hh * GLA_DVH:(hh + 1) * GLA_DVH].astype(BF16)
            scores = lax.dot_general(q_in[r0:r0 + GLA_CHUNK, kc], k_in[r0:r0 + GLA_CHUNK, kc], nt_dims,
                                     preferred_element_type=F32)
            attn[c, hh] = jnp.where(tri, scores, 0.0).astype(BF16)
            d_st[c, hh] = lax.dot_general(vh, k_out, (((0,), (0,)), ((), ())), preferred_element_type=F32)
            dec[c, hh] = jnp.exp(btot)
            vh_all[c, hh] = vh
        pump()
    starts = {}
    for hh in range(GLA_HEADS):
        st = st_ref[hh]
        for c in range(n_chunks):
            starts[c, hh] = st.astype(BF16)
            st = st * dec[c, hh] + d_st[c, hh]
        st_ref[hh] = st
    for c in range(n_chunks):
        r0 = c * GLA_CHUNK
        for hh in range(GLA_HEADS):
            qi = q_in[r0:r0 + GLA_CHUNK, hh * GLA_DKH:(hh + 1) * GLA_DKH]
            inter = lax.dot_general(qi, starts[c, hh], nt_dims, preferred_element_type=F32)
            o_scr[r0:r0 + GLA_CHUNK, hh * GLA_DVH:(hh + 1) * GLA_DVH] = _dot(attn[c, hh], vh_all[c, hh]) + inter

    x1, hf, route = _post(x, o_scr[...], proj(O_GG, O_GG + GLA_DV), proj(O_GA, O_GA + D_MODEL),
                          proj(O_GB, O_GB + D_MODEL), y_a, g1, sh2, sc2, gng_ref[...],
                          pa_ref, pb_ref, wo_ref, l1g_ref[...], l1b_ref[...], wr_ref, br_ref[...], between=pump)
    x1_ref[0] = x1
    _store_token_tiles(hf_ref.at[0], hf)
    route_ref[0] = route
    routet_ref[...] = route.T[0:SUBLANES, :]

    pump(len(pending))
    _gate_logits_to(gaten_scr, xcn, wrg_ref)
    pc_scr[:, O_RY:] = pn_scr[:, O_RY:]
    alrc_scr[...] = alrn_scr[...]
    hbc_scr[...] = hbn_scr[...]
    xcc_scr[...] = xcn_scr[...]
    gatec_scr[...] = gaten_scr[...]

    @pl.when((j1 == tiles_per_seq - 1) & (i >= 1))
    def _final_conv():
        convn_ref[0] = rxbuf[SUBLANES - (CONV_W - 1):SUBLANES, :]

    @pl.when((j == tiles_per_seq - 1) & (i >= PIPE_DEPTH))
    def _final():
        hlast_ref[0] = hcar[0:1, :]
        for hh in range(GLA_HEADS):
            sfin_ref[0, hh] = st_ref[hh].T


def _mixer_call(x, mod3, wts):
    b, t, _ = x.shape
    tt = MIX_TT
    assert t % tt == 0 and tt % GLA_CHUNK == 0
    nt = t // tt
    n_tiles = b * nt
    assert nt > 1
    cur = lambda i: jnp.maximum(i - PIPE_DEPTH, 0)
    nxt = lambda i: jnp.minimum(i, n_tiles - 1)
    tok = lambda i: (cur(i) // nt, cur(i) % nt, 0)
    tok_n = lambda i: (nxt(i) // nt, nxt(i) % nt, 0)
    per_b = lambda i: (cur(i) // nt, 0, 0)
    in_specs = [pl.BlockSpec((1, tt, D_MODEL), tok_n),
                pl.BlockSpec((1, 6, D_MODEL), lambda i: (nxt(i) // nt, 0, 0)),
                pl.BlockSpec((1, tt, D_MODEL), tok),
                pl.BlockSpec((1, 6, D_MODEL), per_b)] + [_const_spec(w.shape) for w in wts]
    out_specs = [pl.BlockSpec((1, tt, D_MODEL), tok),
                 pl.BlockSpec((1, tt * SUBLANES, LANES), tok),
                 pl.BlockSpec((1, tt, LANES), tok),
                 pl.BlockSpec((SUBLANES, tt), lambda i: (0, cur(i))),
                 pl.BlockSpec((1, CONV_W - 1, D_RNN), per_b),
                 pl.BlockSpec((1, 1, D_RNN), per_b),
                 pl.BlockSpec((1, GLA_HEADS, GLA_DKH, GLA_DVH), lambda i: (cur(i) // nt, 0, 0, 0))]
    out_shape = [jax.ShapeDtypeStruct((b, t, D_MODEL), F32),
                 jax.ShapeDtypeStruct((b, t * SUBLANES, LANES), F32),
                 jax.ShapeDtypeStruct((b, t, LANES), F32),
                 jax.ShapeDtypeStruct((SUBLANES, b * t), F32),
                 jax.ShapeDtypeStruct((b, CONV_W - 1, D_RNN), F32),
                 jax.ShapeDtypeStruct((b, 1, D_RNN), F32),
                 jax.ShapeDtypeStruct((b, GLA_HEADS, GLA_DKH, GLA_DVH), F32)]
    scratch = [pltpu.VMEM((tt, D_MODEL), BF16), pltpu.VMEM((tt, D_MODEL), BF16),
               pltpu.VMEM((tt, N_MAIN), F32), pltpu.VMEM((tt, N_MAIN), F32),
               pltpu.VMEM((tt, LANES), F32), pltpu.VMEM((tt, LANES), F32),
               pltpu.VMEM((tt, D_RNN), F32), pltpu.VMEM((tt, D_RNN), F32),
               pltpu.VMEM((tt, 2 * D_RNN), F32), pltpu.VMEM((tt, 2 * D_RNN), F32),
               pltpu.VMEM((tt + SUBLANES, D_RNN), F32),
               pltpu.VMEM((SUBLANES, D_RNN), F32),
               pltpu.VMEM((GLA_HEADS, GLA_DVH, GLA_DKH), F32),
               pltpu.VMEM((tt, GLA_DV), F32)]
    return pl.pallas_call(
        functools.partial(_mixer_kernel, tiles_per_seq=nt),
        grid=(n_tiles + PIPE_DEPTH,),
        in_specs=in_specs,
        out_specs=out_specs,
        out_shape=out_shape,
        scratch_shapes=scratch,
        compiler_params=pltpu.CompilerParams(dimension_semantics=("arbitrary",),
                                             vmem_limit_bytes=VMEM_LIMIT_BYTES),
        name="mixer",
    )(x, mod3, x, mod3, *wts)


def _s_pre_kernel(x_ref, mod_ref, sconv_ref, h0_ref, win_ref, winb_ref, bin_ref, walr_ref, balr_ref, wa2_ref, gba_ref,
                  cw_ref, cb_ref, wrg_ref, rba_ref, rbx_ref, lam_ref,
                  convn_ref, hnew_ref, ya_ref, q_ref, k_ref, v_ref, la_ref, gg_ref, ga_ref, gb_ref):
    x = x_ref[...]
    hb = (_ln(x) * (1.0 + mod_ref[1]) + mod_ref[0]).astype(BF16)

    def proj(lo, hi):
        return _dot(hb, _w(_win(win_ref, winb_ref, lo, hi))) + bin_ref[:, lo:hi]

    rx = proj(O_RX, O_RX + D_RNN)
    cw = cw_ref[...]
    xc = cb_ref[...] + sconv_ref[0] * cw[0:1]
    xc = xc + sconv_ref[1] * cw[1:2]
    xc = xc + sconv_ref[2] * cw[2:3]
    xc = xc + rx * cw[3:4]
    convn_ref[0] = sconv_ref[1]
    convn_ref[1] = sconv_ref[2]
    convn_ref[2] = rx
    a, u = _gates(xc, *_gate_logits(xc, wrg_ref, rba_ref[...], rbx_ref[...]), lam_ref[...])
    hnew = u + a * h0_ref[...]
    hnew_ref[...] = hnew
    ya_ref[...] = hnew * jax.nn.gelu(proj(O_RY, O_RY + D_RNN))
    q_ref[...] = proj(O_Q, O_Q + GLA_DK) * (GLA_DKH ** -0.5)
    k_ref[...] = proj(O_K, O_K + GLA_DK)
    v_ref[...] = proj(O_V, O_V + GLA_DV)
    la_ref[...] = _log_decay(hb, walr_ref, balr_ref, wa2_ref, gba_ref)
    gg_ref[...] = proj(O_GG, O_GG + GLA_DV)
    ga_ref[...] = proj(O_GA, O_GA + D_MODEL)
    gb_ref[...] = proj(O_GB, O_GB + D_MODEL)


def _s_pre_call(x_s, mod_s, sconv, h0, wts):
    n = x_s.shape[0]
    full = lambda shape: pl.BlockSpec(shape, lambda i: (0,) * len(shape))
    ins = [x_s, mod_s, sconv, h0] + list(wts)
    shapes = [(CONV_W - 1, n, D_RNN), (n, D_RNN), (n, D_RNN), (n, GLA_DK), (n, GLA_DK), (n, GLA_DV),
              (n, GLA_DK), (n, GLA_DV), (n, D_MODEL), (n, D_MODEL)]
    return pl.pallas_call(
        _s_pre_kernel,
        grid=(1,),
        in_specs=[full(a.shape) for a in ins],
        out_specs=[full(s) for s in shapes],
        out_shape=[jax.ShapeDtypeStruct(s, F32) for s in shapes],
        compiler_params=pltpu.CompilerParams(dimension_semantics=("arbitrary",),
                                             vmem_limit_bytes=VMEM_LIMIT_BYTES),
        name="s_pre",
    )(*ins)


def _to_column(row, n):
    eye = lax.broadcasted_iota(I32, (n, n), 0) == lax.broadcasted_iota(I32, (n, n), 1)
    return jnp.sum(jnp.where(eye, jnp.broadcast_to(row, (n, n)), 0.0), axis=1, keepdims=True)


def _s_state_kernel(s_ref, q_ref, k_ref, v_ref, la_ref, snew_ref, o_ref):
    sb = s_ref.shape[0]
    qb = q_ref[...].astype(BF16)
    dec = jnp.exp(la_ref[...])
    kk = k_ref[...]
    vv = v_ref[...]
    for si in range(sb):
        for hh in range(GLA_HEADS):
            kc = slice(hh * GLA_DKH, (hh + 1) * GLA_DKH)
            vc = slice(hh * GLA_DVH, (hh + 1) * GLA_DVH)
            d_col = _to_column(dec[si:si + 1, kc], GLA_DKH)
            k_col = _to_column(kk[si:si + 1, kc], GLA_DKH)
            s_new = s_ref[si, hh] * d_col + k_col * vv[si:si + 1, vc]
            snew_ref[si, hh] = s_new
            o_all = _dot(qb[:, kc], s_new.astype(BF16))
            o_ref[si:si + 1, vc] = o_all[si:si + 1, :]


def _s_state_call(state, q, k, v, la):
    n = state.shape[0]
    sb = SAMPLE_SB
    assert n % sb == 0
    blk = lambda w: pl.BlockSpec((sb, w), lambda i: (i, 0))
    st_spec = pl.BlockSpec((sb, GLA_HEADS, GLA_DKH, GLA_DVH), lambda i: (i, 0, 0, 0))
    return pl.pallas_call(
        _s_state_kernel,
        grid=(n // sb,),
        in_specs=[st_spec, blk(GLA_DK), blk(GLA_DK), blk(GLA_DV), blk(GLA_DK)],
        out_specs=[st_spec, blk(GLA_DV)],
        out_shape=[jax.ShapeDtypeStruct(state.shape, F32), jax.ShapeDtypeStruct((n, GLA_DV), F32)],
        compiler_params=pltpu.CompilerParams(dimension_semantics=("arbitrary",),
                                             vmem_limit_bytes=VMEM_LIMIT_BYTES),
        name="s_state",
    )(state, q, k, v, la)


def _s_post_kernel(x_ref, mod_ref, o_ref, gg_ref, ga_ref, gb_ref, ya_ref, gng_ref, pa_ref, pb_ref, wo_ref,
                   l1g_ref, l1b_ref, wr_ref, br_ref, x1_ref, hf_ref, route_ref, routet_ref):
    x1, hf, route = _post(x_ref[...], o_ref[...], gg_ref[...], ga_ref[...], gb_ref[...], ya_ref[...],
                          mod_ref[2], mod_ref[3], mod_ref[4], gng_ref[...], pa_ref, pb_ref, wo_ref,
                          l1g_ref[...], l1b_ref[...], wr_ref, br_ref[...])
    x1_ref[...] = x1
    _store_token_tiles(hf_ref, hf)
    route_ref[...] = route
    routet_ref[...] = route.T[0:SUBLANES, :]


def _s_post_call(x_s, mod_s, o, gg, ga, gb, ya, wts):
    n = x_s.shape[0]
    full = lambda shape: pl.BlockSpec(shape, lambda i: (0,) * len(shape))
    ins = [x_s, mod_s, o, gg, ga, gb, ya] + list(wts)
    shapes = [(n, D_MODEL), (n * SUBLANES, LANES), (n, LANES), (SUBLANES, n)]
    return pl.pallas_call(
        _s_post_kernel,
        grid=(1,),
        in_specs=[full(a.shape) for a in ins],
        out_specs=[full(s) for s in shapes],
        out_shape=[jax.ShapeDtypeStruct(s, F32) for s in shapes],
        compiler_params=pltpu.CompilerParams(dimension_semantics=("arbitrary",),
                                             vmem_limit_bytes=VMEM_LIMIT_BYTES),
        name="s_post",
    )(*ins)


def _rank_kernel(eid_ref, rank_ref, cnt_ref, carry):
    n_chunks = eid_ref.shape[0]
    carry[...] = jnp.zeros_like(carry)
    e_iota = lax.broadcasted_iota(I32, (N_EXPERTS, LANES), 0)
    upper = (lax.broadcasted_iota(I32, (LANES, LANES), 0)
             < lax.broadcasted_iota(I32, (LANES, LANES), 1)).astype(BF16)

    def body(c, _):
        ids = eid_ref[c]
        oh0 = (e_iota == ids[0:1, :])
        oh1 = (e_iota == ids[1:2, :])
        both = (oh0 | oh1).astype(F32)
        before = _dot(both.astype(BF16), upper) + carry[:, 0:1]
        r0 = jnp.sum(jnp.where(oh0, before, 0.0), axis=0, keepdims=True)
        r1 = jnp.sum(jnp.where(oh1, before, 0.0), axis=0, keepdims=True)
        rank_ref[c] = jnp.concatenate([r0, r1], axis=0).astype(I32)
        carry[...] = carry[...] + jnp.sum(both, axis=1, keepdims=True)
        return 0

    lax.fori_loop(0, n_chunks, body, 0)
    cnt_ref[...] = carry[...].astype(I32)


def _rank_call(eid3):
    n_chunks = eid3.shape[0]
    full = lambda shape: pl.BlockSpec(shape, lambda i: (0,) * len(shape))
    return pl.pallas_call(
        _rank_kernel,
        grid=(1,),
        in_specs=[full(eid3.shape)],
        out_specs=[full(eid3.shape), full((N_EXPERTS, LANES))],
        out_shape=[jax.ShapeDtypeStruct(eid3.shape, I32), jax.ShapeDtypeStruct((N_EXPERTS, LANES), I32)],
        scratch_shapes=[pltpu.VMEM((N_EXPERTS, LANES), F32)],
        compiler_params=pltpu.CompilerParams(dimension_semantics=("arbitrary",)),
        name="rank",
    )(eid3)


def _dispatch_kernel(pends_ref, d0_ref, d1_ref, src_p_ref, src_s_ref, buf_ref, sem, zeros_ref, zsem, *,
                     n_p_chunks):
    i = pl.program_id(0)
    n_tiles = buf_ref.shape[0] // (MOE_TM * SUBLANES)

    @pl.when(i == 0)
    def _clear():
        zeros_ref[...] = jnp.zeros_like(zeros_ref)

        def tile_copy(row0):
            return pltpu.make_async_copy(zeros_ref, buf_ref.at[_tok_rows(row0, MOE_TM)], zsem)

        def nonempty(e):
            return pends_ref[e] > jnp.where(e == 0, 0, pends_ref[jnp.maximum(e - 1, 0)])

        def start_tail(e, _):
            @pl.when(nonempty(e))
            def _():
                tile_copy(pends_ref[e] - MOE_TM).start()
            return 0

        def wait_tail(e, _):
            @pl.when(nonempty(e))
            def _():
                tile_copy(0).wait()
            return 0

        def start_unused(j, _):
            tile_copy(j * MOE_TM).start()
            return 0

        def wait_unused(j, _):
            tile_copy(0).wait()
            return 0

        first_unused = pends_ref[N_EXPERTS - 1] // MOE_TM
        lax.fori_loop(0, N_EXPERTS, start_tail, 0)
        lax.fori_loop(first_unused, n_tiles, start_unused, 0)
        lax.fori_loop(0, N_EXPERTS, wait_tail, 0)
        lax.fori_loop(first_unused, n_tiles, wait_unused, 0)

    def scatter_rows(src_ref):
        n = src_ref.shape[0] // SUBLANES

        for t in range(n):
            for k, dest_ref in enumerate((d0_ref, d1_ref)):
                pltpu.make_async_copy(src_ref.at[_tok_rows(t)], buf_ref.at[_tok_rows(dest_ref[t])],
                                      sem).start(priority=k)
        for _ in range(2):
            pltpu.make_async_copy(src_ref, buf_ref.at[_tok_rows(0, n)], sem).wait()

    @pl.when(i < n_p_chunks)
    def _prompt():
        scatter_rows(src_p_ref)

    @pl.when(i >= n_p_chunks)
    def _sample():
        scatter_rows(src_s_ref)


def _dispatch_call(pends, dest0, dest1, src_p, src_s, n_rows):
    chunk_rows = ROW_CHUNK * SUBLANES
    assert src_p.shape[0] % chunk_rows == 0 and src_s.shape[0] % chunk_rows == 0 and n_rows % MOE_TM == 0
    n_p_chunks = src_p.shape[0] // chunk_rows
    n_s_chunks = src_s.shape[0] // chunk_rows
    tile = (chunk_rows, LANES)
    in_specs = [pl.BlockSpec((ROW_CHUNK,), lambda i, pe: (i,), memory_space=pltpu.SMEM),
                pl.BlockSpec((ROW_CHUNK,), lambda i, pe: (i,), memory_space=pltpu.SMEM),
                pl.BlockSpec(tile, lambda i, pe: (jnp.minimum(i, n_p_chunks - 1), 0)),
                pl.BlockSpec(tile, lambda i, pe: (jnp.maximum(i - n_p_chunks, 0), 0))]
    scratch = [pltpu.SemaphoreType.DMA(()), pltpu.VMEM((MOE_TM * SUBLANES, LANES), F32),
               pltpu.SemaphoreType.DMA(())]
    return pl.pallas_call(
        functools.partial(_dispatch_kernel, n_p_chunks=n_p_chunks),
        grid_spec=pltpu.PrefetchScalarGridSpec(
            num_scalar_prefetch=1, grid=(n_p_chunks + n_s_chunks,), in_specs=in_specs,
            out_specs=pl.BlockSpec(memory_space=pl.ANY), scratch_shapes=scratch),
        out_shape=jax.ShapeDtypeStruct((n_rows * SUBLANES, LANES), F32),
        compiler_params=pltpu.CompilerParams(dimension_semantics=("arbitrary",), has_side_effects=True),
        name="dispatch",
    )(pends, dest0, dest1, src_p, src_s)


def _expert_kernel(blk_e_ref, n_used_ref, x_ref, wg0_ref, wu0_ref, wd0_ref, wg1_ref, wu1_ref, wd1_ref, y_ref,
                   wgb, wub, wdb):
    i = pl.program_id(0)
    rows = MOE_TM * SUBLANES
    halves = ((0, wg0_ref, wu0_ref, wd0_ref), (1, wg1_ref, wu1_ref, wd1_ref))

    for h, wg_ref, wu_ref, wd_ref in halves:
        e_now = blk_e_ref[2 * i + h]
        e_before = blk_e_ref[jnp.maximum(2 * i + h - 2, 0)]

        @pl.when((i == 0) | (e_now != e_before))
        def _cast_weights(h=h, wg_ref=wg_ref, wu_ref=wu_ref, wd_ref=wd_ref):
            wgb[h] = wg_ref[0].astype(BF16)
            wub[h] = wu_ref[0].astype(BF16)
            wdb[h] = wd_ref[0].astype(BF16)

    def swiglu(x_view, y_view, h):
        xb = _load_token_tiles(x_view).astype(BF16)
        g = _dot(xb, wgb[h])
        u = _dot(xb, wub[h])
        act = (jax.nn.silu(g) * u).astype(BF16)
        _store_token_tiles(y_view, _dot(act, wdb[h]))

    used = 2 * i < n_used_ref[0]
    same = blk_e_ref[2 * i] == blk_e_ref[2 * i + 1]

    @pl.when(used & same)
    def _one_expert():
        swiglu(x_ref, y_ref, 0)

    @pl.when(used & jnp.logical_not(same))
    def _two_experts():
        for h in range(2):
            swiglu(x_ref.at[pl.ds(h * rows, rows)], y_ref.at[pl.ds(h * rows, rows)], h)

    @pl.when(jnp.logical_not(used))
    def _skip():
        y_ref[...] = jnp.zeros_like(y_ref)


def _expert_call(blk_e, n_used, xbuf, we_gate, we_up, we_down):
    p = xbuf.shape[0] // SUBLANES
    assert p % (2 * MOE_TM) == 0 and blk_e.shape[0] == p // MOE_TM
    tile_spec = pl.BlockSpec((2 * MOE_TM * SUBLANES, LANES), lambda i, be, nu: (i, 0))
    w_in_spec = lambda h: pl.BlockSpec((1, D_MODEL, D_EXPERT), lambda i, be, nu: (be[2 * i + h], 0, 0))
    w_out_spec = lambda h: pl.BlockSpec((1, D_EXPERT, D_MODEL), lambda i, be, nu: (be[2 * i + h], 0, 0))
    grid_spec = pltpu.PrefetchScalarGridSpec(
        num_scalar_prefetch=2,
        grid=(p // (2 * MOE_TM),),
        in_specs=[tile_spec, w_in_spec(0), w_in_spec(0), w_out_spec(0), w_in_spec(1), w_in_spec(1), w_out_spec(1)],
        out_specs=tile_spec,
        scratch_shapes=[pltpu.VMEM((2, D_MODEL, D_EXPERT), BF16), pltpu.VMEM((2, D_MODEL, D_EXPERT), BF16),
                        pltpu.VMEM((2, D_EXPERT, D_MODEL), BF16)],
    )
    return pl.pallas_call(
        _expert_kernel,
        grid_spec=grid_spec,
        out_shape=jax.ShapeDtypeStruct((p * SUBLANES, LANES), F32),
        compiler_params=pltpu.CompilerParams(dimension_semantics=("arbitrary",),
                                             vmem_limit_bytes=VMEM_LIMIT_BYTES),
        name="experts",
    )(blk_e, n_used, xbuf, we_gate, we_up, we_down, we_gate, we_up, we_down)


def _final_kernel(d0_ref, d1_ref, d0n_ref, d1n_ref, x1_ref, ybuf_ref, route_ref, g2_ref, l2g_ref, l2b_ref, out_ref,
                  yg, sems):
    n = x1_ref.shape[0]
    i = pl.program_id(0)
    slot = i % 2

    def start_rows(idx_refs, s, t):
        for k, idx_ref in enumerate(idx_refs):
            pltpu.make_async_copy(ybuf_ref.at[_tok_rows(idx_ref[t])], yg.at[s, k, _tok_rows(t)],
                                  sems.at[s]).start(priority=k)

    def wait_slot(s):
        for k in range(2):
            pltpu.make_async_copy(ybuf_ref.at[_tok_rows(0, n)], yg.at[s, k], sems.at[s]).wait()

    @pl.when(i == 0)
    def _first():
        lax.fori_loop(0, n, lambda t, c: (start_rows((d0_ref, d1_ref), 0, t), c)[1], 0, unroll=4)

    wait_slot(slot)
    for t in range(n):
        start_rows((d0n_ref, d1n_ref), 1 - slot, t)
    route = route_ref[...]
    ff = (_load_token_tiles(yg.at[slot, 0]) * route[:, 2:3]
          + _load_token_tiles(yg.at[slot, 1]) * route[:, 3:4])
    out_ref[...] = _ln(ALPHA * x1_ref[...] + g2_ref[0] * ff) * l2g_ref[...] + l2b_ref[...]

    @pl.when(i + 1 == pl.num_programs(0))
    def _drain():
        wait_slot(1 - slot)


def _final_call(dest0, dest1, first_token, x1, ybuf, route, g2, ln2_g, ln2_b, rows_per_g2, tile):
    n = x1.shape[0]
    assert n % tile == 0 and rows_per_g2 % tile == 0 and first_token % tile == 0
    per = rows_per_g2 // tile
    g2_rows = g2.shape[1]
    steps = n // tile
    off = first_token // tile
    idx = lambda: pl.BlockSpec((tile,), lambda i: (off + i,), memory_space=pltpu.SMEM)
    idx_next = lambda: pl.BlockSpec((tile,), lambda i: (off + jnp.minimum(i + 1, steps - 1),),
                                    memory_space=pltpu.SMEM)
    return pl.pallas_call(
        _final_kernel,
        grid=(steps,),
        in_specs=[idx(), idx(), idx_next(), idx_next(),
                  pl.BlockSpec((tile, D_MODEL), lambda i: (i, 0)),
                  pl.BlockSpec(memory_space=pl.ANY),
                  pl.BlockSpec((tile, LANES), lambda i: (i, 0)),
                  pl.BlockSpec((1, g2_rows, D_MODEL), lambda i: (i // per, 0, 0)),
                  pl.BlockSpec((1, D_MODEL), lambda i: (0, 0)),
                  pl.BlockSpec((1, D_MODEL), lambda i: (0, 0))],
        out_specs=pl.BlockSpec((tile, D_MODEL), lambda i: (i, 0)),
        out_shape=jax.ShapeDtypeStruct((n, D_MODEL), F32),
        scratch_shapes=[pltpu.VMEM((2, 2, tile * SUBLANES, LANES), F32), pltpu.SemaphoreType.DMA((2,))],
        compiler_params=pltpu.CompilerParams(dimension_semantics=("arbitrary",)),
        name="final",
    )(dest0, dest1, dest0, dest1, x1, ybuf, route, g2, ln2_g, ln2_b)


def _block_diag_gate_weights(wa, wx):
    def bd(w):
        w = w.reshape(N_RG_TILES, RG_GROUP, RG_BW, RG_BW)
        eye = jnp.eye(RG_GROUP, dtype=w.dtype)
        return jnp.einsum('tgcd,gh->tgchd', w, eye).reshape(N_RG_TILES, MXU_DIM, MXU_DIM)
    return jnp.concatenate([bd(wa), bd(wx)], axis=2)


def kernel(x_prompt, x_sample, state_conv, state_rglru, state_gla, c_prompt, c_sample, w_mod, b_mod, w_in, b_in,
           conv_w, conv_b, rg_wa, rg_ba, rg_wx, rg_bx, rg_lambda, gla_wa2, gla_ba, gla_norm_g, p_a, p_b, w_o,
           ln1_g, ln1_b, w_grp, b_grp, w_exp, b_exp, we_gate, we_up, we_down, ln2_g, ln2_b):
    assert w_mod.shape[0] == DEPTH == 1
    bp, tp, _ = x_prompt.shape
    ns = x_sample.shape[0]
    n_p = bp * tp
    n_tok = n_p + ns
    row = lambda a: a.reshape(1, -1)

    lo = 2 * D_RNN + 2 * GLA_DK + 2 * GLA_DV
    w_in0 = w_in[0]
    assert lo == O_GA
    win_a = _pack_rows(w_in0, lo)
    win_b = _pack_rows(w_in0[:, lo + GLA_RANK:])
    bin_main = row(jnp.concatenate([b_in[0, :lo], b_in[0, lo + GLA_RANK:]]))
    walr = _pack_rows(jnp.pad(w_in0[:, lo:lo + GLA_RANK], ((0, 0), (0, LANES - GLA_RANK))))
    balr = row(jnp.pad(b_in[0, lo:lo + GLA_RANK], (0, LANES - GLA_RANK)))
    wa2 = _pack_rows(jnp.pad(gla_wa2[0], ((0, LANES - GLA_RANK), (0, 0))))
    wrg = _pack_rows(_block_diag_gate_weights(rg_wa[0], rg_wx[0]).reshape(RG_BLOCKS * RG_BW, 2 * MXU_DIM))
    wrg = wrg.reshape(N_RG_TILES, MXU_DIM // 2, 2 * MXU_DIM)
    w_route = _pack_rows(jnp.pad(jnp.concatenate([w_exp[0], w_grp[0]], axis=1),
                                 ((0, 0), (0, LANES - N_EXPERTS - N_GROUPS))))
    b_route = row(jnp.pad(jnp.concatenate([b_exp[0], b_grp[0]]), (0, LANES - N_EXPERTS - N_GROUPS)))
    pa, pb, wo = _pack_rows(p_a[0]), _pack_rows(p_b[0]), _pack_rows(w_o[0])
    pre_w = [win_a, win_b, bin_main, walr, balr, wa2, row(gla_ba[0])]
    rec_w = [conv_w[0], row(conv_b[0]), wrg, row(rg_ba[0]), row(rg_bx[0]), row(rg_lambda[0])]
    post_w = [row(gla_norm_g[0]), pa, pb, wo, row(ln1_g[0]), row(ln1_b[0]), w_route, b_route]

    mod = _mod_call(jnp.concatenate([c_prompt, c_sample], axis=0), w_mod[0], row(b_mod[0]))
    mod_p = mod[:bp].reshape(bp, 6, D_MODEL)
    mod_s = mod[bp:].reshape(ns, 6, D_MODEL).transpose(1, 0, 2)

    mix_w = pre_w + [post_w[0]] + rec_w + post_w[1:]
    x1_p, hf_p, route_p, routet_p, conv_p, h_p, s_p = _mixer_call(x_prompt, mod_p, mix_w)

    xs = x_sample.reshape(ns, D_MODEL)
    sconv = state_conv[0].transpose(1, 0, 2)
    conv_s, h_s, ya_s, q_s, k_s, v_s, la_s, gg_s, ga_s, gb_s = _s_pre_call(
        xs, mod_s, sconv, state_rglru[0], pre_w + rec_w)
    s_s, o_s = _s_state_call(state_gla[0], q_s, k_s, v_s, la_s)
    x1_s, hf_s, route_s, routet_s = _s_post_call(xs, mod_s, o_s, gg_s, ga_s, gb_s, ya_s, post_w)

    route_pf = route_p.reshape(n_p, LANES)
    eid = jnp.concatenate([routet_p[:2], routet_s[:2]], axis=1).astype(I32)
    assert n_tok % LANES == 0
    n_chunks = n_tok // LANES
    rank3, cnt = _rank_call(eid.reshape(2, n_chunks, LANES).transpose(1, 0, 2))
    rank = rank3.transpose(1, 0, 2).reshape(2, n_tok)
    counts = cnt[:, 0]
    pcounts = (counts + MOE_TM - 1) // MOE_TM * MOE_TM
    pends = jnp.cumsum(pcounts)
    pstarts = pends - pcounts
    experts = jnp.arange(N_EXPERTS, dtype=I32)[:, None, None]
    start_of = jnp.sum(jnp.where(eid[None] == experts, pstarts.astype(I32)[:, None, None], 0), axis=0)
    dest = start_of + rank
    dest0, dest1 = dest[0], dest[1]
    n_tiles = -(-(2 * n_tok + N_EXPERTS * (MOE_TM - 1)) // (2 * MOE_TM)) * 2
    tile_start = jnp.arange(n_tiles, dtype=I32) * MOE_TM
    blk_e = jnp.minimum(jnp.sum(pends[None, :] <= tile_start[:, None], axis=1), N_EXPERTS - 1).astype(I32)
    n_used = (pends[-1] // MOE_TM).astype(I32).reshape(1)

    pends32 = pends.astype(I32)
    xbuf = _dispatch_call(pends32, dest0, dest1, hf_p.reshape(n_p * SUBLANES, LANES), hf_s, n_tiles * MOE_TM)
    ybuf = _expert_call(blk_e, n_used, xbuf, we_gate[0], we_up[0], we_down[0])

    y_p = _final_call(dest0, dest1, 0, x1_p.reshape(n_p, D_MODEL), ybuf, route_pf, mod_p[:, 5:6, :],
                      row(ln2_g[0]), row(ln2_b[0]), tp, ROW_CHUNK)
    y_s = _final_call(dest0, dest1, n_p, x1_s, ybuf, route_s, mod_s[5][None], row(ln2_g[0]), row(ln2_b[0]), ns, ns)

    return (y_p.reshape(bp, tp, D_MODEL), y_s.reshape(ns, 1, D_MODEL),
            conv_p[None], h_p.reshape(1, bp, D_RNN), s_p[None],
            conv_s.transpose(1, 0, 2)[None], h_s[None], s_s[None])
```

```python
import functools

import jax
import jax.numpy as jnp
import numpy as np
from jax import lax
from jax.experimental import pallas as pl
from jax.experimental.pallas import tpu as pltpu

F32 = jnp.float32
BF16 = jnp.bfloat16
I32 = jnp.int32

D_MODEL = 1024
D_RNN = D_MODEL
RG_BLOCKS = 16
RG_BW = D_RNN // RG_BLOCKS
CONV_W = 4
RG_C = 8.0
GLA_HEADS = 4
GLA_DK = D_MODEL // 2
GLA_DV = D_MODEL
GLA_DKH = GLA_DK // GLA_HEADS
GLA_DVH = GLA_DV // GLA_HEADS
GLA_RANK = 16
GLA_TAU = 16.0
GLA_CHUNK = 64
N_GROUPS = 4
EXP_PER_GROUP = 8
N_EXPERTS = N_GROUPS * EXP_PER_GROUP
D_EXPERT = 512
DEPTH = 1
ALPHA = (2.0 * DEPTH) ** 0.25
LN_EPS = 1e-5
RMS_EPS = 1e-6

LANES = 128
SUBLANES = 8
MXU_DIM = 256
VMEM_LIMIT_BYTES = 56 * 1024 * 1024

MIX_TT = 256
STAGE1_COLS = 512
PIPE_DEPTH = 2
MOE_TM = 256
ROW_CHUNK = 128
SAMPLE_SB = 8
RG_GROUP = MXU_DIM // RG_BW
N_RG_TILES = RG_BLOCKS // RG_GROUP
N_MAIN = 2 * D_RNN + 2 * GLA_DK + 2 * GLA_DV + 2 * D_MODEL
ROUTE_G0 = N_EXPERTS

O_RX, O_RY, O_Q, O_K, O_V, O_GG, O_GA, O_GB = 0, 1024, 2048, 2560, 3072, 4096, 5120, 6144


def _ln(x):
    mu = jnp.mean(x, -1, keepdims=True)
    xc = x - mu
    var = jnp.mean(xc * xc, -1, keepdims=True)
    return xc * lax.rsqrt(var + LN_EPS)


def _dot(a, b):
    return jnp.dot(a, b, preferred_element_type=F32)


def _pack_kernel(w_ref, o_ref):
    o_ref[...] = pltpu.bitcast(w_ref[...].astype(BF16), jnp.uint32)


def _pack_rows(w, n=None):
    k = w.shape[0]
    n = w.shape[1] if n is None else n
    cn = min(n, 4 * LANES)
    assert k % (2 * SUBLANES) == 0 and n % cn == 0
    return pl.pallas_call(
        _pack_kernel,
        grid=(n // cn,),
        in_specs=[pl.BlockSpec((k, cn), lambda i: (0, i))],
        out_specs=pl.BlockSpec((k // 2, cn), lambda i: (0, i)),
        out_shape=jax.ShapeDtypeStruct((k // 2, n), jnp.uint32),
        compiler_params=pltpu.CompilerParams(dimension_semantics=("arbitrary",)),
        name="pack",
    )(w)


def _win(win_ref, winb_ref, lo, hi):
    if hi <= O_GA:
        return win_ref[:, lo:hi]
    assert lo >= O_GA
    return winb_ref[:, lo - O_GA:hi - O_GA]


def _w(packed):
    return pltpu.bitcast(packed, BF16)


def _expm1(x):
    u = jnp.exp(x)
    small = (u - 1.0) * x / jnp.log(u)
    return jnp.where(u == 1.0, x, jnp.where(jnp.abs(x) < 0.5, small, u - 1.0))


def _const_spec(shape):
    nd = len(shape)
    return pl.BlockSpec(shape, lambda *_: (0,) * nd, pipeline_mode=pl.Buffered(1))


assert D_MODEL == SUBLANES * LANES


def _tok_rows(t, count=1):
    return pl.ds(pl.multiple_of(t * SUBLANES, SUBLANES), count * SUBLANES)


def _store_token_tiles(ref, rows):
    n = rows.shape[0]
    for s in range(SUBLANES):
        ref[pl.ds(s, n, stride=SUBLANES), :] = rows[:, s * LANES:(s + 1) * LANES]


def _load_token_tiles(ref):
    n = ref.shape[0] // SUBLANES
    return jnp.concatenate([ref[pl.ds(s, n, stride=SUBLANES), :] for s in range(SUBLANES)], axis=1)


def _mod_kernel(c_ref, w_ref, b_ref, o_ref):
    c = c_ref[...]
    s = jax.nn.silu(c)
    o_ref[...] = _dot(s.astype(BF16), w_ref[...].astype(BF16)) + b_ref[...]


def _mod_call(c_all, w_mod, b_mod):
    n = c_all.shape[0]
    tn = 512
    return pl.pallas_call(
        _mod_kernel,
        grid=(6 * D_MODEL // tn,),
        in_specs=[pl.BlockSpec((n, D_MODEL), lambda i: (0, 0)),
                  pl.BlockSpec((D_MODEL, tn), lambda i: (0, i)),
                  pl.BlockSpec((1, tn), lambda i: (0, i))],
        out_specs=pl.BlockSpec((n, tn), lambda i: (0, i)),
        out_shape=jax.ShapeDtypeStruct((n, 6 * D_MODEL), F32),
        compiler_params=pltpu.CompilerParams(dimension_semantics=("arbitrary",)),
        name="mod",
    )(c_all, w_mod, b_mod)


def _no_op():
    pass


def _gate_logits(xc, wrg_ref, rba, rbx):
    xcb = xc.astype(BF16)
    parts = [_dot(xcb[:, c * MXU_DIM:(c + 1) * MXU_DIM], _w(wrg_ref[c])) for c in range(N_RG_TILES)]
    r_pre = jnp.concatenate([p[:, :MXU_DIM] for p in parts], axis=1)
    i_pre = jnp.concatenate([p[:, MXU_DIM:] for p in parts], axis=1)
    return r_pre, i_pre, rba, rbx


def _gate_logits_to(out_ref, xc, wrg_ref):
    xcb = xc.astype(BF16)
    for c in range(N_RG_TILES):
        p = _dot(xcb[:, c * MXU_DIM:(c + 1) * MXU_DIM], _w(wrg_ref[c]))
        out_ref[:, c * MXU_DIM:(c + 1) * MXU_DIM] = p[:, :MXU_DIM]
        out_ref[:, D_RNN + c * MXU_DIM:D_RNN + (c + 1) * MXU_DIM] = p[:, MXU_DIM:]


def _gates(xc, r_pre, i_pre, rba, rbx, lam, between=_no_op):
    r = jax.nn.sigmoid(r_pre + rba)
    between()
    ig = jax.nn.sigmoid(i_pre + rbx)
    log_a = -RG_C * r * jax.nn.softplus(-lam)
    a = jnp.exp(log_a)
    between()
    u = jnp.sqrt(-_expm1(2.0 * log_a)) * (ig * xc)
    between()
    return a, u


def _low_rank(hb, walr_ref, balr_ref):
    return _dot(hb, _w(walr_ref[...])) + balr_ref[...]


def _log_decay_from(alr, wa2_ref, gba_ref):
    return jax.nn.log_sigmoid(_dot(alr.astype(BF16), _w(wa2_ref[...])) + gba_ref[...]) / GLA_TAU


def _log_decay(hb, walr_ref, balr_ref, wa2_ref, gba_ref):
    return _log_decay_from(_low_rank(hb, walr_ref, balr_ref), wa2_ref, gba_ref)


def _post(x, o, gg, ga, gb, y_a, g1, sh2, sc2, gng, pa_ref, pb_ref, wo_ref, l1g, l1b, wr_ref, br,
          between=_no_op):
    rows = x.shape[0]
    heads = []
    for hh in range(GLA_HEADS):
        oh = o[:, hh * GLA_DVH:(hh + 1) * GLA_DVH]
        ms = jnp.mean(oh * oh, -1, keepdims=True)
        heads.append(oh * lax.rsqrt(ms + RMS_EPS) * gng)
    y_b = jnp.concatenate(heads, axis=1) * jax.nn.silu(gg)
    merged = (jax.nn.sigmoid(ga) * _dot(y_a.astype(BF16), _w(pa_ref[...]))
              + jax.nn.sigmoid(gb) * _dot(y_b.astype(BF16), _w(pb_ref[...])))
    mix = _dot(merged.astype(BF16), _w(wo_ref[...]))
    between()
    x1 = _ln(ALPHA * x + g1 * mix) * l1g + l1b
    between()
    hf = _ln(x1) * (1.0 + sc2) + sh2
    logits = _dot(hf.astype(BF16), _w(wr_ref[...])) + br
    lane = lax.broadcasted_iota(I32, (rows, LANES), 1).astype(F32)
    neg = jnp.float32(-jnp.inf)
    big = jnp.float32(LANES)
    g_valid = (lane >= ROUTE_G0) & (lane < ROUTE_G0 + N_GROUPS)
    gl = jnp.where(g_valid, logits, neg)
    gmax = jnp.max(gl, -1, keepdims=True)
    g_lane = jnp.min(jnp.where(gl == gmax, lane, big), -1, keepdims=True)
    g_w = 1.0 / jnp.sum(jnp.exp(gl - gmax), -1, keepdims=True)
    e_lo = (g_lane - ROUTE_G0) * EXP_PER_GROUP
    el = jnp.where((lane >= e_lo) & (lane < e_lo + EXP_PER_GROUP), logits, neg)
    t1 = jnp.max(el, -1, keepdims=True)
    i1 = jnp.min(jnp.where(el == t1, lane, big), -1, keepdims=True)
    el2 = jnp.where(lane == i1, neg, el)
    t2 = jnp.max(el2, -1, keepdims=True)
    i2 = jnp.min(jnp.where(el2 == t2, lane, big), -1, keepdims=True)
    e2 = jnp.exp(t2 - t1)
    den = 1.0 + e2
    w1 = (1.0 / den) * g_w
    w2 = (e2 / den) * g_w
    route = jnp.where(lane == 0, i1.astype(F32),
                      jnp.where(lane == 1, i2.astype(F32),
                                jnp.where(lane == 2, w1, jnp.where(lane == 3, w2, 0.0))))
    return x1, hf, route


def _mixer_kernel(xn_ref, modn_ref, x_ref, mod_ref, win_ref, winb_ref, bin_ref, walr_ref, balr_ref, wa2_ref, gba_ref,
                  gng_ref, cw_ref, cb_ref, wrg_ref, rba_ref, rbx_ref, lam_ref, pa_ref, pb_ref, wo_ref,
                  l1g_ref, l1b_ref, wr_ref, br_ref,
                  x1_ref, hf_ref, route_ref, routet_ref, convn_ref, hlast_ref, sfin_ref,
                  hbn_scr, hbc_scr, pn_scr, pc_scr, alrn_scr, alrc_scr, xcn_scr, xcc_scr, gaten_scr, gatec_scr,
                  rxbuf, hcar, st_ref, o_scr,
                  *, tiles_per_seq):
    i = pl.program_id(0)
    t = jnp.maximum(i - PIPE_DEPTH, 0)
    j = t % tiles_per_seq
    j1 = jnp.maximum(i - 1, 0) % tiles_per_seq
    tt = x_ref.shape[1]

    @pl.when(i == 0)
    def _no_tile_yet():
        hbc_scr[...] = jnp.zeros_like(hbc_scr)
        pc_scr[...] = jnp.zeros_like(pc_scr)
        alrc_scr[...] = jnp.zeros_like(alrc_scr)
        xcc_scr[...] = jnp.zeros_like(xcc_scr)
        gatec_scr[...] = jnp.zeros_like(gatec_scr)

    @pl.when(j1 == 0)
    def _init_conv():
        rxbuf[0:SUBLANES, :] = jnp.zeros((SUBLANES, D_RNN), F32)

    @pl.when(j == 0)
    def _init():
        hcar[...] = jnp.zeros_like(hcar)
        st_ref[...] = jnp.zeros_like(st_ref)

    def stage0():
        modn = modn_ref[0]
        hbn_scr[...] = (_ln(xn_ref[0]) * (1.0 + modn[1:2]) + modn[0:1]).astype(BF16)

    hbn = hbc_scr[...]

    def _proj_block(lo):
        def run():
            hi = lo + STAGE1_COLS
            pn_scr[:, lo:hi] = _dot(hbn, _w(_win(win_ref, winb_ref, lo, hi)))
        return run

    def _alr_block():
        alrn_scr[...] = _dot(hbn, _w(walr_ref[...]))

    pending = [_proj_block(lo) for lo in range(0, N_MAIN, STAGE1_COLS)] + [_alr_block]

    def pump(count=1):
        for _ in range(min(count, len(pending))):
            pending.pop(0)()

    x = x_ref[0]
    mod = mod_ref[0]
    g1, sh2, sc2 = mod[2:3], mod[3:4], mod[4:5]

    def proj(lo, hi):
        return pc_scr[:, lo:hi] + bin_ref[:, lo:hi]

    pump(O_RY // STAGE1_COLS)

    rx = pn_scr[:, O_RX:O_RX + D_RNN] + bin_ref[:, O_RX:O_RX + D_RNN]
    rxbuf[SUBLANES:SUBLANES + tt, :] = rx
    cw = cw_ref[...]
    xcn = cb_ref[...] + rxbuf[SUBLANES - 3:SUBLANES - 3 + tt, :] * cw[0:1]
    xcn = xcn + rxbuf[SUBLANES - 2:SUBLANES - 2 + tt, :] * cw[1:2]
    xcn = xcn + rxbuf[SUBLANES - 1:SUBLANES - 1 + tt, :] * cw[2:3]
    xcn = xcn + rx * cw[3:4]
    rxbuf[0:SUBLANES, :] = rxbuf[tt:tt + SUBLANES, :]
    xcn_scr[...] = xcn
    xc = xcc_scr[...]

    q = proj(O_Q, O_Q + GLA_DK) * (GLA_DKH ** -0.5)
    k = proj(O_K, O_K + GLA_DK)
    v = proj(O_V, O_V + GLA_DV)
    la = _log_decay_from(alrc_scr[...] + balr_ref[...], wa2_ref, gba_ref)
    pump()
    la = la.reshape(tt // SUBLANES, SUBLANES, GLA_DK)
    subk = lax.broadcasted_iota(I32, (1, SUBLANES, GLA_DK), 1)
    s = 1
    while s < SUBLANES:
        la = la + jnp.where(subk >= s, pltpu.roll(la, s, 1), 0.0)
        s *= 2
    groups = []
    for g in range(tt // SUBLANES):
        blk = la[g] if g % (GLA_CHUNK // SUBLANES) == 0 else la[g] + groups[-1][SUBLANES - 1:SUBLANES]
        groups.append(blk)
    bcum = jnp.concatenate(groups, axis=0)
    eb = jnp.exp(bcum)
    q_in = (q * eb).astype(BF16)
    k_in = (k * jnp.exp(-bcum)).astype(BF16)
    pump()
    stage0()
    gelu_ry = jax.nn.gelu(proj(O_RY, O_RY + D_RNN))
    pump()

    a, u = _gates(xc, gatec_scr[:, :D_RNN], gatec_scr[:, D_RNN:], rba_ref[...], rbx_ref[...], lam_ref[...],
                  between=pump)
    s = 1
    a = a.reshape(tt // SUBLANES, SUBLANES, D_RNN)
    u = u.reshape(tt // SUBLANES, SUBLANES, D_RNN)
    sub = lax.broadcasted_iota(I32, (1, SUBLANES, D_RNN), 1)
    while s < SUBLANES:
        keep = sub >= s
        a_s = jnp.where(keep, pltpu.roll(a, s, 1), 1.0)
        u_s = jnp.where(keep, pltpu.roll(u, s, 1), 0.0)
        u = a * u_s + u
        a = a * a_s
        pump()
        s *= 2
    carry = hcar[0:1, :]
    groups = []
    for g in range(tt // SUBLANES):
        hg = a[g] * carry + u[g]
        groups.append(hg)
        carry = hg[SUBLANES - 1:SUBLANES]
    hseq = jnp.concatenate(groups, axis=0)
    hcar[0:1, :] = carry
    y_a = hseq * gelu_ry

    tri = (lax.broadcasted_iota(I32, (GLA_CHUNK, GLA_CHUNK), 0)
           >= lax.broadcasted_iota(I32, (GLA_CHUNK, GLA_CHUNK), 1))
    n_chunks = tt // GLA_CHUNK
    nt_dims = (((1,), (1,)), ((), ()))
    attn, d_st, dec, vh_all = {}, {}, {}, {}
    for c in range(n_chunks):
        r0 = c * GLA_CHUNK
        for hh in range(GLA_HEADS):
            kc = slice(hh * GLA_DKH, (hh + 1) * GLA_DKH)
            b = bcum[r0:r0 + GLA_CHUNK, kc]
            btot = b[GLA_CHUNK - 1:GLA_CHUNK, :]
            k_out = (k[r0:r0 + GLA_CHUNK, kc] * jnp.exp(btot - b)).astype(BF16)
            vh = v[r0:r0 + GLA_CHUNK, hh * GLA_DVH:(hh + 1) * GLA_DVH].astype(BF16)
            scores = lax.dot_general(q_in[r0:r0 + GLA_CHUNK, kc], k_in[r0:r0 + GLA_CHUNK, kc], nt_dims,
                                     preferred_element_type=F32)
            attn[c, hh] = jnp.where(tri, scores, 0.0).astype(BF16)
            d_st[c, hh] = lax.dot_general(vh, k_out, (((0,), (0,)), ((), ())), preferred_element_type=F32)
            dec[c, hh] = jnp.exp(btot)
            vh_all[c, hh] = vh
        pump()
    starts = {}
    for hh in range(GLA_HEADS):
        st = st_ref[hh]
        for c in range(n_chunks):
            starts[c, hh] = st.astype(BF16)
            st = st * dec[c, hh] + d_st[c, hh]
        st_ref[hh] = st
    for c in range(n_chunks):
        r0 = c * GLA_CHUNK
        for hh in range(GLA_HEADS):
            qi = q_in[r0:r0 + GLA_CHUNK, hh * GLA_DKH:(hh + 1) * GLA_DKH]
            inter = lax.dot_general(qi, starts[c, hh], nt_dims, preferred_element_type=F32)
            o_scr[r0:r0 + GLA_CHUNK, hh * GLA_DVH:(hh + 1) * GLA_DVH] = _dot(attn[c, hh], vh_all[c, hh]) + inter

    x1, hf, route = _post(x, o_scr[...], proj(O_GG, O_GG + GLA_DV), proj(O_GA, O_GA + D_MODEL),
                          proj(O_GB, O_GB + D_MODEL), y_a, g1, sh2, sc2, gng_ref[...],
                          pa_ref, pb_ref, wo_ref, l1g_ref[...], l1b_ref[...], wr_ref, br_ref[...], between=pump)
    x1_ref[0] = x1
    _store_token_tiles(hf_ref, hf)
    route_ref[0] = route
    routet_ref[...] = route.T[0:SUBLANES, :]

    pump(len(pending))
    _gate_logits_to(gaten_scr, xcn, wrg_ref)
    pc_scr[:, O_RY:] = pn_scr[:, O_RY:]
    alrc_scr[...] = alrn_scr[...]
    hbc_scr[...] = hbn_scr[...]
    xcc_scr[...] = xcn_scr[...]
    gatec_scr[...] = gaten_scr[...]

    @pl.when((j1 == tiles_per_seq - 1) & (i >= 1))
    def _final_conv():
        convn_ref[0] = rxbuf[SUBLANES - (CONV_W - 1):SUBLANES, :]

    @pl.when((j == tiles_per_seq - 1) & (i >= PIPE_DEPTH))
    def _final():
        hlast_ref[0] = hcar[0:1, :]
        for hh in range(GLA_HEADS):
            sfin_ref[0, hh] = st_ref[hh].T


def _mixer_call(x, mod3, wts):
    b, t, _ = x.shape
    tt = MIX_TT
    assert t % tt == 0 and tt % GLA_CHUNK == 0
    nt = t // tt
    n_tiles = b * nt
    assert nt > 1
    cur = lambda i: jnp.maximum(i - PIPE_DEPTH, 0)
    nxt = lambda i: jnp.minimum(i, n_tiles - 1)
    tok = lambda i: (cur(i) // nt, cur(i) % nt, 0)
    tok_n = lambda i: (nxt(i) // nt, nxt(i) % nt, 0)
    per_b = lambda i: (cur(i) // nt, 0, 0)
    in_specs = [pl.BlockSpec((1, tt, D_MODEL), tok_n),
                pl.BlockSpec((1, 6, D_MODEL), lambda i: (nxt(i) // nt, 0, 0)),
                pl.BlockSpec((1, tt, D_MODEL), tok),
                pl.BlockSpec((1, 6, D_MODEL), per_b)] + [_const_spec(w.shape) for w in wts]
    out_specs = [pl.BlockSpec((1, tt, D_MODEL), tok),
                 pl.BlockSpec((tt * SUBLANES, LANES), lambda i: (jnp.where(i == 0, n_tiles, cur(i)), 0)),
                 pl.BlockSpec((1, tt, LANES), tok),
                 pl.BlockSpec((SUBLANES, tt), lambda i: (0, cur(i))),
                 pl.BlockSpec((1, CONV_W - 1, D_RNN), per_b),
                 pl.BlockSpec((1, 1, D_RNN), per_b),
                 pl.BlockSpec((1, GLA_HEADS, GLA_DKH, GLA_DVH), lambda i: (cur(i) // nt, 0, 0, 0))]
    out_shape = [jax.ShapeDtypeStruct((b, t, D_MODEL), F32),
                 jax.ShapeDtypeStruct(((b * t + tt) * SUBLANES, LANES), F32),
                 jax.ShapeDtypeStruct((b, t, LANES), F32),
                 jax.ShapeDtypeStruct((SUBLANES, b * t), F32),
                 jax.ShapeDtypeStruct((b, CONV_W - 1, D_RNN), F32),
                 jax.ShapeDtypeStruct((b, 1, D_RNN), F32),
                 jax.ShapeDtypeStruct((b, GLA_HEADS, GLA_DKH, GLA_DVH), F32)]
    scratch = [pltpu.VMEM((tt, D_MODEL), BF16), pltpu.VMEM((tt, D_MODEL), BF16),
               pltpu.VMEM((tt, N_MAIN), F32), pltpu.VMEM((tt, N_MAIN), F32),
               pltpu.VMEM((tt, LANES), F32), pltpu.VMEM((tt, LANES), F32),
               pltpu.VMEM((tt, D_RNN), F32), pltpu.VMEM((tt, D_RNN), F32),
               pltpu.VMEM((tt, 2 * D_RNN), F32), pltpu.VMEM((tt, 2 * D_RNN), F32),
               pltpu.VMEM((tt + SUBLANES, D_RNN), F32),
               pltpu.VMEM((SUBLANES, D_RNN), F32),
               pltpu.VMEM((GLA_HEADS, GLA_DVH, GLA_DKH), F32),
               pltpu.VMEM((tt, GLA_DV), F32)]
    return pl.pallas_call(
        functools.partial(_mixer_kernel, tiles_per_seq=nt),
        grid=(n_tiles + PIPE_DEPTH,),
        in_specs=in_specs,
        out_specs=out_specs,
        out_shape=out_shape,
        scratch_shapes=scratch,
        compiler_params=pltpu.CompilerParams(dimension_semantics=("arbitrary",),
                                             vmem_limit_bytes=VMEM_LIMIT_BYTES),
        name="mixer",
    )(x, mod3, x, mod3, *wts)


def _s_pre_kernel(x_ref, mod_ref, sconv_ref, h0_ref, win_ref, winb_ref, bin_ref, walr_ref, balr_ref, wa2_ref, gba_ref,
                  cw_ref, cb_ref, wrg_ref, rba_ref, rbx_ref, lam_ref,
                  convn_ref, hnew_ref, ya_ref, q_ref, k_ref, v_ref, la_ref, gg_ref, ga_ref, gb_ref):
    x = x_ref[...]
    hb = (_ln(x) * (1.0 + mod_ref[1]) + mod_ref[0]).astype(BF16)

    def proj(lo, hi):
        return _dot(hb, _w(_win(win_ref, winb_ref, lo, hi))) + bin_ref[:, lo:hi]

    rx = proj(O_RX, O_RX + D_RNN)
    cw = cw_ref[...]
    xc = cb_ref[...] + sconv_ref[0] * cw[0:1]
    xc = xc + sconv_ref[1] * cw[1:2]
    xc = xc + sconv_ref[2] * cw[2:3]
    xc = xc + rx * cw[3:4]
    convn_ref[0] = sconv_ref[1]
    convn_ref[1] = sconv_ref[2]
    convn_ref[2] = rx
    a, u = _gates(xc, *_gate_logits(xc, wrg_ref, rba_ref[...], rbx_ref[...]), lam_ref[...])
    hnew = u + a * h0_ref[...]
    hnew_ref[...] = hnew
    ya_ref[...] = hnew * jax.nn.gelu(proj(O_RY, O_RY + D_RNN))
    q_ref[...] = proj(O_Q, O_Q + GLA_DK) * (GLA_DKH ** -0.5)
    k_ref[...] = proj(O_K, O_K + GLA_DK)
    v_ref[...] = proj(O_V, O_V + GLA_DV)
    la_ref[...] = _log_decay(hb, walr_ref, balr_ref, wa2_ref, gba_ref)
    gg_ref[...] = proj(O_GG, O_GG + GLA_DV)
    ga_ref[...] = proj(O_GA, O_GA + D_MODEL)
    gb_ref[...] = proj(O_GB, O_GB + D_MODEL)


def _s_pre_call(x_s, mod_s, sconv, h0, wts):
    n = x_s.shape[0]
    full = lambda shape: pl.BlockSpec(shape, lambda i: (0,) * len(shape))
    ins = [x_s, mod_s, sconv, h0] + list(wts)
    shapes = [(CONV_W - 1, n, D_RNN), (n, D_RNN), (n, D_RNN), (n, GLA_DK), (n, GLA_DK), (n, GLA_DV),
              (n, GLA_DK), (n, GLA_DV), (n, D_MODEL), (n, D_MODEL)]
    return pl.pallas_call(
        _s_pre_kernel,
        grid=(1,),
        in_specs=[full(a.shape) for a in ins],
        out_specs=[full(s) for s in shapes],
        out_shape=[jax.ShapeDtypeStruct(s, F32) for s in shapes],
        compiler_params=pltpu.CompilerParams(dimension_semantics=("arbitrary",),
                                             vmem_limit_bytes=VMEM_LIMIT_BYTES),
        name="s_pre",
    )(*ins)


def _to_column(row, n):
    eye = lax.broadcasted_iota(I32, (n, n), 0) == lax.broadcasted_iota(I32, (n, n), 1)
    return jnp.sum(jnp.where(eye, jnp.broadcast_to(row, (n, n)), 0.0), axis=1, keepdims=True)


def _s_state_kernel(s_ref, q_ref, k_ref, v_ref, la_ref, snew_ref, o_ref):
    sb = s_ref.shape[0]
    qb = q_ref[...].astype(BF16)
    dec = jnp.exp(la_ref[...])
    kk = k_ref[...]
    vv = v_ref[...]
    for si in range(sb):
        for hh in range(GLA_HEADS):
            kc = slice(hh * GLA_DKH, (hh + 1) * GLA_DKH)
            vc = slice(hh * GLA_DVH, (hh + 1) * GLA_DVH)
            d_col = _to_column(dec[si:si + 1, kc], GLA_DKH)
            k_col = _to_column(kk[si:si + 1, kc], GLA_DKH)
            s_new = s_ref[si, hh] * d_col + k_col * vv[si:si + 1, vc]
            snew_ref[si, hh] = s_new
            o_all = _dot(qb[:, kc], s_new.astype(BF16))
            o_ref[si:si + 1, vc] = o_all[si:si + 1, :]


def _s_state_call(state, q, k, v, la):
    n = state.shape[0]
    sb = SAMPLE_SB
    assert n % sb == 0
    blk = lambda w: pl.BlockSpec((sb, w), lambda i: (i, 0))
    st_spec = pl.BlockSpec((sb, GLA_HEADS, GLA_DKH, GLA_DVH), lambda i: (i, 0, 0, 0))
    return pl.pallas_call(
        _s_state_kernel,
        grid=(n // sb,),
        in_specs=[st_spec, blk(GLA_DK), blk(GLA_DK), blk(GLA_DV), blk(GLA_DK)],
        out_specs=[st_spec, blk(GLA_DV)],
        out_shape=[jax.ShapeDtypeStruct(state.shape, F32), jax.ShapeDtypeStruct((n, GLA_DV), F32)],
        compiler_params=pltpu.CompilerParams(dimension_semantics=("arbitrary",),
                                             vmem_limit_bytes=VMEM_LIMIT_BYTES),
        name="s_state",
    )(state, q, k, v, la)


def _s_post_kernel(x_ref, mod_ref, o_ref, gg_ref, ga_ref, gb_ref, ya_ref, gng_ref, pa_ref, pb_ref, wo_ref,
                   l1g_ref, l1b_ref, wr_ref, br_ref, hf_all_ref, x1_ref, hf_ref, route_ref, routet_ref):
    del hf_all_ref
    x1, hf, route = _post(x_ref[...], o_ref[...], gg_ref[...], ga_ref[...], gb_ref[...], ya_ref[...],
                          mod_ref[2], mod_ref[3], mod_ref[4], gng_ref[...], pa_ref, pb_ref, wo_ref,
                          l1g_ref[...], l1b_ref[...], wr_ref, br_ref[...])
    x1_ref[...] = x1
    _store_token_tiles(hf_ref, hf)
    route_ref[...] = route
    routet_ref[...] = route.T[0:SUBLANES, :]


def _s_post_call(x_s, mod_s, o, gg, ga, gb, ya, wts, hf_all, first_token):
    n = x_s.shape[0]
    assert first_token % n == 0
    full = lambda shape: pl.BlockSpec(shape, lambda i: (0,) * len(shape))
    ins = [x_s, mod_s, o, gg, ga, gb, ya] + list(wts)
    shapes = [(n, D_MODEL), hf_all.shape, (n, LANES), (SUBLANES, n)]
    out_specs = [full(shapes[0]), pl.BlockSpec((n * SUBLANES, LANES), lambda i: (first_token // n, 0)),
                 full(shapes[2]), full(shapes[3])]
    return pl.pallas_call(
        _s_post_kernel,
        grid=(1,),
        in_specs=[full(a.shape) for a in ins] + [pl.BlockSpec(memory_space=pl.ANY)],
        out_specs=out_specs,
        out_shape=[jax.ShapeDtypeStruct(s, F32) for s in shapes],
        input_output_aliases={len(ins): 1},
        compiler_params=pltpu.CompilerParams(dimension_semantics=("arbitrary",),
                                             vmem_limit_bytes=VMEM_LIMIT_BYTES),
        name="s_post",
    )(*ins, hf_all)


def _rank_kernel(eid_ref, rank_ref, cnt_ref, carry):
    n_chunks = eid_ref.shape[0]
    carry[...] = jnp.zeros_like(carry)
    e_iota = lax.broadcasted_iota(I32, (N_EXPERTS, LANES), 0)
    upper = (lax.broadcasted_iota(I32, (LANES, LANES), 0)
             < lax.broadcasted_iota(I32, (LANES, LANES), 1)).astype(BF16)

    def body(c, _):
        ids = eid_ref[c]
        oh0 = (e_iota == ids[0:1, :])
        oh1 = (e_iota == ids[1:2, :])
        both = (oh0 | oh1).astype(F32)
        before = _dot(both.astype(BF16), upper) + carry[:, 0:1]
        r0 = jnp.sum(jnp.where(oh0, before, 0.0), axis=0, keepdims=True)
        r1 = jnp.sum(jnp.where(oh1, before, 0.0), axis=0, keepdims=True)
        rank_ref[c] = jnp.concatenate([r0, r1], axis=0).astype(I32)
        carry[...] = carry[...] + jnp.sum(both, axis=1, keepdims=True)
        return 0

    lax.fori_loop(0, n_chunks, body, 0)
    cnt_ref[...] = carry[...].astype(I32)


def _rank_call(eid3):
    n_chunks = eid3.shape[0]
    full = lambda shape: pl.BlockSpec(shape, lambda i: (0,) * len(shape))
    return pl.pallas_call(
        _rank_kernel,
        grid=(1,),
        in_specs=[full(eid3.shape)],
        out_specs=[full(eid3.shape), full((N_EXPERTS, LANES))],
        out_shape=[jax.ShapeDtypeStruct(eid3.shape, I32), jax.ShapeDtypeStruct((N_EXPERTS, LANES), I32)],
        scratch_shapes=[pltpu.VMEM((N_EXPERTS, LANES), F32)],
        compiler_params=pltpu.CompilerParams(dimension_semantics=("arbitrary",)),
        name="rank",
    )(eid3)


def _row_token_kernel(d0_ref, d1_ref, tok_ref):
    def clear(r, _):
        tok_ref[r] = 0
        return 0

    lax.fori_loop(0, tok_ref.shape[0], clear, 0, unroll=8)

    def fill(t, _):
        tok_ref[d0_ref[t]] = t
        tok_ref[d1_ref[t]] = t
        return 0

    lax.fori_loop(0, d0_ref.shape[0], fill, 0, unroll=8)


def _row_token_call(dest0, dest1, n_rows):
    smem = pl.BlockSpec(memory_space=pltpu.SMEM)
    return pl.pallas_call(
        _row_token_kernel,
        in_specs=[smem, smem],
        out_specs=smem,
        out_shape=jax.ShapeDtypeStruct((n_rows,), I32),
        name="row_token",
    )(dest0, dest1)


def _expert_kernel(blk_e_ref, n_used_ref, tok_ref, tok_next_ref, hf_ref, wg0_ref, wu0_ref, wd0_ref,
                   wg1_ref, wu1_ref, wd1_ref, y_ref, wgb, wub, wdb, xg, sems):
    i = pl.program_id(0)
    rows = MOE_TM * SUBLANES
    n_rows = 2 * MOE_TM
    slot = i % 2
    halves = ((0, wg0_ref, wu0_ref, wd0_ref), (1, wg1_ref, wu1_ref, wd1_ref))

    def gather(idx_ref, s):
        def start_pair(p, _):
            for k in range(2):
                r = 2 * p + k
                pltpu.make_async_copy(hf_ref.at[_tok_rows(idx_ref[r])], xg.at[s, _tok_rows(r)],
                                      sems.at[s]).start(priority=k)
            return 0
        lax.fori_loop(0, n_rows // 2, start_pair, 0, unroll=4)

    @pl.when(i == 0)
    def _first():
        gather(tok_ref, 0)

    @pl.when(i + 1 < pl.num_programs(0))
    def _prefetch():
        gather(tok_next_ref, 1 - slot)

    pltpu.make_async_copy(hf_ref.at[_tok_rows(0, n_rows)], xg.at[slot], sems.at[slot]).wait()
    x_ref = xg.at[slot]

    for h, wg_ref, wu_ref, wd_ref in halves:
        e_now = blk_e_ref[2 * i + h]
        e_before = blk_e_ref[jnp.maximum(2 * i + h - 2, 0)]

        @pl.when((i == 0) | (e_now != e_before))
        def _cast_weights(h=h, wg_ref=wg_ref, wu_ref=wu_ref, wd_ref=wd_ref):
            wgb[h] = wg_ref[0].astype(BF16)
            wub[h] = wu_ref[0].astype(BF16)
            wdb[h] = wd_ref[0].astype(BF16)

    def swiglu(x_view, y_view, h):
        xb = _load_token_tiles(x_view).astype(BF16)
        g = _dot(xb, wgb[h])
        u = _dot(xb, wub[h])
        act = (jax.nn.silu(g) * u).astype(BF16)
        _store_token_tiles(y_view, _dot(act, wdb[h]))

    used = 2 * i < n_used_ref[0]
    same = blk_e_ref[2 * i] == blk_e_ref[2 * i + 1]

    @pl.when(used & same)
    def _one_expert():
        swiglu(x_ref, y_ref, 0)

    @pl.when(used & jnp.logical_not(same))
    def _two_experts():
        for h in range(2):
            swiglu(x_ref.at[pl.ds(h * rows, rows)], y_ref.at[pl.ds(h * rows, rows)], h)

    @pl.when(jnp.logical_not(used))
    def _skip():
        y_ref[...] = jnp.zeros_like(y_ref)


def _expert_call(blk_e, n_used, row_token, hf_all, we_gate, we_up, we_down):
    p = row_token.shape[0]
    assert p % (2 * MOE_TM) == 0 and blk_e.shape[0] == p // MOE_TM
    steps = p // (2 * MOE_TM)
    tile_spec = pl.BlockSpec((2 * MOE_TM * SUBLANES, LANES), lambda i, be, nu: (i, 0))
    w_in_spec = lambda h: pl.BlockSpec((1, D_MODEL, D_EXPERT), lambda i, be, nu: (be[2 * i + h], 0, 0))
    w_out_spec = lambda h: pl.BlockSpec((1, D_EXPERT, D_MODEL), lambda i, be, nu: (be[2 * i + h], 0, 0))
    grid_spec = pltpu.PrefetchScalarGridSpec(
        num_scalar_prefetch=2,
        grid=(steps,),
        in_specs=[pl.BlockSpec((2 * MOE_TM,), lambda i, be, nu: (i,), memory_space=pltpu.SMEM),
                  pl.BlockSpec((2 * MOE_TM,), lambda i, be, nu: (jnp.minimum(i + 1, steps - 1),),
                               memory_space=pltpu.SMEM),
                  pl.BlockSpec(memory_space=pl.ANY),
                  w_in_spec(0), w_in_spec(0), w_out_spec(0), w_in_spec(1), w_in_spec(1), w_out_spec(1)],
        out_specs=tile_spec,
        scratch_shapes=[pltpu.VMEM((2, D_MODEL, D_EXPERT), BF16), pltpu.VMEM((2, D_MODEL, D_EXPERT), BF16),
                        pltpu.VMEM((2, D_EXPERT, D_MODEL), BF16),
                        pltpu.VMEM((2, 2 * MOE_TM * SUBLANES, LANES), F32), pltpu.SemaphoreType.DMA((2,))],
    )
    return pl.pallas_call(
        _expert_kernel,
        grid_spec=grid_spec,
        out_shape=jax.ShapeDtypeStruct((p * SUBLANES, LANES), F32),
        compiler_params=pltpu.CompilerParams(dimension_semantics=("arbitrary",),
                                             vmem_limit_bytes=VMEM_LIMIT_BYTES),
        name="experts",
    )(blk_e, n_used, row_token, row_token, hf_all, we_gate, we_up, we_down, we_gate, we_up, we_down)


def _final_kernel(d0_ref, d1_ref, d0n_ref, d1n_ref, x1_ref, ybuf_ref, route_ref, g2_ref, l2g_ref, l2b_ref, out_ref,
                  yg, sems):
    n = x1_ref.shape[0]
    i = pl.program_id(0)
    slot = i % 2

    def gather(idx_refs, s):
        def start_rows(t, _):
            for k, idx_ref in enumerate(idx_refs):
                pltpu.make_async_copy(ybuf_ref.at[_tok_rows(idx_ref[t])], yg.at[s, k, _tok_rows(t)],
                                      sems.at[s]).start(priority=k)
            return 0
        lax.fori_loop(0, n, start_rows, 0, unroll=4)

    @pl.when(i == 0)
    def _first():
        gather((d0_ref, d1_ref), 0)

    @pl.when(i + 1 < pl.num_programs(0))
    def _prefetch():
        gather((d0n_ref, d1n_ref), 1 - slot)

    for k in range(2):
        pltpu.make_async_copy(ybuf_ref.at[_tok_rows(0, n)], yg.at[slot, k], sems.at[slot]).wait()
    route = route_ref[...]
    ff = (_load_token_tiles(yg.at[slot, 0]) * route[:, 2:3]
          + _load_token_tiles(yg.at[slot, 1]) * route[:, 3:4])
    out_ref[...] = _ln(ALPHA * x1_ref[...] + g2_ref[0] * ff) * l2g_ref[...] + l2b_ref[...]


def _final_call(dest0, dest1, first_token, x1, ybuf, route, g2, ln2_g, ln2_b, rows_per_g2, tile):
    n = x1.shape[0]
    assert n % tile == 0 and rows_per_g2 % tile == 0 and first_token % tile == 0
    per = rows_per_g2 // tile
    g2_rows = g2.shape[1]
    steps = n // tile
    off = first_token // tile
    idx = lambda: pl.BlockSpec((tile,), lambda i: (off + i,), memory_space=pltpu.SMEM)
    idx_next = lambda: pl.BlockSpec((tile,), lambda i: (off + jnp.minimum(i + 1, steps - 1),),
                                    memory_space=pltpu.SMEM)
    return pl.pallas_call(
        _final_kernel,
        grid=(steps,),
        in_specs=[idx(), idx(), idx_next(), idx_next(),
                  pl.BlockSpec((tile, D_MODEL), lambda i: (i, 0)),
                  pl.BlockSpec(memory_space=pl.ANY),
                  pl.BlockSpec((tile, LANES), lambda i: (i, 0)),
                  pl.BlockSpec((1, g2_rows, D_MODEL), lambda i: (i // per, 0, 0)),
                  pl.BlockSpec((1, D_MODEL), lambda i: (0, 0)),
                  pl.BlockSpec((1, D_MODEL), lambda i: (0, 0))],
        out_specs=pl.BlockSpec((tile, D_MODEL), lambda i: (i, 0)),
        out_shape=jax.ShapeDtypeStruct((n, D_MODEL), F32),
        scratch_shapes=[pltpu.VMEM((2, 2, tile * SUBLANES, LANES), F32), pltpu.SemaphoreType.DMA((2,))],
        compiler_params=pltpu.CompilerParams(dimension_semantics=("arbitrary",)),
        name="final",
    )(dest0, dest1, dest0, dest1, x1, ybuf, route, g2, ln2_g, ln2_b)


def _block_diag_gate_weights(wa, wx):
    def bd(w):
        w = w.reshape(N_RG_TILES, RG_GROUP, RG_BW, RG_BW)
        eye = jnp.eye(RG_GROUP, dtype=w.dtype)
        return jnp.einsum('tgcd,gh->tgchd', w, eye).reshape(N_RG_TILES, MXU_DIM, MXU_DIM)
    return jnp.concatenate([bd(wa), bd(wx)], axis=2)


def kernel(x_prompt, x_sample, state_conv, state_rglru, state_gla, c_prompt, c_sample, w_mod, b_mod, w_in, b_in,
           conv_w, conv_b, rg_wa, rg_ba, rg_wx, rg_bx, rg_lambda, gla_wa2, gla_ba, gla_norm_g, p_a, p_b, w_o,
           ln1_g, ln1_b, w_grp, b_grp, w_exp, b_exp, we_gate, we_up, we_down, ln2_g, ln2_b):
    assert w_mod.shape[0] == DEPTH == 1
    bp, tp, _ = x_prompt.shape
    ns = x_sample.shape[0]
    n_p = bp * tp
    n_tok = n_p + ns
    row = lambda a: a.reshape(1, -1)

    lo = 2 * D_RNN + 2 * GLA_DK + 2 * GLA_DV
    w_in0 = w_in[0]
    assert lo == O_GA
    win_a = _pack_rows(w_in0, lo)
    win_b = _pack_rows(w_in0[:, lo + GLA_RANK:])
    bin_main = row(jnp.concatenate([b_in[0, :lo], b_in[0, lo + GLA_RANK:]]))
    walr = _pack_rows(jnp.pad(w_in0[:, lo:lo + GLA_RANK], ((0, 0), (0, LANES - GLA_RANK))))
    balr = row(jnp.pad(b_in[0, lo:lo + GLA_RANK], (0, LANES - GLA_RANK)))
    wa2 = _pack_rows(jnp.pad(gla_wa2[0], ((0, LANES - GLA_RANK), (0, 0))))
    wrg = _pack_rows(_block_diag_gate_weights(rg_wa[0], rg_wx[0]).reshape(RG_BLOCKS * RG_BW, 2 * MXU_DIM))
    wrg = wrg.reshape(N_RG_TILES, MXU_DIM // 2, 2 * MXU_DIM)
    w_route = _pack_rows(jnp.pad(jnp.concatenate([w_exp[0], w_grp[0]], axis=1),
                                 ((0, 0), (0, LANES - N_EXPERTS - N_GROUPS))))
    b_route = row(jnp.pad(jnp.concatenate([b_exp[0], b_grp[0]]), (0, LANES - N_EXPERTS - N_GROUPS)))
    pa, pb, wo = _pack_rows(p_a[0]), _pack_rows(p_b[0]), _pack_rows(w_o[0])
    pre_w = [win_a, win_b, bin_main, walr, balr, wa2, row(gla_ba[0])]
    rec_w = [conv_w[0], row(conv_b[0]), wrg, row(rg_ba[0]), row(rg_bx[0]), row(rg_lambda[0])]
    post_w = [row(gla_norm_g[0]), pa, pb, wo, row(ln1_g[0]), row(ln1_b[0]), w_route, b_route]

    mod = _mod_call(jnp.concatenate([c_prompt, c_sample], axis=0), w_mod[0], row(b_mod[0]))
    mod_p = mod[:bp].reshape(bp, 6, D_MODEL)
    mod_s = mod[bp:].reshape(ns, 6, D_MODEL).transpose(1, 0, 2)

    mix_w = pre_w + [post_w[0]] + rec_w + post_w[1:]
    assert ns <= MIX_TT
    x1_p, hf_all, route_p, routet_p, conv_p, h_p, s_p = _mixer_call(x_prompt, mod_p, mix_w)

    xs = x_sample.reshape(ns, D_MODEL)
    sconv = state_conv[0].transpose(1, 0, 2)
    conv_s, h_s, ya_s, q_s, k_s, v_s, la_s, gg_s, ga_s, gb_s = _s_pre_call(
        xs, mod_s, sconv, state_rglru[0], pre_w + rec_w)
    s_s, o_s = _s_state_call(state_gla[0], q_s, k_s, v_s, la_s)
    x1_s, hf_all, route_s, routet_s = _s_post_call(xs, mod_s, o_s, gg_s, ga_s, gb_s, ya_s, post_w, hf_all, n_p)

    route_pf = route_p.reshape(n_p, LANES)
    eid = jnp.concatenate([routet_p[:2], routet_s[:2]], axis=1).astype(I32)
    assert n_tok % LANES == 0
    n_chunks = n_tok // LANES
    rank3, cnt = _rank_call(eid.reshape(2, n_chunks, LANES).transpose(1, 0, 2))
    rank = rank3.transpose(1, 0, 2).reshape(2, n_tok)
    counts = cnt[:, 0]
    pcounts = (counts + MOE_TM - 1) // MOE_TM * MOE_TM
    pends = jnp.cumsum(pcounts)
    pstarts = pends - pcounts
    experts = jnp.arange(N_EXPERTS, dtype=I32)[:, None, None]
    start_of = jnp.sum(jnp.where(eid[None] == experts, pstarts.astype(I32)[:, None, None], 0), axis=0)
    dest = start_of + rank
    dest0, dest1 = dest[0], dest[1]
    n_tiles = -(-(2 * n_tok + N_EXPERTS * (MOE_TM - 1)) // (2 * MOE_TM)) * 2
    tile_start = jnp.arange(n_tiles, dtype=I32) * MOE_TM
    blk_e = jnp.minimum(jnp.sum(pends[None, :] <= tile_start[:, None], axis=1), N_EXPERTS - 1).astype(I32)
    n_used = (pends[-1] // MOE_TM).astype(I32).reshape(1)

    row_token = _row_token_call(dest0, dest1, n_tiles * MOE_TM)
    ybuf = _expert_call(blk_e, n_used, row_token, hf_all, we_gate[0], we_up[0], we_down[0])

    y_p = _final_call(dest0, dest1, 0, x1_p.reshape(n_p, D_MODEL), ybuf, route_pf, mod_p[:, 5:6, :],
                      row(ln2_g[0]), row(ln2_b[0]), tp, ROW_CHUNK)
    y_s = _final_call(dest0, dest1, n_p, x1_s, ybuf, route_s, mod_s[5][None], row(ln2_g[0]), row(ln2_b[0]), ns, ns)

    return (y_p.reshape(bp, tp, D_MODEL), y_s.reshape(ns, 1, D_MODEL),
            conv_p[None], h_p.reshape(1, bp, D_RNN), s_p[None],
            conv_s.transpose(1, 0, 2)[None], h_s[None], s_s[None])
```

```python
import functools

import jax
import jax.numpy as jnp
from jax import lax
from jax.experimental import pallas as pl
from jax.experimental.pallas import tpu as pltpu

F32 = jnp.float32
BF16 = jnp.bfloat16
I32 = jnp.int32

D_MODEL = 1024
D_RNN = D_MODEL
RG_BLOCKS = 16
RG_BW = D_RNN // RG_BLOCKS
CONV_W = 4
RG_C = 8.0
GLA_HEADS = 4
GLA_DK = D_MODEL // 2
GLA_DV = D_MODEL
GLA_DKH = GLA_DK // GLA_HEADS
GLA_DVH = GLA_DV // GLA_HEADS
GLA_RANK = 16
GLA_TAU = 16.0
GLA_CHUNK = 64
N_GROUPS = 4
EXP_PER_GROUP = 8
N_EXPERTS = N_GROUPS * EXP_PER_GROUP
D_EXPERT = 512
DEPTH = 1
ALPHA = (2.0 * DEPTH) ** 0.25
LN_EPS = 1e-5
RMS_EPS = 1e-6

LANES = 128
SUBLANES = 8
MXU_DIM = 256
VMEM_LIMIT_BYTES = 56 * 1024 * 1024

MIX_TT = 256
STAGE1_COLS = 512
PIPE_DEPTH = 2
MOE_TM = 256
ROW_CHUNK = 128
SAMPLE_SB = 8
RG_GROUP = MXU_DIM // RG_BW
N_RG_TILES = RG_BLOCKS // RG_GROUP
N_MAIN = 2 * D_RNN + 2 * GLA_DK + 2 * GLA_DV + 2 * D_MODEL
ROUTE_G0 = N_EXPERTS

O_RX, O_RY, O_Q, O_K, O_V, O_GG, O_GA, O_GB = 0, 1024, 2048, 2560, 3072, 4096, 5120, 6144


def _ln(x):
    mu = jnp.mean(x, -1, keepdims=True)
    xc = x - mu
    var = jnp.mean(xc * xc, -1, keepdims=True)
    return xc * lax.rsqrt(var + LN_EPS)


def _dot(a, b):
    return jnp.dot(a, b, preferred_element_type=F32)


def _pack_kernel(w_ref, o_ref):
    o_ref[...] = pltpu.bitcast(w_ref[...].astype(BF16), jnp.uint32)


def _pack_rows(w, n=None):
    k = w.shape[0]
    n = w.shape[1] if n is None else n
    cn = min(n, 4 * LANES)
    assert k % (2 * SUBLANES) == 0 and n % cn == 0
    return pl.pallas_call(
        _pack_kernel,
        grid=(n // cn,),
        in_specs=[pl.BlockSpec((k, cn), lambda i: (0, i))],
        out_specs=pl.BlockSpec((k // 2, cn), lambda i: (0, i)),
        out_shape=jax.ShapeDtypeStruct((k // 2, n), jnp.uint32),
        compiler_params=pltpu.CompilerParams(dimension_semantics=("arbitrary",)),
        name="pack",
    )(w)


def _win(win_ref, winb_ref, lo, hi):
    if hi <= O_GA:
        return win_ref[:, lo:hi]
    assert lo >= O_GA
    return winb_ref[:, lo - O_GA:hi - O_GA]


def _w(packed):
    return pltpu.bitcast(packed, BF16)


def _expm1(x):
    u = jnp.exp(x)
    small = (u - 1.0) * x / jnp.log(u)
    return jnp.where(u == 1.0, x, jnp.where(jnp.abs(x) < 0.5, small, u - 1.0))


def _const_spec(shape):
    nd = len(shape)
    return pl.BlockSpec(shape, lambda *_: (0,) * nd, pipeline_mode=pl.Buffered(1))


assert D_MODEL == SUBLANES * LANES


def _tok_rows(t, count=1):
    return pl.ds(pl.multiple_of(t * SUBLANES, SUBLANES), count * SUBLANES)


def _store_token_tiles(ref, rows):
    n = rows.shape[0]
    for s in range(SUBLANES):
        ref[pl.ds(s, n, stride=SUBLANES), :] = rows[:, s * LANES:(s + 1) * LANES]


def _load_token_tiles(ref):
    n = ref.shape[0] // SUBLANES
    return jnp.concatenate([ref[pl.ds(s, n, stride=SUBLANES), :] for s in range(SUBLANES)], axis=1)


def _mod_kernel(c_ref, w_ref, b_ref, o_ref):
    c = c_ref[...]
    s = jax.nn.silu(c)
    o_ref[...] = _dot(s.astype(BF16), w_ref[...].astype(BF16)) + b_ref[...]


def _mod_call(c_all, w_mod, b_mod):
    n = c_all.shape[0]
    tn = D_MODEL
    return pl.pallas_call(
        _mod_kernel,
        grid=(6 * D_MODEL // tn,),
        in_specs=[pl.BlockSpec((n, D_MODEL), lambda i: (0, 0)),
                  pl.BlockSpec((D_MODEL, tn), lambda i: (0, i)),
                  pl.BlockSpec((1, tn), lambda i: (0, i))],
        out_specs=pl.BlockSpec((n, tn), lambda i: (0, i)),
        out_shape=jax.ShapeDtypeStruct((n, 6 * D_MODEL), F32),
        compiler_params=pltpu.CompilerParams(dimension_semantics=("arbitrary",)),
        name="mod",
    )(c_all, w_mod, b_mod)


def _no_op():
    pass


def _gate_logits(xc, wrg_ref, rba, rbx):
    xcb = xc.astype(BF16)
    parts = [_dot(xcb[:, c * MXU_DIM:(c + 1) * MXU_DIM], _w(wrg_ref[c])) for c in range(N_RG_TILES)]
    r_pre = jnp.concatenate([p[:, :MXU_DIM] for p in parts], axis=1)
    i_pre = jnp.concatenate([p[:, MXU_DIM:] for p in parts], axis=1)
    return r_pre, i_pre, rba, rbx


def _gate_logits_to(out_ref, xc, wrg_ref):
    xcb = xc.astype(BF16)
    for c in range(N_RG_TILES):
        p = _dot(xcb[:, c * MXU_DIM:(c + 1) * MXU_DIM], _w(wrg_ref[c]))
        out_ref[:, c * MXU_DIM:(c + 1) * MXU_DIM] = p[:, :MXU_DIM]
        out_ref[:, D_RNN + c * MXU_DIM:D_RNN + (c + 1) * MXU_DIM] = p[:, MXU_DIM:]


def _gates(xc, r_pre, i_pre, rba, rbx, lam, between=_no_op):
    r = jax.nn.sigmoid(r_pre + rba)
    between()
    ig = jax.nn.sigmoid(i_pre + rbx)
    log_a = -RG_C * r * jax.nn.softplus(-lam)
    a = jnp.exp(log_a)
    between()
    u = jnp.sqrt(-_expm1(2.0 * log_a)) * (ig * xc)
    between()
    return a, u


def _low_rank(hb, walr_ref, balr_ref):
    return _dot(hb, _w(walr_ref[...])) + balr_ref[...]


def _log_decay_from(alr, wa2_ref, gba_ref):
    return jax.nn.log_sigmoid(_dot(alr.astype(BF16), _w(wa2_ref[...])) + gba_ref[...]) / GLA_TAU


def _log_decay(hb, walr_ref, balr_ref, wa2_ref, gba_ref):
    return _log_decay_from(_low_rank(hb, walr_ref, balr_ref), wa2_ref, gba_ref)


def _post(x, o, gg, ga, gb, y_a, g1, sh2, sc2, gng, pa_ref, pb_ref, wo_ref, l1g, l1b, wr_ref, br,
          between=_no_op):
    rows = x.shape[0]
    heads = []
    for hh in range(GLA_HEADS):
        oh = o[:, hh * GLA_DVH:(hh + 1) * GLA_DVH]
        ms = jnp.mean(oh * oh, -1, keepdims=True)
        heads.append(oh * lax.rsqrt(ms + RMS_EPS) * gng)
    y_b = jnp.concatenate(heads, axis=1) * jax.nn.silu(gg)
    merged = (jax.nn.sigmoid(ga) * _dot(y_a.astype(BF16), _w(pa_ref[...]))
              + jax.nn.sigmoid(gb) * _dot(y_b.astype(BF16), _w(pb_ref[...])))
    mix = _dot(merged.astype(BF16), _w(wo_ref[...]))
    between()
    x1 = _ln(ALPHA * x + g1 * mix) * l1g + l1b
    between()
    hf = _ln(x1) * (1.0 + sc2) + sh2
    logits = _dot(hf.astype(BF16), _w(wr_ref[...])) + br
    lane = lax.broadcasted_iota(I32, (rows, LANES), 1).astype(F32)
    neg = jnp.float32(-jnp.inf)
    big = jnp.float32(LANES)
    g_valid = (lane >= ROUTE_G0) & (lane < ROUTE_G0 + N_GROUPS)
    gl = jnp.where(g_valid, logits, neg)
    gmax = jnp.max(gl, -1, keepdims=True)
    g_lane = jnp.min(jnp.where(gl == gmax, lane, big), -1, keepdims=True)
    g_w = 1.0 / jnp.sum(jnp.exp(gl - gmax), -1, keepdims=True)
    e_lo = (g_lane - ROUTE_G0) * EXP_PER_GROUP
    el = jnp.where((lane >= e_lo) & (lane < e_lo + EXP_PER_GROUP), logits, neg)
    t1 = jnp.max(el, -1, keepdims=True)
    i1 = jnp.min(jnp.where(el == t1, lane, big), -1, keepdims=True)
    el2 = jnp.where(lane == i1, neg, el)
    t2 = jnp.max(el2, -1, keepdims=True)
    i2 = jnp.min(jnp.where(el2 == t2, lane, big), -1, keepdims=True)
    e2 = jnp.exp(t2 - t1)
    den = 1.0 + e2
    w1 = (1.0 / den) * g_w
    w2 = (e2 / den) * g_w
    route = jnp.where(lane == 0, i1.astype(F32),
                      jnp.where(lane == 1, i2.astype(F32),
                                jnp.where(lane == 2, w1, jnp.where(lane == 3, w2, 0.0))))
    return x1, hf, route


def _mixer_kernel(xn_ref, modn_ref, x_ref, mod_ref, win_ref, winb_ref, bin_ref, walr_ref, balr_ref, wa2_ref, gba_ref,
                  gng_ref, cw_ref, cb_ref, wrg_ref, rba_ref, rbx_ref, lam_ref, pa_ref, pb_ref, wo_ref,
                  l1g_ref, l1b_ref, wr_ref, br_ref,
                  x1_ref, hf_ref, route_ref, routet_ref, convn_ref, hlast_ref, sfin_ref,
                  hbn_scr, hbc_scr, pn_scr, pc_scr, alrn_scr, alrc_scr, xcn_scr, xcc_scr, gaten_scr, gatec_scr,
                  rxbuf, hcar, st_ref, o_scr,
                  *, tiles_per_seq):
    i = pl.program_id(0)
    t = jnp.maximum(i - PIPE_DEPTH, 0)
    j = t % tiles_per_seq
    j1 = jnp.maximum(i - 1, 0) % tiles_per_seq
    tt = x_ref.shape[1]

    @pl.when(i == 0)
    def _no_tile_yet():
        hbc_scr[...] = jnp.zeros_like(hbc_scr)
        pc_scr[...] = jnp.zeros_like(pc_scr)
        alrc_scr[...] = jnp.zeros_like(alrc_scr)
        xcc_scr[...] = jnp.zeros_like(xcc_scr)
        gatec_scr[...] = jnp.zeros_like(gatec_scr)

    @pl.when(j1 == 0)
    def _init_conv():
        rxbuf[0:SUBLANES, :] = jnp.zeros((SUBLANES, D_RNN), F32)

    @pl.when(j == 0)
    def _init():
        hcar[...] = jnp.zeros_like(hcar)
        st_ref[...] = jnp.zeros_like(st_ref)

    def stage0():
        modn = modn_ref[0]
        hbn_scr[...] = (_ln(xn_ref[0]) * (1.0 + modn[1:2]) + modn[0:1]).astype(BF16)

    hbn = hbc_scr[...]

    def _proj_block(lo):
        def run():
            hi = lo + STAGE1_COLS
            pn_scr[:, lo:hi] = _dot(hbn, _w(_win(win_ref, winb_ref, lo, hi)))
        return run

    def _alr_block():
        alrn_scr[...] = _dot(hbn, _w(walr_ref[...]))

    pending = [_proj_block(lo) for lo in range(0, N_MAIN, STAGE1_COLS)] + [_alr_block]

    def pump(count=1):
        for _ in range(min(count, len(pending))):
            pending.pop(0)()

    x = x_ref[0]
    mod = mod_ref[0]
    g1, sh2, sc2 = mod[2:3], mod[3:4], mod[4:5]

    def proj(lo, hi):
        return pc_scr[:, lo:hi] + bin_ref[:, lo:hi]

    pump(O_RY // STAGE1_COLS)

    rx = pn_scr[:, O_RX:O_RX + D_RNN] + bin_ref[:, O_RX:O_RX + D_RNN]
    rxbuf[SUBLANES:SUBLANES + tt, :] = rx
    cw = cw_ref[...]
    xcn = cb_ref[...] + rxbuf[SUBLANES - 3:SUBLANES - 3 + tt, :] * cw[0:1]
    xcn = xcn + rxbuf[SUBLANES - 2:SUBLANES - 2 + tt, :] * cw[1:2]
    xcn = xcn + rxbuf[SUBLANES - 1:SUBLANES - 1 + tt, :] * cw[2:3]
    xcn = xcn + rx * cw[3:4]
    rxbuf[0:SUBLANES, :] = rxbuf[tt:tt + SUBLANES, :]
    xcn_scr[...] = xcn
    xc = xcc_scr[...]

    q = proj(O_Q, O_Q + GLA_DK) * (GLA_DKH ** -0.5)
    k = proj(O_K, O_K + GLA_DK)
    v = proj(O_V, O_V + GLA_DV)
    la = _log_decay_from(alrc_scr[...] + balr_ref[...], wa2_ref, gba_ref)
    pump()
    la = la.reshape(tt // SUBLANES, SUBLANES, GLA_DK)
    subk = lax.broadcasted_iota(I32, (1, SUBLANES, GLA_DK), 1)
    s = 1
    while s < SUBLANES:
        la = la + jnp.where(subk >= s, pltpu.roll(la, s, 1), 0.0)
        s *= 2
    groups = []
    for g in range(tt // SUBLANES):
        blk = la[g] if g % (GLA_CHUNK // SUBLANES) == 0 else la[g] + groups[-1][SUBLANES - 1:SUBLANES]
        groups.append(blk)
    bcum = jnp.concatenate(groups, axis=0)
    eb = jnp.exp(bcum)
    q_in = (q * eb).astype(BF16)
    k_in = (k * jnp.exp(-bcum)).astype(BF16)
    pump()
    stage0()
    gelu_ry = jax.nn.gelu(proj(O_RY, O_RY + D_RNN))
    pump()

    a, u = _gates(xc, gatec_scr[:, :D_RNN], gatec_scr[:, D_RNN:], rba_ref[...], rbx_ref[...], lam_ref[...],
                  between=pump)
    s = 1
    a = a.reshape(tt // SUBLANES, SUBLANES, D_RNN)
    u = u.reshape(tt // SUBLANES, SUBLANES, D_RNN)
    sub = lax.broadcasted_iota(I32, (1, SUBLANES, D_RNN), 1)
    while s < SUBLANES:
        keep = sub >= s
        a_s = jnp.where(keep, pltpu.roll(a, s, 1), 1.0)
        u_s = jnp.where(keep, pltpu.roll(u, s, 1), 0.0)
        u = a * u_s + u
        a = a * a_s
        pump()
        s *= 2
    carry = hcar[0:1, :]
    groups = []
    for g in range(tt // SUBLANES):
        hg = a[g] * carry + u[g]
        groups.append(hg)
        carry = hg[SUBLANES - 1:SUBLANES]
    hseq = jnp.concatenate(groups, axis=0)
    hcar[0:1, :] = carry
    y_a = hseq * gelu_ry

    tri = (lax.broadcasted_iota(I32, (GLA_CHUNK, GLA_CHUNK), 0)
           >= lax.broadcasted_iota(I32, (GLA_CHUNK, GLA_CHUNK), 1))
    n_chunks = tt // GLA_CHUNK
    nt_dims = (((1,), (1,)), ((), ()))
    attn, d_st, dec, vh_all = {}, {}, {}, {}
    for c in range(n_chunks):
        r0 = c * GLA_CHUNK
        for hh in range(GLA_HEADS):
            kc = slice(hh * GLA_DKH, (hh + 1) * GLA_DKH)
            b = bcum[r0:r0 + GLA_CHUNK, kc]
            btot = b[GLA_CHUNK - 1:GLA_CHUNK, :]
            k_out = (k[r0:r0 + GLA_CHUNK, kc] * jnp.exp(btot - b)).astype(BF16)
            vh = v[r0:r0 + GLA_CHUNK, hh * GLA_DVH:(hh + 1) * GLA_DVH].astype(BF16)
            scores = lax.dot_general(q_in[r0:r0 + GLA_CHUNK, kc], k_in[r0:r0 + GLA_CHUNK, kc], nt_dims,
                                     preferred_element_type=F32)
            attn[c, hh] = jnp.where(tri, scores, 0.0).astype(BF16)
            d_st[c, hh] = lax.dot_general(vh, k_out, (((0,), (0,)), ((), ())), preferred_element_type=F32)
            dec[c, hh] = jnp.exp(btot)
            vh_all[c, hh] = vh
        pump()
    starts = {}
    for hh in range(GLA_HEADS):
        st = st_ref[hh]
        for c in range(n_chunks):
            starts[c, hh] = st.astype(BF16)
            st = st * dec[c, hh] + d_st[c, hh]
        st_ref[hh] = st
    for c in range(n_chunks):
        r0 = c * GLA_CHUNK
        for hh in range(GLA_HEADS):
            qi = q_in[r0:r0 + GLA_CHUNK, hh * GLA_DKH:(hh + 1) * GLA_DKH]
            inter = lax.dot_general(qi, starts[c, hh], nt_dims, preferred_element_type=F32)
            o_scr[r0:r0 + GLA_CHUNK, hh * GLA_DVH:(hh + 1) * GLA_DVH] = _dot(attn[c, hh], vh_all[c, hh]) + inter

    x1, hf, route = _post(x, o_scr[...], proj(O_GG, O_GG + GLA_DV), proj(O_GA, O_GA + D_MODEL),
                          proj(O_GB, O_GB + D_MODEL), y_a, g1, sh2, sc2, gng_ref[...],
                          pa_ref, pb_ref, wo_ref, l1g_ref[...], l1b_ref[...], wr_ref, br_ref[...], between=pump)
    x1_ref[0] = x1
    _store_token_tiles(hf_ref.at[0], hf)
    route_ref[0] = route
    routet_ref[...] = route.T[0:SUBLANES, :]

    pump(len(pending))
    _gate_logits_to(gaten_scr, xcn, wrg_ref)
    pc_scr[:, O_RY:] = pn_scr[:, O_RY:]
    alrc_scr[...] = alrn_scr[...]
    hbc_scr[...] = hbn_scr[...]
    xcc_scr[...] = xcn_scr[...]
    gatec_scr[...] = gaten_scr[...]

    @pl.when((j1 == tiles_per_seq - 1) & (i >= 1))
    def _final_conv():
        convn_ref[0] = rxbuf[SUBLANES - (CONV_W - 1):SUBLANES, :]

    @pl.when((j == tiles_per_seq - 1) & (i >= PIPE_DEPTH))
    def _final():
        hlast_ref[0] = hcar[0:1, :]
        for hh in range(GLA_HEADS):
            sfin_ref[0, hh] = st_ref[hh].T


def _mixer_call(x, mod3, wts):
    b, t, _ = x.shape
    tt = MIX_TT
    assert t % tt == 0 and tt % GLA_CHUNK == 0
    nt = t // tt
    n_tiles = b * nt
    assert nt > 1
    cur = lambda i: jnp.maximum(i - PIPE_DEPTH, 0)
    nxt = lambda i: jnp.minimum(i, n_tiles - 1)
    tok = lambda i: (cur(i) // nt, cur(i) % nt, 0)
    tok_n = lambda i: (nxt(i) // nt, nxt(i) % nt, 0)
    per_b = lambda i: (cur(i) // nt, 0, 0)
    in_specs = [pl.BlockSpec((1, tt, D_MODEL), tok_n),
                pl.BlockSpec((1, 6, D_MODEL), lambda i: (nxt(i) // nt, 0, 0)),
                pl.BlockSpec((1, tt, D_MODEL), tok),
                pl.BlockSpec((1, 6, D_MODEL), per_b)] + [_const_spec(w.shape) for w in wts]
    out_specs = [pl.BlockSpec((1, tt, D_MODEL), tok),
                 pl.BlockSpec((1, tt * SUBLANES, LANES), tok),
                 pl.BlockSpec((1, tt, LANES), tok),
                 pl.BlockSpec((SUBLANES, tt), lambda i: (0, cur(i))),
                 pl.BlockSpec((1, CONV_W - 1, D_RNN), per_b),
                 pl.BlockSpec((1, 1, D_RNN), per_b),
                 pl.BlockSpec((1, GLA_HEADS, GLA_DKH, GLA_DVH), lambda i: (cur(i) // nt, 0, 0, 0))]
    out_shape = [jax.ShapeDtypeStruct((b, t, D_MODEL), F32),
                 jax.ShapeDtypeStruct((b, t * SUBLANES, LANES), F32),
                 jax.ShapeDtypeStruct((b, t, LANES), F32),
                 jax.ShapeDtypeStruct((SUBLANES, b * t), F32),
                 jax.ShapeDtypeStruct((b, CONV_W - 1, D_RNN), F32),
                 jax.ShapeDtypeStruct((b, 1, D_RNN), F32),
                 jax.ShapeDtypeStruct((b, GLA_HEADS, GLA_DKH, GLA_DVH), F32)]
    scratch = [pltpu.VMEM((tt, D_MODEL), BF16), pltpu.VMEM((tt, D_MODEL), BF16),
               pltpu.VMEM((tt, N_MAIN), F32), pltpu.VMEM((tt, N_MAIN), F32),
               pltpu.VMEM((tt, LANES), F32), pltpu.VMEM((tt, LANES), F32),
               pltpu.VMEM((tt, D_RNN), F32), pltpu.VMEM((tt, D_RNN), F32),
               pltpu.VMEM((tt, 2 * D_RNN), F32), pltpu.VMEM((tt, 2 * D_RNN), F32),
               pltpu.VMEM((tt + SUBLANES, D_RNN), F32),
               pltpu.VMEM((SUBLANES, D_RNN), F32),
               pltpu.VMEM((GLA_HEADS, GLA_DVH, GLA_DKH), F32),
               pltpu.VMEM((tt, GLA_DV), F32)]
    return pl.pallas_call(
        functools.partial(_mixer_kernel, tiles_per_seq=nt),
        grid=(n_tiles + PIPE_DEPTH,),
        in_specs=in_specs,
        out_specs=out_specs,
        out_shape=out_shape,
        scratch_shapes=scratch,
        compiler_params=pltpu.CompilerParams(dimension_semantics=("arbitrary",),
                                             vmem_limit_bytes=VMEM_LIMIT_BYTES),
        name="mixer",
    )(x, mod3, x, mod3, *wts)


def _s_pre_kernel(x_ref, mod_ref, sconv_ref, h0_ref, win_ref, winb_ref, bin_ref, walr_ref, balr_ref, wa2_ref, gba_ref,
                  cw_ref, cb_ref, wrg_ref, rba_ref, rbx_ref, lam_ref,
                  convn_ref, hnew_ref, ya_ref, q_ref, k_ref, v_ref, la_ref, gg_ref, ga_ref, gb_ref):
    x = x_ref[...]
    hb = (_ln(x) * (1.0 + mod_ref[1]) + mod_ref[0]).astype(BF16)

    def proj(lo, hi):
        return _dot(hb, _w(_win(win_ref, winb_ref, lo, hi))) + bin_ref[:, lo:hi]

    rx = proj(O_RX, O_RX + D_RNN)
    cw = cw_ref[...]
    xc = cb_ref[...] + sconv_ref[0] * cw[0:1]
    xc = xc + sconv_ref[1] * cw[1:2]
    xc = xc + sconv_ref[2] * cw[2:3]
    xc = xc + rx * cw[3:4]
    convn_ref[0] = sconv_ref[1]
    convn_ref[1] = sconv_ref[2]
    convn_ref[2] = rx
    a, u = _gates(xc, *_gate_logits(xc, wrg_ref, rba_ref[...], rbx_ref[...]), lam_ref[...])
    hnew = u + a * h0_ref[...]
    hnew_ref[...] = hnew
    ya_ref[...] = hnew * jax.nn.gelu(proj(O_RY, O_RY + D_RNN))
    q_ref[...] = proj(O_Q, O_Q + GLA_DK) * (GLA_DKH ** -0.5)
    k_ref[...] = proj(O_K, O_K + GLA_DK)
    v_ref[...] = proj(O_V, O_V + GLA_DV)
    la_ref[...] = _log_decay(hb, walr_ref, balr_ref, wa2_ref, gba_ref)
    gg_ref[...] = proj(O_GG, O_GG + GLA_DV)
    ga_ref[...] = proj(O_GA, O_GA + D_MODEL)
    gb_ref[...] = proj(O_GB, O_GB + D_MODEL)


def _s_pre_call(x_s, mod_s, sconv, h0, wts):
    n = x_s.shape[0]
    full = lambda shape: pl.BlockSpec(shape, lambda i: (0,) * len(shape))
    ins = [x_s, mod_s, sconv, h0] + list(wts)
    shapes = [(CONV_W - 1, n, D_RNN), (n, D_RNN), (n, D_RNN), (n, GLA_DK), (n, GLA_DK), (n, GLA_DV),
              (n, GLA_DK), (n, GLA_DV), (n, D_MODEL), (n, D_MODEL)]
    return pl.pallas_call(
        _s_pre_kernel,
        grid=(1,),
        in_specs=[full(a.shape) for a in ins],
        out_specs=[full(s) for s in shapes],
        out_shape=[jax.ShapeDtypeStruct(s, F32) for s in shapes],
        compiler_params=pltpu.CompilerParams(dimension_semantics=("arbitrary",),
                                             vmem_limit_bytes=VMEM_LIMIT_BYTES),
        name="s_pre",
    )(*ins)


def _to_column(row, n):
    eye = lax.broadcasted_iota(I32, (n, n), 0) == lax.broadcasted_iota(I32, (n, n), 1)
    return jnp.sum(jnp.where(eye, jnp.broadcast_to(row, (n, n)), 0.0), axis=1, keepdims=True)


def _s_state_kernel(s_ref, q_ref, k_ref, v_ref, la_ref, snew_ref, o_ref):
    sb = s_ref.shape[0]
    qb = q_ref[...].astype(BF16)
    dec = jnp.exp(la_ref[...])
    kk = k_ref[...]
    vv = v_ref[...]
    for si in range(sb):
        for hh in range(GLA_HEADS):
            kc = slice(hh * GLA_DKH, (hh + 1) * GLA_DKH)
            vc = slice(hh * GLA_DVH, (hh + 1) * GLA_DVH)
            d_col = _to_column(dec[si:si + 1, kc], GLA_DKH)
            k_col = _to_column(kk[si:si + 1, kc], GLA_DKH)
            s_new = s_ref[si, hh] * d_col + k_col * vv[si:si + 1, vc]
            snew_ref[si, hh] = s_new
            o_all = _dot(qb[:, kc], s_new.astype(BF16))
            o_ref[si:si + 1, vc] = o_all[si:si + 1, :]


def _s_state_call(state, q, k, v, la):
    n = state.shape[0]
    sb = SAMPLE_SB
    assert n % sb == 0
    blk = lambda w: pl.BlockSpec((sb, w), lambda i: (i, 0))
    st_spec = pl.BlockSpec((sb, GLA_HEADS, GLA_DKH, GLA_DVH), lambda i: (i, 0, 0, 0))
    return pl.pallas_call(
        _s_state_kernel,
        grid=(n // sb,),
        in_specs=[st_spec, blk(GLA_DK), blk(GLA_DK), blk(GLA_DV), blk(GLA_DK)],
        out_specs=[st_spec, blk(GLA_DV)],
        out_shape=[jax.ShapeDtypeStruct(state.shape, F32), jax.ShapeDtypeStruct((n, GLA_DV), F32)],
        compiler_params=pltpu.CompilerParams(dimension_semantics=("arbitrary",),
                                             vmem_limit_bytes=VMEM_LIMIT_BYTES),
        name="s_state",
    )(state, q, k, v, la)


def _s_post_kernel(x_ref, mod_ref, o_ref, gg_ref, ga_ref, gb_ref, ya_ref, gng_ref, pa_ref, pb_ref, wo_ref,
                   l1g_ref, l1b_ref, wr_ref, br_ref, x1_ref, hf_ref, route_ref, routet_ref):
    x1, hf, route = _post(x_ref[...], o_ref[...], gg_ref[...], ga_ref[...], gb_ref[...], ya_ref[...],
                          mod_ref[2], mod_ref[3], mod_ref[4], gng_ref[...], pa_ref, pb_ref, wo_ref,
                          l1g_ref[...], l1b_ref[...], wr_ref, br_ref[...])
    x1_ref[...] = x1
    _store_token_tiles(hf_ref, hf)
    route_ref[...] = route
    routet_ref[...] = route.T[0:SUBLANES, :]


def _s_post_call(x_s, mod_s, o, gg, ga, gb, ya, wts):
    n = x_s.shape[0]
    full = lambda shape: pl.BlockSpec(shape, lambda i: (0,) * len(shape))
    ins = [x_s, mod_s, o, gg, ga, gb, ya] + list(wts)
    shapes = [(n, D_MODEL), (n * SUBLANES, LANES), (n, LANES), (SUBLANES, n)]
    return pl.pallas_call(
        _s_post_kernel,
        grid=(1,),
        in_specs=[full(a.shape) for a in ins],
        out_specs=[full(s) for s in shapes],
        out_shape=[jax.ShapeDtypeStruct(s, F32) for s in shapes],
        compiler_params=pltpu.CompilerParams(dimension_semantics=("arbitrary",),
                                             vmem_limit_bytes=VMEM_LIMIT_BYTES),
        name="s_post",
    )(*ins)


def _rank_kernel(eid_ref, rank_ref, cnt_ref, carry):
    n_chunks = eid_ref.shape[0]
    carry[...] = jnp.zeros_like(carry)
    e_iota = lax.broadcasted_iota(I32, (N_EXPERTS, LANES), 0)
    upper = (lax.broadcasted_iota(I32, (LANES, LANES), 0)
             < lax.broadcasted_iota(I32, (LANES, LANES), 1)).astype(BF16)

    def body(c, _):
        ids = eid_ref[c]
        oh0 = (e_iota == ids[0:1, :])
        oh1 = (e_iota == ids[1:2, :])
        both = (oh0 | oh1).astype(F32)
        before = _dot(both.astype(BF16), upper) + carry[:, 0:1]
        r0 = jnp.sum(jnp.where(oh0, before, 0.0), axis=0, keepdims=True)
        r1 = jnp.sum(jnp.where(oh1, before, 0.0), axis=0, keepdims=True)
        rank_ref[c] = jnp.concatenate([r0, r1], axis=0).astype(I32)
        carry[...] = carry[...] + jnp.sum(both, axis=1, keepdims=True)
        return 0

    lax.fori_loop(0, n_chunks, body, 0)
    cnt_ref[...] = carry[...].astype(I32)


def _rank_call(eid3):
    n_chunks = eid3.shape[0]
    full = lambda shape: pl.BlockSpec(shape, lambda i: (0,) * len(shape))
    return pl.pallas_call(
        _rank_kernel,
        grid=(1,),
        in_specs=[full(eid3.shape)],
        out_specs=[full(eid3.shape), full((N_EXPERTS, LANES))],
        out_shape=[jax.ShapeDtypeStruct(eid3.shape, I32), jax.ShapeDtypeStruct((N_EXPERTS, LANES), I32)],
        scratch_shapes=[pltpu.VMEM((N_EXPERTS, LANES), F32)],
        compiler_params=pltpu.CompilerParams(dimension_semantics=("arbitrary",)),
        name="rank",
    )(eid3)


def _dispatch_kernel(pends_ref, d0_ref, d1_ref, src_p_ref, src_s_ref, buf_ref, sem, zeros_ref, zsem, *,
                     n_p_chunks):
    i = pl.program_id(0)
    n_tiles = buf_ref.shape[0] // (MOE_TM * SUBLANES)

    @pl.when(i == 0)
    def _clear():
        zeros_ref[...] = jnp.zeros_like(zeros_ref)

        def tile_copy(row0):
            return pltpu.make_async_copy(zeros_ref, buf_ref.at[_tok_rows(row0, MOE_TM)], zsem)

        def nonempty(e):
            return pends_ref[e] > jnp.where(e == 0, 0, pends_ref[jnp.maximum(e - 1, 0)])

        def start_tail(e, _):
            @pl.when(nonempty(e))
            def _():
                tile_copy(pends_ref[e] - MOE_TM).start()
            return 0

        def wait_tail(e, _):
            @pl.when(nonempty(e))
            def _():
                tile_copy(0).wait()
            return 0

        def start_unused(j, _):
            tile_copy(j * MOE_TM).start()
            return 0

        def wait_unused(j, _):
            tile_copy(0).wait()
            return 0

        first_unused = pends_ref[N_EXPERTS - 1] // MOE_TM
        lax.fori_loop(0, N_EXPERTS, start_tail, 0)
        lax.fori_loop(first_unused, n_tiles, start_unused, 0)
        lax.fori_loop(0, N_EXPERTS, wait_tail, 0)
        lax.fori_loop(first_unused, n_tiles, wait_unused, 0)

    def scatter_rows(src_ref):
        n = src_ref.shape[0] // SUBLANES

        def start_rows(t, _):
            for k, dest_ref in enumerate((d0_ref, d1_ref)):
                pltpu.make_async_copy(src_ref.at[_tok_rows(t)], buf_ref.at[_tok_rows(dest_ref[t])],
                                      sem).start(priority=k)
            return 0

        lax.fori_loop(0, n, start_rows, 0, unroll=4)
        for _ in range(2):
            pltpu.make_async_copy(src_ref, buf_ref.at[_tok_rows(0, n)], sem).wait()

    @pl.when(i < n_p_chunks)
    def _prompt():
        scatter_rows(src_p_ref)

    @pl.when(i >= n_p_chunks)
    def _sample():
        scatter_rows(src_s_ref)


def _dispatch_call(pends, dest0, dest1, src_p, src_s, n_rows):
    chunk_rows = ROW_CHUNK * SUBLANES
    assert src_p.shape[0] % chunk_rows == 0 and src_s.shape[0] % chunk_rows == 0 and n_rows % MOE_TM == 0
    n_p_chunks = src_p.shape[0] // chunk_rows
    n_s_chunks = src_s.shape[0] // chunk_rows
    tile = (chunk_rows, LANES)
    in_specs = [pl.BlockSpec((ROW_CHUNK,), lambda i, pe: (i,), memory_space=pltpu.SMEM),
                pl.BlockSpec((ROW_CHUNK,), lambda i, pe: (i,), memory_space=pltpu.SMEM),
                pl.BlockSpec(tile, lambda i, pe: (jnp.minimum(i, n_p_chunks - 1), 0)),
                pl.BlockSpec(tile, lambda i, pe: (jnp.maximum(i - n_p_chunks, 0), 0))]
    scratch = [pltpu.SemaphoreType.DMA(()), pltpu.VMEM((MOE_TM * SUBLANES, LANES), F32),
               pltpu.SemaphoreType.DMA(())]
    return pl.pallas_call(
        functools.partial(_dispatch_kernel, n_p_chunks=n_p_chunks),
        grid_spec=pltpu.PrefetchScalarGridSpec(
            num_scalar_prefetch=1, grid=(n_p_chunks + n_s_chunks,), in_specs=in_specs,
            out_specs=pl.BlockSpec(memory_space=pl.ANY), scratch_shapes=scratch),
        out_shape=jax.ShapeDtypeStruct((n_rows * SUBLANES, LANES), F32),
        compiler_params=pltpu.CompilerParams(dimension_semantics=("arbitrary",), has_side_effects=True),
        name="dispatch",
    )(pends, dest0, dest1, src_p, src_s)


def _expert_kernel(blk_e_ref, n_used_ref, x_ref, wg0_ref, wu0_ref, wd0_ref, wg1_ref, wu1_ref, wd1_ref, y_ref,
                   wgb, wub, wdb):
    i = pl.program_id(0)
    rows = MOE_TM * SUBLANES
    halves = ((0, wg0_ref, wu0_ref, wd0_ref), (1, wg1_ref, wu1_ref, wd1_ref))

    for h, wg_ref, wu_ref, wd_ref in halves:
        e_now = blk_e_ref[2 * i + h]
        e_before = blk_e_ref[jnp.maximum(2 * i + h - 2, 0)]

        @pl.when((i == 0) | (e_now != e_before))
        def _cast_weights(h=h, wg_ref=wg_ref, wu_ref=wu_ref, wd_ref=wd_ref):
            wgb[h] = wg_ref[0].astype(BF16)
            wub[h] = wu_ref[0].astype(BF16)
            wdb[h] = wd_ref[0].astype(BF16)

    def swiglu(x_view, y_view, h):
        xb = _load_token_tiles(x_view).astype(BF16)
        g = _dot(xb, wgb[h])
        u = _dot(xb, wub[h])
        act = (jax.nn.silu(g) * u).astype(BF16)
        _store_token_tiles(y_view, _dot(act, wdb[h]))

    used = 2 * i < n_used_ref[0]
    same = blk_e_ref[2 * i] == blk_e_ref[2 * i + 1]

    @pl.when(used & same)
    def _one_expert():
        swiglu(x_ref, y_ref, 0)

    @pl.when(used & jnp.logical_not(same))
    def _two_experts():
        for h in range(2):
            swiglu(x_ref.at[pl.ds(h * rows, rows)], y_ref.at[pl.ds(h * rows, rows)], h)

    @pl.when(jnp.logical_not(used))
    def _skip():
        y_ref[...] = jnp.zeros_like(y_ref)


def _expert_call(blk_e, n_used, xbuf, we_gate, we_up, we_down):
    p = xbuf.shape[0] // SUBLANES
    assert p % (2 * MOE_TM) == 0 and blk_e.shape[0] == p // MOE_TM
    tile_spec = pl.BlockSpec((2 * MOE_TM * SUBLANES, LANES), lambda i, be, nu: (i, 0))
    w_in_spec = lambda h: pl.BlockSpec((1, D_MODEL, D_EXPERT), lambda i, be, nu: (be[2 * i + h], 0, 0))
    w_out_spec = lambda h: pl.BlockSpec((1, D_EXPERT, D_MODEL), lambda i, be, nu: (be[2 * i + h], 0, 0))
    grid_spec = pltpu.PrefetchScalarGridSpec(
        num_scalar_prefetch=2,
        grid=(p // (2 * MOE_TM),),
        in_specs=[tile_spec, w_in_spec(0), w_in_spec(0), w_out_spec(0), w_in_spec(1), w_in_spec(1), w_out_spec(1)],
        out_specs=tile_spec,
        scratch_shapes=[pltpu.VMEM((2, D_MODEL, D_EXPERT), BF16), pltpu.VMEM((2, D_MODEL, D_EXPERT), BF16),
                        pltpu.VMEM((2, D_EXPERT, D_MODEL), BF16)],
    )
    return pl.pallas_call(
        _expert_kernel,
        grid_spec=grid_spec,
        out_shape=jax.ShapeDtypeStruct((p * SUBLANES, LANES), F32),
        compiler_params=pltpu.CompilerParams(dimension_semantics=("arbitrary",),
                                             vmem_limit_bytes=VMEM_LIMIT_BYTES),
        name="experts",
    )(blk_e, n_used, xbuf, we_gate, we_up, we_down, we_gate, we_up, we_down)


def _final_kernel(d0_ref, d1_ref, d0n_ref, d1n_ref, x1_ref, ybuf_ref, route_ref, g2_ref, l2g_ref, l2b_ref, out_ref,
                  yg, sems):
    n = x1_ref.shape[0]
    i = pl.program_id(0)
    slot = i % 2

    def gather(idx_refs, s):
        def start_rows(t, _):
            for k, idx_ref in enumerate(idx_refs):
                pltpu.make_async_copy(ybuf_ref.at[_tok_rows(idx_ref[t])], yg.at[s, k, _tok_rows(t)],
                                      sems.at[s]).start(priority=k)
            return 0
        lax.fori_loop(0, n, start_rows, 0, unroll=4)

    @pl.when(i == 0)
    def _first():
        gather((d0_ref, d1_ref), 0)

    @pl.when(i + 1 < pl.num_programs(0))
    def _prefetch():
        gather((d0n_ref, d1n_ref), 1 - slot)

    for k in range(2):
        pltpu.make_async_copy(ybuf_ref.at[_tok_rows(0, n)], yg.at[slot, k], sems.at[slot]).wait()
    route = route_ref[...]
    ff = (_load_token_tiles(yg.at[slot, 0]) * route[:, 2:3]
          + _load_token_tiles(yg.at[slot, 1]) * route[:, 3:4])
    out_ref[...] = _ln(ALPHA * x1_ref[...] + g2_ref[0] * ff) * l2g_ref[...] + l2b_ref[...]


def _final_call(dest0, dest1, first_token, x1, ybuf, route, g2, ln2_g, ln2_b, rows_per_g2, tile):
    n = x1.shape[0]
    assert n % tile == 0 and rows_per_g2 % tile == 0 and first_token % tile == 0
    per = rows_per_g2 // tile
    g2_rows = g2.shape[1]
    steps = n // tile
    off = first_token // tile
    idx = lambda: pl.BlockSpec((tile,), lambda i: (off + i,), memory_space=pltpu.SMEM)
    idx_next = lambda: pl.BlockSpec((tile,), lambda i: (off + jnp.minimum(i + 1, steps - 1),),
                                    memory_space=pltpu.SMEM)
    return pl.pallas_call(
        _final_kernel,
        grid=(steps,),
        in_specs=[idx(), idx(), idx_next(), idx_next(),
                  pl.BlockSpec((tile, D_MODEL), lambda i: (i, 0)),
                  pl.BlockSpec(memory_space=pl.ANY),
                  pl.BlockSpec((tile, LANES), lambda i: (i, 0)),
                  pl.BlockSpec((1, g2_rows, D_MODEL), lambda i: (i // per, 0, 0)),
                  pl.BlockSpec((1, D_MODEL), lambda i: (0, 0)),
                  pl.BlockSpec((1, D_MODEL), lambda i: (0, 0))],
        out_specs=pl.BlockSpec((tile, D_MODEL), lambda i: (i, 0)),
        out_shape=jax.ShapeDtypeStruct((n, D_MODEL), F32),
        scratch_shapes=[pltpu.VMEM((2, 2, tile * SUBLANES, LANES), F32), pltpu.SemaphoreType.DMA((2,))],
        compiler_params=pltpu.CompilerParams(dimension_semantics=("arbitrary",)),
        name="final",
    )(dest0, dest1, dest0, dest1, x1, ybuf, route, g2, ln2_g, ln2_b)


def _block_diag_gate_weights(wa, wx):
    def bd(w):
        w = w.reshape(N_RG_TILES, RG_GROUP, RG_BW, RG_BW)
        eye = jnp.eye(RG_GROUP, dtype=w.dtype)
        return jnp.einsum('tgcd,gh->tgchd', w, eye).reshape(N_RG_TILES, MXU_DIM, MXU_DIM)
    return jnp.concatenate([bd(wa), bd(wx)], axis=2)


def kernel(x_prompt, x_sample, state_conv, state_rglru, state_gla, c_prompt, c_sample, w_mod, b_mod, w_in, b_in,
           conv_w, conv_b, rg_wa, rg_ba, rg_wx, rg_bx, rg_lambda, gla_wa2, gla_ba, gla_norm_g, p_a, p_b, w_o,
           ln1_g, ln1_b, w_grp, b_grp, w_exp, b_exp, we_gate, we_up, we_down, ln2_g, ln2_b):
    assert w_mod.shape[0] == DEPTH == 1
    bp, tp, _ = x_prompt.shape
    ns = x_sample.shape[0]
    n_p = bp * tp
    n_tok = n_p + ns
    row = lambda a: a.reshape(1, -1)

    lo = 2 * D_RNN + 2 * GLA_DK + 2 * GLA_DV
    w_in0 = w_in[0]
    assert lo == O_GA
    win_a = _pack_rows(w_in0, lo)
    win_b = _pack_rows(w_in0[:, lo + GLA_RANK:])
    bin_main = row(jnp.concatenate([b_in[0, :lo], b_in[0, lo + GLA_RANK:]]))
    walr = _pack_rows(jnp.pad(w_in0[:, lo:lo + GLA_RANK], ((0, 0), (0, LANES - GLA_RANK))))
    balr = row(jnp.pad(b_in[0, lo:lo + GLA_RANK], (0, LANES - GLA_RANK)))
    wa2 = _pack_rows(jnp.pad(gla_wa2[0], ((0, LANES - GLA_RANK), (0, 0))))
    wrg = _pack_rows(_block_diag_gate_weights(rg_wa[0], rg_wx[0]).reshape(RG_BLOCKS * RG_BW, 2 * MXU_DIM))
    wrg = wrg.reshape(N_RG_TILES, MXU_DIM // 2, 2 * MXU_DIM)
    w_route = _pack_rows(jnp.pad(jnp.concatenate([w_exp[0], w_grp[0]], axis=1),
                                 ((0, 0), (0, LANES - N_EXPERTS - N_GROUPS))))
    b_route = row(jnp.pad(jnp.concatenate([b_exp[0], b_grp[0]]), (0, LANES - N_EXPERTS - N_GROUPS)))
    pa, pb, wo = _pack_rows(p_a[0]), _pack_rows(p_b[0]), _pack_rows(w_o[0])
    pre_w = [win_a, win_b, bin_main, walr, balr, wa2, row(gla_ba[0])]
    rec_w = [conv_w[0], row(conv_b[0]), wrg, row(rg_ba[0]), row(rg_bx[0]), row(rg_lambda[0])]
    post_w = [row(gla_norm_g[0]), pa, pb, wo, row(ln1_g[0]), row(ln1_b[0]), w_route, b_route]

    mod = _mod_call(jnp.concatenate([c_prompt, c_sample], axis=0), w_mod[0], row(b_mod[0]))
    mod_p = mod[:bp].reshape(bp, 6, D_MODEL)
    mod_s = mod[bp:].reshape(ns, 6, D_MODEL).transpose(1, 0, 2)

    mix_w = pre_w + [post_w[0]] + rec_w + post_w[1:]
    x1_p, hf_p, route_p, routet_p, conv_p, h_p, s_p = _mixer_call(x_prompt, mod_p, mix_w)

    xs = x_sample.reshape(ns, D_MODEL)
    sconv = state_conv[0].transpose(1, 0, 2)
    conv_s, h_s, ya_s, q_s, k_s, v_s, la_s, gg_s, ga_s, gb_s = _s_pre_call(
        xs, mod_s, sconv, state_rglru[0], pre_w + rec_w)
    s_s, o_s = _s_state_call(state_gla[0], q_s, k_s, v_s, la_s)
    x1_s, hf_s, route_s, routet_s = _s_post_call(xs, mod_s, o_s, gg_s, ga_s, gb_s, ya_s, post_w)

    route_pf = route_p.reshape(n_p, LANES)
    eid = jnp.concatenate([routet_p[:2], routet_s[:2]], axis=1).astype(I32)
    assert n_tok % LANES == 0
    n_chunks = n_tok // LANES
    rank3, cnt = _rank_call(eid.reshape(2, n_chunks, LANES).transpose(1, 0, 2))
    rank = rank3.transpose(1, 0, 2).reshape(2, n_tok)
    counts = cnt[:, 0]
    pcounts = (counts + MOE_TM - 1) // MOE_TM * MOE_TM
    pends = jnp.cumsum(pcounts)
    pstarts = pends - pcounts
    experts = jnp.arange(N_EXPERTS, dtype=I32)[:, None, None]
    start_of = jnp.sum(jnp.where(eid[None] == experts, pstarts.astype(I32)[:, None, None], 0), axis=0)
    dest = start_of + rank
    dest0, dest1 = dest[0], dest[1]
    n_tiles = -(-(2 * n_tok + N_EXPERTS * (MOE_TM - 1)) // (2 * MOE_TM)) * 2
    tile_start = jnp.arange(n_tiles, dtype=I32) * MOE_TM
    blk_e = jnp.minimum(jnp.sum(pends[None, :] <= tile_start[:, None], axis=1), N_EXPERTS - 1).astype(I32)
    n_used = (pends[-1] // MOE_TM).astype(I32).reshape(1)

    pends32 = pends.astype(I32)
    xbuf = _dispatch_call(pends32, dest0, dest1, hf_p.reshape(n_p * SUBLANES, LANES), hf_s, n_tiles * MOE_TM)
    ybuf = _expert_call(blk_e, n_used, xbuf, we_gate[0], we_up[0], we_down[0])

    y_p = _final_call(dest0, dest1, 0, x1_p.reshape(n_p, D_MODEL), ybuf, route_pf, mod_p[:, 5:6, :],
                      row(ln2_g[0]), row(ln2_b[0]), tp, ROW_CHUNK)
    y_s = _final_call(dest0, dest1, n_p, x1_s, ybuf, route_s, mod_s[5][None], row(ln2_g[0]), row(ln2_b[0]), ns, ns)

    return (y_p.reshape(bp, tp, D_MODEL), y_s.reshape(ns, 1, D_MODEL),
            conv_p[None], h_p.reshape(1, bp, D_RNN), s_p[None],
            conv_s.transpose(1, 0, 2)[None], h_s[None], s_s[None])
```

```python
import functools

import jax
import jax.numpy as jnp
from jax import lax
from jax.experimental import pallas as pl
from jax.experimental.pallas import tpu as pltpu

F32 = jnp.float32
BF16 = jnp.bfloat16
I32 = jnp.int32

D_MODEL = 1024
D_RNN = D_MODEL
RG_BLOCKS = 16
RG_BW = D_RNN // RG_BLOCKS
CONV_W = 4
RG_C = 8.0
GLA_HEADS = 4
GLA_DK = D_MODEL // 2
GLA_DV = D_MODEL
GLA_DKH = GLA_DK // GLA_HEADS
GLA_DVH = GLA_DV // GLA_HEADS
GLA_RANK = 16
GLA_TAU = 16.0
GLA_CHUNK = 64
N_GROUPS = 4
EXP_PER_GROUP = 8
N_EXPERTS = N_GROUPS * EXP_PER_GROUP
D_EXPERT = 512
DEPTH = 1
ALPHA = (2.0 * DEPTH) ** 0.25
LN_EPS = 1e-5
RMS_EPS = 1e-6

LANES = 128
SUBLANES = 8
MXU_DIM = 256
VMEM_LIMIT_BYTES = 56 * 1024 * 1024

MIX_TT = 256
STAGE1_COLS = 512
PIPE_DEPTH = 2
MOE_TM = 256
ROW_CHUNK = 128
SAMPLE_SB = 8
RG_GROUP = MXU_DIM // RG_BW
N_RG_TILES = RG_BLOCKS // RG_GROUP
N_MAIN = 2 * D_RNN + 2 * GLA_DK + 2 * GLA_DV + 2 * D_MODEL
ROUTE_G0 = N_EXPERTS

O_RX, O_RY, O_Q, O_K, O_V, O_GG, O_GA, O_GB = 0, 1024, 2048, 2560, 3072, 4096, 5120, 6144


def _ln(x):
    mu = jnp.mean(x, -1, keepdims=True)
    xc = x - mu
    var = jnp.mean(xc * xc, -1, keepdims=True)
    return xc * lax.rsqrt(var + LN_EPS)


def _dot(a, b):
    return jnp.dot(a, b, preferred_element_type=F32)


def _pack_kernel(w_ref, o_ref):
    o_ref[...] = pltpu.bitcast(w_ref[...].astype(BF16), jnp.uint32)


def _pack_rows(w, n=None):
    k = w.shape[0]
    n = w.shape[1] if n is None else n
    cn = min(n, 4 * LANES)
    assert k % (2 * SUBLANES) == 0 and n % cn == 0
    return pl.pallas_call(
        _pack_kernel,
        grid=(n // cn,),
        in_specs=[pl.BlockSpec((k, cn), lambda i: (0, i))],
        out_specs=pl.BlockSpec((k // 2, cn), lambda i: (0, i)),
        out_shape=jax.ShapeDtypeStruct((k // 2, n), jnp.uint32),
        compiler_params=pltpu.CompilerParams(dimension_semantics=("arbitrary",)),
        name="pack",
    )(w)


def _win(win_ref, winb_ref, lo, hi):
    if hi <= O_GA:
        return win_ref[:, lo:hi]
    assert lo >= O_GA
    return winb_ref[:, lo - O_GA:hi - O_GA]


def _w(packed):
    return pltpu.bitcast(packed, BF16)


def _expm1(x):
    u = jnp.exp(x)
    small = (u - 1.0) * x / jnp.log(u)
    return jnp.where(u == 1.0, x, jnp.where(jnp.abs(x) < 0.5, small, u - 1.0))


def _const_spec(shape):
    nd = len(shape)
    return pl.BlockSpec(shape, lambda *_: (0,) * nd, pipeline_mode=pl.Buffered(1))


assert D_MODEL == SUBLANES * LANES


def _tok_rows(t, count=1):
    return pl.ds(pl.multiple_of(t * SUBLANES, SUBLANES), count * SUBLANES)


def _store_token_tiles(ref, rows):
    n = rows.shape[0]
    for s in range(SUBLANES):
        ref[pl.ds(s, n, stride=SUBLANES), :] = rows[:, s * LANES:(s + 1) * LANES]


def _load_token_tiles(ref):
    n = ref.shape[0] // SUBLANES
    return jnp.concatenate([ref[pl.ds(s, n, stride=SUBLANES), :] for s in range(SUBLANES)], axis=1)


def _mod_kernel(c_ref, w_ref, b_ref, o_ref):
    c = c_ref[...]
    s = jax.nn.silu(c)
    o_ref[...] = _dot(s.astype(BF16), w_ref[...].astype(BF16)) + b_ref[...]


def _mod_call(c_all, w_mod, b_mod):
    n = c_all.shape[0]
    tn = D_MODEL
    return pl.pallas_call(
        _mod_kernel,
        grid=(6 * D_MODEL // tn,),
        in_specs=[pl.BlockSpec((n, D_MODEL), lambda i: (0, 0)),
                  pl.BlockSpec((D_MODEL, tn), lambda i: (0, i)),
                  pl.BlockSpec((1, tn), lambda i: (0, i))],
        out_specs=pl.BlockSpec((n, tn), lambda i: (0, i)),
        out_shape=jax.ShapeDtypeStruct((n, 6 * D_MODEL), F32),
        compiler_params=pltpu.CompilerParams(dimension_semantics=("arbitrary",)),
        name="mod",
    )(c_all, w_mod, b_mod)


def _no_op():
    pass


def _gate_logits(xc, wrg_ref, rba, rbx):
    xcb = xc.astype(BF16)
    parts = [_dot(xcb[:, c * MXU_DIM:(c + 1) * MXU_DIM], _w(wrg_ref[c])) for c in range(N_RG_TILES)]
    r_pre = jnp.concatenate([p[:, :MXU_DIM] for p in parts], axis=1)
    i_pre = jnp.concatenate([p[:, MXU_DIM:] for p in parts], axis=1)
    return r_pre, i_pre, rba, rbx


def _gate_logits_to(out_ref, xc, wrg_ref):
    xcb = xc.astype(BF16)
    for c in range(N_RG_TILES):
        p = _dot(xcb[:, c * MXU_DIM:(c + 1) * MXU_DIM], _w(wrg_ref[c]))
        out_ref[:, c * MXU_DIM:(c + 1) * MXU_DIM] = p[:, :MXU_DIM]
        out_ref[:, D_RNN + c * MXU_DIM:D_RNN + (c + 1) * MXU_DIM] = p[:, MXU_DIM:]


def _gates(xc, r_pre, i_pre, rba, rbx, lam, between=_no_op):
    r = jax.nn.sigmoid(r_pre + rba)
    between()
    ig = jax.nn.sigmoid(i_pre + rbx)
    log_a = -RG_C * r * jax.nn.softplus(-lam)
    a = jnp.exp(log_a)
    between()
    u = jnp.sqrt(-_expm1(2.0 * log_a)) * (ig * xc)
    between()
    return a, u


def _low_rank(hb, walr_ref, balr_ref):
    return _dot(hb, _w(walr_ref[...])) + balr_ref[...]


def _log_decay_from(alr, wa2_ref, gba_ref):
    return jax.nn.log_sigmoid(_dot(alr.astype(BF16), _w(wa2_ref[...])) + gba_ref[...]) / GLA_TAU


def _log_decay(hb, walr_ref, balr_ref, wa2_ref, gba_ref):
    return _log_decay_from(_low_rank(hb, walr_ref, balr_ref), wa2_ref, gba_ref)


def _post(x, o, gg, ga, gb, y_a, g1, sh2, sc2, gng, pa_ref, pb_ref, wo_ref, l1g, l1b, wr_ref, br,
          between=_no_op):
    rows = x.shape[0]
    heads = []
    for hh in range(GLA_HEADS):
        oh = o[:, hh * GLA_DVH:(hh + 1) * GLA_DVH]
        ms = jnp.mean(oh * oh, -1, keepdims=True)
        heads.append(oh * lax.rsqrt(ms + RMS_EPS) * gng)
    y_b = jnp.concatenate(heads, axis=1) * jax.nn.silu(gg)
    merged = (jax.nn.sigmoid(ga) * _dot(y_a.astype(BF16), _w(pa_ref[...]))
              + jax.nn.sigmoid(gb) * _dot(y_b.astype(BF16), _w(pb_ref[...])))
    mix = _dot(merged.astype(BF16), _w(wo_ref[...]))
    between()
    x1 = _ln(ALPHA * x + g1 * mix) * l1g + l1b
    between()
    hf = _ln(x1) * (1.0 + sc2) + sh2
    logits = _dot(hf.astype(BF16), _w(wr_ref[...])) + br
    lane = lax.broadcasted_iota(I32, (rows, LANES), 1).astype(F32)
    neg = jnp.float32(-jnp.inf)
    big = jnp.float32(LANES)
    g_valid = (lane >= ROUTE_G0) & (lane < ROUTE_G0 + N_GROUPS)
    gl = jnp.where(g_valid, logits, neg)
    gmax = jnp.max(gl, -1, keepdims=True)
    g_lane = jnp.min(jnp.where(gl == gmax, lane, big), -1, keepdims=True)
    g_w = 1.0 / jnp.sum(jnp.exp(gl - gmax), -1, keepdims=True)
    e_lo = (g_lane - ROUTE_G0) * EXP_PER_GROUP
    el = jnp.where((lane >= e_lo) & (lane < e_lo + EXP_PER_GROUP), logits, neg)
    t1 = jnp.max(el, -1, keepdims=True)
    i1 = jnp.min(jnp.where(el == t1, lane, big), -1, keepdims=True)
    el2 = jnp.where(lane == i1, neg, el)
    t2 = jnp.max(el2, -1, keepdims=True)
    i2 = jnp.min(jnp.where(el2 == t2, lane, big), -1, keepdims=True)
    e2 = jnp.exp(t2 - t1)
    den = 1.0 + e2
    w1 = (1.0 / den) * g_w
    w2 = (e2 / den) * g_w
    route = jnp.where(lane == 0, i1.astype(F32),
                      jnp.where(lane == 1, i2.astype(F32),
                                jnp.where(lane == 2, w1, jnp.where(lane == 3, w2, 0.0))))
    return x1, hf, route


def _mixer_kernel(xn_ref, modn_ref, x_ref, mod_ref, win_ref, winb_ref, bin_ref, walr_ref, balr_ref, wa2_ref, gba_ref,
                  gng_ref, cw_ref, cb_ref, wrg_ref, rba_ref, rbx_ref, lam_ref, pa_ref, pb_ref, wo_ref,
                  l1g_ref, l1b_ref, wr_ref, br_ref,
                  x1_ref, hf_ref, route_ref, routet_ref, convn_ref, hlast_ref, sfin_ref,
                  hbn_scr, hbc_scr, pn_scr, pc_scr, alrn_scr, alrc_scr, xcn_scr, xcc_scr, gaten_scr, gatec_scr,
                  rxbuf, hcar, st_ref, o_scr,
                  *, tiles_per_seq):
    i = pl.program_id(0)
    t = jnp.maximum(i - PIPE_DEPTH, 0)
    j = t % tiles_per_seq
    j1 = jnp.maximum(i - 1, 0) % tiles_per_seq
    tt = x_ref.shape[1]

    @pl.when(i == 0)
    def _no_tile_yet():
        hbc_scr[...] = jnp.zeros_like(hbc_scr)
        pc_scr[...] = jnp.zeros_like(pc_scr)
        alrc_scr[...] = jnp.zeros_like(alrc_scr)
        xcc_scr[...] = jnp.zeros_like(xcc_scr)
        gatec_scr[...] = jnp.zeros_like(gatec_scr)

    @pl.when(j1 == 0)
    def _init_conv():
        rxbuf[0:SUBLANES, :] = jnp.zeros((SUBLANES, D_RNN), F32)

    @pl.when(j == 0)
    def _init():
        hcar[...] = jnp.zeros_like(hcar)
        st_ref[...] = jnp.zeros_like(st_ref)

    def stage0():
        modn = modn_ref[0]
        hbn_scr[...] = (_ln(xn_ref[0]) * (1.0 + modn[1:2]) + modn[0:1]).astype(BF16)

    hbn = hbc_scr[...]

    def _proj_block(lo):
        def run():
            hi = lo + STAGE1_COLS
            pn_scr[:, lo:hi] = _dot(hbn, _w(_win(win_ref, winb_ref, lo, hi)))
        return run

    def _alr_block():
        alrn_scr[...] = _dot(hbn, _w(walr_ref[...]))

    pending = [_proj_block(lo) for lo in range(0, N_MAIN, STAGE1_COLS)] + [_alr_block]

    def pump(count=1):
        for _ in range(min(count, len(pending))):
            pending.pop(0)()

    x = x_ref[0]
    mod = mod_ref[0]
    g1, sh2, sc2 = mod[2:3], mod[3:4], mod[4:5]

    def proj(lo, hi):
        return pc_scr[:, lo:hi] + bin_ref[:, lo:hi]

    pump(O_RY // STAGE1_COLS)

    rx = pn_scr[:, O_RX:O_RX + D_RNN] + bin_ref[:, O_RX:O_RX + D_RNN]
    rxbuf[SUBLANES:SUBLANES + tt, :] = rx
    cw = cw_ref[...]
    xcn = cb_ref[...] + rxbuf[SUBLANES - 3:SUBLANES - 3 + tt, :] * cw[0:1]
    xcn = xcn + rxbuf[SUBLANES - 2:SUBLANES - 2 + tt, :] * cw[1:2]
    xcn = xcn + rxbuf[SUBLANES - 1:SUBLANES - 1 + tt, :] * cw[2:3]
    xcn = xcn + rx * cw[3:4]
    rxbuf[0:SUBLANES, :] = rxbuf[tt:tt + SUBLANES, :]
    xcn_scr[...] = xcn
    xc = xcc_scr[...]

    q = proj(O_Q, O_Q + GLA_DK) * (GLA_DKH ** -0.5)
    k = proj(O_K, O_K + GLA_DK)
    v = proj(O_V, O_V + GLA_DV)
    la = _log_decay_from(alrc_scr[...] + balr_ref[...], wa2_ref, gba_ref)
    pump()
    la = la.reshape(tt // SUBLANES, SUBLANES, GLA_DK)
    subk = lax.broadcasted_iota(I32, (1, SUBLANES, GLA_DK), 1)
    s = 1
    while s < SUBLANES:
        la = la + jnp.where(subk >= s, pltpu.roll(la, s, 1), 0.0)
        s *= 2
    groups = []
    for g in range(tt // SUBLANES):
        blk = la[g] if g % (GLA_CHUNK // SUBLANES) == 0 else la[g] + groups[-1][SUBLANES - 1:SUBLANES]
        groups.append(blk)
    bcum = jnp.concatenate(groups, axis=0)
    eb = jnp.exp(bcum)
    q_in = (q * eb).astype(BF16)
    k_in = (k * jnp.exp(-bcum)).astype(BF16)
    pump()
    stage0()
    gelu_ry = jax.nn.gelu(proj(O_RY, O_RY + D_RNN))
    pump()

    a, u = _gates(xc, gatec_scr[:, :D_RNN], gatec_scr[:, D_RNN:], rba_ref[...], rbx_ref[...], lam_ref[...],
                  between=pump)
    s = 1
    a = a.reshape(tt // SUBLANES, SUBLANES, D_RNN)
    u = u.reshape(tt // SUBLANES, SUBLANES, D_RNN)
    sub = lax.broadcasted_iota(I32, (1, SUBLANES, D_RNN), 1)
    while s < SUBLANES:
        keep = sub >= s
        a_s = jnp.where(keep, pltpu.roll(a, s, 1), 1.0)
        u_s = jnp.where(keep, pltpu.roll(u, s, 1), 0.0)
        u = a * u_s + u
        a = a * a_s
        pump()
        s *= 2
    carry = hcar[0:1, :]
    groups = []
    for g in range(tt // SUBLANES):
        hg = a[g] * carry + u[g]
        groups.append(hg)
        carry = hg[SUBLANES - 1:SUBLANES]
    hseq = jnp.concatenate(groups, axis=0)
    hcar[0:1, :] = carry
    y_a = hseq * gelu_ry

    tri = (lax.broadcasted_iota(I32, (GLA_CHUNK, GLA_CHUNK), 0)
           >= lax.broadcasted_iota(I32, (GLA_CHUNK, GLA_CHUNK), 1))
    n_chunks = tt // GLA_CHUNK
    nt_dims = (((1,), (1,)), ((), ()))
    attn, d_st, dec, vh_all = {}, {}, {}, {}
    for c in range(n_chunks):
        r0 = c * GLA_CHUNK
        for hh in range(GLA_HEADS):
            kc = slice(hh * GLA_DKH, (hh + 1) * GLA_DKH)
            b = bcum[r0:r0 + GLA_CHUNK, kc]
            btot = b[GLA_CHUNK - 1:GLA_CHUNK, :]
            k_out = (k[r0:r0 + GLA_CHUNK, kc] * jnp.exp(btot - b)).astype(BF16)
            vh = v[r0:r0 + GLA_CHUNK, hh * GLA_DVH:(hh + 1) * GLA_DVH].astype(BF16)
            scores = lax.dot_general(q_in[r0:r0 + GLA_CHUNK, kc], k_in[r0:r0 + GLA_CHUNK, kc], nt_dims,
                                     preferred_element_type=F32)
            attn[c, hh] = jnp.where(tri, scores, 0.0).astype(BF16)
            d_st[c, hh] = lax.dot_general(vh, k_out, (((0,), (0,)), ((), ())), preferred_element_type=F32)
            dec[c, hh] = jnp.exp(btot)
            vh_all[c, hh] = vh
        if c % 2 == 0:
            pump()
    starts = {}
    for hh in range(GLA_HEADS):
        st = st_ref[hh]
        for c in range(n_chunks):
            starts[c, hh] = st.astype(BF16)
            st = st * dec[c, hh] + d_st[c, hh]
        st_ref[hh] = st
    for c in range(n_chunks):
        r0 = c * GLA_CHUNK
        for hh in range(GLA_HEADS):
            qi = q_in[r0:r0 + GLA_CHUNK, hh * GLA_DKH:(hh + 1) * GLA_DKH]
            inter = lax.dot_general(qi, starts[c, hh], nt_dims, preferred_element_type=F32)
            o_scr[r0:r0 + GLA_CHUNK, hh * GLA_DVH:(hh + 1) * GLA_DVH] = _dot(attn[c, hh], vh_all[c, hh]) + inter

    x1, hf, route = _post(x, o_scr[...], proj(O_GG, O_GG + GLA_DV), proj(O_GA, O_GA + D_MODEL),
                          proj(O_GB, O_GB + D_MODEL), y_a, g1, sh2, sc2, gng_ref[...],
                          pa_ref, pb_ref, wo_ref, l1g_ref[...], l1b_ref[...], wr_ref, br_ref[...], between=pump)
    x1_ref[0] = x1
    _store_token_tiles(hf_ref.at[0], hf)
    route_ref[0] = route
    routet_ref[...] = route.T[0:SUBLANES, :]

    pump(len(pending))
    _gate_logits_to(gaten_scr, xcn, wrg_ref)
    pc_scr[:, O_RY:] = pn_scr[:, O_RY:]
    alrc_scr[...] = alrn_scr[...]
    hbc_scr[...] = hbn_scr[...]
    xcc_scr[...] = xcn_scr[...]
    gatec_scr[...] = gaten_scr[...]

    @pl.when((j1 == tiles_per_seq - 1) & (i >= 1))
    def _final_conv():
        convn_ref[0] = rxbuf[SUBLANES - (CONV_W - 1):SUBLANES, :]

    @pl.when((j == tiles_per_seq - 1) & (i >= PIPE_DEPTH))
    def _final():
        hlast_ref[0] = hcar[0:1, :]
        for hh in range(GLA_HEADS):
            sfin_ref[0, hh] = st_ref[hh].T


def _mixer_call(x, mod3, wts):
    b, t, _ = x.shape
    tt = MIX_TT
    assert t % tt == 0 and tt % GLA_CHUNK == 0
    nt = t // tt
    n_tiles = b * nt
    assert nt > 1
    cur = lambda i: jnp.maximum(i - PIPE_DEPTH, 0)
    nxt = lambda i: jnp.minimum(i, n_tiles - 1)
    tok = lambda i: (cur(i) // nt, cur(i) % nt, 0)
    tok_n = lambda i: (nxt(i) // nt, nxt(i) % nt, 0)
    per_b = lambda i: (cur(i) // nt, 0, 0)
    in_specs = [pl.BlockSpec((1, tt, D_MODEL), tok_n),
                pl.BlockSpec((1, 6, D_MODEL), lambda i: (nxt(i) // nt, 0, 0)),
                pl.BlockSpec((1, tt, D_MODEL), tok),
                pl.BlockSpec((1, 6, D_MODEL), per_b)] + [_const_spec(w.shape) for w in wts]
    out_specs = [pl.BlockSpec((1, tt, D_MODEL), tok),
                 pl.BlockSpec((1, tt * SUBLANES, LANES), tok),
                 pl.BlockSpec((1, tt, LANES), tok),
                 pl.BlockSpec((SUBLANES, tt), lambda i: (0, cur(i))),
                 pl.BlockSpec((1, CONV_W - 1, D_RNN), per_b),
                 pl.BlockSpec((1, 1, D_RNN), per_b),
                 pl.BlockSpec((1, GLA_HEADS, GLA_DKH, GLA_DVH), lambda i: (cur(i) // nt, 0, 0, 0))]
    out_shape = [jax.ShapeDtypeStruct((b, t, D_MODEL), F32),
                 jax.ShapeDtypeStruct((b, t * SUBLANES, LANES), F32),
                 jax.ShapeDtypeStruct((b, t, LANES), F32),
                 jax.ShapeDtypeStruct((SUBLANES, b * t), F32),
                 jax.ShapeDtypeStruct((b, CONV_W - 1, D_RNN), F32),
                 jax.ShapeDtypeStruct((b, 1, D_RNN), F32),
                 jax.ShapeDtypeStruct((b, GLA_HEADS, GLA_DKH, GLA_DVH), F32)]
    scratch = [pltpu.VMEM((tt, D_MODEL), BF16), pltpu.VMEM((tt, D_MODEL), BF16),
               pltpu.VMEM((tt, N_MAIN), F32), pltpu.VMEM((tt, N_MAIN), F32),
               pltpu.VMEM((tt, LANES), F32), pltpu.VMEM((tt, LANES), F32),
               pltpu.VMEM((tt, D_RNN), F32), pltpu.VMEM((tt, D_RNN), F32),
               pltpu.VMEM((tt, 2 * D_RNN), F32), pltpu.VMEM((tt, 2 * D_RNN), F32),
               pltpu.VMEM((tt + SUBLANES, D_RNN), F32),
               pltpu.VMEM((SUBLANES, D_RNN), F32),
               pltpu.VMEM((GLA_HEADS, GLA_DVH, GLA_DKH), F32),
               pltpu.VMEM((tt, GLA_DV), F32)]
    return pl.pallas_call(
        functools.partial(_mixer_kernel, tiles_per_seq=nt),
        grid=(n_tiles + PIPE_DEPTH,),
        in_specs=in_specs,
        out_specs=out_specs,
        out_shape=out_shape,
        scratch_shapes=scratch,
        compiler_params=pltpu.CompilerParams(dimension_semantics=("arbitrary",),
                                             vmem_limit_bytes=VMEM_LIMIT_BYTES),
        name="mixer",
    )(x, mod3, x, mod3, *wts)


def _s_pre_kernel(x_ref, mod_ref, sconv_ref, h0_ref, win_ref, winb_ref, bin_ref, walr_ref, balr_ref, wa2_ref, gba_ref,
                  cw_ref, cb_ref, wrg_ref, rba_ref, rbx_ref, lam_ref,
                  convn_ref, hnew_ref, ya_ref, q_ref, k_ref, v_ref, la_ref, gg_ref, ga_ref, gb_ref):
    x = x_ref[...]
    hb = (_ln(x) * (1.0 + mod_ref[1]) + mod_ref[0]).astype(BF16)

    def proj(lo, hi):
        return _dot(hb, _w(_win(win_ref, winb_ref, lo, hi))) + bin_ref[:, lo:hi]

    rx = proj(O_RX, O_RX + D_RNN)
    cw = cw_ref[...]
    xc = cb_ref[...] + sconv_ref[0] * cw[0:1]
    xc = xc + sconv_ref[1] * cw[1:2]
    xc = xc + sconv_ref[2] * cw[2:3]
    xc = xc + rx * cw[3:4]
    convn_ref[0] = sconv_ref[1]
    convn_ref[1] = sconv_ref[2]
    convn_ref[2] = rx
    a, u = _gates(xc, *_gate_logits(xc, wrg_ref, rba_ref[...], rbx_ref[...]), lam_ref[...])
    hnew = u + a * h0_ref[...]
    hnew_ref[...] = hnew
    ya_ref[...] = hnew * jax.nn.gelu(proj(O_RY, O_RY + D_RNN))
    q_ref[...] = proj(O_Q, O_Q + GLA_DK) * (GLA_DKH ** -0.5)
    k_ref[...] = proj(O_K, O_K + GLA_DK)
    v_ref[...] = proj(O_V, O_V + GLA_DV)
    la_ref[...] = _log_decay(hb, walr_ref, balr_ref, wa2_ref, gba_ref)
    gg_ref[...] = proj(O_GG, O_GG + GLA_DV)
    ga_ref[...] = proj(O_GA, O_GA + D_MODEL)
    gb_ref[...] = proj(O_GB, O_GB + D_MODEL)


def _s_pre_call(x_s, mod_s, sconv, h0, wts):
    n = x_s.shape[0]
    full = lambda shape: pl.BlockSpec(shape, lambda i: (0,) * len(shape))
    ins = [x_s, mod_s, sconv, h0] + list(wts)
    shapes = [(CONV_W - 1, n, D_RNN), (n, D_RNN), (n, D_RNN), (n, GLA_DK), (n, GLA_DK), (n, GLA_DV),
              (n, GLA_DK), (n, GLA_DV), (n, D_MODEL), (n, D_MODEL)]
    return pl.pallas_call(
        _s_pre_kernel,
        grid=(1,),
        in_specs=[full(a.shape) for a in ins],
        out_specs=[full(s) for s in shapes],
        out_shape=[jax.ShapeDtypeStruct(s, F32) for s in shapes],
        compiler_params=pltpu.CompilerParams(dimension_semantics=("arbitrary",),
                                             vmem_limit_bytes=VMEM_LIMIT_BYTES),
        name="s_pre",
    )(*ins)


def _to_column(row, n):
    eye = lax.broadcasted_iota(I32, (n, n), 0) == lax.broadcasted_iota(I32, (n, n), 1)
    return jnp.sum(jnp.where(eye, jnp.broadcast_to(row, (n, n)), 0.0), axis=1, keepdims=True)


def _s_state_kernel(s_ref, q_ref, k_ref, v_ref, la_ref, snew_ref, o_ref):
    sb = s_ref.shape[0]
    qb = q_ref[...].astype(BF16)
    dec = jnp.exp(la_ref[...])
    kk = k_ref[...]
    vv = v_ref[...]
    for si in range(sb):
        for hh in range(GLA_HEADS):
            kc = slice(hh * GLA_DKH, (hh + 1) * GLA_DKH)
            vc = slice(hh * GLA_DVH, (hh + 1) * GLA_DVH)
            d_col = _to_column(dec[si:si + 1, kc], GLA_DKH)
            k_col = _to_column(kk[si:si + 1, kc], GLA_DKH)
            s_new = s_ref[si, hh] * d_col + k_col * vv[si:si + 1, vc]
            snew_ref[si, hh] = s_new
            o_all = _dot(qb[:, kc], s_new.astype(BF16))
            o_ref[si:si + 1, vc] = o_all[si:si + 1, :]


def _s_state_call(state, q, k, v, la):
    n = state.shape[0]
    sb = SAMPLE_SB
    assert n % sb == 0
    blk = lambda w: pl.BlockSpec((sb, w), lambda i: (i, 0))
    st_spec = pl.BlockSpec((sb, GLA_HEADS, GLA_DKH, GLA_DVH), lambda i: (i, 0, 0, 0))
    return pl.pallas_call(
        _s_state_kernel,
        grid=(n // sb,),
        in_specs=[st_spec, blk(GLA_DK), blk(GLA_DK), blk(GLA_DV), blk(GLA_DK)],
        out_specs=[st_spec, blk(GLA_DV)],
        out_shape=[jax.ShapeDtypeStruct(state.shape, F32), jax.ShapeDtypeStruct((n, GLA_DV), F32)],
        compiler_params=pltpu.CompilerParams(dimension_semantics=("arbitrary",),
                                             vmem_limit_bytes=VMEM_LIMIT_BYTES),
        name="s_state",
    )(state, q, k, v, la)


def _s_post_kernel(x_ref, mod_ref, o_ref, gg_ref, ga_ref, gb_ref, ya_ref, gng_ref, pa_ref, pb_ref, wo_ref,
                   l1g_ref, l1b_ref, wr_ref, br_ref, x1_ref, hf_ref, route_ref, routet_ref):
    x1, hf, route = _post(x_ref[...], o_ref[...], gg_ref[...], ga_ref[...], gb_ref[...], ya_ref[...],
                          mod_ref[2], mod_ref[3], mod_ref[4], gng_ref[...], pa_ref, pb_ref, wo_ref,
                          l1g_ref[...], l1b_ref[...], wr_ref, br_ref[...])
    x1_ref[...] = x1
    _store_token_tiles(hf_ref, hf)
    route_ref[...] = route
    routet_ref[...] = route.T[0:SUBLANES, :]


def _s_post_call(x_s, mod_s, o, gg, ga, gb, ya, wts):
    n = x_s.shape[0]
    full = lambda shape: pl.BlockSpec(shape, lambda i: (0,) * len(shape))
    ins = [x_s, mod_s, o, gg, ga, gb, ya] + list(wts)
    shapes = [(n, D_MODEL), (n * SUBLANES, LANES), (n, LANES), (SUBLANES, n)]
    return pl.pallas_call(
        _s_post_kernel,
        grid=(1,),
        in_specs=[full(a.shape) for a in ins],
        out_specs=[full(s) for s in shapes],
        out_shape=[jax.ShapeDtypeStruct(s, F32) for s in shapes],
        compiler_params=pltpu.CompilerParams(dimension_semantics=("arbitrary",),
                                             vmem_limit_bytes=VMEM_LIMIT_BYTES),
        name="s_post",
    )(*ins)


def _rank_kernel(eid_ref, rank_ref, cnt_ref, carry):
    n_chunks = eid_ref.shape[0]
    carry[...] = jnp.zeros_like(carry)
    e_iota = lax.broadcasted_iota(I32, (N_EXPERTS, LANES), 0)
    upper = (lax.broadcasted_iota(I32, (LANES, LANES), 0)
             < lax.broadcasted_iota(I32, (LANES, LANES), 1)).astype(BF16)

    def body(c, _):
        ids = eid_ref[c]
        oh0 = (e_iota == ids[0:1, :])
        oh1 = (e_iota == ids[1:2, :])
        both = (oh0 | oh1).astype(F32)
        before = _dot(both.astype(BF16), upper) + carry[:, 0:1]
        r0 = jnp.sum(jnp.where(oh0, before, 0.0), axis=0, keepdims=True)
        r1 = jnp.sum(jnp.where(oh1, before, 0.0), axis=0, keepdims=True)
        rank_ref[c] = jnp.concatenate([r0, r1], axis=0).astype(I32)
        carry[...] = carry[...] + jnp.sum(both, axis=1, keepdims=True)
        return 0

    lax.fori_loop(0, n_chunks, body, 0, unroll=3)
    cnt_ref[...] = carry[...].astype(I32)


def _rank_call(eid3):
    n_chunks = eid3.shape[0]
    full = lambda shape: pl.BlockSpec(shape, lambda i: (0,) * len(shape))
    return pl.pallas_call(
        _rank_kernel,
        grid=(1,),
        in_specs=[full(eid3.shape)],
        out_specs=[full(eid3.shape), full((N_EXPERTS, LANES))],
        out_shape=[jax.ShapeDtypeStruct(eid3.shape, I32), jax.ShapeDtypeStruct((N_EXPERTS, LANES), I32)],
        scratch_shapes=[pltpu.VMEM((N_EXPERTS, LANES), F32)],
        compiler_params=pltpu.CompilerParams(dimension_semantics=("arbitrary",)),
        name="rank",
    )(eid3)


def _dispatch_kernel(pends_ref, d0_ref, d1_ref, src_p_ref, src_s_ref, buf_ref, sem, zeros_ref, zsem, *,
                     n_p_chunks):
    i = pl.program_id(0)
    n_tiles = buf_ref.shape[0] // (MOE_TM * SUBLANES)

    @pl.when(i == 0)
    def _clear():
        zeros_ref[...] = jnp.zeros_like(zeros_ref)

        def tile_copy(row0):
            return pltpu.make_async_copy(zeros_ref, buf_ref.at[_tok_rows(row0, MOE_TM)], zsem)

        def nonempty(e):
            return pends_ref[e] > jnp.where(e == 0, 0, pends_ref[jnp.maximum(e - 1, 0)])

        def start_tail(e, _):
            @pl.when(nonempty(e))
            def _():
                tile_copy(pends_ref[e] - MOE_TM).start()
            return 0

        def wait_tail(e, _):
            @pl.when(nonempty(e))
            def _():
                tile_copy(0).wait()
            return 0

        def start_unused(j, _):
            tile_copy(j * MOE_TM).start()
            return 0

        def wait_unused(j, _):
            tile_copy(0).wait()
            return 0

        first_unused = pends_ref[N_EXPERTS - 1] // MOE_TM
        lax.fori_loop(0, N_EXPERTS, start_tail, 0)
        lax.fori_loop(first_unused, n_tiles, start_unused, 0)
        lax.fori_loop(0, N_EXPERTS, wait_tail, 0)
        lax.fori_loop(first_unused, n_tiles, wait_unused, 0)

    def scatter_rows(src_ref):
        n = src_ref.shape[0] // SUBLANES

        def start_rows(t, _):
            for k, dest_ref in enumerate((d0_ref, d1_ref)):
                pltpu.make_async_copy(src_ref.at[_tok_rows(t)], buf_ref.at[_tok_rows(dest_ref[t])],
                                      sem).start(priority=k)
            return 0

        lax.fori_loop(0, n, start_rows, 0, unroll=4)
        for _ in range(2):
            pltpu.make_async_copy(src_ref, buf_ref.at[_tok_rows(0, n)], sem).wait()

    @pl.when(i < n_p_chunks)
    def _prompt():
        scatter_rows(src_p_ref)

    @pl.when(i >= n_p_chunks)
    def _sample():
        scatter_rows(src_s_ref)


def _dispatch_call(pends, dest0, dest1, src_p, src_s, n_rows):
    chunk_rows = ROW_CHUNK * SUBLANES
    assert src_p.shape[0] % chunk_rows == 0 and src_s.shape[0] % chunk_rows == 0 and n_rows % MOE_TM == 0
    n_p_chunks = src_p.shape[0] // chunk_rows
    n_s_chunks = src_s.shape[0] // chunk_rows
    tile = (chunk_rows, LANES)
    in_specs = [pl.BlockSpec((ROW_CHUNK,), lambda i, pe: (i,), memory_space=pltpu.SMEM),
                pl.BlockSpec((ROW_CHUNK,), lambda i, pe: (i,), memory_space=pltpu.SMEM),
                pl.BlockSpec(tile, lambda i, pe: (jnp.minimum(i, n_p_chunks - 1), 0)),
                pl.BlockSpec(tile, lambda i, pe: (jnp.maximum(i - n_p_chunks, 0), 0))]
    scratch = [pltpu.SemaphoreType.DMA(()), pltpu.VMEM((MOE_TM * SUBLANES, LANES), F32),
               pltpu.SemaphoreType.DMA(())]
    return pl.pallas_call(
        functools.partial(_dispatch_kernel, n_p_chunks=n_p_chunks),
        grid_spec=pltpu.PrefetchScalarGridSpec(
            num_scalar_prefetch=1, grid=(n_p_chunks + n_s_chunks,), in_specs=in_specs,
            out_specs=pl.BlockSpec(memory_space=pl.ANY), scratch_shapes=scratch),
        out_shape=jax.ShapeDtypeStruct((n_rows * SUBLANES, LANES), F32),
        compiler_params=pltpu.CompilerParams(dimension_semantics=("arbitrary",), has_side_effects=True),
        name="dispatch",
    )(pends, dest0, dest1, src_p, src_s)


def _expert_kernel(blk_e_ref, n_used_ref, x_ref, wg0_ref, wu0_ref, wd0_ref, wg1_ref, wu1_ref, wd1_ref, y_ref,
                   wgb, wub, wdb):
    i = pl.program_id(0)
    rows = MOE_TM * SUBLANES
    halves = ((0, wg0_ref, wu0_ref, wd0_ref), (1, wg1_ref, wu1_ref, wd1_ref))

    for h, wg_ref, wu_ref, wd_ref in halves:
        e_now = blk_e_ref[2 * i + h]
        e_before = blk_e_ref[jnp.maximum(2 * i + h - 2, 0)]

        @pl.when((i == 0) | (e_now != e_before))
        def _cast_weights(h=h, wg_ref=wg_ref, wu_ref=wu_ref, wd_ref=wd_ref):
            wgb[h] = wg_ref[0].astype(BF16)
            wub[h] = wu_ref[0].astype(BF16)
            wdb[h] = wd_ref[0].astype(BF16)

    def swiglu(x_view, y_view, h):
        xb = _load_token_tiles(x_view).astype(BF16)
        g = _dot(xb, wgb[h])
        u = _dot(xb, wub[h])
        act = (jax.nn.silu(g) * u).astype(BF16)
        _store_token_tiles(y_view, _dot(act, wdb[h]))

    used = 2 * i < n_used_ref[0]
    same = blk_e_ref[2 * i] == blk_e_ref[2 * i + 1]

    @pl.when(used & same)
    def _one_expert():
        swiglu(x_ref, y_ref, 0)

    @pl.when(used & jnp.logical_not(same))
    def _two_experts():
        for h in range(2):
            swiglu(x_ref.at[pl.ds(h * rows, rows)], y_ref.at[pl.ds(h * rows, rows)], h)

    @pl.when(jnp.logical_not(used))
    def _skip():
        y_ref[...] = jnp.zeros_like(y_ref)


def _expert_call(blk_e, n_used, xbuf, we_gate, we_up, we_down):
    p = xbuf.shape[0] // SUBLANES
    assert p % (2 * MOE_TM) == 0 and blk_e.shape[0] == p // MOE_TM
    tile_spec = pl.BlockSpec((2 * MOE_TM * SUBLANES, LANES), lambda i, be, nu: (i, 0))
    w_in_spec = lambda h: pl.BlockSpec((1, D_MODEL, D_EXPERT), lambda i, be, nu: (be[2 * i + h], 0, 0))
    w_out_spec = lambda h: pl.BlockSpec((1, D_EXPERT, D_MODEL), lambda i, be, nu: (be[2 * i + h], 0, 0))
    grid_spec = pltpu.PrefetchScalarGridSpec(
        num_scalar_prefetch=2,
        grid=(p // (2 * MOE_TM),),
        in_specs=[tile_spec, w_in_spec(0), w_in_spec(0), w_out_spec(0), w_in_spec(1), w_in_spec(1), w_out_spec(1)],
        out_specs=tile_spec,
        scratch_shapes=[pltpu.VMEM((2, D_MODEL, D_EXPERT), BF16), pltpu.VMEM((2, D_MODEL, D_EXPERT), BF16),
                        pltpu.VMEM((2, D_EXPERT, D_MODEL), BF16)],
    )
    return pl.pallas_call(
        _expert_kernel,
        grid_spec=grid_spec,
        out_shape=jax.ShapeDtypeStruct((p * SUBLANES, LANES), F32),
        compiler_params=pltpu.CompilerParams(dimension_semantics=("arbitrary",),
                                             vmem_limit_bytes=VMEM_LIMIT_BYTES),
        name="experts",
    )(blk_e, n_used, xbuf, we_gate, we_up, we_down, we_gate, we_up, we_down)


def _final_kernel(d0_ref, d1_ref, d0n_ref, d1n_ref, x1_ref, ybuf_ref, route_ref, g2_ref, l2g_ref, l2b_ref, out_ref,
                  yg, sems):
    n = x1_ref.shape[0]
    i = pl.program_id(0)
    slot = i % 2

    def gather(idx_refs, s):
        def start_rows(t, _):
            for k, idx_ref in enumerate(idx_refs):
                pltpu.make_async_copy(ybuf_ref.at[_tok_rows(idx_ref[t])], yg.at[s, k, _tok_rows(t)],
                                      sems.at[s]).start(priority=k)
            return 0
        lax.fori_loop(0, n, start_rows, 0, unroll=4)

    @pl.when(i == 0)
    def _first():
        gather((d0_ref, d1_ref), 0)

    @pl.when(i + 1 < pl.num_programs(0))
    def _prefetch():
        gather((d0n_ref, d1n_ref), 1 - slot)

    for k in range(2):
        pltpu.make_async_copy(ybuf_ref.at[_tok_rows(0, n)], yg.at[slot, k], sems.at[slot]).wait()
    route = route_ref[...]
    ff = (_load_token_tiles(yg.at[slot, 0]) * route[:, 2:3]
          + _load_token_tiles(yg.at[slot, 1]) * route[:, 3:4])
    out_ref[...] = _ln(ALPHA * x1_ref[...] + g2_ref[0] * ff) * l2g_ref[...] + l2b_ref[...]


def _final_call(dest0, dest1, first_token, x1, ybuf, route, g2, ln2_g, ln2_b, rows_per_g2, tile):
    n = x1.shape[0]
    assert n % tile == 0 and rows_per_g2 % tile == 0 and first_token % tile == 0
    per = rows_per_g2 // tile
    g2_rows = g2.shape[1]
    steps = n // tile
    off = first_token // tile
    idx = lambda: pl.BlockSpec((tile,), lambda i: (off + i,), memory_space=pltpu.SMEM)
    idx_next = lambda: pl.BlockSpec((tile,), lambda i: (off + jnp.minimum(i + 1, steps - 1),),
                                    memory_space=pltpu.SMEM)
    return pl.pallas_call(
        _final_kernel,
        grid=(steps,),
        in_specs=[idx(), idx(), idx_next(), idx_next(),
                  pl.BlockSpec((tile, D_MODEL), lambda i: (i, 0)),
                  pl.BlockSpec(memory_space=pl.ANY),
                  pl.BlockSpec((tile, LANES), lambda i: (i, 0)),
                  pl.BlockSpec((1, g2_rows, D_MODEL), lambda i: (i // per, 0, 0)),
                  pl.BlockSpec((1, D_MODEL), lambda i: (0, 0)),
                  pl.BlockSpec((1, D_MODEL), lambda i: (0, 0))],
        out_specs=pl.BlockSpec((tile, D_MODEL), lambda i: (i, 0)),
        out_shape=jax.ShapeDtypeStruct((n, D_MODEL), F32),
        scratch_shapes=[pltpu.VMEM((2, 2, tile * SUBLANES, LANES), F32), pltpu.SemaphoreType.DMA((2,))],
        compiler_params=pltpu.CompilerParams(dimension_semantics=("arbitrary",)),
        name="final",
    )(dest0, dest1, dest0, dest1, x1, ybuf, route, g2, ln2_g, ln2_b)


def _block_diag_gate_weights(wa, wx):
    def bd(w):
        w = w.reshape(N_RG_TILES, RG_GROUP, RG_BW, RG_BW)
        eye = jnp.eye(RG_GROUP, dtype=w.dtype)
        return jnp.einsum('tgcd,gh->tgchd', w, eye).reshape(N_RG_TILES, MXU_DIM, MXU_DIM)
    return jnp.concatenate([bd(wa), bd(wx)], axis=2)


def kernel(x_prompt, x_sample, state_conv, state_rglru, state_gla, c_prompt, c_sample, w_mod, b_mod, w_in, b_in,
           conv_w, conv_b, rg_wa, rg_ba, rg_wx, rg_bx, rg_lambda, gla_wa2, gla_ba, gla_norm_g, p_a, p_b, w_o,
           ln1_g, ln1_b, w_grp, b_grp, w_exp, b_exp, we_gate, we_up, we_down, ln2_g, ln2_b):
    assert w_mod.shape[0] == DEPTH == 1
    bp, tp, _ = x_prompt.shape
    ns = x_sample.shape[0]
    n_p = bp * tp
    n_tok = n_p + ns
    row = lambda a: a.reshape(1, -1)

    lo = 2 * D_RNN + 2 * GLA_DK + 2 * GLA_DV
    w_in0 = w_in[0]
    assert lo == O_GA
    win_a = _pack_rows(w_in0, lo)
    win_b = _pack_rows(w_in0[:, lo + GLA_RANK:])
    bin_main = row(jnp.concatenate([b_in[0, :lo], b_in[0, lo + GLA_RANK:]]))
    walr = _pack_rows(jnp.pad(w_in0[:, lo:lo + GLA_RANK], ((0, 0), (0, LANES - GLA_RANK))))
    balr = row(jnp.pad(b_in[0, lo:lo + GLA_RANK], (0, LANES - GLA_RANK)))
    wa2 = _pack_rows(jnp.pad(gla_wa2[0], ((0, LANES - GLA_RANK), (0, 0))))
    wrg = _pack_rows(_block_diag_gate_weights(rg_wa[0], rg_wx[0]).reshape(RG_BLOCKS * RG_BW, 2 * MXU_DIM))
    wrg = wrg.reshape(N_RG_TILES, MXU_DIM // 2, 2 * MXU_DIM)
    w_route = _pack_rows(jnp.pad(jnp.concatenate([w_exp[0], w_grp[0]], axis=1),
                                 ((0, 0), (0, LANES - N_EXPERTS - N_GROUPS))))
    b_route = row(jnp.pad(jnp.concatenate([b_exp[0], b_grp[0]]), (0, LANES - N_EXPERTS - N_GROUPS)))
    pa, pb, wo = _pack_rows(p_a[0]), _pack_rows(p_b[0]), _pack_rows(w_o[0])
    pre_w = [win_a, win_b, bin_main, walr, balr, wa2, row(gla_ba[0])]
    rec_w = [conv_w[0], row(conv_b[0]), wrg, row(rg_ba[0]), row(rg_bx[0]), row(rg_lambda[0])]
    post_w = [row(gla_norm_g[0]), pa, pb, wo, row(ln1_g[0]), row(ln1_b[0]), w_route, b_route]

    mod = _mod_call(jnp.concatenate([c_prompt, c_sample], axis=0), w_mod[0], row(b_mod[0]))
    mod_p = mod[:bp].reshape(bp, 6, D_MODEL)
    mod_s = mod[bp:].reshape(ns, 6, D_MODEL).transpose(1, 0, 2)

    mix_w = pre_w + [post_w[0]] + rec_w + post_w[1:]
    x1_p, hf_p, route_p, routet_p, conv_p, h_p, s_p = _mixer_call(x_prompt, mod_p, mix_w)

    xs = x_sample.reshape(ns, D_MODEL)
    sconv = state_conv[0].transpose(1, 0, 2)
    conv_s, h_s, ya_s, q_s, k_s, v_s, la_s, gg_s, ga_s, gb_s = _s_pre_call(
        xs, mod_s, sconv, state_rglru[0], pre_w + rec_w)
    s_s, o_s = _s_state_call(state_gla[0], q_s, k_s, v_s, la_s)
    x1_s, hf_s, route_s, routet_s = _s_post_call(xs, mod_s, o_s, gg_s, ga_s, gb_s, ya_s, post_w)

    route_pf = route_p.reshape(n_p, LANES)
    eid = jnp.concatenate([routet_p[:2], routet_s[:2]], axis=1).astype(I32)
    assert n_tok % LANES == 0
    n_chunks = n_tok // LANES
    rank3, cnt = _rank_call(eid.reshape(2, n_chunks, LANES).transpose(1, 0, 2))
    rank = rank3.transpose(1, 0, 2).reshape(2, n_tok)
    counts = cnt[:, 0]
    pcounts = (counts + MOE_TM - 1) // MOE_TM * MOE_TM
    pends = jnp.cumsum(pcounts)
    pstarts = pends - pcounts
    experts = jnp.arange(N_EXPERTS, dtype=I32)[:, None, None]
    start_of = jnp.sum(jnp.where(eid[None] == experts, pstarts.astype(I32)[:, None, None], 0), axis=0)
    dest = start_of + rank
    dest0, dest1 = dest[0], dest[1]
    n_tiles = -(-(2 * n_tok + N_EXPERTS * (MOE_TM - 1)) // (2 * MOE_TM)) * 2
    tile_start = jnp.arange(n_tiles, dtype=I32) * MOE_TM
    blk_e = jnp.minimum(jnp.sum(pends[None, :] <= tile_start[:, None], axis=1), N_EXPERTS - 1).astype(I32)
    n_used = (pends[-1] // MOE_TM).astype(I32).reshape(1)

    pends32 = pends.astype(I32)
    xbuf = _dispatch_call(pends32, dest0, dest1, hf_p.reshape(n_p * SUBLANES, LANES), hf_s, n_tiles * MOE_TM)
    ybuf = _expert_call(blk_e, n_used, xbuf, we_gate[0], we_up[0], we_down[0])

    y_p = _final_call(dest0, dest1, 0, x1_p.reshape(n_p, D_MODEL), ybuf, route_pf, mod_p[:, 5:6, :],
                      row(ln2_g[0]), row(ln2_b[0]), tp, ROW_CHUNK)
    y_s = _final_call(dest0, dest1, n_p, x1_s, ybuf, route_s, mod_s[5][None], row(ln2_g[0]), row(ln2_b[0]), ns, ns)

    return (y_p.reshape(bp, tp, D_MODEL), y_s.reshape(ns, 1, D_MODEL),
            conv_p[None], h_p.reshape(1, bp, D_RNN), s_p[None],
            conv_s.transpose(1, 0, 2)[None], h_s[None], s_s[None])
```

```python
import functools

import jax
import jax.numpy as jnp
from jax import lax
from jax.experimental import pallas as pl
from jax.experimental.pallas import tpu as pltpu

F32 = jnp.float32
BF16 = jnp.bfloat16
I32 = jnp.int32

D_MODEL = 1024
D_RNN = D_MODEL
RG_BLOCKS = 16
RG_BW = D_RNN // RG_BLOCKS
CONV_W = 4
RG_C = 8.0
GLA_HEADS = 4
GLA_DK = D_MODEL // 2
GLA_DV = D_MODEL
GLA_DKH = GLA_DK // GLA_HEADS
GLA_DVH = GLA_DV // GLA_HEADS
GLA_RANK = 16
GLA_TAU = 16.0
GLA_CHUNK = 64
N_GROUPS = 4
EXP_PER_GROUP = 8
N_EXPERTS = N_GROUPS * EXP_PER_GROUP
D_EXPERT = 512
DEPTH = 1
ALPHA = (2.0 * DEPTH) ** 0.25
LN_EPS = 1e-5
RMS_EPS = 1e-6

LANES = 128
SUBLANES = 8
MXU_DIM = 256
VMEM_LIMIT_BYTES = 56 * 1024 * 1024

MIX_TT = 256
STAGE1_COLS = 512
PIPE_DEPTH = 2
MOE_TM = 256
ROW_CHUNK = 128
SAMPLE_SB = 8
RG_GROUP = MXU_DIM // RG_BW
N_RG_TILES = RG_BLOCKS // RG_GROUP
N_MAIN = 2 * D_RNN + 2 * GLA_DK + 2 * GLA_DV + 2 * D_MODEL
ROUTE_G0 = N_EXPERTS

O_RX, O_RY, O_Q, O_K, O_V, O_GG, O_GA, O_GB = 0, 1024, 2048, 2560, 3072, 4096, 5120, 6144


def _ln(x):
    mu = jnp.mean(x, -1, keepdims=True)
    xc = x - mu
    var = jnp.mean(xc * xc, -1, keepdims=True)
    return xc * lax.rsqrt(var + LN_EPS)


def _dot(a, b):
    return jnp.dot(a, b, preferred_element_type=F32)


def _pack_kernel(w_ref, o_ref):
    o_ref[...] = pltpu.bitcast(w_ref[...].astype(BF16), jnp.uint32)


def _pack_rows(w, n=None):
    k = w.shape[0]
    n = w.shape[1] if n is None else n
    cn = min(n, 4 * LANES)
    assert k % (2 * SUBLANES) == 0 and n % cn == 0
    return pl.pallas_call(
        _pack_kernel,
        grid=(n // cn,),
        in_specs=[pl.BlockSpec((k, cn), lambda i: (0, i))],
        out_specs=pl.BlockSpec((k // 2, cn), lambda i: (0, i)),
        out_shape=jax.ShapeDtypeStruct((k // 2, n), jnp.uint32),
        compiler_params=pltpu.CompilerParams(dimension_semantics=("arbitrary",)),
        name="pack",
    )(w)


def _win(win_ref, winb_ref, lo, hi):
    if hi <= O_GA:
        return win_ref[:, lo:hi]
    assert lo >= O_GA
    return winb_ref[:, lo - O_GA:hi - O_GA]


def _w(packed):
    return pltpu.bitcast(packed, BF16)


def _expm1(x):
    u = jnp.exp(x)
    small = (u - 1.0) * x / jnp.log(u)
    return jnp.where(u == 1.0, x, jnp.where(jnp.abs(x) < 0.5, small, u - 1.0))


def _const_spec(shape):
    nd = len(shape)
    return pl.BlockSpec(shape, lambda *_: (0,) * nd, pipeline_mode=pl.Buffered(1))


assert D_MODEL == SUBLANES * LANES


def _tok_rows(t, count=1):
    return pl.ds(pl.multiple_of(t * SUBLANES, SUBLANES), count * SUBLANES)


def _store_token_tiles(ref, rows):
    n = rows.shape[0]
    for s in range(SUBLANES):
        ref[pl.ds(s, n, stride=SUBLANES), :] = rows[:, s * LANES:(s + 1) * LANES]


def _load_token_tiles(ref):
    n = ref.shape[0] // SUBLANES
    return jnp.concatenate([ref[pl.ds(s, n, stride=SUBLANES), :] for s in range(SUBLANES)], axis=1)


def _mod_kernel(c_ref, w_ref, b_ref, o_ref):
    c = c_ref[...]
    s = jax.nn.silu(c)
    o_ref[...] = _dot(s.astype(BF16), w_ref[...].astype(BF16)) + b_ref[...]


def _mod_call(c_all, w_mod, b_mod):
    n = c_all.shape[0]
    tn = D_MODEL
    return pl.pallas_call(
        _mod_kernel,
        grid=(6 * D_MODEL // tn,),
        in_specs=[pl.BlockSpec((n, D_MODEL), lambda i: (0, 0)),
                  pl.BlockSpec((D_MODEL, tn), lambda i: (0, i)),
                  pl.BlockSpec((1, tn), lambda i: (0, i))],
        out_specs=pl.BlockSpec((n, tn), lambda i: (0, i)),
        out_shape=jax.ShapeDtypeStruct((n, 6 * D_MODEL), F32),
        compiler_params=pltpu.CompilerParams(dimension_semantics=("arbitrary",)),
        name="mod",
    )(c_all, w_mod, b_mod)


def _no_op():
    pass


def _gate_logits(xc, wrg_ref, rba, rbx):
    xcb = xc.astype(BF16)
    parts = [_dot(xcb[:, c * MXU_DIM:(c + 1) * MXU_DIM], _w(wrg_ref[c])) for c in range(N_RG_TILES)]
    r_pre = jnp.concatenate([p[:, :MXU_DIM] for p in parts], axis=1)
    i_pre = jnp.concatenate([p[:, MXU_DIM:] for p in parts], axis=1)
    return r_pre, i_pre, rba, rbx


def _gate_logits_to(out_ref, xc, wrg_ref):
    xcb = xc.astype(BF16)
    for c in range(N_RG_TILES):
        p = _dot(xcb[:, c * MXU_DIM:(c + 1) * MXU_DIM], _w(wrg_ref[c]))
        out_ref[:, c * MXU_DIM:(c + 1) * MXU_DIM] = p[:, :MXU_DIM]
        out_ref[:, D_RNN + c * MXU_DIM:D_RNN + (c + 1) * MXU_DIM] = p[:, MXU_DIM:]


def _gates(xc, r_pre, i_pre, rba, rbx, lam, between=_no_op):
    r = jax.nn.sigmoid(r_pre + rba)
    between()
    ig = jax.nn.sigmoid(i_pre + rbx)
    log_a = -RG_C * r * jax.nn.softplus(-lam)
    a = jnp.exp(log_a)
    between()
    u = jnp.sqrt(-_expm1(2.0 * log_a)) * (ig * xc)
    between()
    return a, u


def _low_rank(hb, walr_ref, balr_ref):
    return _dot(hb, _w(walr_ref[...])) + balr_ref[...]


def _log_decay_from(alr, wa2_ref, gba_ref):
    return jax.nn.log_sigmoid(_dot(alr.astype(BF16), _w(wa2_ref[...])) + gba_ref[...]) / GLA_TAU


def _log_decay(hb, walr_ref, balr_ref, wa2_ref, gba_ref):
    return _log_decay_from(_low_rank(hb, walr_ref, balr_ref), wa2_ref, gba_ref)


def _post(x, o, gg, ga, gb, y_a, g1, sh2, sc2, gng, pa_ref, pb_ref, wo_ref, l1g, l1b, wr_ref, br,
          between=_no_op):
    rows = x.shape[0]
    heads = []
    for hh in range(GLA_HEADS):
        oh = o[:, hh * GLA_DVH:(hh + 1) * GLA_DVH]
        ms = jnp.mean(oh * oh, -1, keepdims=True)
        heads.append(oh * lax.rsqrt(ms + RMS_EPS) * gng)
    y_b = jnp.concatenate(heads, axis=1) * jax.nn.silu(gg)
    merged = (jax.nn.sigmoid(ga) * _dot(y_a.astype(BF16), _w(pa_ref[...]))
              + jax.nn.sigmoid(gb) * _dot(y_b.astype(BF16), _w(pb_ref[...])))
    mix = _dot(merged.astype(BF16), _w(wo_ref[...]))
    between()
    x1 = _ln(ALPHA * x + g1 * mix) * l1g + l1b
    between()
    hf = _ln(x1) * (1.0 + sc2) + sh2
    logits = _dot(hf.astype(BF16), _w(wr_ref[...])) + br
    lane = lax.broadcasted_iota(I32, (rows, LANES), 1).astype(F32)
    neg = jnp.float32(-jnp.inf)
    big = jnp.float32(LANES)
    g_valid = (lane >= ROUTE_G0) & (lane < ROUTE_G0 + N_GROUPS)
    gl = jnp.where(g_valid, logits, neg)
    gmax = jnp.max(gl, -1, keepdims=True)
    g_lane = jnp.min(jnp.where(gl == gmax, lane, big), -1, keepdims=True)
    g_w = 1.0 / jnp.sum(jnp.exp(gl - gmax), -1, keepdims=True)
    e_lo = (g_lane - ROUTE_G0) * EXP_PER_GROUP
    el = jnp.where((lane >= e_lo) & (lane < e_lo + EXP_PER_GROUP), logits, neg)
    t1 = jnp.max(el, -1, keepdims=True)
    i1 = jnp.min(jnp.where(el == t1, lane, big), -1, keepdims=True)
    el2 = jnp.where(lane == i1, neg, el)
    t2 = jnp.max(el2, -1, keepdims=True)
    i2 = jnp.min(jnp.where(el2 == t2, lane, big), -1, keepdims=True)
    e2 = jnp.exp(t2 - t1)
    den = 1.0 + e2
    w1 = (1.0 / den) * g_w
    w2 = (e2 / den) * g_w
    route = jnp.where(lane == 0, i1.astype(F32),
                      jnp.where(lane == 1, i2.astype(F32),
                                jnp.where(lane == 2, w1, jnp.where(lane == 3, w2, 0.0))))
    return x1, hf, route


def _mixer_kernel(xn_ref, modn_ref, x_ref, mod_ref, win_ref, winb_ref, bin_ref, walr_ref, balr_ref, wa2_ref, gba_ref,
                  gng_ref, cw_ref, cb_ref, wrg_ref, rba_ref, rbx_ref, lam_ref, pa_ref, pb_ref, wo_ref,
                  l1g_ref, l1b_ref, wr_ref, br_ref,
                  x1_ref, hf_ref, route_ref, routet_ref, convn_ref, hlast_ref, sfin_ref,
                  hbn_scr, hbc_scr, pn_scr, pc_scr, alrn_scr, alrc_scr, xcn_scr, xcc_scr, gaten_scr, gatec_scr,
                  rxbuf, hcar, st_ref, o_scr,
                  *, tiles_per_seq):
    i = pl.program_id(0)
    t = jnp.maximum(i - PIPE_DEPTH, 0)
    j = t % tiles_per_seq
    j1 = jnp.maximum(i - 1, 0) % tiles_per_seq
    tt = x_ref.shape[1]

    @pl.when(i == 0)
    def _no_tile_yet():
        hbc_scr[...] = jnp.zeros_like(hbc_scr)
        pc_scr[...] = jnp.zeros_like(pc_scr)
        alrc_scr[...] = jnp.zeros_like(alrc_scr)
        xcc_scr[...] = jnp.zeros_like(xcc_scr)
        gatec_scr[...] = jnp.zeros_like(gatec_scr)

    @pl.when(j1 == 0)
    def _init_conv():
        rxbuf[0:SUBLANES, :] = jnp.zeros((SUBLANES, D_RNN), F32)

    @pl.when(j == 0)
    def _init():
        hcar[...] = jnp.zeros_like(hcar)
        st_ref[...] = jnp.zeros_like(st_ref)

    def stage0():
        modn = modn_ref[0]
        hbn_scr[...] = (_ln(xn_ref[0]) * (1.0 + modn[1:2]) + modn[0:1]).astype(BF16)

    hbn = hbc_scr[...]

    def _proj_block(lo):
        def run():
            hi = lo + STAGE1_COLS
            pn_scr[:, lo:hi] = _dot(hbn, _w(_win(win_ref, winb_ref, lo, hi)))
        return run

    def _alr_block():
        alrn_scr[...] = _dot(hbn, _w(walr_ref[...]))

    pending = [_proj_block(lo) for lo in range(0, N_MAIN, STAGE1_COLS)] + [_alr_block]

    def pump(count=1):
        for _ in range(min(count, len(pending))):
            pending.pop(0)()

    x = x_ref[0]
    mod = mod_ref[0]
    g1, sh2, sc2 = mod[2:3], mod[3:4], mod[4:5]

    def proj(lo, hi):
        return pc_scr[:, lo:hi] + bin_ref[:, lo:hi]

    pump(O_RY // STAGE1_COLS)

    rx = pn_scr[:, O_RX:O_RX + D_RNN] + bin_ref[:, O_RX:O_RX + D_RNN]
    rxbuf[SUBLANES:SUBLANES + tt, :] = rx
    cw = cw_ref[...]
    xcn = cb_ref[...] + rxbuf[SUBLANES - 3:SUBLANES - 3 + tt, :] * cw[0:1]
    xcn = xcn + rxbuf[SUBLANES - 2:SUBLANES - 2 + tt, :] * cw[1:2]
    xcn = xcn + rxbuf[SUBLANES - 1:SUBLANES - 1 + tt, :] * cw[2:3]
    xcn = xcn + rx * cw[3:4]
    rxbuf[0:SUBLANES, :] = rxbuf[tt:tt + SUBLANES, :]
    xcn_scr[...] = xcn
    xc = xcc_scr[...]

    gelu_ry = jax.nn.gelu(proj(O_RY, O_RY + D_RNN))
    pump()
    a, u = _gates(xc, gatec_scr[:, :D_RNN], gatec_scr[:, D_RNN:], rba_ref[...], rbx_ref[...], lam_ref[...],
                  between=pump)
    s = 1
    a = a.reshape(tt // SUBLANES, SUBLANES, D_RNN)
    u = u.reshape(tt // SUBLANES, SUBLANES, D_RNN)
    sub = lax.broadcasted_iota(I32, (1, SUBLANES, D_RNN), 1)
    while s < SUBLANES:
        keep = sub >= s
        a_s = jnp.where(keep, pltpu.roll(a, s, 1), 1.0)
        u_s = jnp.where(keep, pltpu.roll(u, s, 1), 0.0)
        u = a * u_s + u
        a = a * a_s
        s *= 2
    carry = hcar[0:1, :]
    groups = []
    for g in range(tt // SUBLANES):
        hg = a[g] * carry + u[g]
        groups.append(hg)
        carry = hg[SUBLANES - 1:SUBLANES]
    hseq = jnp.concatenate(groups, axis=0)
    hcar[0:1, :] = carry
    y_a = hseq * gelu_ry

    q = proj(O_Q, O_Q + GLA_DK) * (GLA_DKH ** -0.5)
    k = proj(O_K, O_K + GLA_DK)
    v = proj(O_V, O_V + GLA_DV)
    la = _log_decay_from(alrc_scr[...] + balr_ref[...], wa2_ref, gba_ref)
    pump()
    la = la.reshape(tt // SUBLANES, SUBLANES, GLA_DK)
    subk = lax.broadcasted_iota(I32, (1, SUBLANES, GLA_DK), 1)
    s = 1
    while s < SUBLANES:
        la = la + jnp.where(subk >= s, pltpu.roll(la, s, 1), 0.0)
        s *= 2
    groups = []
    for g in range(tt // SUBLANES):
        blk = la[g] if g % (GLA_CHUNK // SUBLANES) == 0 else la[g] + groups[-1][SUBLANES - 1:SUBLANES]
        groups.append(blk)
    bcum = jnp.concatenate(groups, axis=0)
    pump(2)
    eb = jnp.exp(bcum)
    q_in = (q * eb).astype(BF16)
    k_in = (k * jnp.exp(-bcum)).astype(BF16)
    pump(2)
    stage0()

    tri = (lax.broadcasted_iota(I32, (GLA_CHUNK, GLA_CHUNK), 0)
           >= lax.broadcasted_iota(I32, (GLA_CHUNK, GLA_CHUNK), 1))
    n_chunks = tt // GLA_CHUNK
    nt_dims = (((1,), (1,)), ((), ()))
    attn, d_st, dec, vh_all = {}, {}, {}, {}
    for c in range(n_chunks):
        r0 = c * GLA_CHUNK
        for hh in range(GLA_HEADS):
            kc = slice(hh * GLA_DKH, (hh + 1) * GLA_DKH)
            b = bcum[r0:r0 + GLA_CHUNK, kc]
            btot = b[GLA_CHUNK - 1:GLA_CHUNK, :]
            k_out = (k[r0:r0 + GLA_CHUNK, kc] * jnp.exp(btot - b)).astype(BF16)
            vh = v[r0:r0 + GLA_CHUNK, hh * GLA_DVH:(hh + 1) * GLA_DVH].astype(BF16)
            scores = lax.dot_general(q_in[r0:r0 + GLA_CHUNK, kc], k_in[r0:r0 + GLA_CHUNK, kc], nt_dims,
                                     preferred_element_type=F32)
            attn[c, hh] = jnp.where(tri, scores, 0.0).astype(BF16)
            d_st[c, hh] = lax.dot_general(vh, k_out, (((0,), (0,)), ((), ())), preferred_element_type=F32)
            dec[c, hh] = jnp.exp(btot)
            vh_all[c, hh] = vh
        if c % 2 == 0:
            pump()
    starts = {}
    for hh in range(GLA_HEADS):
        st = st_ref[hh]
        for c in range(n_chunks):
            starts[c, hh] = st.astype(BF16)
            st = st * dec[c, hh] + d_st[c, hh]
        st_ref[hh] = st
    for c in range(n_chunks):
        r0 = c * GLA_CHUNK
        for hh in range(GLA_HEADS):
            qi = q_in[r0:r0 + GLA_CHUNK, hh * GLA_DKH:(hh + 1) * GLA_DKH]
            inter = lax.dot_general(qi, starts[c, hh], nt_dims, preferred_element_type=F32)
            o_scr[r0:r0 + GLA_CHUNK, hh * GLA_DVH:(hh + 1) * GLA_DVH] = _dot(attn[c, hh], vh_all[c, hh]) + inter

    x1, hf, route = _post(x, o_scr[...], proj(O_GG, O_GG + GLA_DV), proj(O_GA, O_GA + D_MODEL),
                          proj(O_GB, O_GB + D_MODEL), y_a, g1, sh2, sc2, gng_ref[...],
                          pa_ref, pb_ref, wo_ref, l1g_ref[...], l1b_ref[...], wr_ref, br_ref[...], between=pump)
    x1_ref[0] = x1
    _store_token_tiles(hf_ref.at[0], hf)
    route_ref[0] = route
    routet_ref[...] = route.T[0:SUBLANES, :]

    pump(len(pending))
    _gate_logits_to(gaten_scr, xcn, wrg_ref)
    pc_scr[:, O_RY:] = pn_scr[:, O_RY:]
    alrc_scr[...] = alrn_scr[...]
    hbc_scr[...] = hbn_scr[...]
    xcc_scr[...] = xcn_scr[...]
    gatec_scr[...] = gaten_scr[...]

    @pl.when((j1 == tiles_per_seq - 1) & (i >= 1))
    def _final_conv():
        convn_ref[0] = rxbuf[SUBLANES - (CONV_W - 1):SUBLANES, :]

    @pl.when((j == tiles_per_seq - 1) & (i >= PIPE_DEPTH))
    def _final():
        hlast_ref[0] = hcar[0:1, :]
        for hh in range(GLA_HEADS):
            sfin_ref[0, hh] = st_ref[hh].T


def _mixer_call(x, mod3, wts):
    b, t, _ = x.shape
    tt = MIX_TT
    assert t % tt == 0 and tt % GLA_CHUNK == 0
    nt = t // tt
    n_tiles = b * nt
    assert nt > 1
    cur = lambda i: jnp.maximum(i - PIPE_DEPTH, 0)
    nxt = lambda i: jnp.minimum(i, n_tiles - 1)
    tok = lambda i: (cur(i) // nt, cur(i) % nt, 0)
    tok_n = lambda i: (nxt(i) // nt, nxt(i) % nt, 0)
    per_b = lambda i: (cur(i) // nt, 0, 0)
    in_specs = [pl.BlockSpec((1, tt, D_MODEL), tok_n),
                pl.BlockSpec((1, 6, D_MODEL), lambda i: (nxt(i) // nt, 0, 0)),
                pl.BlockSpec((1, tt, D_MODEL), tok),
                pl.BlockSpec((1, 6, D_MODEL), per_b)] + [_const_spec(w.shape) for w in wts]
    out_specs = [pl.BlockSpec((1, tt, D_MODEL), tok),
                 pl.BlockSpec((1, tt * SUBLANES, LANES), tok),
                 pl.BlockSpec((1, tt, LANES), tok),
                 pl.BlockSpec((SUBLANES, tt), lambda i: (0, cur(i))),
                 pl.BlockSpec((1, CONV_W - 1, D_RNN), per_b),
                 pl.BlockSpec((1, 1, D_RNN), per_b),
                 pl.BlockSpec((1, GLA_HEADS, GLA_DKH, GLA_DVH), lambda i: (cur(i) // nt, 0, 0, 0))]
    out_shape = [jax.ShapeDtypeStruct((b, t, D_MODEL), F32),
                 jax.ShapeDtypeStruct((b, t * SUBLANES, LANES), F32),
                 jax.ShapeDtypeStruct((b, t, LANES), F32),
                 jax.ShapeDtypeStruct((SUBLANES, b * t), F32),
                 jax.ShapeDtypeStruct((b, CONV_W - 1, D_RNN), F32),
                 jax.ShapeDtypeStruct((b, 1, D_RNN), F32),
                 jax.ShapeDtypeStruct((b, GLA_HEADS, GLA_DKH, GLA_DVH), F32)]
    scratch = [pltpu.VMEM((tt, D_MODEL), BF16), pltpu.VMEM((tt, D_MODEL), BF16),
               pltpu.VMEM((tt, N_MAIN), F32), pltpu.VMEM((tt, N_MAIN), F32),
               pltpu.VMEM((tt, LANES), F32), pltpu.VMEM((tt, LANES), F32),
               pltpu.VMEM((tt, D_RNN), F32), pltpu.VMEM((tt, D_RNN), F32),
               pltpu.VMEM((tt, 2 * D_RNN), F32), pltpu.VMEM((tt, 2 * D_RNN), F32),
               pltpu.VMEM((tt + SUBLANES, D_RNN), F32),
               pltpu.VMEM((SUBLANES, D_RNN), F32),
               pltpu.VMEM((GLA_HEADS, GLA_DVH, GLA_DKH), F32),
               pltpu.VMEM((tt, GLA_DV), F32)]
    return pl.pallas_call(
        functools.partial(_mixer_kernel, tiles_per_seq=nt),
        grid=(n_tiles + PIPE_DEPTH,),
        in_specs=in_specs,
        out_specs=out_specs,
        out_shape=out_shape,
        scratch_shapes=scratch,
        compiler_params=pltpu.CompilerParams(dimension_semantics=("arbitrary",),
                                             vmem_limit_bytes=VMEM_LIMIT_BYTES),
        name="mixer",
    )(x, mod3, x, mod3, *wts)


def _s_pre_kernel(x_ref, mod_ref, sconv_ref, h0_ref, win_ref, winb_ref, bin_ref, walr_ref, balr_ref, wa2_ref, gba_ref,
                  cw_ref, cb_ref, wrg_ref, rba_ref, rbx_ref, lam_ref,
                  convn_ref, hnew_ref, ya_ref, q_ref, k_ref, v_ref, la_ref, gg_ref, ga_ref, gb_ref):
    x = x_ref[...]
    hb = (_ln(x) * (1.0 + mod_ref[1]) + mod_ref[0]).astype(BF16)

    def proj(lo, hi):
        return _dot(hb, _w(_win(win_ref, winb_ref, lo, hi))) + bin_ref[:, lo:hi]

    rx = proj(O_RX, O_RX + D_RNN)
    cw = cw_ref[...]
    xc = cb_ref[...] + sconv_ref[0] * cw[0:1]
    xc = xc + sconv_ref[1] * cw[1:2]
    xc = xc + sconv_ref[2] * cw[2:3]
    xc = xc + rx * cw[3:4]
    convn_ref[0] = sconv_ref[1]
    convn_ref[1] = sconv_ref[2]
    convn_ref[2] = rx
    a, u = _gates(xc, *_gate_logits(xc, wrg_ref, rba_ref[...], rbx_ref[...]), lam_ref[...])
    hnew = u + a * h0_ref[...]
    hnew_ref[...] = hnew
    ya_ref[...] = hnew * jax.nn.gelu(proj(O_RY, O_RY + D_RNN))
    q_ref[...] = proj(O_Q, O_Q + GLA_DK) * (GLA_DKH ** -0.5)
    k_ref[...] = proj(O_K, O_K + GLA_DK)
    v_ref[...] = proj(O_V, O_V + GLA_DV)
    la_ref[...] = _log_decay(hb, walr_ref, balr_ref, wa2_ref, gba_ref)
    gg_ref[...] = proj(O_GG, O_GG + GLA_DV)
    ga_ref[...] = proj(O_GA, O_GA + D_MODEL)
    gb_ref[...] = proj(O_GB, O_GB + D_MODEL)


def _s_pre_call(x_s, mod_s, sconv, h0, wts):
    n = x_s.shape[0]
    full = lambda shape: pl.BlockSpec(shape, lambda i: (0,) * len(shape))
    ins = [x_s, mod_s, sconv, h0] + list(wts)
    shapes = [(CONV_W - 1, n, D_RNN), (n, D_RNN), (n, D_RNN), (n, GLA_DK), (n, GLA_DK), (n, GLA_DV),
              (n, GLA_DK), (n, GLA_DV), (n, D_MODEL), (n, D_MODEL)]
    return pl.pallas_call(
        _s_pre_kernel,
        grid=(1,),
        in_specs=[full(a.shape) for a in ins],
        out_specs=[full(s) for s in shapes],
        out_shape=[jax.ShapeDtypeStruct(s, F32) for s in shapes],
        compiler_params=pltpu.CompilerParams(dimension_semantics=("arbitrary",),
                                             vmem_limit_bytes=VMEM_LIMIT_BYTES),
        name="s_pre",
    )(*ins)


def _to_column(row, n):
    eye = lax.broadcasted_iota(I32, (n, n), 0) == lax.broadcasted_iota(I32, (n, n), 1)
    return jnp.sum(jnp.where(eye, jnp.broadcast_to(row, (n, n)), 0.0), axis=1, keepdims=True)


def _s_state_kernel(s_ref, q_ref, k_ref, v_ref, la_ref, snew_ref, o_ref):
    sb = s_ref.shape[0]
    qb = q_ref[...].astype(BF16)
    dec = jnp.exp(la_ref[...])
    kk = k_ref[...]
    vv = v_ref[...]
    for si in range(sb):
        for hh in range(GLA_HEADS):
            kc = slice(hh * GLA_DKH, (hh + 1) * GLA_DKH)
            vc = slice(hh * GLA_DVH, (hh + 1) * GLA_DVH)
            d_col = _to_column(dec[si:si + 1, kc], GLA_DKH)
            k_col = _to_column(kk[si:si + 1, kc], GLA_DKH)
            s_new = s_ref[si, hh] * d_col + k_col * vv[si:si + 1, vc]
            snew_ref[si, hh] = s_new
            o_all = _dot(qb[:, kc], s_new.astype(BF16))
            o_ref[si:si + 1, vc] = o_all[si:si + 1, :]


def _s_state_call(state, q, k, v, la):
    n = state.shape[0]
    sb = SAMPLE_SB
    assert n % sb == 0
    blk = lambda w: pl.BlockSpec((sb, w), lambda i: (i, 0))
    st_spec = pl.BlockSpec((sb, GLA_HEADS, GLA_DKH, GLA_DVH), lambda i: (i, 0, 0, 0))
    return pl.pallas_call(
        _s_state_kernel,
        grid=(n // sb,),
        in_specs=[st_spec, blk(GLA_DK), blk(GLA_DK), blk(GLA_DV), blk(GLA_DK)],
        out_specs=[st_spec, blk(GLA_DV)],
        out_shape=[jax.ShapeDtypeStruct(state.shape, F32), jax.ShapeDtypeStruct((n, GLA_DV), F32)],
        compiler_params=pltpu.CompilerParams(dimension_semantics=("arbitrary",),
                                             vmem_limit_bytes=VMEM_LIMIT_BYTES),
        name="s_state",
    )(state, q, k, v, la)


def _s_post_kernel(x_ref, mod_ref, o_ref, gg_ref, ga_ref, gb_ref, ya_ref, gng_ref, pa_ref, pb_ref, wo_ref,
                   l1g_ref, l1b_ref, wr_ref, br_ref, x1_ref, hf_ref, route_ref, routet_ref):
    x1, hf, route = _post(x_ref[...], o_ref[...], gg_ref[...], ga_ref[...], gb_ref[...], ya_ref[...],
                          mod_ref[2], mod_ref[3], mod_ref[4], gng_ref[...], pa_ref, pb_ref, wo_ref,
                          l1g_ref[...], l1b_ref[...], wr_ref, br_ref[...])
    x1_ref[...] = x1
    _store_token_tiles(hf_ref, hf)
    route_ref[...] = route
    routet_ref[...] = route.T[0:SUBLANES, :]


def _s_post_call(x_s, mod_s, o, gg, ga, gb, ya, wts):
    n = x_s.shape[0]
    full = lambda shape: pl.BlockSpec(shape, lambda i: (0,) * len(shape))
    ins = [x_s, mod_s, o, gg, ga, gb, ya] + list(wts)
    shapes = [(n, D_MODEL), (n * SUBLANES, LANES), (n, LANES), (SUBLANES, n)]
    return pl.pallas_call(
        _s_post_kernel,
        grid=(1,),
        in_specs=[full(a.shape) for a in ins],
        out_specs=[full(s) for s in shapes],
        out_shape=[jax.ShapeDtypeStruct(s, F32) for s in shapes],
        compiler_params=pltpu.CompilerParams(dimension_semantics=("arbitrary",),
                                             vmem_limit_bytes=VMEM_LIMIT_BYTES),
        name="s_post",
    )(*ins)


def _rank_kernel(eid_ref, rank_ref, cnt_ref, carry):
    n_chunks = eid_ref.shape[0]
    carry[...] = jnp.zeros_like(carry)
    e_iota = lax.broadcasted_iota(I32, (N_EXPERTS, LANES), 0)
    upper = (lax.broadcasted_iota(I32, (LANES, LANES), 0)
             < lax.broadcasted_iota(I32, (LANES, LANES), 1)).astype(BF16)

    def body(c, _):
        ids = eid_ref[c]
        oh0 = (e_iota == ids[0:1, :])
        oh1 = (e_iota == ids[1:2, :])
        both = (oh0 | oh1).astype(F32)
        before = _dot(both.astype(BF16), upper) + carry[:, 0:1]
        r0 = jnp.sum(jnp.where(oh0, before, 0.0), axis=0, keepdims=True)
        r1 = jnp.sum(jnp.where(oh1, before, 0.0), axis=0, keepdims=True)
        rank_ref[c] = jnp.concatenate([r0, r1], axis=0).astype(I32)
        carry[...] = carry[...] + jnp.sum(both, axis=1, keepdims=True)
        return 0

    lax.fori_loop(0, n_chunks, body, 0, unroll=3)
    cnt_ref[...] = carry[...].astype(I32)


def _rank_call(eid3):
    n_chunks = eid3.shape[0]
    full = lambda shape: pl.BlockSpec(shape, lambda i: (0,) * len(shape))
    return pl.pallas_call(
        _rank_kernel,
        grid=(1,),
        in_specs=[full(eid3.shape)],
        out_specs=[full(eid3.shape), full((N_EXPERTS, LANES))],
        out_shape=[jax.ShapeDtypeStruct(eid3.shape, I32), jax.ShapeDtypeStruct((N_EXPERTS, LANES), I32)],
        scratch_shapes=[pltpu.VMEM((N_EXPERTS, LANES), F32)],
        compiler_params=pltpu.CompilerParams(dimension_semantics=("arbitrary",)),
        name="rank",
    )(eid3)


def _dispatch_kernel(pends_ref, d0_ref, d1_ref, src_p_ref, src_s_ref, buf_ref, sem, zeros_ref, zsem, *,
                     n_p_chunks):
    i = pl.program_id(0)
    n_tiles = buf_ref.shape[0] // (MOE_TM * SUBLANES)

    @pl.when(i == 0)
    def _clear():
        zeros_ref[...] = jnp.zeros_like(zeros_ref)

        def tile_copy(row0):
            return pltpu.make_async_copy(zeros_ref, buf_ref.at[_tok_rows(row0, MOE_TM)], zsem)

        def nonempty(e):
            return pends_ref[e] > jnp.where(e == 0, 0, pends_ref[jnp.maximum(e - 1, 0)])

        def start_tail(e, _):
            @pl.when(nonempty(e))
            def _():
                tile_copy(pends_ref[e] - MOE_TM).start()
            return 0

        def wait_tail(e, _):
            @pl.when(nonempty(e))
            def _():
                tile_copy(0).wait()
            return 0

        def start_unused(j, _):
            tile_copy(j * MOE_TM).start()
            return 0

        def wait_unused(j, _):
            tile_copy(0).wait()
            return 0

        first_unused = pends_ref[N_EXPERTS - 1] // MOE_TM
        lax.fori_loop(0, N_EXPERTS, start_tail, 0)
        lax.fori_loop(first_unused, n_tiles, start_unused, 0)
        lax.fori_loop(0, N_EXPERTS, wait_tail, 0)
        lax.fori_loop(first_unused, n_tiles, wait_unused, 0)

    def scatter_rows(src_ref):
        n = src_ref.shape[0] // SUBLANES

        def start_rows(t, _):
            for k, dest_ref in enumerate((d0_ref, d1_ref)):
                pltpu.make_async_copy(src_ref.at[_tok_rows(t)], buf_ref.at[_tok_rows(dest_ref[t])],
                                      sem).start(priority=k)
            return 0

        lax.fori_loop(0, n, start_rows, 0, unroll=4)
        for _ in range(2):
            pltpu.make_async_copy(src_ref, buf_ref.at[_tok_rows(0, n)], sem).wait()

    @pl.when(i < n_p_chunks)
    def _prompt():
        scatter_rows(src_p_ref)

    @pl.when(i >= n_p_chunks)
    def _sample():
        scatter_rows(src_s_ref)


def _dispatch_call(pends, dest0, dest1, src_p, src_s, n_rows):
    chunk_rows = ROW_CHUNK * SUBLANES
    assert src_p.shape[0] % chunk_rows == 0 and src_s.shape[0] % chunk_rows == 0 and n_rows % MOE_TM == 0
    n_p_chunks = src_p.shape[0] // chunk_rows
    n_s_chunks = src_s.shape[0] // chunk_rows
    tile = (chunk_rows, LANES)
    in_specs = [pl.BlockSpec((ROW_CHUNK,), lambda i, pe: (i,), memory_space=pltpu.SMEM),
                pl.BlockSpec((ROW_CHUNK,), lambda i, pe: (i,), memory_space=pltpu.SMEM),
                pl.BlockSpec(tile, lambda i, pe: (jnp.minimum(i, n_p_chunks - 1), 0)),
                pl.BlockSpec(tile, lambda i, pe: (jnp.maximum(i - n_p_chunks, 0), 0))]
    scratch = [pltpu.SemaphoreType.DMA(()), pltpu.VMEM((MOE_TM * SUBLANES, LANES), F32),
               pltpu.SemaphoreType.DMA(())]
    return pl.pallas_call(
        functools.partial(_dispatch_kernel, n_p_chunks=n_p_chunks),
        grid_spec=pltpu.PrefetchScalarGridSpec(
            num_scalar_prefetch=1, grid=(n_p_chunks + n_s_chunks,), in_specs=in_specs,
            out_specs=pl.BlockSpec(memory_space=pl.ANY), scratch_shapes=scratch),
        out_shape=jax.ShapeDtypeStruct((n_rows * SUBLANES, LANES), F32),
        compiler_params=pltpu.CompilerParams(dimension_semantics=("arbitrary",), has_side_effects=True),
        name="dispatch",
    )(pends, dest0, dest1, src_p, src_s)


def _expert_kernel(blk_e_ref, n_used_ref, x_ref, wg0_ref, wu0_ref, wd0_ref, wg1_ref, wu1_ref, wd1_ref, y_ref,
                   wgb, wub, wdb):
    i = pl.program_id(0)
    rows = MOE_TM * SUBLANES
    halves = ((0, wg0_ref, wu0_ref, wd0_ref), (1, wg1_ref, wu1_ref, wd1_ref))

    for h, wg_ref, wu_ref, wd_ref in halves:
        e_now = blk_e_ref[2 * i + h]
        e_before = blk_e_ref[jnp.maximum(2 * i + h - 2, 0)]

        @pl.when((i == 0) | (e_now != e_before))
        def _cast_weights(h=h, wg_ref=wg_ref, wu_ref=wu_ref, wd_ref=wd_ref):
            wgb[h] = wg_ref[0].astype(BF16)
            wub[h] = wu_ref[0].astype(BF16)
            wdb[h] = wd_ref[0].astype(BF16)

    def swiglu(x_view, y_view, h):
        xb = _load_token_tiles(x_view).astype(BF16)
        g = _dot(xb, wgb[h])
        u = _dot(xb, wub[h])
        act = (jax.nn.silu(g) * u).astype(BF16)
        _store_token_tiles(y_view, _dot(act, wdb[h]))

    used = 2 * i < n_used_ref[0]
    same = blk_e_ref[2 * i] == blk_e_ref[2 * i + 1]

    @pl.when(used & same)
    def _one_expert():
        swiglu(x_ref, y_ref, 0)

    @pl.when(used & jnp.logical_not(same))
    def _two_experts():
        for h in range(2):
            swiglu(x_ref.at[pl.ds(h * rows, rows)], y_ref.at[pl.ds(h * rows, rows)], h)

    @pl.when(jnp.logical_not(used))
    def _skip():
        y_ref[...] = jnp.zeros_like(y_ref)


def _expert_call(blk_e, n_used, xbuf, we_gate, we_up, we_down):
    p = xbuf.shape[0] // SUBLANES
    assert p % (2 * MOE_TM) == 0 and blk_e.shape[0] == p // MOE_TM
    tile_spec = pl.BlockSpec((2 * MOE_TM * SUBLANES, LANES), lambda i, be, nu: (i, 0))
    w_in_spec = lambda h: pl.BlockSpec((1, D_MODEL, D_EXPERT), lambda i, be, nu: (be[2 * i + h], 0, 0))
    w_out_spec = lambda h: pl.BlockSpec((1, D_EXPERT, D_MODEL), lambda i, be, nu: (be[2 * i + h], 0, 0))
    grid_spec = pltpu.PrefetchScalarGridSpec(
        num_scalar_prefetch=2,
        grid=(p // (2 * MOE_TM),),
        in_specs=[tile_spec, w_in_spec(0), w_in_spec(0), w_out_spec(0), w_in_spec(1), w_in_spec(1), w_out_spec(1)],
        out_specs=tile_spec,
        scratch_shapes=[pltpu.VMEM((2, D_MODEL, D_EXPERT), BF16), pltpu.VMEM((2, D_MODEL, D_EXPERT), BF16),
                        pltpu.VMEM((2, D_EXPERT, D_MODEL), BF16)],
    )
    return pl.pallas_call(
        _expert_kernel,
        grid_spec=grid_spec,
        out_shape=jax.ShapeDtypeStruct((p * SUBLANES, LANES), F32),
        compiler_params=pltpu.CompilerParams(dimension_semantics=("arbitrary",),
                                             vmem_limit_bytes=VMEM_LIMIT_BYTES),
        name="experts",
    )(blk_e, n_used, xbuf, we_gate, we_up, we_down, we_gate, we_up, we_down)


def _final_kernel(d0_ref, d1_ref, d0n_ref, d1n_ref, x1_ref, ybuf_ref, route_ref, g2_ref, l2g_ref, l2b_ref, out_ref,
                  yg, sems):
    n = x1_ref.shape[0]
    i = pl.program_id(0)
    slot = i % 2

    def gather(idx_refs, s):
        def start_rows(t, _):
            for k, idx_ref in enumerate(idx_refs):
                pltpu.make_async_copy(ybuf_ref.at[_tok_rows(idx_ref[t])], yg.at[s, k, _tok_rows(t)],
                                      sems.at[s]).start(priority=k)
            return 0
        lax.fori_loop(0, n, start_rows, 0, unroll=4)

    @pl.when(i == 0)
    def _first():
        gather((d0_ref, d1_ref), 0)

    @pl.when(i + 1 < pl.num_programs(0))
    def _prefetch():
        gather((d0n_ref, d1n_ref), 1 - slot)

    for k in range(2):
        pltpu.make_async_copy(ybuf_ref.at[_tok_rows(0, n)], yg.at[slot, k], sems.at[slot]).wait()
    route = route_ref[...]
    ff = (_load_token_tiles(yg.at[slot, 0]) * route[:, 2:3]
          + _load_token_tiles(yg.at[slot, 1]) * route[:, 3:4])
    out_ref[...] = _ln(ALPHA * x1_ref[...] + g2_ref[0] * ff) * l2g_ref[...] + l2b_ref[...]


def _final_call(dest0, dest1, first_token, x1, ybuf, route, g2, ln2_g, ln2_b, rows_per_g2, tile):
    n = x1.shape[0]
    assert n % tile == 0 and rows_per_g2 % tile == 0 and first_token % tile == 0
    per = rows_per_g2 // tile
    g2_rows = g2.shape[1]
    steps = n // tile
    off = first_token // tile
    idx = lambda: pl.BlockSpec((tile,), lambda i: (off + i,), memory_space=pltpu.SMEM)
    idx_next = lambda: pl.BlockSpec((tile,), lambda i: (off + jnp.minimum(i + 1, steps - 1),),
                                    memory_space=pltpu.SMEM)
    return pl.pallas_call(
        _final_kernel,
        grid=(steps,),
        in_specs=[idx(), idx(), idx_next(), idx_next(),
                  pl.BlockSpec((tile, D_MODEL), lambda i: (i, 0)),
                  pl.BlockSpec(memory_space=pl.ANY),
                  pl.BlockSpec((tile, LANES), lambda i: (i, 0)),
                  pl.BlockSpec((1, g2_rows, D_MODEL), lambda i: (i // per, 0, 0)),
                  pl.BlockSpec((1, D_MODEL), lambda i: (0, 0)),
                  pl.BlockSpec((1, D_MODEL), lambda i: (0, 0))],
        out_specs=pl.BlockSpec((tile, D_MODEL), lambda i: (i, 0)),
        out_shape=jax.ShapeDtypeStruct((n, D_MODEL), F32),
        scratch_shapes=[pltpu.VMEM((2, 2, tile * SUBLANES, LANES), F32), pltpu.SemaphoreType.DMA((2,))],
        compiler_params=pltpu.CompilerParams(dimension_semantics=("arbitrary",)),
        name="final",
    )(dest0, dest1, dest0, dest1, x1, ybuf, route, g2, ln2_g, ln2_b)


def _block_diag_gate_weights(wa, wx):
    def bd(w):
        w = w.reshape(N_RG_TILES, RG_GROUP, RG_BW, RG_BW)
        eye = jnp.eye(RG_GROUP, dtype=w.dtype)
        return jnp.einsum('tgcd,gh->tgchd', w, eye).reshape(N_RG_TILES, MXU_DIM, MXU_DIM)
    return jnp.concatenate([bd(wa), bd(wx)], axis=2)


def kernel(x_prompt, x_sample, state_conv, state_rglru, state_gla, c_prompt, c_sample, w_mod, b_mod, w_in, b_in,
           conv_w, conv_b, rg_wa, rg_ba, rg_wx, rg_bx, rg_lambda, gla_wa2, gla_ba, gla_norm_g, p_a, p_b, w_o,
           ln1_g, ln1_b, w_grp, b_grp, w_exp, b_exp, we_gate, we_up, we_down, ln2_g, ln2_b):
    assert w_mod.shape[0] == DEPTH == 1
    bp, tp, _ = x_prompt.shape
    ns = x_sample.shape[0]
    n_p = bp * tp
    n_tok = n_p + ns
    row = lambda a: a.reshape(1, -1)

    lo = 2 * D_RNN + 2 * GLA_DK + 2 * GLA_DV
    w_in0 = w_in[0]
    assert lo == O_GA
    win_a = _pack_rows(w_in0, lo)
    win_b = _pack_rows(w_in0[:, lo + GLA_RANK:])
    bin_main = row(jnp.concatenate([b_in[0, :lo], b_in[0, lo + GLA_RANK:]]))
    walr = _pack_rows(jnp.pad(w_in0[:, lo:lo + GLA_RANK], ((0, 0), (0, LANES - GLA_RANK))))
    balr = row(jnp.pad(b_in[0, lo:lo + GLA_RANK], (0, LANES - GLA_RANK)))
    wa2 = _pack_rows(jnp.pad(gla_wa2[0], ((0, LANES - GLA_RANK), (0, 0))))
    wrg = _pack_rows(_block_diag_gate_weights(rg_wa[0], rg_wx[0]).reshape(RG_BLOCKS * RG_BW, 2 * MXU_DIM))
    wrg = wrg.reshape(N_RG_TILES, MXU_DIM // 2, 2 * MXU_DIM)
    w_route = _pack_rows(jnp.pad(jnp.concatenate([w_exp[0], w_grp[0]], axis=1),
                                 ((0, 0), (0, LANES - N_EXPERTS - N_GROUPS))))
    b_route = row(jnp.pad(jnp.concatenate([b_exp[0], b_grp[0]]), (0, LANES - N_EXPERTS - N_GROUPS)))
    pa, pb, wo = _pack_rows(p_a[0]), _pack_rows(p_b[0]), _pack_rows(w_o[0])
    pre_w = [win_a, win_b, bin_main, walr, balr, wa2, row(gla_ba[0])]
    rec_w = [conv_w[0], row(conv_b[0]), wrg, row(rg_ba[0]), row(rg_bx[0]), row(rg_lambda[0])]
    post_w = [row(gla_norm_g[0]), pa, pb, wo, row(ln1_g[0]), row(ln1_b[0]), w_route, b_route]

    mod = _mod_call(jnp.concatenate([c_prompt, c_sample], axis=0), w_mod[0], row(b_mod[0]))
    mod_p = mod[:bp].reshape(bp, 6, D_MODEL)
    mod_s = mod[bp:].reshape(ns, 6, D_MODEL).transpose(1, 0, 2)

    mix_w = pre_w + [post_w[0]] + rec_w + post_w[1:]
    x1_p, hf_p, route_p, routet_p, conv_p, h_p, s_p = _mixer_call(x_prompt, mod_p, mix_w)

    xs = x_sample.reshape(ns, D_MODEL)
    sconv = state_conv[0].transpose(1, 0, 2)
    conv_s, h_s, ya_s, q_s, k_s, v_s, la_s, gg_s, ga_s, gb_s = _s_pre_call(
        xs, mod_s, sconv, state_rglru[0], pre_w + rec_w)
    s_s, o_s = _s_state_call(state_gla[0], q_s, k_s, v_s, la_s)
    x1_s, hf_s, route_s, routet_s = _s_post_call(xs, mod_s, o_s, gg_s, ga_s, gb_s, ya_s, post_w)

    route_pf = route_p.reshape(n_p, LANES)
    eid = jnp.concatenate([routet_p[:2], routet_s[:2]], axis=1).astype(I32)
    assert n_tok % LANES == 0
    n_chunks = n_tok // LANES
    rank3, cnt = _rank_call(eid.reshape(2, n_chunks, LANES).transpose(1, 0, 2))
    rank = rank3.transpose(1, 0, 2).reshape(2, n_tok)
    counts = cnt[:, 0]
    pcounts = (counts + MOE_TM - 1) // MOE_TM * MOE_TM
    pends = jnp.cumsum(pcounts)
    pstarts = pends - pcounts
    experts = jnp.arange(N_EXPERTS, dtype=I32)[:, None, None]
    start_of = jnp.sum(jnp.where(eid[None] == experts, pstarts.astype(I32)[:, None, None], 0), axis=0)
    dest = start_of + rank
    dest0, dest1 = dest[0], dest[1]
    n_tiles = -(-(2 * n_tok + N_EXPERTS * (MOE_TM - 1)) // (2 * MOE_TM)) * 2
    tile_start = jnp.arange(n_tiles, dtype=I32) * MOE_TM
    blk_e = jnp.minimum(jnp.sum(pends[None, :] <= tile_start[:, None], axis=1), N_EXPERTS - 1).astype(I32)
    n_used = (pends[-1] // MOE_TM).astype(I32).reshape(1)

    pends32 = pends.astype(I32)
    xbuf = _dispatch_call(pends32, dest0, dest1, hf_p.reshape(n_p * SUBLANES, LANES), hf_s, n_tiles * MOE_TM)
    ybuf = _expert_call(blk_e, n_used, xbuf, we_gate[0], we_up[0], we_down[0])

    y_p = _final_call(dest0, dest1, 0, x1_p.reshape(n_p, D_MODEL), ybuf, route_pf, mod_p[:, 5:6, :],
                      row(ln2_g[0]), row(ln2_b[0]), tp, ROW_CHUNK)
    y_s = _final_call(dest0, dest1, n_p, x1_s, ybuf, route_s, mod_s[5][None], row(ln2_g[0]), row(ln2_b[0]), ns, ns)

    return (y_p.reshape(bp, tp, D_MODEL), y_s.reshape(ns, 1, D_MODEL),
            conv_p[None], h_p.reshape(1, bp, D_RNN), s_p[None],
            conv_s.transpose(1, 0, 2)[None], h_s[None], s_s[None])
```

```python
import functools

import jax
import jax.numpy as jnp
from jax import lax
from jax.experimental import pallas as pl
from jax.experimental.pallas import tpu as pltpu

F32 = jnp.float32
BF16 = jnp.bfloat16
I32 = jnp.int32

D_MODEL = 1024
D_RNN = D_MODEL
RG_BLOCKS = 16
RG_BW = D_RNN // RG_BLOCKS
CONV_W = 4
RG_C = 8.0
GLA_HEADS = 4
GLA_DK = D_MODEL // 2
GLA_DV = D_MODEL
GLA_DKH = GLA_DK // GLA_HEADS
GLA_DVH = GLA_DV // GLA_HEADS
GLA_RANK = 16
GLA_TAU = 16.0
GLA_CHUNK = 64
N_GROUPS = 4
EXP_PER_GROUP = 8
N_EXPERTS = N_GROUPS * EXP_PER_GROUP
D_EXPERT = 512
DEPTH = 1
ALPHA = (2.0 * DEPTH) ** 0.25
LN_EPS = 1e-5
RMS_EPS = 1e-6

LANES = 128
SUBLANES = 8
MXU_DIM = 256
VMEM_LIMIT_BYTES = 56 * 1024 * 1024

MIX_TT = 256
STAGE1_COLS = 512
PIPE_DEPTH = 2
MOE_TM = 256
ROW_CHUNK = 128
SAMPLE_SB = 8
RG_GROUP = MXU_DIM // RG_BW
N_RG_TILES = RG_BLOCKS // RG_GROUP
N_MAIN = 2 * D_RNN + 2 * GLA_DK + 2 * GLA_DV + 2 * D_MODEL
ROUTE_G0 = N_EXPERTS

O_RX, O_RY, O_Q, O_K, O_V, O_GG, O_GA, O_GB = 0, 1024, 2048, 2560, 3072, 4096, 5120, 6144


def _ln(x):
    mu = jnp.mean(x, -1, keepdims=True)
    xc = x - mu
    var = jnp.mean(xc * xc, -1, keepdims=True)
    return xc * lax.rsqrt(var + LN_EPS)


def _dot(a, b):
    return jnp.dot(a, b, preferred_element_type=F32)


def _pack_kernel(w_ref, o_ref):
    o_ref[...] = pltpu.bitcast(w_ref[...].astype(BF16), jnp.uint32)


def _pack_rows(w, n=None):
    k = w.shape[0]
    n = w.shape[1] if n is None else n
    cn = min(n, 4 * LANES)
    assert k % (2 * SUBLANES) == 0 and n % cn == 0
    return pl.pallas_call(
        _pack_kernel,
        grid=(n // cn,),
        in_specs=[pl.BlockSpec((k, cn), lambda i: (0, i))],
        out_specs=pl.BlockSpec((k // 2, cn), lambda i: (0, i)),
        out_shape=jax.ShapeDtypeStruct((k // 2, n), jnp.uint32),
        compiler_params=pltpu.CompilerParams(dimension_semantics=("arbitrary",)),
        name="pack",
    )(w)


def _win(win_ref, winb_ref, lo, hi):
    if hi <= O_GA:
        return win_ref[:, lo:hi]
    assert lo >= O_GA
    return winb_ref[:, lo - O_GA:hi - O_GA]


def _w(packed):
    return pltpu.bitcast(packed, BF16)


def _expm1(x):
    u = jnp.exp(x)
    small = (u - 1.0) * x / jnp.log(u)
    return jnp.where(u == 1.0, x, jnp.where(jnp.abs(x) < 0.5, small, u - 1.0))


def _const_spec(shape):
    nd = len(shape)
    return pl.BlockSpec(shape, lambda *_: (0,) * nd, pipeline_mode=pl.Buffered(1))


assert D_MODEL == SUBLANES * LANES


def _tok_rows(t, count=1):
    return pl.ds(pl.multiple_of(t * SUBLANES, SUBLANES), count * SUBLANES)


def _store_token_tiles(ref, rows):
    n = rows.shape[0]
    for s in range(SUBLANES):
        ref[pl.ds(s, n, stride=SUBLANES), :] = rows[:, s * LANES:(s + 1) * LANES]


def _load_token_tiles(ref):
    n = ref.shape[0] // SUBLANES
    return jnp.concatenate([ref[pl.ds(s, n, stride=SUBLANES), :] for s in range(SUBLANES)], axis=1)


def _mod_kernel(c_ref, w_ref, b_ref, o_ref):
    c = c_ref[...]
    s = jax.nn.silu(c)
    o_ref[...] = _dot(s.astype(BF16), w_ref[...].astype(BF16)) + b_ref[...]


def _mod_call(c_all, w_mod, b_mod):
    n = c_all.shape[0]
    tn = D_MODEL
    return pl.pallas_call(
        _mod_kernel,
        grid=(6 * D_MODEL // tn,),
        in_specs=[pl.BlockSpec((n, D_MODEL), lambda i: (0, 0)),
                  pl.BlockSpec((D_MODEL, tn), lambda i: (0, i)),
                  pl.BlockSpec((1, tn), lambda i: (0, i))],
        out_specs=pl.BlockSpec((n, tn), lambda i: (0, i)),
        out_shape=jax.ShapeDtypeStruct((n, 6 * D_MODEL), F32),
        compiler_params=pltpu.CompilerParams(dimension_semantics=("arbitrary",)),
        name="mod",
    )(c_all, w_mod, b_mod)


def _no_op():
    pass


def _gate_logits(xc, wrg_ref, rba, rbx):
    xcb = xc.astype(BF16)
    parts = [_dot(xcb[:, c * MXU_DIM:(c + 1) * MXU_DIM], _w(wrg_ref[c])) for c in range(N_RG_TILES)]
    r_pre = jnp.concatenate([p[:, :MXU_DIM] for p in parts], axis=1)
    i_pre = jnp.concatenate([p[:, MXU_DIM:] for p in parts], axis=1)
    return r_pre, i_pre, rba, rbx


def _gate_logits_to(out_ref, xc, wrg_ref):
    xcb = xc.astype(BF16)
    for c in range(N_RG_TILES):
        p = _dot(xcb[:, c * MXU_DIM:(c + 1) * MXU_DIM], _w(wrg_ref[c]))
        out_ref[:, c * MXU_DIM:(c + 1) * MXU_DIM] = p[:, :MXU_DIM]
        out_ref[:, D_RNN + c * MXU_DIM:D_RNN + (c + 1) * MXU_DIM] = p[:, MXU_DIM:]


def _gates(xc, r_pre, i_pre, rba, rbx, lam, between=_no_op):
    r = jax.nn.sigmoid(r_pre + rba)
    between()
    ig = jax.nn.sigmoid(i_pre + rbx)
    log_a = -RG_C * r * jax.nn.softplus(-lam)
    a = jnp.exp(log_a)
    between()
    u = jnp.sqrt(-_expm1(2.0 * log_a)) * (ig * xc)
    between()
    return a, u


def _low_rank(hb, walr_ref, balr_ref):
    return _dot(hb, _w(walr_ref[...])) + balr_ref[...]


def _log_decay_from(alr, wa2_ref, gba_ref):
    return jax.nn.log_sigmoid(_dot(alr.astype(BF16), _w(wa2_ref[...])) + gba_ref[...]) / GLA_TAU


def _log_decay(hb, walr_ref, balr_ref, wa2_ref, gba_ref):
    return _log_decay_from(_low_rank(hb, walr_ref, balr_ref), wa2_ref, gba_ref)


def _post(x, o, gg, ga, gb, y_a, g1, sh2, sc2, gng, pa_ref, pb_ref, wo_ref, l1g, l1b, wr_ref, br,
          between=_no_op):
    rows = x.shape[0]
    heads = []
    for hh in range(GLA_HEADS):
        oh = o[:, hh * GLA_DVH:(hh + 1) * GLA_DVH]
        ms = jnp.mean(oh * oh, -1, keepdims=True)
        heads.append(oh * lax.rsqrt(ms + RMS_EPS) * gng)
    y_b = jnp.concatenate(heads, axis=1) * jax.nn.silu(gg)
    merged = (jax.nn.sigmoid(ga) * _dot(y_a.astype(BF16), _w(pa_ref[...]))
              + jax.nn.sigmoid(gb) * _dot(y_b.astype(BF16), _w(pb_ref[...])))
    mix = _dot(merged.astype(BF16), _w(wo_ref[...]))
    between()
    x1 = _ln(ALPHA * x + g1 * mix) * l1g + l1b
    between()
    hf = _ln(x1) * (1.0 + sc2) + sh2
    between()
    logits = _dot(hf.astype(BF16), _w(wr_ref[...])) + br
    lane = lax.broadcasted_iota(I32, (rows, LANES), 1).astype(F32)
    neg = jnp.float32(-jnp.inf)
    big = jnp.float32(LANES)
    g_valid = (lane >= ROUTE_G0) & (lane < ROUTE_G0 + N_GROUPS)
    gl = jnp.where(g_valid, logits, neg)
    gmax = jnp.max(gl, -1, keepdims=True)
    g_lane = jnp.min(jnp.where(gl == gmax, lane, big), -1, keepdims=True)
    g_w = 1.0 / jnp.sum(jnp.exp(gl - gmax), -1, keepdims=True)
    e_lo = (g_lane - ROUTE_G0) * EXP_PER_GROUP
    el = jnp.where((lane >= e_lo) & (lane < e_lo + EXP_PER_GROUP), logits, neg)
    t1 = jnp.max(el, -1, keepdims=True)
    i1 = jnp.min(jnp.where(el == t1, lane, big), -1, keepdims=True)
    el2 = jnp.where(lane == i1, neg, el)
    t2 = jnp.max(el2, -1, keepdims=True)
    i2 = jnp.min(jnp.where(el2 == t2, lane, big), -1, keepdims=True)
    e2 = jnp.exp(t2 - t1)
    den = 1.0 + e2
    w1 = (1.0 / den) * g_w
    w2 = (e2 / den) * g_w
    route = jnp.where(lane == 0, i1.astype(F32),
                      jnp.where(lane == 1, i2.astype(F32),
                                jnp.where(lane == 2, w1, jnp.where(lane == 3, w2, 0.0))))
    return x1, hf, route


def _mixer_kernel(xn_ref, modn_ref, x_ref, mod_ref, win_ref, winb_ref, bin_ref, walr_ref, balr_ref, wa2_ref, gba_ref,
                  gng_ref, cw_ref, cb_ref, wrg_ref, rba_ref, rbx_ref, lam_ref, pa_ref, pb_ref, wo_ref,
                  l1g_ref, l1b_ref, wr_ref, br_ref,
                  x1_ref, hf_ref, route_ref, routet_ref, convn_ref, hlast_ref, sfin_ref,
                  hbn_scr, hbc_scr, pn_scr, pc_scr, alrn_scr, alrc_scr, xcn_scr, xcc_scr, gaten_scr, gatec_scr,
                  rxbuf, hcar, st_ref, o_scr,
                  *, tiles_per_seq):
    i = pl.program_id(0)
    t = jnp.maximum(i - PIPE_DEPTH, 0)
    j = t % tiles_per_seq
    j1 = jnp.maximum(i - 1, 0) % tiles_per_seq
    tt = x_ref.shape[1]

    @pl.when(i == 0)
    def _no_tile_yet():
        hbc_scr[...] = jnp.zeros_like(hbc_scr)
        pc_scr[...] = jnp.zeros_like(pc_scr)
        alrc_scr[...] = jnp.zeros_like(alrc_scr)
        xcc_scr[...] = jnp.zeros_like(xcc_scr)
        gatec_scr[...] = jnp.zeros_like(gatec_scr)

    @pl.when(j1 == 0)
    def _init_conv():
        rxbuf[0:SUBLANES, :] = jnp.zeros((SUBLANES, D_RNN), F32)

    @pl.when(j == 0)
    def _init():
        hcar[...] = jnp.zeros_like(hcar)
        st_ref[...] = jnp.zeros_like(st_ref)

    def stage0():
        modn = modn_ref[0]
        hbn_scr[...] = (_ln(xn_ref[0]) * (1.0 + modn[1:2]) + modn[0:1]).astype(BF16)

    hbn = hbc_scr[...]

    def _proj_block(lo):
        def run():
            hi = lo + STAGE1_COLS
            pn_scr[:, lo:hi] = _dot(hbn, _w(_win(win_ref, winb_ref, lo, hi)))
        return run

    def _alr_block():
        alrn_scr[...] = _dot(hbn, _w(walr_ref[...]))

    pending = [_proj_block(lo) for lo in range(0, N_MAIN, STAGE1_COLS)] + [_alr_block]

    def pump(count=1):
        for _ in range(min(count, len(pending))):
            pending.pop(0)()

    x = x_ref[0]
    mod = mod_ref[0]
    g1, sh2, sc2 = mod[2:3], mod[3:4], mod[4:5]

    def proj(lo, hi):
        return pc_scr[:, lo:hi] + bin_ref[:, lo:hi]

    pump(O_RY // STAGE1_COLS)

    rx = pn_scr[:, O_RX:O_RX + D_RNN] + bin_ref[:, O_RX:O_RX + D_RNN]
    rxbuf[SUBLANES:SUBLANES + tt, :] = rx
    cw = cw_ref[...]
    xcn = cb_ref[...] + rxbuf[SUBLANES - 3:SUBLANES - 3 + tt, :] * cw[0:1]
    xcn = xcn + rxbuf[SUBLANES - 2:SUBLANES - 2 + tt, :] * cw[1:2]
    xcn = xcn + rxbuf[SUBLANES - 1:SUBLANES - 1 + tt, :] * cw[2:3]
    xcn = xcn + rx * cw[3:4]
    rxbuf[0:SUBLANES, :] = rxbuf[tt:tt + SUBLANES, :]
    xcn_scr[...] = xcn
    xc = xcc_scr[...]

    q = proj(O_Q, O_Q + GLA_DK) * (GLA_DKH ** -0.5)
    k = proj(O_K, O_K + GLA_DK)
    v = proj(O_V, O_V + GLA_DV)
    la = _log_decay_from(alrc_scr[...] + balr_ref[...], wa2_ref, gba_ref)
    pump()
    la = la.reshape(tt // SUBLANES, SUBLANES, GLA_DK)
    subk = lax.broadcasted_iota(I32, (1, SUBLANES, GLA_DK), 1)
    s = 1
    while s < SUBLANES:
        la = la + jnp.where(subk >= s, pltpu.roll(la, s, 1), 0.0)
        s *= 2
    groups = []
    for g in range(tt // SUBLANES):
        blk = la[g] if g % (GLA_CHUNK // SUBLANES) == 0 else la[g] + groups[-1][SUBLANES - 1:SUBLANES]
        groups.append(blk)
    bcum = jnp.concatenate(groups, axis=0)
    eb = jnp.exp(bcum)
    q_in = (q * eb).astype(BF16)
    k_in = (k * jnp.exp(-bcum)).astype(BF16)
    pump()
    stage0()
    gelu_ry = jax.nn.gelu(proj(O_RY, O_RY + D_RNN))
    pump()

    a, u = _gates(xc, gatec_scr[:, :D_RNN], gatec_scr[:, D_RNN:], rba_ref[...], rbx_ref[...], lam_ref[...],
                  between=pump)
    s = 1
    a = a.reshape(tt // SUBLANES, SUBLANES, D_RNN)
    u = u.reshape(tt // SUBLANES, SUBLANES, D_RNN)
    sub = lax.broadcasted_iota(I32, (1, SUBLANES, D_RNN), 1)
    while s < SUBLANES:
        keep = sub >= s
        a_s = jnp.where(keep, pltpu.roll(a, s, 1), 1.0)
        u_s = jnp.where(keep, pltpu.roll(u, s, 1), 0.0)
        u = a * u_s + u
        a = a * a_s
        if s < SUBLANES // 2:
            pump()
        s *= 2
    carry = hcar[0:1, :]
    groups = []
    for g in range(tt // SUBLANES):
        hg = a[g] * carry + u[g]
        groups.append(hg)
        carry = hg[SUBLANES - 1:SUBLANES]
    hseq = jnp.concatenate(groups, axis=0)
    hcar[0:1, :] = carry
    y_a = hseq * gelu_ry

    tri = (lax.broadcasted_iota(I32, (GLA_CHUNK, GLA_CHUNK), 0)
           >= lax.broadcasted_iota(I32, (GLA_CHUNK, GLA_CHUNK), 1))
    n_chunks = tt // GLA_CHUNK
    nt_dims = (((1,), (1,)), ((), ()))
    attn, d_st, dec, vh_all = {}, {}, {}, {}
    for c in range(n_chunks):
        r0 = c * GLA_CHUNK
        for hh in range(GLA_HEADS):
            kc = slice(hh * GLA_DKH, (hh + 1) * GLA_DKH)
            b = bcum[r0:r0 + GLA_CHUNK, kc]
            btot = b[GLA_CHUNK - 1:GLA_CHUNK, :]
            k_out = (k[r0:r0 + GLA_CHUNK, kc] * jnp.exp(btot - b)).astype(BF16)
            vh = v[r0:r0 + GLA_CHUNK, hh * GLA_DVH:(hh + 1) * GLA_DVH].astype(BF16)
            scores = lax.dot_general(q_in[r0:r0 + GLA_CHUNK, kc], k_in[r0:r0 + GLA_CHUNK, kc], nt_dims,
                                     preferred_element_type=F32)
            attn[c, hh] = jnp.where(tri, scores, 0.0).astype(BF16)
            d_st[c, hh] = lax.dot_general(vh, k_out, (((0,), (0,)), ((), ())), preferred_element_type=F32)
            dec[c, hh] = jnp.exp(btot)
            vh_all[c, hh] = vh
        if c % 2 == 0:
            pump()
    starts = {}
    for hh in range(GLA_HEADS):
        st = st_ref[hh]
        for c in range(n_chunks):
            starts[c, hh] = st.astype(BF16)
            st = st * dec[c, hh] + d_st[c, hh]
        st_ref[hh] = st
    for c in range(n_chunks):
        r0 = c * GLA_CHUNK
        for hh in range(GLA_HEADS):
            qi = q_in[r0:r0 + GLA_CHUNK, hh * GLA_DKH:(hh + 1) * GLA_DKH]
            inter = lax.dot_general(qi, starts[c, hh], nt_dims, preferred_element_type=F32)
            o_scr[r0:r0 + GLA_CHUNK, hh * GLA_DVH:(hh + 1) * GLA_DVH] = _dot(attn[c, hh], vh_all[c, hh]) + inter

    x1, hf, route = _post(x, o_scr[...], proj(O_GG, O_GG + GLA_DV), proj(O_GA, O_GA + D_MODEL),
                          proj(O_GB, O_GB + D_MODEL), y_a, g1, sh2, sc2, gng_ref[...],
                          pa_ref, pb_ref, wo_ref, l1g_ref[...], l1b_ref[...], wr_ref, br_ref[...], between=pump)
    x1_ref[0] = x1
    _store_token_tiles(hf_ref.at[0], hf)
    route_ref[0] = route
    routet_ref[...] = route.T[0:SUBLANES, :]

    pump(len(pending))
    _gate_logits_to(gaten_scr, xcn, wrg_ref)
    pc_scr[:, O_RY:] = pn_scr[:, O_RY:]
    alrc_scr[...] = alrn_scr[...]
    hbc_scr[...] = hbn_scr[...]
    xcc_scr[...] = xcn_scr[...]
    gatec_scr[...] = gaten_scr[...]

    @pl.when((j1 == tiles_per_seq - 1) & (i >= 1))
    def _final_conv():
        convn_ref[0] = rxbuf[SUBLANES - (CONV_W - 1):SUBLANES, :]

    @pl.when((j == tiles_per_seq - 1) & (i >= PIPE_DEPTH))
    def _final():
        hlast_ref[0] = hcar[0:1, :]
        for hh in range(GLA_HEADS):
            sfin_ref[0, hh] = st_ref[hh].T


def _mixer_call(x, mod3, wts):
    b, t, _ = x.shape
    tt = MIX_TT
    assert t % tt == 0 and tt % GLA_CHUNK == 0
    nt = t // tt
    n_tiles = b * nt
    assert nt > 1
    cur = lambda i: jnp.maximum(i - PIPE_DEPTH, 0)
    nxt = lambda i: jnp.minimum(i, n_tiles - 1)
    tok = lambda i: (cur(i) // nt, cur(i) % nt, 0)
    tok_n = lambda i: (nxt(i) // nt, nxt(i) % nt, 0)
    per_b = lambda i: (cur(i) // nt, 0, 0)
    in_specs = [pl.BlockSpec((1, tt, D_MODEL), tok_n),
                pl.BlockSpec((1, 6, D_MODEL), lambda i: (nxt(i) // nt, 0, 0)),
                pl.BlockSpec((1, tt, D_MODEL), tok),
                pl.BlockSpec((1, 6, D_MODEL), per_b)] + [_const_spec(w.shape) for w in wts]
    out_specs = [pl.BlockSpec((1, tt, D_MODEL), tok),
                 pl.BlockSpec((1, tt * SUBLANES, LANES), tok),
                 pl.BlockSpec((1, tt, LANES), tok),
                 pl.BlockSpec((SUBLANES, tt), lambda i: (0, cur(i))),
                 pl.BlockSpec((1, CONV_W - 1, D_RNN), per_b),
                 pl.BlockSpec((1, 1, D_RNN), per_b),
                 pl.BlockSpec((1, GLA_HEADS, GLA_DKH, GLA_DVH), lambda i: (cur(i) // nt, 0, 0, 0))]
    out_shape = [jax.ShapeDtypeStruct((b, t, D_MODEL), F32),
                 jax.ShapeDtypeStruct((b, t * SUBLANES, LANES), F32),
                 jax.ShapeDtypeStruct((b, t, LANES), F32),
                 jax.ShapeDtypeStruct((SUBLANES, b * t), F32),
                 jax.ShapeDtypeStruct((b, CONV_W - 1, D_RNN), F32),
                 jax.ShapeDtypeStruct((b, 1, D_RNN), F32),
                 jax.ShapeDtypeStruct((b, GLA_HEADS, GLA_DKH, GLA_DVH), F32)]
    scratch = [pltpu.VMEM((tt, D_MODEL), BF16), pltpu.VMEM((tt, D_MODEL), BF16),
               pltpu.VMEM((tt, N_MAIN), F32), pltpu.VMEM((tt, N_MAIN), F32),
               pltpu.VMEM((tt, LANES), F32), pltpu.VMEM((tt, LANES), F32),
               pltpu.VMEM((tt, D_RNN), F32), pltpu.VMEM((tt, D_RNN), F32),
               pltpu.VMEM((tt, 2 * D_RNN), F32), pltpu.VMEM((tt, 2 * D_RNN), F32),
               pltpu.VMEM((tt + SUBLANES, D_RNN), F32),
               pltpu.VMEM((SUBLANES, D_RNN), F32),
               pltpu.VMEM((GLA_HEADS, GLA_DVH, GLA_DKH), F32),
               pltpu.VMEM((tt, GLA_DV), F32)]
    return pl.pallas_call(
        functools.partial(_mixer_kernel, tiles_per_seq=nt),
        grid=(n_tiles + PIPE_DEPTH,),
        in_specs=in_specs,
        out_specs=out_specs,
        out_shape=out_shape,
        scratch_shapes=scratch,
        compiler_params=pltpu.CompilerParams(dimension_semantics=("arbitrary",),
                                             vmem_limit_bytes=VMEM_LIMIT_BYTES),
        name="mixer",
    )(x, mod3, x, mod3, *wts)


def _s_pre_kernel(x_ref, mod_ref, sconv_ref, h0_ref, win_ref, winb_ref, bin_ref, walr_ref, balr_ref, wa2_ref, gba_ref,
                  cw_ref, cb_ref, wrg_ref, rba_ref, rbx_ref, lam_ref,
                  convn_ref, hnew_ref, ya_ref, q_ref, k_ref, v_ref, la_ref, gg_ref, ga_ref, gb_ref):
    x = x_ref[...]
    hb = (_ln(x) * (1.0 + mod_ref[1]) + mod_ref[0]).astype(BF16)

    def proj(lo, hi):
        return _dot(hb, _w(_win(win_ref, winb_ref, lo, hi))) + bin_ref[:, lo:hi]

    rx = proj(O_RX, O_RX + D_RNN)
    cw = cw_ref[...]
    xc = cb_ref[...] + sconv_ref[0] * cw[0:1]
    xc = xc + sconv_ref[1] * cw[1:2]
    xc = xc + sconv_ref[2] * cw[2:3]
    xc = xc + rx * cw[3:4]
    convn_ref[0] = sconv_ref[1]
    convn_ref[1] = sconv_ref[2]
    convn_ref[2] = rx
    a, u = _gates(xc, *_gate_logits(xc, wrg_ref, rba_ref[...], rbx_ref[...]), lam_ref[...])
    hnew = u + a * h0_ref[...]
    hnew_ref[...] = hnew
    ya_ref[...] = hnew * jax.nn.gelu(proj(O_RY, O_RY + D_RNN))
    q_ref[...] = proj(O_Q, O_Q + GLA_DK) * (GLA_DKH ** -0.5)
    k_ref[...] = proj(O_K, O_K + GLA_DK)
    v_ref[...] = proj(O_V, O_V + GLA_DV)
    la_ref[...] = _log_decay(hb, walr_ref, balr_ref, wa2_ref, gba_ref)
    gg_ref[...] = proj(O_GG, O_GG + GLA_DV)
    ga_ref[...] = proj(O_GA, O_GA + D_MODEL)
    gb_ref[...] = proj(O_GB, O_GB + D_MODEL)


def _s_pre_call(x_s, mod_s, sconv, h0, wts):
    n = x_s.shape[0]
    full = lambda shape: pl.BlockSpec(shape, lambda i: (0,) * len(shape))
    ins = [x_s, mod_s, sconv, h0] + list(wts)
    shapes = [(CONV_W - 1, n, D_RNN), (n, D_RNN), (n, D_RNN), (n, GLA_DK), (n, GLA_DK), (n, GLA_DV),
              (n, GLA_DK), (n, GLA_DV), (n, D_MODEL), (n, D_MODEL)]
    return pl.pallas_call(
        _s_pre_kernel,
        grid=(1,),
        in_specs=[full(a.shape) for a in ins],
        out_specs=[full(s) for s in shapes],
        out_shape=[jax.ShapeDtypeStruct(s, F32) for s in shapes],
        compiler_params=pltpu.CompilerParams(dimension_semantics=("arbitrary",),
                                             vmem_limit_bytes=VMEM_LIMIT_BYTES),
        name="s_pre",
    )(*ins)


def _to_column(row, n):
    eye = lax.broadcasted_iota(I32, (n, n), 0) == lax.broadcasted_iota(I32, (n, n), 1)
    return jnp.sum(jnp.where(eye, jnp.broadcast_to(row, (n, n)), 0.0), axis=1, keepdims=True)


def _s_state_kernel(s_ref, q_ref, k_ref, v_ref, la_ref, snew_ref, o_ref):
    sb = s_ref.shape[0]
    qb = q_ref[...].astype(BF16)
    dec = jnp.exp(la_ref[...])
    kk = k_ref[...]
    vv = v_ref[...]
    for si in range(sb):
        for hh in range(GLA_HEADS):
            kc = slice(hh * GLA_DKH, (hh + 1) * GLA_DKH)
            vc = slice(hh * GLA_DVH, (hh + 1) * GLA_DVH)
            d_col = _to_column(dec[si:si + 1, kc], GLA_DKH)
            k_col = _to_column(kk[si:si + 1, kc], GLA_DKH)
            s_new = s_ref[si, hh] * d_col + k_col * vv[si:si + 1, vc]
            snew_ref[si, hh] = s_new
            o_all = _dot(qb[:, kc], s_new.astype(BF16))
            o_ref[si:si + 1, vc] = o_all[si:si + 1, :]


def _s_state_call(state, q, k, v, la):
    n = state.shape[0]
    sb = SAMPLE_SB
    assert n % sb == 0
    blk = lambda w: pl.BlockSpec((sb, w), lambda i: (i, 0))
    st_spec = pl.BlockSpec((sb, GLA_HEADS, GLA_DKH, GLA_DVH), lambda i: (i, 0, 0, 0))
    return pl.pallas_call(
        _s_state_kernel,
        grid=(n // sb,),
        in_specs=[st_spec, blk(GLA_DK), blk(GLA_DK), blk(GLA_DV), blk(GLA_DK)],
        out_specs=[st_spec, blk(GLA_DV)],
        out_shape=[jax.ShapeDtypeStruct(state.shape, F32), jax.ShapeDtypeStruct((n, GLA_DV), F32)],
        compiler_params=pltpu.CompilerParams(dimension_semantics=("arbitrary",),
                                             vmem_limit_bytes=VMEM_LIMIT_BYTES),
        name="s_state",
    )(state, q, k, v, la)


def _s_post_kernel(x_ref, mod_ref, o_ref, gg_ref, ga_ref, gb_ref, ya_ref, gng_ref, pa_ref, pb_ref, wo_ref,
                   l1g_ref, l1b_ref, wr_ref, br_ref, x1_ref, hf_ref, route_ref, routet_ref):
    x1, hf, route = _post(x_ref[...], o_ref[...], gg_ref[...], ga_ref[...], gb_ref[...], ya_ref[...],
                          mod_ref[2], mod_ref[3], mod_ref[4], gng_ref[...], pa_ref, pb_ref, wo_ref,
                          l1g_ref[...], l1b_ref[...], wr_ref, br_ref[...])
    x1_ref[...] = x1
    _store_token_tiles(hf_ref, hf)
    route_ref[...] = route
    routet_ref[...] = route.T[0:SUBLANES, :]


def _s_post_call(x_s, mod_s, o, gg, ga, gb, ya, wts):
    n = x_s.shape[0]
    full = lambda shape: pl.BlockSpec(shape, lambda i: (0,) * len(shape))
    ins = [x_s, mod_s, o, gg, ga, gb, ya] + list(wts)
    shapes = [(n, D_MODEL), (n * SUBLANES, LANES), (n, LANES), (SUBLANES, n)]
    return pl.pallas_call(
        _s_post_kernel,
        grid=(1,),
        in_specs=[full(a.shape) for a in ins],
        out_specs=[full(s) for s in shapes],
        out_shape=[jax.ShapeDtypeStruct(s, F32) for s in shapes],
        compiler_params=pltpu.CompilerParams(dimension_semantics=("arbitrary",),
                                             vmem_limit_bytes=VMEM_LIMIT_BYTES),
        name="s_post",
    )(*ins)


def _rank_kernel(eid_ref, rank_ref, cnt_ref, carry):
    n_chunks = eid_ref.shape[0]
    carry[...] = jnp.zeros_like(carry)
    e_iota = lax.broadcasted_iota(I32, (N_EXPERTS, LANES), 0)
    upper = (lax.broadcasted_iota(I32, (LANES, LANES), 0)
             < lax.broadcasted_iota(I32, (LANES, LANES), 1)).astype(BF16)

    def body(c, _):
        ids = eid_ref[c]
        oh0 = (e_iota == ids[0:1, :])
        oh1 = (e_iota == ids[1:2, :])
        both = (oh0 | oh1).astype(F32)
        before = _dot(both.astype(BF16), upper) + carry[:, 0:1]
        r0 = jnp.sum(jnp.where(oh0, before, 0.0), axis=0, keepdims=True)
        r1 = jnp.sum(jnp.where(oh1, before, 0.0), axis=0, keepdims=True)
        rank_ref[c] = jnp.concatenate([r0, r1], axis=0).astype(I32)
        carry[...] = carry[...] + jnp.sum(both, axis=1, keepdims=True)
        return 0

    lax.fori_loop(0, n_chunks, body, 0, unroll=3)
    cnt_ref[...] = carry[...].astype(I32)


def _rank_call(eid3):
    n_chunks = eid3.shape[0]
    full = lambda shape: pl.BlockSpec(shape, lambda i: (0,) * len(shape))
    return pl.pallas_call(
        _rank_kernel,
        grid=(1,),
        in_specs=[full(eid3.shape)],
        out_specs=[full(eid3.shape), full((N_EXPERTS, LANES))],
        out_shape=[jax.ShapeDtypeStruct(eid3.shape, I32), jax.ShapeDtypeStruct((N_EXPERTS, LANES), I32)],
        scratch_shapes=[pltpu.VMEM((N_EXPERTS, LANES), F32)],
        compiler_params=pltpu.CompilerParams(dimension_semantics=("arbitrary",)),
        name="rank",
    )(eid3)


def _dispatch_kernel(pends_ref, d0_ref, d1_ref, src_p_ref, src_s_ref, buf_ref, sem, zeros_ref, zsem, *,
                     n_p_chunks):
    i = pl.program_id(0)
    n_tiles = buf_ref.shape[0] // (MOE_TM * SUBLANES)

    @pl.when(i == 0)
    def _clear():
        zeros_ref[...] = jnp.zeros_like(zeros_ref)

        def tile_copy(row0):
            return pltpu.make_async_copy(zeros_ref, buf_ref.at[_tok_rows(row0, MOE_TM)], zsem)

        def nonempty(e):
            return pends_ref[e] > jnp.where(e == 0, 0, pends_ref[jnp.maximum(e - 1, 0)])

        def start_tail(e, _):
            @pl.when(nonempty(e))
            def _():
                tile_copy(pends_ref[e] - MOE_TM).start()
            return 0

        def wait_tail(e, _):
            @pl.when(nonempty(e))
            def _():
                tile_copy(0).wait()
            return 0

        def start_unused(j, _):
            tile_copy(j * MOE_TM).start()
            return 0

        def wait_unused(j, _):
            tile_copy(0).wait()
            return 0

        first_unused = pends_ref[N_EXPERTS - 1] // MOE_TM
        lax.fori_loop(0, N_EXPERTS, start_tail, 0)
        lax.fori_loop(first_unused, n_tiles, start_unused, 0)
        lax.fori_loop(0, N_EXPERTS, wait_tail, 0)
        lax.fori_loop(first_unused, n_tiles, wait_unused, 0)

    def scatter_rows(src_ref):
        n = src_ref.shape[0] // SUBLANES

        def start_rows(t, _):
            for k, dest_ref in enumerate((d0_ref, d1_ref)):
                pltpu.make_async_copy(src_ref.at[_tok_rows(t)], buf_ref.at[_tok_rows(dest_ref[t])],
                                      sem).start(priority=k)
            return 0

        lax.fori_loop(0, n, start_rows, 0, unroll=4)
        for _ in range(2):
            pltpu.make_async_copy(src_ref, buf_ref.at[_tok_rows(0, n)], sem).wait()

    @pl.when(i < n_p_chunks)
    def _prompt():
        scatter_rows(src_p_ref)

    @pl.when(i >= n_p_chunks)
    def _sample():
        scatter_rows(src_s_ref)


def _dispatch_call(pends, dest0, dest1, src_p, src_s, n_rows):
    chunk_rows = ROW_CHUNK * SUBLANES
    assert src_p.shape[0] % chunk_rows == 0 and src_s.shape[0] % chunk_rows == 0 and n_rows % MOE_TM == 0
    n_p_chunks = src_p.shape[0] // chunk_rows
    n_s_chunks = src_s.shape[0] // chunk_rows
    tile = (chunk_rows, LANES)
    in_specs = [pl.BlockSpec((ROW_CHUNK,), lambda i, pe: (i,), memory_space=pltpu.SMEM),
                pl.BlockSpec((ROW_CHUNK,), lambda i, pe: (i,), memory_space=pltpu.SMEM),
                pl.BlockSpec(tile, lambda i, pe: (jnp.minimum(i, n_p_chunks - 1), 0)),
                pl.BlockSpec(tile, lambda i, pe: (jnp.maximum(i - n_p_chunks, 0), 0))]
    scratch = [pltpu.SemaphoreType.DMA(()), pltpu.VMEM((MOE_TM * SUBLANES, LANES), F32),
               pltpu.SemaphoreType.DMA(())]
    return pl.pallas_call(
        functools.partial(_dispatch_kernel, n_p_chunks=n_p_chunks),
        grid_spec=pltpu.PrefetchScalarGridSpec(
            num_scalar_prefetch=1, grid=(n_p_chunks + n_s_chunks,), in_specs=in_specs,
            out_specs=pl.BlockSpec(memory_space=pl.ANY), scratch_shapes=scratch),
        out_shape=jax.ShapeDtypeStruct((n_rows * SUBLANES, LANES), F32),
        compiler_params=pltpu.CompilerParams(dimension_semantics=("arbitrary",), has_side_effects=True),
        name="dispatch",
    )(pends, dest0, dest1, src_p, src_s)


def _expert_kernel(blk_e_ref, n_used_ref, x_ref, wg0_ref, wu0_ref, wd0_ref, wg1_ref, wu1_ref, wd1_ref, y_ref,
                   wgb, wub, wdb):
    i = pl.program_id(0)
    rows = MOE_TM * SUBLANES
    halves = ((0, wg0_ref, wu0_ref, wd0_ref), (1, wg1_ref, wu1_ref, wd1_ref))

    for h, wg_ref, wu_ref, wd_ref in halves:
        e_now = blk_e_ref[2 * i + h]
        e_before = blk_e_ref[jnp.maximum(2 * i + h - 2, 0)]

        @pl.when((i == 0) | (e_now != e_before))
        def _cast_weights(h=h, wg_ref=wg_ref, wu_ref=wu_ref, wd_ref=wd_ref):
            wgb[h] = wg_ref[0].astype(BF16)
            wub[h] = wu_ref[0].astype(BF16)
            wdb[h] = wd_ref[0].astype(BF16)

    def swiglu(x_view, y_view, h):
        xb = _load_token_tiles(x_view).astype(BF16)
        g = _dot(xb, wgb[h])
        u = _dot(xb, wub[h])
        act = (jax.nn.silu(g) * u).astype(BF16)
        _store_token_tiles(y_view, _dot(act, wdb[h]))

    used = 2 * i < n_used_ref[0]
    same = blk_e_ref[2 * i] == blk_e_ref[2 * i + 1]

    @pl.when(used & same)
    def _one_expert():
        swiglu(x_ref, y_ref, 0)

    @pl.when(used & jnp.logical_not(same))
    def _two_experts():
        for h in range(2):
            swiglu(x_ref.at[pl.ds(h * rows, rows)], y_ref.at[pl.ds(h * rows, rows)], h)

    @pl.when(jnp.logical_not(used))
    def _skip():
        y_ref[...] = jnp.zeros_like(y_ref)


def _expert_call(blk_e, n_used, xbuf, we_gate, we_up, we_down):
    p = xbuf.shape[0] // SUBLANES
    assert p % (2 * MOE_TM) == 0 and blk_e.shape[0] == p // MOE_TM
    tile_spec = pl.BlockSpec((2 * MOE_TM * SUBLANES, LANES), lambda i, be, nu: (i, 0))
    w_in_spec = lambda h: pl.BlockSpec((1, D_MODEL, D_EXPERT), lambda i, be, nu: (be[2 * i + h], 0, 0))
    w_out_spec = lambda h: pl.BlockSpec((1, D_EXPERT, D_MODEL), lambda i, be, nu: (be[2 * i + h], 0, 0))
    grid_spec = pltpu.PrefetchScalarGridSpec(
        num_scalar_prefetch=2,
        grid=(p // (2 * MOE_TM),),
        in_specs=[tile_spec, w_in_spec(0), w_in_spec(0), w_out_spec(0), w_in_spec(1), w_in_spec(1), w_out_spec(1)],
        out_specs=tile_spec,
        scratch_shapes=[pltpu.VMEM((2, D_MODEL, D_EXPERT), BF16), pltpu.VMEM((2, D_MODEL, D_EXPERT), BF16),
                        pltpu.VMEM((2, D_EXPERT, D_MODEL), BF16)],
    )
    return pl.pallas_call(
        _expert_kernel,
        grid_spec=grid_spec,
        out_shape=jax.ShapeDtypeStruct((p * SUBLANES, LANES), F32),
        compiler_params=pltpu.CompilerParams(dimension_semantics=("arbitrary",),
                                             vmem_limit_bytes=VMEM_LIMIT_BYTES),
        name="experts",
    )(blk_e, n_used, xbuf, we_gate, we_up, we_down, we_gate, we_up, we_down)


def _final_kernel(d0_ref, d1_ref, d0n_ref, d1n_ref, x1_ref, ybuf_ref, route_ref, g2_ref, l2g_ref, l2b_ref, out_ref,
                  yg, sems):
    n = x1_ref.shape[0]
    i = pl.program_id(0)
    slot = i % 2

    def gather(idx_refs, s):
        def start_rows(t, _):
            for k, idx_ref in enumerate(idx_refs):
                pltpu.make_async_copy(ybuf_ref.at[_tok_rows(idx_ref[t])], yg.at[s, k, _tok_rows(t)],
                                      sems.at[s]).start(priority=k)
            return 0
        lax.fori_loop(0, n, start_rows, 0, unroll=4)

    @pl.when(i == 0)
    def _first():
        gather((d0_ref, d1_ref), 0)

    @pl.when(i + 1 < pl.num_programs(0))
    def _prefetch():
        gather((d0n_ref, d1n_ref), 1 - slot)

    for k in range(2):
        pltpu.make_async_copy(ybuf_ref.at[_tok_rows(0, n)], yg.at[slot, k], sems.at[slot]).wait()
    route = route_ref[...]
    ff = (_load_token_tiles(yg.at[slot, 0]) * route[:, 2:3]
          + _load_token_tiles(yg.at[slot, 1]) * route[:, 3:4])
    out_ref[...] = _ln(ALPHA * x1_ref[...] + g2_ref[0] * ff) * l2g_ref[...] + l2b_ref[...]


def _final_call(dest0, dest1, first_token, x1, ybuf, route, g2, ln2_g, ln2_b, rows_per_g2, tile):
    n = x1.shape[0]
    assert n % tile == 0 and rows_per_g2 % tile == 0 and first_token % tile == 0
    per = rows_per_g2 // tile
    g2_rows = g2.shape[1]
    steps = n // tile
    off = first_token // tile
    idx = lambda: pl.BlockSpec((tile,), lambda i: (off + i,), memory_space=pltpu.SMEM)
    idx_next = lambda: pl.BlockSpec((tile,), lambda i: (off + jnp.minimum(i + 1, steps - 1),),
                                    memory_space=pltpu.SMEM)
    return pl.pallas_call(
        _final_kernel,
        grid=(steps,),
        in_specs=[idx(), idx(), idx_next(), idx_next(),
                  pl.BlockSpec((tile, D_MODEL), lambda i: (i, 0)),
                  pl.BlockSpec(memory_space=pl.ANY),
                  pl.BlockSpec((tile, LANES), lambda i: (i, 0)),
                  pl.BlockSpec((1, g2_rows, D_MODEL), lambda i: (i // per, 0, 0)),
                  pl.BlockSpec((1, D_MODEL), lambda i: (0, 0)),
                  pl.BlockSpec((1, D_MODEL), lambda i: (0, 0))],
        out_specs=pl.BlockSpec((tile, D_MODEL), lambda i: (i, 0)),
        out_shape=jax.ShapeDtypeStruct((n, D_MODEL), F32),
        scratch_shapes=[pltpu.VMEM((2, 2, tile * SUBLANES, LANES), F32), pltpu.SemaphoreType.DMA((2,))],
        compiler_params=pltpu.CompilerParams(dimension_semantics=("arbitrary",)),
        name="final",
    )(dest0, dest1, dest0, dest1, x1, ybuf, route, g2, ln2_g, ln2_b)


def _block_diag_gate_weights(wa, wx):
    def bd(w):
        w = w.reshape(N_RG_TILES, RG_GROUP, RG_BW, RG_BW)
        eye = jnp.eye(RG_GROUP, dtype=w.dtype)
        return jnp.einsum('tgcd,gh->tgchd', w, eye).reshape(N_RG_TILES, MXU_DIM, MXU_DIM)
    return jnp.concatenate([bd(wa), bd(wx)], axis=2)


def kernel(x_prompt, x_sample, state_conv, state_rglru, state_gla, c_prompt, c_sample, w_mod, b_mod, w_in, b_in,
           conv_w, conv_b, rg_wa, rg_ba, rg_wx, rg_bx, rg_lambda, gla_wa2, gla_ba, gla_norm_g, p_a, p_b, w_o,
           ln1_g, ln1_b, w_grp, b_grp, w_exp, b_exp, we_gate, we_up, we_down, ln2_g, ln2_b):
    assert w_mod.shape[0] == DEPTH == 1
    bp, tp, _ = x_prompt.shape
    ns = x_sample.shape[0]
    n_p = bp * tp
    n_tok = n_p + ns
    row = lambda a: a.reshape(1, -1)

    lo = 2 * D_RNN + 2 * GLA_DK + 2 * GLA_DV
    w_in0 = w_in[0]
    assert lo == O_GA
    win_a = _pack_rows(w_in0, lo)
    win_b = _pack_rows(w_in0[:, lo + GLA_RANK:])
    bin_main = row(jnp.concatenate([b_in[0, :lo], b_in[0, lo + GLA_RANK:]]))
    walr = _pack_rows(jnp.pad(w_in0[:, lo:lo + GLA_RANK], ((0, 0), (0, LANES - GLA_RANK))))
    balr = row(jnp.pad(b_in[0, lo:lo + GLA_RANK], (0, LANES - GLA_RANK)))
    wa2 = _pack_rows(jnp.pad(gla_wa2[0], ((0, LANES - GLA_RANK), (0, 0))))
    wrg = _pack_rows(_block_diag_gate_weights(rg_wa[0], rg_wx[0]).reshape(RG_BLOCKS * RG_BW, 2 * MXU_DIM))
    wrg = wrg.reshape(N_RG_TILES, MXU_DIM // 2, 2 * MXU_DIM)
    w_route = _pack_rows(jnp.pad(jnp.concatenate([w_exp[0], w_grp[0]], axis=1),
                                 ((0, 0), (0, LANES - N_EXPERTS - N_GROUPS))))
    b_route = row(jnp.pad(jnp.concatenate([b_exp[0], b_grp[0]]), (0, LANES - N_EXPERTS - N_GROUPS)))
    pa, pb, wo = _pack_rows(p_a[0]), _pack_rows(p_b[0]), _pack_rows(w_o[0])
    pre_w = [win_a, win_b, bin_main, walr, balr, wa2, row(gla_ba[0])]
    rec_w = [conv_w[0], row(conv_b[0]), wrg, row(rg_ba[0]), row(rg_bx[0]), row(rg_lambda[0])]
    post_w = [row(gla_norm_g[0]), pa, pb, wo, row(ln1_g[0]), row(ln1_b[0]), w_route, b_route]

    mod = _mod_call(jnp.concatenate([c_prompt, c_sample], axis=0), w_mod[0], row(b_mod[0]))
    mod_p = mod[:bp].reshape(bp, 6, D_MODEL)
    mod_s = mod[bp:].reshape(ns, 6, D_MODEL).transpose(1, 0, 2)

    mix_w = pre_w + [post_w[0]] + rec_w + post_w[1:]
    x1_p, hf_p, route_p, routet_p, conv_p, h_p, s_p = _mixer_call(x_prompt, mod_p, mix_w)

    xs = x_sample.reshape(ns, D_MODEL)
    sconv = state_conv[0].transpose(1, 0, 2)
    conv_s, h_s, ya_s, q_s, k_s, v_s, la_s, gg_s, ga_s, gb_s = _s_pre_call(
        xs, mod_s, sconv, state_rglru[0], pre_w + rec_w)
    s_s, o_s = _s_state_call(state_gla[0], q_s, k_s, v_s, la_s)
    x1_s, hf_s, route_s, routet_s = _s_post_call(xs, mod_s, o_s, gg_s, ga_s, gb_s, ya_s, post_w)

    route_pf = route_p.reshape(n_p, LANES)
    eid = jnp.concatenate([routet_p[:2], routet_s[:2]], axis=1).astype(I32)
    assert n_tok % LANES == 0
    n_chunks = n_tok // LANES
    rank3, cnt = _rank_call(eid.reshape(2, n_chunks, LANES).transpose(1, 0, 2))
    rank = rank3.transpose(1, 0, 2).reshape(2, n_tok)
    counts = cnt[:, 0]
    pcounts = (counts + MOE_TM - 1) // MOE_TM * MOE_TM
    pends = jnp.cumsum(pcounts)
    pstarts = pends - pcounts
    experts = jnp.arange(N_EXPERTS, dtype=I32)[:, None, None]
    start_of = jnp.sum(jnp.where(eid[None] == experts, pstarts.astype(I32)[:, None, None], 0), axis=0)
    dest = start_of + rank
    dest0, dest1 = dest[0], dest[1]
    n_tiles = -(-(2 * n_tok + N_EXPERTS * (MOE_TM - 1)) // (2 * MOE_TM)) * 2
    tile_start = jnp.arange(n_tiles, dtype=I32) * MOE_TM
    blk_e = jnp.minimum(jnp.sum(pends[None, :] <= tile_start[:, None], axis=1), N_EXPERTS - 1).astype(I32)
    n_used = (pends[-1] // MOE_TM).astype(I32).reshape(1)

    pends32 = pends.astype(I32)
    xbuf = _dispatch_call(pends32, dest0, dest1, hf_p.reshape(n_p * SUBLANES, LANES), hf_s, n_tiles * MOE_TM)
    ybuf = _expert_call(blk_e, n_used, xbuf, we_gate[0], we_up[0], we_down[0])

    y_p = _final_call(dest0, dest1, 0, x1_p.reshape(n_p, D_MODEL), ybuf, route_pf, mod_p[:, 5:6, :],
                      row(ln2_g[0]), row(ln2_b[0]), tp, ROW_CHUNK)
    y_s = _final_call(dest0, dest1, n_p, x1_s, ybuf, route_s, mod_s[5][None], row(ln2_g[0]), row(ln2_b[0]), ns, ns)

    return (y_p.reshape(bp, tp, D_MODEL), y_s.reshape(ns, 1, D_MODEL),
            conv_p[None], h_p.reshape(1, bp, D_RNN), s_p[None],
            conv_s.transpose(1, 0, 2)[None], h_s[None], s_s[None])
```
